```python
import jax, jax.numpy as jnp
from jax import lax
import numpy as np

D_MODEL = 2048
BATCH = 8
SEQ = 4096
DEPTH = 4

HEAD_DIM = 64
EPS = 1e-6
Q_BLOCK = 128
FOX_HEADS = 8
FOX_WIDTH = FOX_HEADS * HEAD_DIM
CONV_WIDTH = 512
CONV_TAPS = 3
SGU_GROUPS = 4
SGU_GROUP_DIM = 128
SGU_WIDTH = SGU_GROUPS * SGU_GROUP_DIM
SGU_CHUNK = 128
DIL_PATTERNS = ((128, 1), (512, 4), (2048, 16))
DIL_HEADS_PER_GROUP = 4
DIL_HEADS = DIL_HEADS_PER_GROUP * len(DIL_PATTERNS)
DIL_WIDTH = DIL_HEADS * HEAD_DIM
DIL_OUT = DIL_HEADS_PER_GROUP * HEAD_DIM
ROPE_THETA = 500000.0
ROPE_DIM = HEAD_DIM // 4
N_BRANCH = 4
IN_SIZES = (3 * FOX_WIDTH, FOX_HEADS, 3 * CONV_WIDTH, 2 * SGU_WIDTH, 3 * DIL_WIDTH, N_BRANCH * D_MODEL)
D_IN = 3 * FOX_WIDTH + FOX_HEADS + 3 * CONV_WIDTH + 2 * SGU_WIDTH + 3 * DIL_WIDTH + N_BRANCH * D_MODEL
D_FF = 5632
FFN_TAPS = 3
PLE_DIM = 256

kernel_name = "hybrid_parallel_gated_fox_conv_sgu_dilated"


def rms_norm(x, g):
    xf = x.astype(jnp.float32)
    var = jnp.mean(xf * xf, axis=-1, keepdims=True)
    return (xf * lax.rsqrt(var + EPS)).astype(x.dtype) * g


def causal_dwconv(z, w):
    K = w.shape[0]
    S = z.shape[1]
    zp = jnp.pad(z, ((0, 0), (K - 1, 0), (0, 0)))
    return sum(w[k] * zp[:, k:k + S] for k in range(K))


def partial_rope(x, positions):
    half = ROPE_DIM // 2
    inv = ROPE_THETA ** (-jnp.arange(half, dtype=jnp.float32) * (2.0 / ROPE_DIM))
    ang = positions.astype(jnp.float32)[..., None] * inv
    cos = jnp.cos(ang)[:, :, None, :]
    sin = jnp.sin(ang)[:, :, None, :]
    x1 = x[..., :half].astype(jnp.float32)
    x2 = x[..., half:ROPE_DIM].astype(jnp.float32)
    rot = jnp.concatenate([(x1 * cos - x2 * sin).astype(x.dtype),
                           (x1 * sin + x2 * cos).astype(x.dtype),
                           x[..., ROPE_DIM:]], axis=-1)
    return rot


def forgetting_attention(q, k, v, log_f):
    B, S, H, Dh = q.shape
    nb = S // Q_BLOCK
    F = jnp.cumsum(log_f, axis=1).transpose(0, 2, 1)
    qb = q.reshape(B, nb, Q_BLOCK, H, Dh).transpose(1, 0, 2, 3, 4)
    Fb = F.reshape(B, H, nb, Q_BLOCK).transpose(2, 0, 1, 3)
    kpos = jnp.arange(S)
    scale = Dh ** -0.5

    def one_block(args):
        blk, q_blk, f_blk = args
        s = jnp.einsum('bqhd,bkhd->bhqk', q_blk, k, preferred_element_type=jnp.float32) * scale
        s = s + (f_blk[..., :, None] - F[..., None, :])
        qpos = blk * Q_BLOCK + jnp.arange(Q_BLOCK)
        s = jnp.where(kpos[None, :] <= qpos[:, None], s, -jnp.inf)
        p = jax.nn.softmax(s, axis=-1).astype(v.dtype)
        return jnp.einsum('bhqk,bkhd->bqhd', p, v)

    out = lax.map(one_block, (jnp.arange(nb), qb, Fb))
    return out.transpose(1, 0, 2, 3, 4).reshape(B, S, H * Dh)


def dilated_window_attention(q, k, v, window, dilation):
    B, S, H, Dh = q.shape
    L = S // dilation
    span = window // dilation
    nb = -(-L // span)
    Lp = nb * span

    def strided_blocks(t):
        t = t.reshape(B, L, dilation, H, Dh).transpose(0, 2, 1, 3, 4)
        t = jnp.pad(t, ((0, 0), (0, 0), (0, Lp - L), (0, 0), (0, 0)))
        return t.reshape(B, dilation, nb, span, H, Dh)

    def with_prev(t):
        prev = jnp.pad(t[:, :, :-1], ((0, 0), (0, 0), (1, 0), (0, 0), (0, 0), (0, 0)))
        return jnp.concatenate([prev, t], axis=3)

    qs = strided_blocks(q)
    kc = with_prev(strided_blocks(k))
    vc = with_prev(strided_blocks(v))
    s = jnp.einsum('brnqhd,brnkhd->brnhqk', qs, kc, preferred_element_type=jnp.float32) * (Dh ** -0.5)
    blk = jnp.arange(nb)[:, None, None]
    qi = jnp.arange(span)[None, :, None]
    ki = jnp.arange(2 * span)[None, None, :]
    dist = span + qi - ki
    valid = (dist >= 0) & (dist <= span) & ((blk > 0) | (ki >= span))
    s = jnp.where(valid[:, None], s, -jnp.inf)
    lse = jax.nn.logsumexp(s, axis=-1, keepdims=True)
    p = jnp.exp(s - lse).astype(v.dtype)
    o = jnp.einsum('brnhqk,brnkhd->brnqhd', p, vc)
    o = o.reshape(B, dilation, Lp, H, Dh)[:, :, :L].transpose(0, 2, 1, 3, 4).reshape(B, S, H, Dh)
    lse = lse[..., 0].transpose(0, 1, 2, 4, 3).reshape(B, dilation, Lp, H)[:, :, :L]
    lse = lse.transpose(0, 2, 1, 3).reshape(B, S, H)
    return o, lse


def chunked_spatial_gating(z, norm_g, w_s, b_s):
    u, v = jnp.split(z, 2, axis=-1)
    v = rms_norm(v, norm_g)
    B, S, _ = v.shape
    nc = S // SGU_CHUNK
    vc = v.reshape(B, nc, SGU_CHUNK, SGU_GROUPS, SGU_GROUP_DIM)
    mask = jnp.tril(jnp.ones((SGU_CHUNK, SGU_CHUNK), dtype=bool))
    ws = jnp.where(mask[None], w_s, jnp.zeros_like(w_s))
    mixed = jnp.einsum('gts,bcsgd->bctgd', ws, vc) + b_s.T[None, None, :, :, None]
    return u * mixed.reshape(B, S, SGU_WIDTH)


def _fwd_setup_inputs(seed: int = 0) -> dict:
    key = jax.random.key(seed)
    ks = jax.random.split(key, 24)

    def nrm(k, shape, scale):
        return jax.random.normal(k, shape, jnp.float32) * scale

    def gain(k, shape):
        return 1.0 + 0.1 * jax.random.normal(k, shape, jnp.float32)

    offset = jax.random.randint(ks[2], (BATCH, 1), 0, 1024, dtype=jnp.int32)
    positions = offset + jnp.arange(SEQ, dtype=jnp.int32)[None, :]
    return {
        "x": nrm(ks[0], (BATCH, SEQ, D_MODEL), 1.0),
        "p": nrm(ks[1], (DEPTH, BATCH, SEQ, PLE_DIM), 1.0),
        "positions": positions,
        "norm_mix_g": gain(ks[3], (DEPTH, D_MODEL)),
        "w_in": nrm(ks[4], (DEPTH, D_MODEL, D_IN), D_MODEL ** -0.5),
        "fox_forget_b": 2.0 + 3.0 * jax.random.uniform(ks[5], (DEPTH, FOX_HEADS), jnp.float32),
        "shortconv_w": nrm(ks[6], (DEPTH, CONV_TAPS, CONV_WIDTH), CONV_TAPS ** -0.5),
        "sgu_norm_g": gain(ks[7], (DEPTH, SGU_WIDTH)),
        "sgu_w": nrm(ks[8], (DEPTH, SGU_GROUPS, SGU_CHUNK, SGU_CHUNK), SGU_CHUNK ** -0.5),
        "sgu_b": gain(ks[9], (DEPTH, SGU_GROUPS, SGU_CHUNK)),
        "w_br_fox": nrm(ks[10], (DEPTH, FOX_WIDTH, D_MODEL), FOX_WIDTH ** -0.5),
        "w_br_conv": nrm(ks[11], (DEPTH, CONV_WIDTH, D_MODEL), CONV_WIDTH ** -0.5),
        "w_br_sgu": nrm(ks[12], (DEPTH, SGU_WIDTH, D_MODEL), SGU_WIDTH ** -0.5),
        "w_br_dil": nrm(ks[13], (DEPTH, DIL_OUT, D_MODEL), DIL_OUT ** -0.5),
        "w_out": nrm(ks[14], (DEPTH, D_MODEL, D_MODEL), D_MODEL ** -0.5),
        "norm_ffn_g": gain(ks[15], (DEPTH, D_MODEL)),
        "w_up": nrm(ks[16], (DEPTH, D_MODEL, 2 * D_FF), D_MODEL ** -0.5),
        "ffn_conv_w": nrm(ks[17], (DEPTH, FFN_TAPS, 2 * D_FF), FFN_TAPS ** -0.5),
        "w_down": nrm(ks[18], (DEPTH, D_FF, D_MODEL), D_FF ** -0.5),
        "norm_ple_g": gain(ks[19], (DEPTH, D_MODEL)),
        "w_ple_gate": nrm(ks[20], (DEPTH, D_MODEL, D_MODEL), D_MODEL ** -0.5),
        "w_ple_proj": nrm(ks[21], (DEPTH, PLE_DIM, D_MODEL), PLE_DIM ** -0.5),
        "final_norm_g": gain(ks[22], (D_MODEL,)),
    }


def _fwd_reference(x, p, positions, norm_mix_g, w_in, fox_forget_b, shortconv_w, sgu_norm_g, sgu_w, sgu_b,
              w_br_fox, w_br_conv, w_br_sgu, w_br_dil, w_out, norm_ffn_g, w_up, ffn_conv_w, w_down,
              norm_ple_g, w_ple_gate, w_ple_proj, final_norm_g):
    B, S, _ = x.shape
    split_points = [int(c) for c in np.cumsum(IN_SIZES)[:-1]]
    for i in range(DEPTH):
        h = rms_norm(x, norm_mix_g[i])
        proj = h @ w_in[i]
        a_qkv, a_f, b_in, c_in, d_qkv, gate_logits = jnp.split(proj, split_points, axis=-1)

        a_qkv = a_qkv.reshape(B, S, 3, FOX_HEADS, HEAD_DIM)
        log_f = jax.nn.log_sigmoid(a_f.astype(jnp.float32) + fox_forget_b[i].astype(jnp.float32))
        o_a = forgetting_attention(a_qkv[:, :, 0], a_qkv[:, :, 1], a_qkv[:, :, 2], log_f)

        xb, gate_b, gate_c = jnp.split(b_in, 3, axis=-1)
        o_b = gate_b * causal_dwconv(gate_c * xb, shortconv_w[i])

        o_c = chunked_spatial_gating(jax.nn.gelu(c_in), sgu_norm_g[i], sgu_w[i], sgu_b[i])

        d_qkv = d_qkv.reshape(B, S, 3, DIL_HEADS, HEAD_DIM)
        qd = partial_rope(d_qkv[:, :, 0], positions)
        kd = partial_rope(d_qkv[:, :, 1], positions)
        vd = d_qkv[:, :, 2]
        outs, lses = [], []
        for g, (window, dil) in enumerate(DIL_PATTERNS):
            hs = slice(g * DIL_HEADS_PER_GROUP, (g + 1) * DIL_HEADS_PER_GROUP)
            o_g, l_g = dilated_window_attention(qd[:, :, hs], kd[:, :, hs], vd[:, :, hs], window, dil)
            outs.append(o_g)
            lses.append(l_g)
        wts = jax.nn.softmax(jnp.stack(lses, axis=0), axis=0)
        o_d = jnp.sum(wts[..., None] * jnp.stack(outs, axis=0), axis=0).astype(x.dtype).reshape(B, S, DIL_OUT)

        gates = jax.nn.sigmoid(gate_logits).reshape(B, S, N_BRANCH, D_MODEL)
        merged = (gates[:, :, 0] * (o_a @ w_br_fox[i]) + gates[:, :, 1] * (o_b @ w_br_conv[i])
                  + gates[:, :, 2] * (o_c @ w_br_sgu[i]) + gates[:, :, 3] * (o_d @ w_br_dil[i]))
        x = x + merged @ w_out[i]

        h = rms_norm(x, norm_ffn_g[i])
        up = causal_dwconv(h @ w_up[i], ffn_conv_w[i])
        up_gate, up_val = jnp.split(up, 2, axis=-1)
        x = x + (jax.nn.silu(up_gate) * up_val) @ w_down[i]

        ple_gate = jax.nn.sigmoid(rms_norm(x, norm_ple_g[i]) @ w_ple_gate[i])
        x = x + ple_gate * (p[i] @ w_ple_proj[i])
    return rms_norm(x, final_norm_g)


import jax as _jax
import jax.numpy as _jnp

TWIN_FORMAT = 'train_step'
FWD_PARAMS = ['x', 'p', 'positions', 'norm_mix_g', 'w_in', 'fox_forget_b', 'shortconv_w', 'sgu_norm_g', 'sgu_w', 'sgu_b', 'w_br_fox', 'w_br_conv', 'w_br_sgu', 'w_br_dil', 'w_out', 'norm_ffn_g', 'w_up', 'ffn_conv_w', 'w_down', 'norm_ple_g', 'w_ple_gate', 'w_ple_proj', 'final_norm_g']
TWIN_WEIGHTS = ['norm_mix_g', 'w_in', 'fox_forget_b', 'shortconv_w', 'sgu_norm_g', 'sgu_w', 'sgu_b', 'w_br_fox', 'w_br_conv', 'w_br_sgu', 'w_br_dil', 'w_out', 'norm_ffn_g', 'w_up', 'ffn_conv_w', 'w_down', 'norm_ple_g', 'w_ple_gate', 'w_ple_proj', 'final_norm_g']
TWIN_DIFF_INPUT = 'x'
TWIN_INPUTS = ['x', 'p', 'positions', 'norm_mix_g', 'w_in', 'fox_forget_b', 'shortconv_w', 'sgu_norm_g', 'sgu_w', 'sgu_b', 'w_br_fox', 'w_br_conv', 'w_br_sgu', 'w_br_dil', 'w_out', 'norm_ffn_g', 'w_up', 'ffn_conv_w', 'w_down', 'norm_ple_g', 'w_ple_gate', 'w_ple_proj', 'final_norm_g', 'loss_target', 'm_norm_mix_g', 'm_w_in', 'm_fox_forget_b', 'm_shortconv_w', 'm_sgu_norm_g', 'm_sgu_w', 'm_sgu_b', 'm_w_br_fox', 'm_w_br_conv', 'm_w_br_sgu', 'm_w_br_dil', 'm_w_out', 'm_norm_ffn_g', 'm_w_up', 'm_ffn_conv_w', 'm_w_down', 'm_norm_ple_g', 'm_w_ple_gate', 'm_w_ple_proj', 'm_final_norm_g', 'v_norm_mix_g', 'v_w_in', 'v_fox_forget_b', 'v_shortconv_w', 'v_sgu_norm_g', 'v_sgu_w', 'v_sgu_b', 'v_w_br_fox', 'v_w_br_conv', 'v_w_br_sgu', 'v_w_br_dil', 'v_w_out', 'v_norm_ffn_g', 'v_w_up', 'v_ffn_conv_w', 'v_w_down', 'v_norm_ple_g', 'v_w_ple_gate', 'v_w_ple_proj', 'v_final_norm_g']
TWIN_OUTPUTS = ['loss', 'grad_x', 'grad_norm_mix_g', 'grad_w_in', 'grad_fox_forget_b', 'grad_shortconv_w', 'grad_sgu_norm_g', 'grad_sgu_w', 'grad_sgu_b', 'grad_w_br_fox', 'grad_w_br_conv', 'grad_w_br_sgu', 'grad_w_br_dil', 'grad_w_out', 'grad_norm_ffn_g', 'grad_w_up', 'grad_ffn_conv_w', 'grad_w_down', 'grad_norm_ple_g', 'grad_w_ple_gate', 'grad_w_ple_proj', 'grad_final_norm_g', 'delta_norm_mix_g', 'delta_w_in', 'delta_fox_forget_b', 'delta_shortconv_w', 'delta_sgu_norm_g', 'delta_sgu_w', 'delta_sgu_b', 'delta_w_br_fox', 'delta_w_br_conv', 'delta_w_br_sgu', 'delta_w_br_dil', 'delta_w_out', 'delta_norm_ffn_g', 'delta_w_up', 'delta_ffn_conv_w', 'delta_w_down', 'delta_norm_ple_g', 'delta_w_ple_gate', 'delta_w_ple_proj', 'delta_final_norm_g', 'new_m_norm_mix_g', 'new_m_w_in', 'new_m_fox_forget_b', 'new_m_shortconv_w', 'new_m_sgu_norm_g', 'new_m_sgu_w', 'new_m_sgu_b', 'new_m_w_br_fox', 'new_m_w_br_conv', 'new_m_w_br_sgu', 'new_m_w_br_dil', 'new_m_w_out', 'new_m_norm_ffn_g', 'new_m_w_up', 'new_m_ffn_conv_w', 'new_m_w_down', 'new_m_norm_ple_g', 'new_m_w_ple_gate', 'new_m_w_ple_proj', 'new_m_final_norm_g', 'new_v_norm_mix_g', 'new_v_w_in', 'new_v_fox_forget_b', 'new_v_shortconv_w', 'new_v_sgu_norm_g', 'new_v_sgu_w', 'new_v_sgu_b', 'new_v_w_br_fox', 'new_v_w_br_conv', 'new_v_w_br_sgu', 'new_v_w_br_dil', 'new_v_w_out', 'new_v_norm_ffn_g', 'new_v_w_up', 'new_v_ffn_conv_w', 'new_v_w_down', 'new_v_norm_ple_g', 'new_v_w_ple_gate', 'new_v_w_ple_proj', 'new_v_final_norm_g']
TWIN_LEAF_KINDS = {'loss': 'loss', 'grad_x': 'grad_x', 'grad_norm_mix_g': 'grad_w', 'grad_w_in': 'grad_w', 'grad_fox_forget_b': 'grad_w', 'grad_shortconv_w': 'grad_w', 'grad_sgu_norm_g': 'grad_w', 'grad_sgu_w': 'grad_w', 'grad_sgu_b': 'grad_w', 'grad_w_br_fox': 'grad_w', 'grad_w_br_conv': 'grad_w', 'grad_w_br_sgu': 'grad_w', 'grad_w_br_dil': 'grad_w', 'grad_w_out': 'grad_w', 'grad_norm_ffn_g': 'grad_w', 'grad_w_up': 'grad_w', 'grad_ffn_conv_w': 'grad_w', 'grad_w_down': 'grad_w', 'grad_norm_ple_g': 'grad_w', 'grad_w_ple_gate': 'grad_w', 'grad_w_ple_proj': 'grad_w', 'grad_final_norm_g': 'grad_w', 'delta_norm_mix_g': 'delta_w', 'delta_w_in': 'delta_w', 'delta_fox_forget_b': 'delta_w', 'delta_shortconv_w': 'delta_w', 'delta_sgu_norm_g': 'delta_w', 'delta_sgu_w': 'delta_w', 'delta_sgu_b': 'delta_w', 'delta_w_br_fox': 'delta_w', 'delta_w_br_conv': 'delta_w', 'delta_w_br_sgu': 'delta_w', 'delta_w_br_dil': 'delta_w', 'delta_w_out': 'delta_w', 'delta_norm_ffn_g': 'delta_w', 'delta_w_up': 'delta_w', 'delta_ffn_conv_w': 'delta_w', 'delta_w_down': 'delta_w', 'delta_norm_ple_g': 'delta_w', 'delta_w_ple_gate': 'delta_w', 'delta_w_ple_proj': 'delta_w', 'delta_final_norm_g': 'delta_w', 'new_m_norm_mix_g': 'new_m', 'new_m_w_in': 'new_m', 'new_m_fox_forget_b': 'new_m', 'new_m_shortconv_w': 'new_m', 'new_m_sgu_norm_g': 'new_m', 'new_m_sgu_w': 'new_m', 'new_m_sgu_b': 'new_m', 'new_m_w_br_fox': 'new_m', 'new_m_w_br_conv': 'new_m', 'new_m_w_br_sgu': 'new_m', 'new_m_w_br_dil': 'new_m', 'new_m_w_out': 'new_m', 'new_m_norm_ffn_g': 'new_m', 'new_m_w_up': 'new_m', 'new_m_ffn_conv_w': 'new_m', 'new_m_w_down': 'new_m', 'new_m_norm_ple_g': 'new_m', 'new_m_w_ple_gate': 'new_m', 'new_m_w_ple_proj': 'new_m', 'new_m_final_norm_g': 'new_m', 'new_v_norm_mix_g': 'new_v', 'new_v_w_in': 'new_v', 'new_v_fox_forget_b': 'new_v', 'new_v_shortconv_w': 'new_v', 'new_v_sgu_norm_g': 'new_v', 'new_v_sgu_w': 'new_v', 'new_v_sgu_b': 'new_v', 'new_v_w_br_fox': 'new_v', 'new_v_w_br_conv': 'new_v', 'new_v_w_br_sgu': 'new_v', 'new_v_w_br_dil': 'new_v', 'new_v_w_out': 'new_v', 'new_v_norm_ffn_g': 'new_v', 'new_v_w_up': 'new_v', 'new_v_ffn_conv_w': 'new_v', 'new_v_w_down': 'new_v', 'new_v_norm_ple_g': 'new_v', 'new_v_w_ple_gate': 'new_v', 'new_v_w_ple_proj': 'new_v', 'new_v_final_norm_g': 'new_v'}


def _forward(args):
    return _fwd_reference(*[args[k] for k in FWD_PARAMS])


def _output_shape():
    def fwd():
        inp = _fwd_setup_inputs(0)
        return _fwd_reference(*[inp[k] for k in FWD_PARAMS])
    out = _jax.eval_shape(fwd)
    return out.shape, out.dtype

N_MICROBATCH = 1
ADAM_LR = 0.001
ADAM_B1 = 0.9
ADAM_B2 = 0.999
ADAM_EPS = 1e-08
ADAM_WD = 0.01
ADAM_STEP = 10
PER_EXAMPLE_BATCH_AXIS = {'x': 0, 'p': 1, 'positions': 0, 'loss_target': 0}
SHARED_INPUTS = []
_WEIGHT_DTYPES = {'norm_mix_g': _jnp.float32, 'w_in': _jnp.float32, 'fox_forget_b': _jnp.float32, 'shortconv_w': _jnp.float32, 'sgu_norm_g': _jnp.float32, 'sgu_w': _jnp.float32, 'sgu_b': _jnp.float32, 'w_br_fox': _jnp.float32, 'w_br_conv': _jnp.float32, 'w_br_sgu': _jnp.float32, 'w_br_dil': _jnp.float32, 'w_out': _jnp.float32, 'norm_ffn_g': _jnp.float32, 'w_up': _jnp.float32, 'ffn_conv_w': _jnp.float32, 'w_down': _jnp.float32, 'norm_ple_g': _jnp.float32, 'w_ple_gate': _jnp.float32, 'w_ple_proj': _jnp.float32, 'final_norm_g': _jnp.float32}
MOMENT_SCALE = {'norm_mix_g': 8.469723e-02, 'w_in': 3.099341e-02, 'fox_forget_b': 7.321529e-02, 'shortconv_w': 7.969087e-02, 'sgu_norm_g': 3.757246e-02, 'sgu_w': 3.567258e-02, 'sgu_b': 5.166419e-02, 'w_br_fox': 1.167879e-02, 'w_br_conv': 3.990756e-02, 'w_br_sgu': 3.833886e-02, 'w_br_dil': 8.241599e-03, 'w_out': 5.668478e-02, 'norm_ffn_g': 5.418025e-02, 'w_up': 2.339748e-02, 'ffn_conv_w': 2.337341e-02, 'w_down': 3.849386e-02, 'norm_ple_g': 1.392170e-02, 'w_ple_gate': 1.388540e-02, 'w_ple_proj': 3.385788e-02, 'final_norm_g': 1.605218e+01}


def _to_microbatches(a, axis):
    t = _jnp.moveaxis(a, axis, 0)
    t = t.reshape((N_MICROBATCH, t.shape[0] // N_MICROBATCH) + t.shape[1:])
    return _jnp.moveaxis(t, 1, axis + 1)


def setup_inputs(seed: int = 0) -> dict:
    inp = _fwd_setup_inputs(seed)
    key = _jax.random.fold_in(_jax.random.key(seed), 7919)
    shape, _ = _output_shape()
    out = dict(inp)
    out["loss_target"] = _jax.random.normal(_jax.random.fold_in(key, 0), shape, _jnp.float32)
    for i, name in enumerate(TWIN_WEIGHTS):
        w = inp[name].astype(_jnp.float32)
        if MOMENT_SCALE is None:
            s = _jnp.sqrt(_jnp.mean(_jnp.square(w)) + 1e-30)
        else:
            s = MOMENT_SCALE[name]
        km, kv = _jax.random.split(_jax.random.fold_in(key, i + 1))
        out[name] = w
        out["m_" + name] = s * _jax.random.normal(km, w.shape, _jnp.float32)
        out["v_" + name] = (s * s) * _jax.random.uniform(kv, w.shape, _jnp.float32, 0.5, 1.5)
    if N_MICROBATCH > 1:
        for name, axis in PER_EXAMPLE_BATCH_AXIS.items():
            out[name] = _to_microbatches(out[name], axis)
    return {'x': out['x'], 'p': out['p'], 'positions': out['positions'], 'norm_mix_g': out['norm_mix_g'], 'w_in': out['w_in'], 'fox_forget_b': out['fox_forget_b'], 'shortconv_w': out['shortconv_w'], 'sgu_norm_g': out['sgu_norm_g'], 'sgu_w': out['sgu_w'], 'sgu_b': out['sgu_b'], 'w_br_fox': out['w_br_fox'], 'w_br_conv': out['w_br_conv'], 'w_br_sgu': out['w_br_sgu'], 'w_br_dil': out['w_br_dil'], 'w_out': out['w_out'], 'norm_ffn_g': out['norm_ffn_g'], 'w_up': out['w_up'], 'ffn_conv_w': out['ffn_conv_w'], 'w_down': out['w_down'], 'norm_ple_g': out['norm_ple_g'], 'w_ple_gate': out['w_ple_gate'], 'w_ple_proj': out['w_ple_proj'], 'final_norm_g': out['final_norm_g'], 'loss_target': out['loss_target'], 'm_norm_mix_g': out['m_norm_mix_g'], 'm_w_in': out['m_w_in'], 'm_fox_forget_b': out['m_fox_forget_b'], 'm_shortconv_w': out['m_shortconv_w'], 'm_sgu_norm_g': out['m_sgu_norm_g'], 'm_sgu_w': out['m_sgu_w'], 'm_sgu_b': out['m_sgu_b'], 'm_w_br_fox': out['m_w_br_fox'], 'm_w_br_conv': out['m_w_br_conv'], 'm_w_br_sgu': out['m_w_br_sgu'], 'm_w_br_dil': out['m_w_br_dil'], 'm_w_out': out['m_w_out'], 'm_norm_ffn_g': out['m_norm_ffn_g'], 'm_w_up': out['m_w_up'], 'm_ffn_conv_w': out['m_ffn_conv_w'], 'm_w_down': out['m_w_down'], 'm_norm_ple_g': out['m_norm_ple_g'], 'm_w_ple_gate': out['m_w_ple_gate'], 'm_w_ple_proj': out['m_w_ple_proj'], 'm_final_norm_g': out['m_final_norm_g'], 'v_norm_mix_g': out['v_norm_mix_g'], 'v_w_in': out['v_w_in'], 'v_fox_forget_b': out['v_fox_forget_b'], 'v_shortconv_w': out['v_shortconv_w'], 'v_sgu_norm_g': out['v_sgu_norm_g'], 'v_sgu_w': out['v_sgu_w'], 'v_sgu_b': out['v_sgu_b'], 'v_w_br_fox': out['v_w_br_fox'], 'v_w_br_conv': out['v_w_br_conv'], 'v_w_br_sgu': out['v_w_br_sgu'], 'v_w_br_dil': out['v_w_br_dil'], 'v_w_out': out['v_w_out'], 'v_norm_ffn_g': out['v_norm_ffn_g'], 'v_w_up': out['v_w_up'], 'v_ffn_conv_w': out['v_ffn_conv_w'], 'v_w_down': out['v_w_down'], 'v_norm_ple_g': out['v_norm_ple_g'], 'v_w_ple_gate': out['v_w_ple_gate'], 'v_w_ple_proj': out['v_w_ple_proj'], 'v_final_norm_g': out['v_final_norm_g']}


def _loss(weights, diff, rest, loss_target):
    with _jax.named_scope("forward"):
        args = {**rest, TWIN_DIFF_INPUT: diff, **{k: w.astype(_WEIGHT_DTYPES[k]) for k, w in weights.items()}}
        y = _forward(args)
    with _jax.named_scope("loss_head"):
        err = _jnp.square(y.astype(_jnp.float32) - loss_target)
        return 0.5 * _jnp.sum(_jnp.mean(err, axis=-1)) if err.ndim else 0.5 * err


def _adamw(w, g, m, v):
    m = ADAM_B1 * m + (1.0 - ADAM_B1) * g
    v = ADAM_B2 * v + (1.0 - ADAM_B2) * _jnp.square(g)
    m_hat = m / (1.0 - ADAM_B1 ** ADAM_STEP)
    v_hat = v / (1.0 - ADAM_B2 ** ADAM_STEP)
    delta = -ADAM_LR * (m_hat / (_jnp.sqrt(v_hat) + ADAM_EPS) + ADAM_WD * w)
    return delta, m, v


def reference(x, p, positions, norm_mix_g, w_in, fox_forget_b, shortconv_w, sgu_norm_g, sgu_w, sgu_b, w_br_fox, w_br_conv, w_br_sgu, w_br_dil, w_out, norm_ffn_g, w_up, ffn_conv_w, w_down, norm_ple_g, w_ple_gate, w_ple_proj, final_norm_g, loss_target, m_norm_mix_g, m_w_in, m_fox_forget_b, m_shortconv_w, m_sgu_norm_g, m_sgu_w, m_sgu_b, m_w_br_fox, m_w_br_conv, m_w_br_sgu, m_w_br_dil, m_w_out, m_norm_ffn_g, m_w_up, m_ffn_conv_w, m_w_down, m_norm_ple_g, m_w_ple_gate, m_w_ple_proj, m_final_norm_g, v_norm_mix_g, v_w_in, v_fox_forget_b, v_shortconv_w, v_sgu_norm_g, v_sgu_w, v_sgu_b, v_w_br_fox, v_w_br_conv, v_w_br_sgu, v_w_br_dil, v_w_out, v_norm_ffn_g, v_w_up, v_ffn_conv_w, v_w_down, v_norm_ple_g, v_w_ple_gate, v_w_ple_proj, v_final_norm_g):
    given = dict(x=x, p=p, positions=positions, norm_mix_g=norm_mix_g, w_in=w_in, fox_forget_b=fox_forget_b, shortconv_w=shortconv_w, sgu_norm_g=sgu_norm_g, sgu_w=sgu_w, sgu_b=sgu_b, w_br_fox=w_br_fox, w_br_conv=w_br_conv, w_br_sgu=w_br_sgu, w_br_dil=w_br_dil, w_out=w_out, norm_ffn_g=norm_ffn_g, w_up=w_up, ffn_conv_w=ffn_conv_w, w_down=w_down, norm_ple_g=norm_ple_g, w_ple_gate=w_ple_gate, w_ple_proj=w_ple_proj, final_norm_g=final_norm_g, loss_target=loss_target, m_norm_mix_g=m_norm_mix_g, m_w_in=m_w_in, m_fox_forget_b=m_fox_forget_b, m_shortconv_w=m_shortconv_w, m_sgu_norm_g=m_sgu_norm_g, m_sgu_w=m_sgu_w, m_sgu_b=m_sgu_b, m_w_br_fox=m_w_br_fox, m_w_br_conv=m_w_br_conv, m_w_br_sgu=m_w_br_sgu, m_w_br_dil=m_w_br_dil, m_w_out=m_w_out, m_norm_ffn_g=m_norm_ffn_g, m_w_up=m_w_up, m_ffn_conv_w=m_ffn_conv_w, m_w_down=m_w_down, m_norm_ple_g=m_norm_ple_g, m_w_ple_gate=m_w_ple_gate, m_w_ple_proj=m_w_ple_proj, m_final_norm_g=m_final_norm_g, v_norm_mix_g=v_norm_mix_g, v_w_in=v_w_in, v_fox_forget_b=v_fox_forget_b, v_shortconv_w=v_shortconv_w, v_sgu_norm_g=v_sgu_norm_g, v_sgu_w=v_sgu_w, v_sgu_b=v_sgu_b, v_w_br_fox=v_w_br_fox, v_w_br_conv=v_w_br_conv, v_w_br_sgu=v_w_br_sgu, v_w_br_dil=v_w_br_dil, v_w_out=v_w_out, v_norm_ffn_g=v_norm_ffn_g, v_w_up=v_w_up, v_ffn_conv_w=v_ffn_conv_w, v_w_down=v_w_down, v_norm_ple_g=v_norm_ple_g, v_w_ple_gate=v_w_ple_gate, v_w_ple_proj=v_w_ple_proj, v_final_norm_g=v_final_norm_g)
    weights = {n: given[n] for n in TWIN_WEIGHTS}
    shared = {n: given[n] for n in SHARED_INPUTS}
    per_example = {n: given[n] for n in ['x', 'p', 'positions']}
    grad_fn = _jax.value_and_grad(_loss, argnums=(0, 1))

    def one_microbatch(ex, loss_target):
        ex = dict(ex)
        diff = ex.pop(TWIN_DIFF_INPUT)
        return grad_fn(weights, diff, {**shared, **ex}, loss_target)

    if N_MICROBATCH == 1:
        loss, (grad_w, grad_x) = one_microbatch(per_example, given["loss_target"])
    else:
        def body(carry, xs):
            loss_sum, grad_sum = carry
            l_k, (gw_k, gx_k) = one_microbatch(xs[0], xs[1])
            with _jax.named_scope("update"):
                return (loss_sum + l_k, _jax.tree.map(_jnp.add, grad_sum, gw_k)), gx_k

        init = (_jnp.zeros((), _jnp.float32), _jax.tree.map(_jnp.zeros_like, weights))
        (loss, grad_w), grad_x = _jax.lax.scan(body, init, (per_example, given["loss_target"]))
    with _jax.named_scope("update"):
        delta_w, new_m, new_v = {}, {}, {}
        for n in TWIN_WEIGHTS:
            delta_w[n], new_m[n], new_v[n] = _adamw(weights[n], grad_w[n], given["m_" + n], given["v_" + n])
    return (loss, grad_x, *[grad_w[n] for n in TWIN_WEIGHTS], *[delta_w[n] for n in TWIN_WEIGHTS],
            *[new_m[n] for n in TWIN_WEIGHTS], *[new_v[n] for n in TWIN_WEIGHTS])
```

```python
import functools
import math

import jax
import jax.numpy as jnp
from jax import lax
from jax.experimental import pallas as pl
from jax.experimental.pallas import tpu as pltpu

F32 = jnp.float32
BF16 = jnp.bfloat16
EPS = 1e-6
NEG = -1e30
HEAD_DIM = 64
SGU_CHUNK = 128
DIL_PATTERNS = ((128, 1), (512, 4), (2048, 16))
DIL_SPAN = 128
ROPE_THETA = 500000.0
ROPE_DIM = 16
N_DEV = 8
LANES = 128
VMEM_LIMIT = 56 * 1024 * 1024
ADAM_LR, ADAM_B1, ADAM_B2, ADAM_EPS, ADAM_WD, ADAM_STEP = 0.001, 0.9, 0.999, 1e-08, 0.01, 10
MESH_AXES = ("x", "y", "c")

BS = pl.BlockSpec
SDS = jax.ShapeDtypeStruct


def _cp(*sem):
    return pltpu.CompilerParams(dimension_semantics=sem, vmem_limit_bytes=VMEM_LIMIT)


def _tile(dim, pref, unit=LANES):
    if dim % unit:
        return dim
    best, t = unit, unit
    while t <= min(dim, pref):
        if dim % t == 0:
            best = t
        t += unit
    return best


def _sigmoid(z):
    return 1.0 / (1.0 + jnp.exp(-z))


def _exchange(x, gather, name):
    shape = x.shape if not gather else (N_DEV,) + x.shape

    def body(x_ref, o_ref, send_sems, recv_sems, local_sem):
        ix, iy, ic = lax.axis_index("x"), lax.axis_index("y"), lax.axis_index("c")
        me = 4 * ix + 2 * iy + ic

        def src(j):
            return x_ref if gather else x_ref.at[j]

        local = pltpu.make_async_copy(src(me), o_ref.at[me], local_sem)
        local.start()
        sends, recvs = [], []
        for r in range(1, N_DEV):
            px = 1 - ix if (r >> 2) & 1 else ix
            py = 1 - iy if (r >> 1) & 1 else iy
            pc = 1 - ic if r & 1 else ic
            peer = 4 * px + 2 * py + pc
            sends.append(pltpu.make_async_remote_copy(
                src_ref=src(peer), dst_ref=o_ref.at[me], send_sem=send_sems.at[r - 1], recv_sem=recv_sems.at[r - 1],
                device_id=(px, py, pc), device_id_type=pl.DeviceIdType.MESH))
            recvs.append(pltpu.make_async_remote_copy(
                src_ref=src(peer), dst_ref=o_ref.at[peer], send_sem=send_sems.at[r - 1], recv_sem=recv_sems.at[r - 1],
                device_id=(px, py, pc), device_id_type=pl.DeviceIdType.MESH))
        for cp in sends:
            cp.start()
        for cp in recvs:
            cp.wait_recv()
        for cp in sends:
            cp.wait_send()
        local.wait()

    return pl.pallas_call(
        body, name=name, out_shape=SDS(shape, x.dtype),
        in_specs=[BS(memory_space=pl.ANY)], out_specs=BS(memory_space=pl.ANY),
        scratch_shapes=[pltpu.SemaphoreType.DMA((N_DEV - 1,)), pltpu.SemaphoreType.DMA((N_DEV - 1,)),
                        pltpu.SemaphoreType.DMA],
    )(x)


def _sum8(x, name):
    _, R, C = x.shape
    tr = _tile(R, 256, 16)

    def body(x_ref, o_ref):
        acc = x_ref[0].astype(F32)
        for j in range(1, N_DEV):
            acc = acc + x_ref[j].astype(F32)
        o_ref[...] = acc

    return pl.pallas_call(
        body, name=name, grid=(R // tr,), out_shape=SDS((R, C), F32),
        in_specs=[BS((N_DEV, tr, C), lambda i: (0, i, 0))], out_specs=BS((tr, C), lambda i: (i, 0)),
        compiler_params=_cp("parallel"))(x)


_DIMS = {"nn": ((1,), (0,)), "nt": ((1,), (1,)), "tn": ((0,), (0,))}


def _mm(a, b, mode, out_dtype, name, res=None):
    if mode == "nn":
        (M, K), N = a.shape, b.shape[1]
    elif mode == "nt":
        (M, K), N = a.shape, b.shape[0]
    else:
        (K, M), N = a.shape, b.shape[1]
    tm, tn, tk = _tile(M, 512), _tile(N, 1024), _tile(K, 512)
    nk = K // tk
    a_spec = BS((tk, tm), lambda i, j, k: (k, i)) if mode == "tn" else BS((tm, tk), lambda i, j, k: (i, k))
    b_spec = BS((tn, tk), lambda i, j, k: (j, k)) if mode == "nt" else BS((tk, tn), lambda i, j, k: (k, j))
    o_spec = BS((tm, tn), lambda i, j, k: (i, j))
    dims = (_DIMS[mode], ((), ()))

    def body(a_ref, b_ref, *rest):
        o_ref, acc = rest[-2], rest[-1]
        k = pl.program_id(2)

        @pl.when(k == 0)
        def _():
            acc[...] = jnp.zeros_like(acc)

        acc[...] += lax.dot_general(a_ref[...].astype(BF16), b_ref[...].astype(BF16), dims,
                                    preferred_element_type=F32)

        @pl.when(k == nk - 1)
        def _():
            r = acc[...]
            if res is not None:
                r = r + rest[0][...]
            o_ref[...] = r.astype(o_ref.dtype)

    ins, specs = [a, b], [a_spec, b_spec]
    if res is not None:
        ins.append(res)
        specs.append(o_spec)
    return pl.pallas_call(
        body, name=name, grid=(M // tm, N // tn, nk), out_shape=SDS((M, N), out_dtype),
        in_specs=specs, out_specs=o_spec, scratch_shapes=[pltpu.VMEM((tm, tn), F32)],
        compiler_params=_cp("parallel", "parallel", "arbitrary"))(*ins)


def _rms_fwd(x, g, name):
    S, D = x.shape
    ts = _tile(S, 256, 8)

    def body(x_ref, g_ref, o_ref):
        xv = x_ref[...]
        rstd = lax.rsqrt(jnp.mean(xv * xv, axis=-1, keepdims=True) + EPS)
        o_ref[...] = (xv * rstd * g_ref[...]).astype(o_ref.dtype)

    return pl.pallas_call(
        body, name=name, grid=(S // ts,), out_shape=SDS((S, D), BF16),
        in_specs=[BS((ts, D), lambda i: (i, 0)), BS((1, D), lambda i: (0, 0))],
        out_specs=BS((ts, D), lambda i: (i, 0)), compiler_params=_cp("parallel"))(x, g.reshape(1, D))


def _rms_bwd(x, g, dh, dres, name):
    S, D = x.shape
    ts = _tile(S, 256, 8)

    def body(x_ref, g_ref, dh_ref, dres_ref, dx_ref, dxb_ref, dg_ref):
        xv = x_ref[...]
        rstd = lax.rsqrt(jnp.mean(xv * xv, axis=-1, keepdims=True) + EPS)
        xn = xv * rstd
        dhv = dh_ref[...]

        @pl.when(pl.program_id(0) == 0)
        def _():
            dg_ref[...] = jnp.zeros_like(dg_ref)

        dg_ref[...] += jnp.sum(dhv * xn, axis=0, keepdims=True)
        dhn = dhv * g_ref[...]
        dxv = rstd * (dhn - xn * jnp.mean(dhn * xn, axis=-1, keepdims=True)) + dres_ref[...]
        dx_ref[...] = dxv
        dxb_ref[...] = dxv.astype(BF16)

    row = BS((ts, D), lambda i: (i, 0))
    vec = BS((1, D), lambda i: (0, 0))
    return pl.pallas_call(
        body, name=name, grid=(S // ts,), out_shape=(SDS((S, D), F32), SDS((S, D), BF16), SDS((1, D), F32)),
        in_specs=[row, vec, row, row], out_specs=(row, row, vec), compiler_params=_cp("arbitrary"),
    )(x, g.reshape(1, D), dh, dres)


def _final_loss(x, g, target, name):
    S, D = x.shape
    ts = _tile(S, 256, 8)

    def body(x_ref, g_ref, t_ref, loss_ref, dx_ref, dxb_ref, dg_ref):
        xv = x_ref[...]
        rstd = lax.rsqrt(jnp.mean(xv * xv, axis=-1, keepdims=True) + EPS)
        xn = xv * rstd
        gv = g_ref[...]
        e = xn * gv - t_ref[...]

        @pl.when(pl.program_id(0) == 0)
        def _():
            dg_ref[...] = jnp.zeros_like(dg_ref)
            loss_ref[...] = jnp.zeros_like(loss_ref)

        loss_ref[...] += 0.5 * jnp.sum(jnp.mean(e * e, axis=-1, keepdims=True), axis=0, keepdims=True)
        dy = e * (1.0 / D)
        dg_ref[...] += jnp.sum(dy * xn, axis=0, keepdims=True)
        dhn = dy * gv
        dxv = rstd * (dhn - xn * jnp.mean(dhn * xn, axis=-1, keepdims=True))
        dx_ref[...] = dxv
        dxb_ref[...] = dxv.astype(BF16)

    row = BS((ts, D), lambda i: (i, 0))
    vec = BS((1, D), lambda i: (0, 0))
    return pl.pallas_call(
        body, name=name, grid=(S // ts,),
        out_shape=(SDS((1, LANES), F32), SDS((S, D), F32), SDS((S, D), BF16), SDS((1, D), F32)),
        in_specs=[row, vec, row], out_specs=(BS((1, LANES), lambda i: (0, 0)), row, row, vec),
        compiler_params=_cp("arbitrary"))(x, g.reshape(1, D), target)


def _shift_down(z, k):
    rows = lax.broadcasted_iota(jnp.int32, z.shape, 0)
    return jnp.where(rows >= k, pltpu.roll(z, k, 0), 0.0)


def _shift_up(z, k):
    n = z.shape[0]
    rows = lax.broadcasted_iota(jnp.int32, z.shape, 0)
    return jnp.where(rows < n - k, pltpu.roll(z, n - k, 0), 0.0)


def _cumsum_rows(z, reverse=False):
    n = z.shape[0]
    k = 1
    while k < n:
        z = z + (_shift_up(z, k) if reverse else _shift_down(z, k))
        k *= 2
    return z


def _conv3(z, w_ref):
    return w_ref[0:1, :] * _shift_down(z, 2) + w_ref[1:2, :] * _shift_down(z, 1) + w_ref[2:3, :] * z


def _conv3_bwd(z, dy, w_ref, dw_ref):
    dw_ref[0:1, :] = jnp.sum(dy * _shift_down(z, 2), axis=0, keepdims=True)
    dw_ref[1:2, :] = jnp.sum(dy * _shift_down(z, 1), axis=0, keepdims=True)
    dw_ref[2:3, :] = jnp.sum(dy * z, axis=0, keepdims=True)
    return w_ref[2:3, :] * dy + w_ref[1:2, :] * _shift_up(dy, 1) + w_ref[0:1, :] * _shift_up(dy, 2)


def _fox_prep(proj, bias, col, name):
    S = proj.shape[0]

    def body(a_ref, b_ref, f_ref):
        z = a_ref[...] + b_ref[...]
        logf = jnp.minimum(z, 0.0) - jnp.log(1.0 + jnp.exp(-jnp.abs(z)))
        f_ref[...] = _cumsum_rows(logf)

    return pl.pallas_call(
        body, name=name, grid=(1,), out_shape=SDS((S, LANES), F32),
        in_specs=[BS((S, LANES), lambda i: (0, col)), BS((1, LANES), lambda i: (0, 0))],
        out_specs=BS((S, LANES), lambda i: (0, 0)), compiler_params=_cp("arbitrary"))(proj, bias)


def _fox_prep_bwd(proj, bias, col, df, name):
    S = proj.shape[0]

    def body(a_ref, b_ref, df_ref, da_ref, db_ref):
        z = a_ref[...] + b_ref[...]
        da = _cumsum_rows(df_ref[...], reverse=True) * _sigmoid(-z)
        da_ref[...] = da.astype(BF16)
        db_ref[...] = jnp.sum(da, axis=0, keepdims=True)

    full = BS((S, LANES), lambda i: (0, 0))
    vec = BS((1, LANES), lambda i: (0, 0))
    return pl.pallas_call(
        body, name=name, grid=(1,), out_shape=(SDS((S, LANES), BF16), SDS((1, LANES), F32)),
        in_specs=[BS((S, LANES), lambda i: (0, col)), vec, full], out_specs=(full, vec),
        compiler_params=_cp("arbitrary"))(proj, bias, df)


def _fox_scores(qv, kv, fqv, fkv, i, j, t, scale):
    s = lax.dot_general(qv, kv, (_DIMS["nt"], ((), ())), preferred_element_type=F32) * scale + fqv - fkv
    row = i * t + lax.broadcasted_iota(jnp.int32, (t, t), 0)
    col = j * t + lax.broadcasted_iota(jnp.int32, (t, t), 1)
    return jnp.where(col <= row, s, NEG)


def _fox_fwd(q, k, v, fq, fk, name):
    H, S, Dh = q.shape
    t = _tile(S, 256)
    n = S // t
    scale = Dh ** -0.5

    def body(q_ref, k_ref, v_ref, fq_ref, fk_ref, o_ref, lse_ref):
        i = pl.program_id(1)
        qv, fqv = q_ref[0], fq_ref[0]

        def step(j, carry):
            m, l, acc = carry
            s = _fox_scores(qv, k_ref[0, j], fqv, fk_ref[0, j], i, j, t, scale)
            mn = jnp.maximum(m, jnp.max(s, axis=-1, keepdims=True))
            al = jnp.exp(m - mn)
            p = jnp.exp(s - mn)
            l = l * al + jnp.sum(p, axis=-1, keepdims=True)
            acc = acc * al + jnp.dot(p.astype(BF16), v_ref[0, j], preferred_element_type=F32)
            return mn, l, acc

        m, l, acc = lax.fori_loop(0, i + 1, step, (jnp.full((t, 1), NEG, F32), jnp.zeros((t, 1), F32),
                                                   jnp.zeros((t, Dh), F32)))
        o_ref[0] = acc / l
        lse_ref[0] = m + jnp.log(l)

    blk = BS((1, t, Dh), lambda h, i: (h, i, 0))
    col = BS((1, t, 1), lambda h, i: (h, i, 0))
    res = BS((1, n, t, Dh), lambda h, i: (h, 0, 0, 0))
    return pl.pallas_call(
        body, name=name, grid=(H, n), out_shape=(SDS((H, S, Dh), F32), SDS((H, S, 1), F32)),
        in_specs=[blk, res, res, col, BS((1, n, 1, t), lambda h, i: (h, 0, 0, 0))], out_specs=(blk, col),
        compiler_params=_cp("parallel", "arbitrary"))(q, k.reshape(H, n, t, Dh), v.reshape(H, n, t, Dh), fq, fk)


def _fox_bwd_dq(q, k, v, fq, fk, do, lse, name):
    H, S, Dh = q.shape
    t = _tile(S, 256)
    n = S // t
    scale = Dh ** -0.5

    def body(q_ref, k_ref, v_ref, fq_ref, fk_ref, do_ref, lse_ref, dq_ref, dl_ref):
        i = pl.program_id(1)
        qv, fqv, dob, lsev = q_ref[0], fq_ref[0], do_ref[0], lse_ref[0]

        def p_dp(j):
            p = jnp.exp(_fox_scores(qv, k_ref[0, j], fqv, fk_ref[0, j], i, j, t, scale) - lsev)
            return p, lax.dot_general(dob, v_ref[0, j], (_DIMS["nt"], ((), ())), preferred_element_type=F32)

        def sum_step(j, delta):
            p, dp = p_dp(j)
            return delta + jnp.sum(p * dp, axis=-1, keepdims=True)

        delta = lax.fori_loop(0, i + 1, sum_step, jnp.zeros((t, 1), F32))

        def step(j, dq):
            p, dp = p_dp(j)
            return dq + jnp.dot((p * (dp - delta)).astype(BF16), k_ref[0, j], preferred_element_type=F32)

        dq_ref[0] = lax.fori_loop(0, i + 1, step, jnp.zeros((t, Dh), F32)) * scale
        dl_ref[0] = delta

    blk = BS((1, t, Dh), lambda h, i: (h, i, 0))
    col = BS((1, t, 1), lambda h, i: (h, i, 0))
    res = BS((1, n, t, Dh), lambda h, i: (h, 0, 0, 0))
    return pl.pallas_call(
        body, name=name, grid=(H, n), out_shape=(SDS((H, S, Dh), F32), SDS((H, S, 1), F32)),
        in_specs=[blk, res, res, col, BS((1, n, 1, t), lambda h, i: (h, 0, 0, 0)), blk, col],
        out_specs=(blk, col), compiler_params=_cp("parallel", "arbitrary"),
    )(q, k.reshape(H, n, t, Dh), v.reshape(H, n, t, Dh), fq, fk, do, lse)


def _fox_bwd_dkv(q, k, v, fq, fk, do, lse, delta, name):
    H, S, Dh = q.shape
    t = _tile(S, 256)
    n = S // t
    scale = Dh ** -0.5
    tn_dims = (_DIMS["tn"], ((), ()))

    def body(q_ref, k_ref, v_ref, fq_ref, fk_ref, do_ref, lse_ref, dl_ref, dk_ref, dv_ref, dfk_ref):
        j = pl.program_id(1)
        kv, vv, fkv = k_ref[0], v_ref[0], fk_ref[0, 0]

        def step(i, carry):
            dk, dv, dfk = carry
            qv, dob = q_ref[0, i], do_ref[0, i]
            p = jnp.exp(_fox_scores(qv, kv, fq_ref[0, i], fkv, i, j, t, scale) - lse_ref[0, i])
            dv = dv + lax.dot_general(p.astype(BF16), dob, tn_dims, preferred_element_type=F32)
            dp = lax.dot_general(dob, vv, (_DIMS["nt"], ((), ())), preferred_element_type=F32)
            ds = p * (dp - dl_ref[0, i])
            dk = dk + lax.dot_general(ds.astype(BF16), qv, tn_dims, preferred_element_type=F32)
            return dk, dv, dfk - jnp.sum(ds, axis=0, keepdims=True)

        dk, dv, dfk = lax.fori_loop(j, n, step, (jnp.zeros((t, Dh), F32), jnp.zeros((t, Dh), F32),
                                                 jnp.zeros((1, t), F32)))
        dk_ref[0] = dk * scale
        dv_ref[0] = dv
        dfk_ref[0, 0] = dfk

    blk = BS((1, t, Dh), lambda h, j: (h, j, 0))
    res = BS((1, n, t, Dh), lambda h, j: (h, 0, 0, 0))
    rcol = BS((1, n, t, 1), lambda h, j: (h, 0, 0, 0))
    frow = BS((1, 1, 1, t), lambda h, j: (h, j, 0, 0))
    return pl.pallas_call(
        body, name=name, grid=(H, n),
        out_shape=(SDS((H, S, Dh), F32), SDS((H, S, Dh), F32), SDS((H, n, 1, t), F32)),
        in_specs=[res, blk, blk, rcol, frow, res, rcol, rcol], out_specs=(blk, blk, frow),
        compiler_params=_cp("parallel", "arbitrary"),
    )(q.reshape(H, n, t, Dh), k, v, fq.reshape(H, n, t, 1), fk, do.astype(BF16).reshape(H, n, t, Dh),
      lse.reshape(H, n, t, 1), delta.reshape(H, n, t, 1))


def _sconv_fwd(proj, w, col0, width, name):
    S = proj.shape[0]
    tc = LANES
    nb, c0 = width // tc, col0 // tc

    def body(x_ref, gb_ref, gc_ref, w_ref, o_ref):
        o_ref[...] = (gb_ref[...] * _conv3(gc_ref[...] * x_ref[...], w_ref)).astype(BF16)

    return pl.pallas_call(
        body, name=name, grid=(nb,), out_shape=SDS((S, width), BF16),
        in_specs=[BS((S, tc), lambda j: (0, c0 + j)), BS((S, tc), lambda j: (0, c0 + nb + j)),
                  BS((S, tc), lambda j: (0, c0 + 2 * nb + j)), BS((3, tc), lambda j: (0, j))],
        out_specs=BS((S, tc), lambda j: (0, j)), compiler_params=_cp("parallel"))(proj, proj, proj, w)


def _sconv_bwd(proj, w, col0, width, do, name):
    S = proj.shape[0]
    tc = LANES
    nb, c0 = width // tc, col0 // tc

    def body(x_ref, gb_ref, gc_ref, w_ref, do_ref, dx_ref, dgb_ref, dgc_ref, dw_ref):
        xb, gb, gc, dov = x_ref[...], gb_ref[...], gc_ref[...], do_ref[...]
        z = gc * xb
        dgb_ref[...] = (dov * _conv3(z, w_ref)).astype(BF16)
        dz = _conv3_bwd(z, dov * gb, w_ref, dw_ref)
        dgc_ref[...] = (dz * xb).astype(BF16)
        dx_ref[...] = (dz * gc).astype(BF16)

    out = BS((S, tc), lambda j: (0, j))
    wspec = BS((3, tc), lambda j: (0, j))
    return pl.pallas_call(
        body, name=name, grid=(nb,),
        out_shape=(SDS((S, width), BF16), SDS((S, width), BF16), SDS((S, width), BF16), SDS((3, width), F32)),
        in_specs=[BS((S, tc), lambda j: (0, c0 + j)), BS((S, tc), lambda j: (0, c0 + nb + j)),
                  BS((S, tc), lambda j: (0, c0 + 2 * nb + j)), wspec, out],
        out_specs=(out, out, out, wspec), compiler_params=_cp("parallel"))(proj, proj, proj, w, do)


def _gelu(x):
    return 0.5 * x * (1.0 + jnp.tanh(0.7978845608028654 * (x + 0.044715 * x * x * x)))


def _gelu_grad(x):
    u = 0.7978845608028654 * (x + 0.044715 * x * x * x)
    th = jnp.tanh(u)
    return 0.5 * (1.0 + th) + 0.5 * x * (1.0 - th * th) * 0.7978845608028654 * (1.0 + 3.0 * 0.044715 * x * x)


def _tril_w(w_ref, g):
    r = lax.broadcasted_iota(jnp.int32, (SGU_CHUNK, SGU_CHUNK), 0)
    c = lax.broadcasted_iota(jnp.int32, (SGU_CHUNK, SGU_CHUNK), 1)
    return jnp.where(c <= r, w_ref[g], 0.0), c <= r


def _sgu_mixed(vn, w_ref, b_ref, nch, G):
    rows = []
    for ch in range(nch):
        cols = []
        for g in range(G):
            wt, _ = _tril_w(w_ref, g)
            blk = vn[ch * SGU_CHUNK:(ch + 1) * SGU_CHUNK, g * LANES:(g + 1) * LANES]
            cols.append(jnp.dot(wt.astype(BF16), blk.astype(BF16), preferred_element_type=F32) + b_ref[g])
        rows.append(jnp.concatenate(cols, axis=1))
    return jnp.concatenate(rows, axis=0)


def _sgu_fwd(proj, gn, w, b, col0, width, name):
    S = proj.shape[0]
    G = w.shape[0]
    ts = _tile(S, 512)
    nch = ts // SGU_CHUNK
    c0 = col0 // width

    def body(u_ref, v_ref, gn_ref, w_ref, b_ref, o_ref):
        cv = _gelu(v_ref[...])
        rstd = lax.rsqrt(jnp.mean(cv * cv, axis=-1, keepdims=True) + EPS)
        mixed = _sgu_mixed(cv * rstd * gn_ref[...], w_ref, b_ref, nch, G)
        o_ref[...] = (_gelu(u_ref[...]) * mixed).astype(BF16)

    return pl.pallas_call(
        body, name=name, grid=(S // ts,), out_shape=SDS((S, width), BF16),
        in_specs=[BS((ts, width), lambda i: (i, c0)), BS((ts, width), lambda i: (i, c0 + 1)),
                  BS((1, width), lambda i: (0, 0)), BS(w.shape, lambda i: (0, 0, 0)), BS(b.shape, lambda i: (0, 0, 0))],
        out_specs=BS((ts, width), lambda i: (i, 0)), compiler_params=_cp("parallel"))(proj, proj, gn, w, b)


def _sgu_bwd(proj, gn, w, b, col0, width, do, name):
    S = proj.shape[0]
    G = w.shape[0]
    ts = _tile(S, 512)
    nch = ts // SGU_CHUNK
    c0 = col0 // width

    def body(u_ref, v_ref, gn_ref, w_ref, b_ref, do_ref, du_ref, dv_ref, dw_ref, db_ref, dgn_ref):
        uin, vin, dov, gnv = u_ref[...], v_ref[...], do_ref[...], gn_ref[...]
        cu, cv = _gelu(uin), _gelu(vin)
        rstd = lax.rsqrt(jnp.mean(cv * cv, axis=-1, keepdims=True) + EPS)
        vhat = cv * rstd
        vn = vhat * gnv
        mixed = _sgu_mixed(vn, w_ref, b_ref, nch, G)
        du_ref[...] = (dov * mixed * _gelu_grad(uin)).astype(BF16)
        dmix = dov * cu

        @pl.when(pl.program_id(0) == 0)
        def _():
            dw_ref[...] = jnp.zeros_like(dw_ref)
            db_ref[...] = jnp.zeros_like(db_ref)
            dgn_ref[...] = jnp.zeros_like(dgn_ref)

        rows = []
        for ch in range(nch):
            cols = []
            for g in range(G):
                wt, mask = _tril_w(w_ref, g)
                sl = (slice(ch * SGU_CHUNK, (ch + 1) * SGU_CHUNK), slice(g * LANES, (g + 1) * LANES))
                dm, vb = dmix[sl], vn[sl].astype(BF16)
                db_ref[g] += jnp.sum(dm, axis=-1, keepdims=True)
                dwg = lax.dot_general(dm.astype(BF16), vb, (_DIMS["nt"], ((), ())), preferred_element_type=F32)
                dw_ref[g] += jnp.where(mask, dwg, 0.0)
                cols.append(lax.dot_general(wt.astype(BF16), dm.astype(BF16), (_DIMS["tn"], ((), ())),
                                            preferred_element_type=F32))
            rows.append(jnp.concatenate(cols, axis=1))
        dvn = jnp.concatenate(rows, axis=0)
        dgn_ref[...] += jnp.sum(dvn * vhat, axis=0, keepdims=True)
        dvh = dvn * gnv
        dcv = rstd * (dvh - vhat * jnp.mean(dvh * vhat, axis=-1, keepdims=True))
        dv_ref[...] = (dcv * _gelu_grad(vin)).astype(BF16)

    row = BS((ts, width), lambda i: (i, 0))
    wsp, bsp, gsp = BS(w.shape, lambda i: (0, 0, 0)), BS(b.shape, lambda i: (0, 0, 0)), BS((1, width), lambda i: (0, 0))
    return pl.pallas_call(
        body, name=name, grid=(S // ts,),
        out_shape=(SDS((S, width), BF16), SDS((S, width), BF16), SDS(w.shape, F32), SDS(b.shape, F32),
                   SDS((1, width), F32)),
        in_specs=[BS((ts, width), lambda i: (i, c0)), BS((ts, width), lambda i: (i, c0 + 1)), gsp, wsp, bsp, row],
        out_specs=(row, row, wsp, bsp, gsp), compiler_params=_cp("arbitrary"))(proj, proj, gn, w, b, do)


def _rope_tables(positions, inv, name):
    S = positions.shape[0]
    ts = _tile(S, 512, 8)
    half = ROPE_DIM // 2

    def body(p_ref, inv_ref, c_ref, sa_ref, sb_ref):
        ang = p_ref[...].astype(F32) * inv_ref[...]
        lane = lax.broadcasted_iota(jnp.int32, (ts, LANES), 1) % HEAD_DIM
        sn = jnp.sin(ang)
        c_ref[...] = jnp.cos(ang)
        sa_ref[...] = jnp.where(lane < half, -sn, 0.0)
        sb_ref[...] = jnp.where((lane >= half) & (lane < ROPE_DIM), sn, 0.0)

    out = BS((ts, LANES), lambda i: (i, 0))
    return pl.pallas_call(
        body, name=name, grid=(S // ts,), out_shape=(SDS((S, LANES), F32),) * 3,
        in_specs=[BS((ts, 1), lambda i: (i, 0)), BS((1, LANES), lambda i: (0, 0))], out_specs=(out, out, out),
        compiler_params=_cp("parallel"))(positions, inv)


def _rope_apply(x, col0, width, tables, sign, out_dtype, name):
    S = x.shape[0]
    ts = _tile(S, 512, 8)
    tc = 2 * LANES
    nb, c0 = width // tc, col0 // tc
    half = ROPE_DIM // 2

    def body(x_ref, c_ref, sa_ref, sb_ref, o_ref):
        xv = x_ref[...].astype(F32)
        wide = lambda r: jnp.concatenate([r[...], r[...]], axis=1)
        y = (xv * wide(c_ref) + pltpu.roll(xv, tc - half, 1) * (sign * wide(sa_ref))
             + pltpu.roll(xv, half, 1) * (sign * wide(sb_ref)))
        o_ref[...] = y.astype(o_ref.dtype)

    tab = BS((ts, LANES), lambda i, j: (i, 0))
    return pl.pallas_call(
        body, name=name, grid=(S // ts, nb), out_shape=SDS((S, width), out_dtype),
        in_specs=[BS((ts, tc), lambda i, j: (i, c0 + j)), tab, tab, tab], out_specs=BS((ts, tc), lambda i, j: (i, j)),
        compiler_params=_cp("parallel", "parallel"))(x, *tables)


def _dil_masks(hh, m, bpcs, hpg):
    g = hh // hpg
    bpc = jnp.where(g == 0, bpcs[0], jnp.where(g == 1, bpcs[1], bpcs[2]))
    r = lax.broadcasted_iota(jnp.int32, (DIL_SPAN, DIL_SPAN), 0)
    c = lax.broadcasted_iota(jnp.int32, (DIL_SPAN, DIL_SPAN), 1)
    return (c >= r) & (lax.rem(m, bpc) != 0), c <= r


def _nt(a, b):
    return lax.dot_general(a, b, (_DIMS["nt"], ((), ())), preferred_element_type=F32)


def _tn(a, b):
    return lax.dot_general(a, b, (_DIMS["tn"], ((), ())), preferred_element_type=F32)


def _dil_fwd(q, k, v, bpcs, hpg, name):
    HH, S, Dh = q.shape
    nb = S // DIL_SPAN
    scale = Dh ** -0.5

    def body(q_ref, kp_ref, kc_ref, vp_ref, vc_ref, o_ref, lse_ref):
        mp, mc = _dil_masks(pl.program_id(0), pl.program_id(1), bpcs, hpg)
        qv = q_ref[0]
        sp = jnp.where(mp, _nt(qv, kp_ref[0]) * scale, NEG)
        sc = jnp.where(mc, _nt(qv, kc_ref[0]) * scale, NEG)
        mx = jnp.maximum(jnp.max(sp, axis=-1, keepdims=True), jnp.max(sc, axis=-1, keepdims=True))
        pp, pc = jnp.exp(sp - mx), jnp.exp(sc - mx)
        l = jnp.sum(pp, axis=-1, keepdims=True) + jnp.sum(pc, axis=-1, keepdims=True)
        acc = (jnp.dot(pp.astype(BF16), vp_ref[0], preferred_element_type=F32)
               + jnp.dot(pc.astype(BF16), vc_ref[0], preferred_element_type=F32))
        o_ref[0] = acc / l
        lse_ref[0] = mx + jnp.log(l)

    cur = BS((1, DIL_SPAN, Dh), lambda h, m: (h, m, 0))
    prev = BS((1, DIL_SPAN, Dh), lambda h, m: (h, jnp.maximum(m - 1, 0), 0))
    return pl.pallas_call(
        body, name=name, grid=(HH, nb), out_shape=(SDS((HH, S, Dh), F32), SDS((HH, S, 1), F32)),
        in_specs=[cur, prev, cur, prev, cur], out_specs=(cur, BS((1, DIL_SPAN, 1), lambda h, m: (h, m, 0))),
        compiler_params=_cp("parallel", "parallel"))(q, k, k, v, v)


def _dil_bwd_dq(q, k, v, do, lse, coef, bpcs, hpg, name):
    HH, S, Dh = q.shape
    nb = S // DIL_SPAN
    scale = Dh ** -0.5

    def body(q_ref, kp_ref, kc_ref, vp_ref, vc_ref, do_ref, lse_ref, cf_ref, dq_ref):
        mp, mc = _dil_masks(pl.program_id(0), pl.program_id(1), bpcs, hpg)
        qv, dob, lsev, cf = q_ref[0], do_ref[0], lse_ref[0], cf_ref[0]
        pp = jnp.exp(jnp.where(mp, _nt(qv, kp_ref[0]) * scale, NEG) - lsev)
        pc = jnp.exp(jnp.where(mc, _nt(qv, kc_ref[0]) * scale, NEG) - lsev)
        dsp = pp * (_nt(dob, vp_ref[0]) + cf)
        dsc = pc * (_nt(dob, vc_ref[0]) + cf)
        dq_ref[0] = (jnp.dot(dsp.astype(BF16), kp_ref[0], preferred_element_type=F32)
                     + jnp.dot(dsc.astype(BF16), kc_ref[0], preferred_element_type=F32)) * scale

    cur = BS((1, DIL_SPAN, Dh), lambda h, m: (h, m, 0))
    prev = BS((1, DIL_SPAN, Dh), lambda h, m: (h, jnp.maximum(m - 1, 0), 0))
    col = BS((1, DIL_SPAN, 1), lambda h, m: (h, m, 0))
    return pl.pallas_call(
        body, name=name, grid=(HH, nb), out_shape=SDS((HH, S, Dh), F32),
        in_specs=[cur, prev, cur, prev, cur, cur, col, col], out_specs=cur,
        compiler_params=_cp("parallel", "parallel"))(q, k, k, v, v, do, lse, coef)


def _dil_bwd_dkv(q, k, v, do, lse, coef, bpcs, hpg, name):
    HH, S, Dh = q.shape
    nb = S // DIL_SPAN
    scale = Dh ** -0.5

    def body(k_ref, v_ref, qc_ref, qn_ref, doc_ref, don_ref, lc_ref, ln_ref, cc_ref, cn_ref, dk_ref, dv_ref):
        hh, m = pl.program_id(0), pl.program_id(1)
        _, mc = _dil_masks(hh, m, bpcs, hpg)
        mp, _ = _dil_masks(hh, m + 1, bpcs, hpg)
        mp = mp & (m + 1 < nb)
        kv, vv = k_ref[0], v_ref[0]
        pc = jnp.exp(jnp.where(mc, _nt(qc_ref[0], kv) * scale, NEG) - lc_ref[0])
        pn = jnp.exp(jnp.where(mp, _nt(qn_ref[0], kv) * scale, NEG) - ln_ref[0])
        dsc = pc * (_nt(doc_ref[0], vv) + cc_ref[0])
        dsn = pn * (_nt(don_ref[0], vv) + cn_ref[0])
        dv_ref[0] = _tn(pc.astype(BF16), doc_ref[0]) + _tn(pn.astype(BF16), don_ref[0])
        dk_ref[0] = (_tn(dsc.astype(BF16), qc_ref[0]) + _tn(dsn.astype(BF16), qn_ref[0])) * scale

    cur = BS((1, DIL_SPAN, Dh), lambda h, m: (h, m, 0))
    nxt = BS((1, DIL_SPAN, Dh), lambda h, m: (h, jnp.minimum(m + 1, nb - 1), 0))
    col = BS((1, DIL_SPAN, 1), lambda h, m: (h, m, 0))
    ncol = BS((1, DIL_SPAN, 1), lambda h, m: (h, jnp.minimum(m + 1, nb - 1), 0))
    return pl.pallas_call(
        body, name=name, grid=(HH, nb), out_shape=(SDS((HH, S, Dh), F32), SDS((HH, S, Dh), F32)),
        in_specs=[cur, cur, cur, nxt, cur, nxt, col, ncol, col, ncol], out_specs=(cur, cur),
        compiler_params=_cp("parallel", "parallel"))(k, v, q, q, do, do, lse, lse, coef, coef)


def _dil_merge(o3, lse3, name):
    G, H, S, Dh = o3.shape
    ts = _tile(S, 512, 8)

    def body(o_ref, l_ref, out_ref):
        lv = [l_ref[g, 0] for g in range(G)]
        mx = functools.reduce(jnp.maximum, lv)
        ex = [jnp.exp(v - mx) for v in lv]
        den = functools.reduce(lambda a, b: a + b, ex)
        out_ref[0] = functools.reduce(lambda a, b: a + b, [(ex[g] / den) * o_ref[g, 0] for g in range(G)])

    return pl.pallas_call(
        body, name=name, grid=(H, S // ts), out_shape=SDS((H, S, Dh), F32),
        in_specs=[BS((G, 1, ts, Dh), lambda h, i: (0, h, i, 0)), BS((G, 1, ts, 1), lambda h, i: (0, h, i, 0))],
        out_specs=BS((1, ts, Dh), lambda h, i: (h, i, 0)), compiler_params=_cp("parallel", "parallel"))(o3, lse3)


def _dil_merge_bwd(o3, lse3, do, name):
    G, H, S, Dh = o3.shape
    ts = _tile(S, 512, 8)

    def body(o_ref, l_ref, do_ref, do3_ref, cf_ref):
        lv = [l_ref[g, 0] for g in range(G)]
        mx = functools.reduce(jnp.maximum, lv)
        ex = [jnp.exp(v - mx) for v in lv]
        den = functools.reduce(lambda a, b: a + b, ex)
        wt = [e / den for e in ex]
        dov = do_ref[0]
        dw = [jnp.sum(dov * o_ref[g, 0], axis=-1, keepdims=True) for g in range(G)]
        mean = functools.reduce(lambda a, b: a + b, [wt[g] * dw[g] for g in range(G)])
        for g in range(G):
            do3_ref[g, 0] = (wt[g] * dov).astype(BF16)
            cf_ref[g, 0] = wt[g] * (dw[g] - mean) - wt[g] * dw[g]

    o_spec = BS((G, 1, ts, Dh), lambda h, i: (0, h, i, 0))
    l_spec = BS((G, 1, ts, 1), lambda h, i: (0, h, i, 0))
    return pl.pallas_call(
        body, name=name, grid=(H, S // ts), out_shape=(SDS((G, H, S, Dh), BF16), SDS((G, H, S, 1), F32)),
        in_specs=[o_spec, l_spec, BS((1, ts, Dh), lambda h, i: (h, i, 0))], out_specs=(o_spec, l_spec),
        compiler_params=_cp("parallel", "parallel"))(o3, lse3, do)


def _merge_fwd(proj, gate_col, os_, ws, name):
    S = proj.shape[0]
    D = ws[0].shape[1]
    nbr = len(os_)
    ts = _tile(S, 512, 8)
    tn = LANES * 2 if gate_col % (2 * LANES) == 0 and D % (2 * LANES) == 0 else LANES
    g0, gstep = gate_col // tn, D // tn

    def body(*refs):
        p_refs, o_refs, w_refs, out_ref = refs[:nbr], refs[nbr:2 * nbr], refs[2 * nbr:3 * nbr], refs[-1]
        acc = None
        for i in range(nbr):
            t = _sigmoid(p_refs[i][...]) * jnp.dot(o_refs[i][...], w_refs[i][...], preferred_element_type=F32)
            acc = t if acc is None else acc + t
        out_ref[...] = acc.astype(BF16)

    specs = ([BS((ts, tn), lambda s, j, i=i: (s, g0 + i * gstep + j)) for i in range(nbr)]
             + [BS((ts, o.shape[1]), lambda s, j: (s, 0)) for o in os_]
             + [BS((w.shape[0], tn), lambda s, j: (0, j)) for w in ws])
    return pl.pallas_call(
        body, name=name, grid=(S // ts, D // tn), out_shape=SDS((S, D), BF16), in_specs=specs,
        out_specs=BS((ts, tn), lambda s, j: (s, j)), compiler_params=_cp("parallel", "parallel"),
    )(*([proj] * nbr), *os_, *ws)


def _merge_bwd(proj, gate_col, os_, ws, dm, name):
    S = proj.shape[0]
    D = ws[0].shape[1]
    nbr = len(os_)
    ts = _tile(S, 512, 8)
    tn = LANES * 2 if gate_col % (2 * LANES) == 0 and D % (2 * LANES) == 0 else LANES
    g0, gstep = gate_col // tn, D // tn

    def body(*refs):
        p_refs, o_refs, w_refs = refs[:nbr], refs[nbr:2 * nbr], refs[2 * nbr:3 * nbr]
        dmv = refs[3 * nbr][...]
        dbo_refs, dgl_refs = refs[3 * nbr + 1:4 * nbr + 1], refs[4 * nbr + 1:]
        for i in range(nbr):
            gt = _sigmoid(p_refs[i][...])
            bo = jnp.dot(o_refs[i][...], w_refs[i][...], preferred_element_type=F32)
            dbo_refs[i][...] = (dmv * gt).astype(BF16)
            dgl_refs[i][...] = (dmv * bo * gt * (1.0 - gt)).astype(BF16)

    tile = BS((ts, tn), lambda s, j: (s, j))
    specs = ([BS((ts, tn), lambda s, j, i=i: (s, g0 + i * gstep + j)) for i in range(nbr)]
             + [BS((ts, o.shape[1]), lambda s, j: (s, 0)) for o in os_]
             + [BS((w.shape[0], tn), lambda s, j: (0, j)) for w in ws] + [tile])
    outs = pl.pallas_call(
        body, name=name, grid=(S // ts, D // tn), out_shape=(SDS((S, D), BF16),) * (2 * nbr), in_specs=specs,
        out_specs=(tile,) * (2 * nbr), compiler_params=_cp("parallel", "parallel"),
    )(*([proj] * nbr), *os_, *ws, dm)
    return outs[:nbr], outs[nbr:]


def _ffn_act(up, w, name):
    S, F2 = up.shape
    F = F2 // 2
    tc = LANES
    nb = F // tc

    def body(g_ref, v_ref, wg_ref, wv_ref, a_ref):
        ug = _conv3(g_ref[...], wg_ref)
        a_ref[...] = (ug * _sigmoid(ug) * _conv3(v_ref[...], wv_ref)).astype(BF16)

    return pl.pallas_call(
        body, name=name, grid=(nb,), out_shape=SDS((S, F), BF16),
        in_specs=[BS((S, tc), lambda j: (0, j)), BS((S, tc), lambda j: (0, nb + j)),
                  BS((3, tc), lambda j: (0, j)), BS((3, tc), lambda j: (0, nb + j))],
        out_specs=BS((S, tc), lambda j: (0, j)), compiler_params=_cp("parallel"))(up, up, w, w)


def _ffn_act_bwd(up, w, da, name):
    S, F2 = up.shape
    F = F2 // 2
    tc = LANES
    nb = F // tc

    def body(g_ref, v_ref, wg_ref, wv_ref, da_ref, dg_ref, dv_ref, dwg_ref, dwv_ref):
        pg, pv, dav = g_ref[...], v_ref[...], da_ref[...].astype(F32)
        ug, uv = _conv3(pg, wg_ref), _conv3(pv, wv_ref)
        sg = _sigmoid(ug)
        dg_ref[...] = _conv3_bwd(pg, dav * uv * (sg * (1.0 + ug * (1.0 - sg))), wg_ref, dwg_ref).astype(BF16)
        dv_ref[...] = _conv3_bwd(pv, dav * ug * sg, wv_ref, dwv_ref).astype(BF16)

    col = BS((S, tc), lambda j: (0, j))
    wsp = BS((3, tc), lambda j: (0, j))
    return pl.pallas_call(
        body, name=name, grid=(nb,),
        out_shape=(SDS((S, F), BF16), SDS((S, F), BF16), SDS((3, F), F32), SDS((3, F), F32)),
        in_specs=[col, BS((S, tc), lambda j: (0, nb + j)), wsp, BS((3, tc), lambda j: (0, nb + j)), col],
        out_specs=(col, col, wsp, wsp), compiler_params=_cp("parallel"))(up, up, w, w, da)


def _ple_fwd(x, pgl, pe, name):
    S, D = x.shape
    ts = _tile(S, 256, 8)

    def body(x_ref, g_ref, e_ref, o_ref):
        o_ref[...] = x_ref[...] + _sigmoid(g_ref[...]) * e_ref[...]

    row = BS((ts, D), lambda i: (i, 0))
    return pl.pallas_call(body, name=name, grid=(S // ts,), out_shape=SDS((S, D), F32), in_specs=[row] * 3,
                          out_specs=row, compiler_params=_cp("parallel"))(x, pgl, pe)


def _ple_bwd(dx, pgl, pe, name):
    S, D = dx.shape
    ts = _tile(S, 256, 8)

    def body(dx_ref, g_ref, e_ref, dg_ref, de_ref):
        dxv, sg = dx_ref[...], _sigmoid(g_ref[...])
        dg_ref[...] = (dxv * e_ref[...] * sg * (1.0 - sg)).astype(BF16)
        de_ref[...] = (dxv * sg).astype(BF16)

    row = BS((ts, D), lambda i: (i, 0))
    return pl.pallas_call(body, name=name, grid=(S // ts,), out_shape=(SDS((S, D), BF16),) * 2, in_specs=[row] * 3,
                          out_specs=(row, row), compiler_params=_cp("parallel"))(dx, pgl, pe)


def _adamw(w, g, m, v, name):
    shape = w.shape
    cols = shape[-1]
    rows = w.size // cols
    tr = _tile(rows, 256, 8)
    c1, c2 = 1.0 / (1.0 - ADAM_B1 ** ADAM_STEP), 1.0 / (1.0 - ADAM_B2 ** ADAM_STEP)

    def body(w_ref, g_ref, m_ref, v_ref, d_ref, nm_ref, nv_ref):
        gv = g_ref[...]
        nm = ADAM_B1 * m_ref[...] + (1.0 - ADAM_B1) * gv
        nv = ADAM_B2 * v_ref[...] + (1.0 - ADAM_B2) * (gv * gv)
        d_ref[...] = -ADAM_LR * ((nm * c1) / (jnp.sqrt(nv * c2) + ADAM_EPS) + ADAM_WD * w_ref[...])
        nm_ref[...] = nm
        nv_ref[...] = nv

    blk = BS((tr, cols), lambda i: (i, 0))
    outs = pl.pallas_call(
        body, name=name, grid=(rows // tr,), out_shape=(SDS((rows, cols), F32),) * 3, in_specs=[blk] * 4,
        out_specs=(blk,) * 3, compiler_params=_cp("parallel"),
    )(*(a.reshape(rows, cols) for a in (w, g, m, v)))
    return tuple(o.reshape(shape) for o in outs)


SHARDED = ("w_in", "shortconv_w", "w_br_fox", "w_br_conv", "w_br_sgu", "w_br_dil", "w_out", "w_up", "ffn_conv_w",
           "w_down", "w_ple_gate", "w_ple_proj")
ROW_SHARDED = ("w_out", "w_down", "w_ple_gate")
REPLICATED = ("norm_mix_g", "fox_forget_b", "sgu_norm_g", "sgu_w", "sgu_b", "norm_ffn_g", "norm_ple_g")
WEIGHTS = ("norm_mix_g", "w_in", "fox_forget_b", "shortconv_w", "sgu_norm_g", "sgu_w", "sgu_b", "w_br_fox",
           "w_br_conv", "w_br_sgu", "w_br_dil", "w_out", "norm_ffn_g", "w_up", "ffn_conv_w", "w_down", "norm_ple_g",
           "w_ple_gate", "w_ple_proj", "final_norm_g")
PACK_COLS = 1024
PACK_ROWS = 256


def _pack(parts, dtype, unit):
    lead = parts[0].shape[:-1] if parts[0].ndim > 1 else ()
    flat = jnp.concatenate([p.astype(dtype) for p in parts], axis=-1)
    n = flat.shape[-1]
    rows = -(-n // (PACK_COLS * unit)) * unit
    flat = jnp.pad(flat, [(0, 0)] * len(lead) + [(0, rows * PACK_COLS - n)])
    return flat.reshape(lead + (rows, PACK_COLS))


def _to_heads(x, heads):
    S = x.shape[0]
    return x.reshape(S, heads, HEAD_DIM).transpose(1, 0, 2)


def _from_heads(x):
    H, S, Dh = x.shape
    return x.transpose(1, 0, 2).reshape(S, H * Dh)


def _to_residue(x, hpg):
    S = x.shape[0]
    gw = hpg * HEAD_DIM
    outs = []
    for g, (_, d) in enumerate(DIL_PATTERNS):
        seg = x[:, g * gw:(g + 1) * gw].reshape(S // d, d, hpg, HEAD_DIM)
        outs.append(seg.transpose(2, 1, 0, 3).reshape(hpg, S, HEAD_DIM))
    return jnp.concatenate(outs, axis=0)


def _from_residue(x, hpg):
    _, S, C = x.shape
    outs = []
    for g, (_, d) in enumerate(DIL_PATTERNS):
        seg = x[g * hpg:(g + 1) * hpg].reshape(hpg, d, S // d, C)
        outs.append(seg.transpose(0, 2, 1, 3).reshape(hpg, S, C))
    return jnp.stack(outs, axis=0)


def _groups_to_residue(x):
    G, hpg, S, C = x.shape
    outs = []
    for g, (_, d) in enumerate(DIL_PATTERNS):
        outs.append(x[g].reshape(hpg, S // d, d, C).transpose(0, 2, 1, 3).reshape(hpg, S, C))
    return jnp.concatenate(outs, axis=0)


def kernel(x, p, positions, norm_mix_g, w_in, fox_forget_b, shortconv_w, sgu_norm_g, sgu_w, sgu_b, w_br_fox, w_br_conv, w_br_sgu, w_br_dil, w_out, norm_ffn_g, w_up, ffn_conv_w, w_down, norm_ple_g, w_ple_gate, w_ple_proj, final_norm_g, loss_target, m_norm_mix_g, m_w_in, m_fox_forget_b, m_shortconv_w, m_sgu_norm_g, m_sgu_w, m_sgu_b, m_w_br_fox, m_w_br_conv, m_w_br_sgu, m_w_br_dil, m_w_out, m_norm_ffn_g, m_w_up, m_ffn_conv_w, m_w_down, m_norm_ple_g, m_w_ple_gate, m_w_ple_proj, m_final_norm_g, v_norm_mix_g, v_w_in, v_fox_forget_b, v_shortconv_w, v_sgu_norm_g, v_sgu_w, v_sgu_b, v_w_br_fox, v_w_br_conv, v_w_br_sgu, v_w_br_dil, v_w_out, v_norm_ffn_g, v_w_up, v_ffn_conv_w, v_w_down, v_norm_ple_g, v_w_ple_gate, v_w_ple_proj, v_final_norm_g):
    given = dict(locals())
    W = {n: given[n] for n in WEIGHTS}
    depth = w_in.shape[0]
    S, D = x.shape[1], x.shape[2]
    x0, target, pos = x[0], loss_target[0], positions[0].reshape(S, 1)
    FH = fox_forget_b.shape[1]
    FW = FH * HEAD_DIM
    CW = shortconv_w.shape[2] * N_DEV
    G = sgu_w.shape[1]
    SW = G * sgu_w.shape[2]
    DOUT = w_br_dil.shape[1]
    hpg = DOUT // HEAD_DIM
    NG = len(DIL_PATTERNS)
    DW = NG * DOUT
    bpcs = tuple(S // d // DIL_SPAN for _, d in DIL_PATTERNS)
    A0, B0 = 0, 3 * FW
    C0 = B0 + 3 * CW
    D0 = C0 + 2 * SW
    G0 = D0 + 3 * DW
    F0 = G0 + 4 * D
    NP = F0 + LANES
    orig = [3 * FW, FH, 3 * CW, 2 * SW, 3 * DW, 4 * D]
    o_off = [sum(orig[:i]) for i in range(len(orig) + 1)]

    def permute_cols(w):
        seg = [w[:, o_off[i]:o_off[i + 1]] for i in range(6)]
        pad = jnp.zeros((w.shape[0], LANES - FH), w.dtype)
        return jnp.concatenate([seg[0], seg[2], seg[3], seg[4], seg[5], seg[1], pad], axis=1)

    def unpermute_cols(w):
        return jnp.concatenate([w[:, A0:B0], w[:, F0:F0 + FH], w[:, B0:G0], w[:, G0:F0]], axis=1)

    shard_shapes = {n: W[n].shape[1:] for n in SHARDED}
    sizes = [math.prod(shard_shapes[n]) for n in SHARDED]
    offs = [sum(sizes[:i]) for i in range(len(sizes) + 1)]

    def unpack_full(buf, i):
        out = {}
        for n, a, b in zip(SHARDED, offs[:-1], offs[1:]):
            sh = buf[:, a:b].reshape((N_DEV,) + shard_shapes[n])
            if n in ROW_SHARDED:
                out[n] = sh.reshape((N_DEV * sh.shape[1],) + sh.shape[2:])
            else:
                out[n] = jnp.moveaxis(sh, 0, -2).reshape(sh.shape[1:-1] + (N_DEV * sh.shape[-1],))
        return out

    full = []
    for i in range(depth):
        mine = _pack([W[n][i].reshape(-1) for n in SHARDED], BF16, PACK_ROWS)
        got = _exchange(mine, True, "comm_gather_weights")
        fw = unpack_full(got.reshape(N_DEV, -1), i)
        fw["w_in"] = permute_cols(fw["w_in"])
        full.append(fw)

    inv = ROPE_THETA ** (-jnp.arange(ROPE_DIM // 2, dtype=F32) * (2.0 / ROPE_DIM))
    inv_row = jnp.tile(jnp.concatenate([inv, inv, jnp.zeros((HEAD_DIM - ROPE_DIM,), F32)]), LANES // HEAD_DIM)
    tables = _rope_tables(pos, inv_row.reshape(1, LANES), "rope_tables")

    def pad_lanes(v):
        return jnp.pad(v, ((0, 0), (0, LANES - v.shape[1])))

    saved = []
    xc = x0
    for i in range(depth):
        fw, sv = full[i], {}
        sv["x"] = xc
        h1 = _rms_fwd(xc, norm_mix_g[i], "rms_mix")
        proj = _mm(h1, fw["w_in"], "nn", F32, "mm_in")
        sv["h1"], sv["proj"] = h1, proj
        bias = pad_lanes(fox_forget_b[i].reshape(1, FH))
        Fc = _fox_prep(proj, bias, F0 // LANES, "fox_prep")
        t = _tile(S, 256)
        Ft = Fc[:, :FH].T
        fq, fk = Ft.reshape(FH, S, 1), Ft.reshape(FH, S // t, 1, t)
        qa, ka, va = (_to_heads(proj[:, A0 + j * FW:A0 + (j + 1) * FW], FH).astype(BF16) for j in range(3))
        oa, lse_a = _fox_fwd(qa, ka, va, fq, fk, "fox_fwd")
        oa_b = _from_heads(oa).astype(BF16)
        sv.update(bias=bias, fq=fq, fk=fk, qa=qa, ka=ka, va=va, oa=oa, lse_a=lse_a, oa_b=oa_b)
        ob = _sconv_fwd(proj, fw["shortconv_w"], B0, CW, "sconv_fwd")
        sgb = sgu_b[i].reshape(G, SGU_CHUNK, 1)
        oc = _sgu_fwd(proj, sgu_norm_g[i].reshape(1, SW), sgu_w[i], sgb, C0, SW, "sgu_fwd")
        qd = _rope_apply(proj, D0, DW, tables, 1.0, BF16, "rope_q")
        kd = _rope_apply(proj, D0 + DW, DW, tables, 1.0, BF16, "rope_k")
        qr, kr = _to_residue(qd, hpg), _to_residue(kd, hpg)
        vr = _to_residue(proj[:, D0 + 2 * DW:D0 + 3 * DW].astype(BF16), hpg)
        od_r, lse_r = _dil_fwd(qr, kr, vr, bpcs, hpg, "dil_fwd")
        o3, lse3 = _from_residue(od_r, hpg), _from_residue(lse_r, hpg)
        od = _from_heads(_dil_merge(o3, lse3, "dil_merge")).astype(BF16)
        sv.update(ob=ob, oc=oc, sgb=sgb, qr=qr, kr=kr, vr=vr, lse_r=lse_r, o3=o3, lse3=lse3, od=od)
        brs = [oa_b, ob, oc, od]
        wbr = [fw["w_br_fox"], fw["w_br_conv"], fw["w_br_sgu"], fw["w_br_dil"]]
        merged = _merge_fwd(proj, G0, brs, wbr, "merge_fwd")
        x2 = _mm(merged, fw["w_out"], "nn", F32, "mm_out", res=xc)
        h2 = _rms_fwd(x2, norm_ffn_g[i], "rms_ffn")
        up = _mm(h2, fw["w_up"], "nn", F32, "mm_up")
        act = _ffn_act(up, fw["ffn_conv_w"], "ffn_act")
        x3 = _mm(act, fw["w_down"], "nn", F32, "mm_down", res=x2)
        h3 = _rms_fwd(x3, norm_ple_g[i], "rms_ple")
        pgl = _mm(h3, fw["w_ple_gate"], "nn", F32, "mm_ple_gate")
        pb = p[i, 0].astype(BF16)
        pe = _mm(pb, fw["w_ple_proj"], "nn", F32, "mm_ple_proj")
        xc = _ple_fwd(x3, pgl, pe, "ple_fwd")
        sv.update(merged=merged, x2=x2, h2=h2, up=up, act=act, x3=x3, h3=h3, pgl=pgl, pb=pb, pe=pe)
        saved.append(sv)

    loss_row, dx, dxb, dg_final = _final_loss(xc, final_norm_g, target, "final_loss")
    loss = lax.psum(loss_row[0, 0], MESH_AXES)

    grads_sh = [None] * depth
    grads_rep = [None] * depth
    for i in reversed(range(depth)):
        fw, sv = full[i], saved[i]
        gs, gr = {}, {}
        proj = sv["proj"]
        dpgl, dpe = _ple_bwd(dx, sv["pgl"], sv["pe"], "ple_bwd")
        gs["w_ple_proj"] = _mm(sv["pb"], dpe, "tn", F32, "mm_dw_ple_proj")
        gs["w_ple_gate"] = _mm(sv["h3"], dpgl, "tn", F32, "mm_dw_ple_gate")
        dh3 = _mm(dpgl, fw["w_ple_gate"], "nt", F32, "mm_dh3")
        dx, dxb, gr["norm_ple_g"] = _rms_bwd(sv["x3"], norm_ple_g[i], dh3, dx, "rms_ple_bwd")
        gs["w_down"] = _mm(sv["act"], dxb, "tn", F32, "mm_dw_down")
        dact = _mm(dxb, fw["w_down"], "nt", BF16, "mm_dact")
        dug, duv, dwg, dwv = _ffn_act_bwd(sv["up"], fw["ffn_conv_w"], dact, "ffn_act_bwd")
        gs["ffn_conv_w"] = jnp.concatenate([dwg, dwv], axis=1)
        dup = jnp.concatenate([dug, duv], axis=1)
        gs["w_up"] = _mm(sv["h2"], dup, "tn", F32, "mm_dw_up")
        dh2 = _mm(dup, fw["w_up"], "nt", F32, "mm_dh2")
        dx, dxb, gr["norm_ffn_g"] = _rms_bwd(sv["x2"], norm_ffn_g[i], dh2, dx, "rms_ffn_bwd")
        gs["w_out"] = _mm(sv["merged"], dxb, "tn", F32, "mm_dw_out")
        dmerged = _mm(dxb, fw["w_out"], "nt", F32, "mm_dmerged")
        brs = [sv["oa_b"], sv["ob"], sv["oc"], sv["od"]]
        names = ["w_br_fox", "w_br_conv", "w_br_sgu", "w_br_dil"]
        wbr = [fw[n] for n in names]
        dbo, dgl = _merge_bwd(proj, G0, brs, wbr, dmerged, "merge_bwd")
        dos = []
        for j, n in enumerate(names):
            gs[n] = _mm(brs[j], dbo[j], "tn", F32, "mm_dw_" + n)
            dos.append(_mm(dbo[j], wbr[j], "nt", F32, "mm_do_" + n))
        doa = _to_heads(dos[0], FH).astype(BF16)
        dqa, delta = _fox_bwd_dq(sv["qa"], sv["ka"], sv["va"], sv["fq"], sv["fk"], doa, sv["lse_a"], "fox_bwd_dq")
        dka, dva, dfk = _fox_bwd_dkv(sv["qa"], sv["ka"], sv["va"], sv["fq"], sv["fk"], doa, sv["lse_a"], delta,
                                     "fox_bwd_dkv")
        dF = pad_lanes(dfk.reshape(FH, S).T)
        daf, dbias = _fox_prep_bwd(proj, sv["bias"], F0 // LANES, dF, "fox_prep_bwd")
        gr["fox_forget_b"] = dbias[0, :FH]
        d_a = [_from_heads(t).astype(BF16) for t in (dqa, dka, dva)]
        dxb_, dgb, dgc, gs["shortconv_w"] = _sconv_bwd(proj, fw["shortconv_w"], B0, CW, dos[1], "sconv_bwd")
        du, dv, dsw, dsb, dsg = _sgu_bwd(proj, sgu_norm_g[i].reshape(1, SW), sgu_w[i], sv["sgb"], C0, SW, dos[2],
                                         "sgu_bwd")
        gr["sgu_w"], gr["sgu_b"], gr["sgu_norm_g"] = dsw, dsb.reshape(G, SGU_CHUNK), dsg[0]
        do3, cf3 = _dil_merge_bwd(sv["o3"], sv["lse3"], _to_heads(dos[3], hpg), "dil_merge_bwd")
        do_r, cf_r = _groups_to_residue(do3), _groups_to_residue(cf3)
        dq_r = _dil_bwd_dq(sv["qr"], sv["kr"], sv["vr"], do_r, sv["lse_r"], cf_r, bpcs, hpg, "dil_bwd_dq")
        dk_r, dv_r = _dil_bwd_dkv(sv["qr"], sv["kr"], sv["vr"], do_r, sv["lse_r"], cf_r, bpcs, hpg, "dil_bwd_dkv")

        def natural(t):
            return _from_heads(_from_residue(t, hpg).reshape(NG * hpg, S, HEAD_DIM))

        dqd = _rope_apply(natural(dq_r), 0, DW, tables, -1.0, BF16, "rope_q_bwd")
        dkd = _rope_apply(natural(dk_r), 0, DW, tables, -1.0, BF16, "rope_k_bwd")
        dvd = natural(dv_r).astype(BF16)
        dproj = jnp.concatenate(d_a + [dxb_, dgb, dgc, du, dv, dqd, dkd, dvd] + list(dgl) + [daf], axis=1)
        gs["w_in"] = unpermute_cols(_mm(sv["h1"], dproj, "tn", F32, "mm_dw_in"))
        dh1 = _mm(dproj, fw["w_in"], "nt", F32, "mm_dh1")
        dx, dxb, gr["norm_mix_g"] = _rms_bwd(sv["x"], norm_mix_g[i], dh1, dx, "rms_mix_bwd")
        gr["norm_mix_g"], gr["norm_ffn_g"], gr["norm_ple_g"] = (gr[n][0] for n in
                                                               ("norm_mix_g", "norm_ffn_g", "norm_ple_g"))
        grads_sh[i], grads_rep[i] = gs, gr
    grad_x = dx.reshape(1, S, D)

    def by_dest(n, gfull):
        if n in ROW_SHARDED:
            return gfull.reshape(N_DEV, -1)
        sh = shard_shapes[n]
        return jnp.moveaxis(gfull.reshape(gfull.shape[:-1] + (N_DEV, sh[-1])), -2, 0).reshape(N_DEV, -1)

    summed = []
    for i in range(depth):
        part = _pack([by_dest(n, grads_sh[i][n]) for n in SHARDED], BF16, PACK_ROWS)
        got = _exchange(part, False, "comm_scatter_grads")
        summed.append(_sum8(got, "sum_grads").reshape(-1))
    summed = jnp.stack(summed, axis=0)
    grads = {n: summed[:, a:b].reshape((depth,) + shard_shapes[n]) for n, a, b in zip(SHARDED, offs[:-1], offs[1:])}

    rep_parts = [jnp.stack([grads_rep[i][n] for i in range(depth)], axis=0).reshape(-1) for n in REPLICATED]
    rep_parts.append(dg_final.reshape(-1))
    rep_sizes = [int(r.shape[0]) for r in rep_parts]
    rep_offs = [sum(rep_sizes[:i]) for i in range(len(rep_sizes) + 1)]
    rep = _sum8(_exchange(_pack(rep_parts, F32, 8), True, "comm_gather_small"), "sum_small").reshape(-1)
    for n, a, b in zip(REPLICATED + ("final_norm_g",), rep_offs[:-1], rep_offs[1:]):
        grads[n] = rep[a:b].reshape(W[n].shape)

    deltas, new_m, new_v = {}, {}, {}
    for n in WEIGHTS:
        deltas[n], new_m[n], new_v[n] = _adamw(W[n], grads[n], given["m_" + n], given["v_" + n], "adamw_" + n)
    return (loss, grad_x, *[grads[n] for n in WEIGHTS], *[deltas[n] for n in WEIGHTS],
            *[new_m[n] for n in WEIGHTS], *[new_v[n] for n in WEIGHTS])
```

```python
import functools

import jax
import jax.numpy as jnp
from jax import lax
from jax.experimental import pallas as pl
from jax.experimental.pallas import tpu as pltpu

F32 = jnp.float32
BF16 = jnp.bfloat16
EPS = 1e-6
NEG = -1e30
HEAD_DIM = 64
SGU_CHUNK = 128
DIL_PATTERNS = ((128, 1), (512, 4), (2048, 16))
DIL_SPAN = 128
ROPE_THETA = 500000.0
ROPE_DIM = 16
N_DEV = 8
LANES = 128
VMEM_LIMIT = 56 * 1024 * 1024
ADAM_LR, ADAM_B1, ADAM_B2, ADAM_EPS, ADAM_WD, ADAM_STEP = 0.001, 0.9, 0.999, 1e-08, 0.01, 10
MESH_AXES = ("x", "y", "c")

BS = pl.BlockSpec
SDS = jax.ShapeDtypeStruct


def _cp(*sem):
    return pltpu.CompilerParams(dimension_semantics=sem, vmem_limit_bytes=VMEM_LIMIT)


def _tile(dim, pref, unit=LANES):
    if dim % unit:
        return dim
    best, t = unit, unit
    while t <= min(dim, pref):
        if dim % t == 0:
            best = t
        t += unit
    return best


def _sigmoid(z):
    return 1.0 / (1.0 + jnp.exp(-z))


def _exchange(x, gather, name):
    shape = x.shape if not gather else (N_DEV,) + x.shape

    def body(x_ref, o_ref, send_sems, recv_sems, local_sem):
        ix, iy, ic = lax.axis_index("x"), lax.axis_index("y"), lax.axis_index("c")
        me = 4 * ix + 2 * iy + ic

        def src(j):
            return x_ref if gather else x_ref.at[j]

        local = pltpu.make_async_copy(src(me), o_ref.at[me], local_sem)
        local.start()
        sends, recvs = [], []
        for r in range(1, N_DEV):
            px = 1 - ix if (r >> 2) & 1 else ix
            py = 1 - iy if (r >> 1) & 1 else iy
            pc = 1 - ic if r & 1 else ic
            peer = 4 * px + 2 * py + pc
            sends.append(pltpu.make_async_remote_copy(
                src_ref=src(peer), dst_ref=o_ref.at[me], send_sem=send_sems.at[r - 1], recv_sem=recv_sems.at[r - 1],
                device_id=(px, py, pc), device_id_type=pl.DeviceIdType.MESH))
            recvs.append(pltpu.make_async_remote_copy(
                src_ref=src(peer), dst_ref=o_ref.at[peer], send_sem=send_sems.at[r - 1], recv_sem=recv_sems.at[r - 1],
                device_id=(px, py, pc), device_id_type=pl.DeviceIdType.MESH))
        for cp in sends:
            cp.start()
        for cp in recvs:
            cp.wait_recv()
        for cp in sends:
            cp.wait_send()
        local.wait()

    return pl.pallas_call(
        body, name=name, out_shape=SDS(shape, x.dtype),
        in_specs=[BS(memory_space=pl.ANY)], out_specs=BS(memory_space=pl.ANY),
        scratch_shapes=[pltpu.SemaphoreType.DMA((N_DEV - 1,)), pltpu.SemaphoreType.DMA((N_DEV - 1,)),
                        pltpu.SemaphoreType.DMA],
    )(x)


def _comm(name, srcs, out_shapes, src_views, dst_views):
    n = len(srcs)

    def body(*refs):
        s_refs, o_refs = refs[:n], refs[n:2 * n]
        send_sems, recv_sems, local_sems = refs[2 * n:]
        ix, iy, ic = lax.axis_index("x"), lax.axis_index("y"), lax.axis_index("c")
        me = 4 * ix + 2 * iy + ic
        local = [pltpu.make_async_copy(src_views[k](s_refs[k], me), dst_views[k](o_refs[k], me), local_sems.at[k])
                 for k in range(n)]
        for cp in local:
            cp.start()
        sends, recvs = [], []
        for r in range(1, N_DEV):
            px = 1 - ix if (r >> 2) & 1 else ix
            py = 1 - iy if (r >> 1) & 1 else iy
            pc = 1 - ic if r & 1 else ic
            peer = 4 * px + 2 * py + pc
            for k in range(n):
                common = dict(send_sem=send_sems.at[r - 1, k], recv_sem=recv_sems.at[r - 1, k],
                              device_id=(px, py, pc), device_id_type=pl.DeviceIdType.MESH)
                sends.append(pltpu.make_async_remote_copy(
                    src_ref=src_views[k](s_refs[k], peer), dst_ref=dst_views[k](o_refs[k], me), **common))
                recvs.append(pltpu.make_async_remote_copy(
                    src_ref=src_views[k](s_refs[k], peer), dst_ref=dst_views[k](o_refs[k], peer), **common))
        for cp in sends:
            cp.start()
        for cp in recvs:
            cp.wait_recv()
        for cp in sends:
            cp.wait_send()
        for cp in local:
            cp.wait()

    anyspec = BS(memory_space=pl.ANY)
    return pl.pallas_call(
        body, name=name, out_shape=tuple(out_shapes), in_specs=[anyspec] * n, out_specs=(anyspec,) * n,
        scratch_shapes=[pltpu.SemaphoreType.DMA((N_DEV - 1, n)), pltpu.SemaphoreType.DMA((N_DEV - 1, n)),
                        pltpu.SemaphoreType.DMA((n,))],
    )(*srcs)


def _assemble(pieces, starts, wb, npb, name):
    _, K, pw = pieces.shape
    tr = _tile(K, 256, 16)

    def body(p_ref, o_ref):
        for b in range(npb - 1):
            acc = None
            for j in range(N_DEV):
                if starts[j] <= b < starts[j] + wb:
                    blk = p_ref[j, :, (b - starts[j]) * LANES:(b - starts[j] + 1) * LANES]
                    acc = blk if acc is None else acc + blk
            o_ref[:, b * LANES:(b + 1) * LANES] = acc
        acc = p_ref[0, :, wb * LANES:]
        for j in range(1, N_DEV):
            acc = acc + p_ref[j, :, wb * LANES:]
        o_ref[:, (npb - 1) * LANES:] = acc

    return pl.pallas_call(
        body, name=name, grid=(K // tr,), out_shape=SDS((K, npb * LANES), pieces.dtype),
        in_specs=[BS((N_DEV, tr, pw), lambda i: (0, i, 0))], out_specs=BS((tr, npb * LANES), lambda i: (i, 0)),
        compiler_params=_cp("parallel"))(pieces)


def _sum8(x, name):
    _, R, C = x.shape
    tr = _tile(R, 256, 16)

    def body(x_ref, o_ref):
        acc = x_ref[0].astype(F32)
        for j in range(1, N_DEV):
            acc = acc + x_ref[j].astype(F32)
        o_ref[...] = acc

    return pl.pallas_call(
        body, name=name, grid=(R // tr,), out_shape=SDS((R, C), F32),
        in_specs=[BS((N_DEV, tr, C), lambda i: (0, i, 0))], out_specs=BS((tr, C), lambda i: (i, 0)),
        compiler_params=_cp("parallel"))(x)


_DIMS = {"nn": ((1,), (0,)), "nt": ((1,), (1,)), "tn": ((0,), (0,))}


def _mm(a, b, mode, out_dtype, name, res=None):
    if mode == "nn":
        (M, K), N = a.shape, b.shape[1]
    elif mode == "nt":
        (M, K), N = a.shape, b.shape[0]
    else:
        (K, M), N = a.shape, b.shape[1]
    tm, tn, tk = _tile(M, 1024), _tile(N, 1024), _tile(K, 1024)
    nk = K // tk
    a_spec = BS((tk, tm), lambda i, j, k: (k, i)) if mode == "tn" else BS((tm, tk), lambda i, j, k: (i, k))
    b_spec = BS((tn, tk), lambda i, j, k: (j, k)) if mode == "nt" else BS((tk, tn), lambda i, j, k: (k, j))
    o_spec = BS((tm, tn), lambda i, j, k: (i, j))
    dims = (_DIMS[mode], ((), ()))

    def body(a_ref, b_ref, *rest):
        o_ref, acc = rest[-2], rest[-1]
        k = pl.program_id(2)

        @pl.when(k == 0)
        def _():
            acc[...] = jnp.zeros_like(acc)

        acc[...] += lax.dot_general(a_ref[...].astype(BF16), b_ref[...].astype(BF16), dims,
                                    preferred_element_type=F32)

        @pl.when(k == nk - 1)
        def _():
            r = acc[...]
            if res is not None:
                r = r + rest[0][...]
            o_ref[...] = r.astype(o_ref.dtype)

    ins, specs = [a, b], [a_spec, b_spec]
    if res is not None:
        ins.append(res)
        specs.append(o_spec)
    return pl.pallas_call(
        body, name=name, grid=(M // tm, N // tn, nk), out_shape=SDS((M, N), out_dtype),
        in_specs=specs, out_specs=o_spec, scratch_shapes=[pltpu.VMEM((tm, tn), F32)],
        compiler_params=_cp("parallel", "parallel", "arbitrary"))(*ins)


def _rms_fwd(x, g, name):
    S, D = x.shape
    ts = _tile(S, 256, 8)

    def body(x_ref, g_ref, o_ref):
        xv = x_ref[...]
        rstd = lax.rsqrt(jnp.mean(xv * xv, axis=-1, keepdims=True) + EPS)
        o_ref[...] = (xv * rstd * g_ref[...]).astype(o_ref.dtype)

    return pl.pallas_call(
        body, name=name, grid=(S // ts,), out_shape=SDS((S, D), BF16),
        in_specs=[BS((ts, D), lambda i: (i, 0)), BS((1, D), lambda i: (0, 0))],
        out_specs=BS((ts, D), lambda i: (i, 0)), compiler_params=_cp("parallel"))(x, g.reshape(1, D))


def _rms_bwd(x, g, dh, dres, name):
    S, D = x.shape
    ts = _tile(S, 256, 8)

    def body(x_ref, g_ref, dh_ref, dres_ref, dx_ref, dxb_ref, dg_ref):
        xv = x_ref[...]
        rstd = lax.rsqrt(jnp.mean(xv * xv, axis=-1, keepdims=True) + EPS)
        xn = xv * rstd
        dhv = dh_ref[...]

        @pl.when(pl.program_id(0) == 0)
        def _():
            dg_ref[...] = jnp.zeros_like(dg_ref)

        dg_ref[...] += jnp.sum(dhv * xn, axis=0, keepdims=True)
        dhn = dhv * g_ref[...]
        dxv = rstd * (dhn - xn * jnp.mean(dhn * xn, axis=-1, keepdims=True)) + dres_ref[...]
        dx_ref[...] = dxv
        dxb_ref[...] = dxv.astype(BF16)

    row = BS((ts, D), lambda i: (i, 0))
    vec = BS((1, D), lambda i: (0, 0))
    return pl.pallas_call(
        body, name=name, grid=(S // ts,), out_shape=(SDS((S, D), F32), SDS((S, D), BF16), SDS((1, D), F32)),
        in_specs=[row, vec, row, row], out_specs=(row, row, vec), compiler_params=_cp("arbitrary"),
    )(x, g.reshape(1, D), dh, dres)


def _final_loss(x, g, target, name):
    S, D = x.shape
    ts = _tile(S, 256, 8)

    def body(x_ref, g_ref, t_ref, loss_ref, dx_ref, dxb_ref, dg_ref):
        xv = x_ref[...]
        rstd = lax.rsqrt(jnp.mean(xv * xv, axis=-1, keepdims=True) + EPS)
        xn = xv * rstd
        gv = g_ref[...]
        e = xn * gv - t_ref[...]

        @pl.when(pl.program_id(0) == 0)
        def _():
            dg_ref[...] = jnp.zeros_like(dg_ref)
            loss_ref[...] = jnp.zeros_like(loss_ref)

        loss_ref[...] += 0.5 * jnp.sum(jnp.mean(e * e, axis=-1, keepdims=True), axis=0, keepdims=True)
        dy = e * (1.0 / D)
        dg_ref[...] += jnp.sum(dy * xn, axis=0, keepdims=True)
        dhn = dy * gv
        dxv = rstd * (dhn - xn * jnp.mean(dhn * xn, axis=-1, keepdims=True))
        dx_ref[...] = dxv
        dxb_ref[...] = dxv.astype(BF16)

    row = BS((ts, D), lambda i: (i, 0))
    vec = BS((1, D), lambda i: (0, 0))
    return pl.pallas_call(
        body, name=name, grid=(S // ts,),
        out_shape=(SDS((1, LANES), F32), SDS((S, D), F32), SDS((S, D), BF16), SDS((1, D), F32)),
        in_specs=[row, vec, row], out_specs=(BS((1, LANES), lambda i: (0, 0)), row, row, vec),
        compiler_params=_cp("arbitrary"))(x, g.reshape(1, D), target)


def _shift_down(z, k):
    rows = lax.broadcasted_iota(jnp.int32, z.shape, 0)
    return jnp.where(rows >= k, pltpu.roll(z, k, 0), 0.0)


def _shift_up(z, k):
    n = z.shape[0]
    rows = lax.broadcasted_iota(jnp.int32, z.shape, 0)
    return jnp.where(rows < n - k, pltpu.roll(z, n - k, 0), 0.0)


def _cumsum_rows(z, reverse=False):
    n = z.shape[0]
    k = 1
    while k < n:
        z = z + (_shift_up(z, k) if reverse else _shift_down(z, k))
        k *= 2
    return z


def _conv3(z, w_ref):
    return w_ref[0:1, :] * _shift_down(z, 2) + w_ref[1:2, :] * _shift_down(z, 1) + w_ref[2:3, :] * z


def _conv3_bwd(z, dy, w_ref, dw_ref):
    dw_ref[0:1, :] = jnp.sum(dy * _shift_down(z, 2), axis=0, keepdims=True)
    dw_ref[1:2, :] = jnp.sum(dy * _shift_down(z, 1), axis=0, keepdims=True)
    dw_ref[2:3, :] = jnp.sum(dy * z, axis=0, keepdims=True)
    return w_ref[2:3, :] * dy + w_ref[1:2, :] * _shift_up(dy, 1) + w_ref[0:1, :] * _shift_up(dy, 2)


def _fox_prep(proj, bias, col, name):
    S = proj.shape[0]

    def body(a_ref, b_ref, f_ref):
        z = a_ref[...] + b_ref[...]
        logf = jnp.minimum(z, 0.0) - jnp.log(1.0 + jnp.exp(-jnp.abs(z)))
        f_ref[...] = _cumsum_rows(logf)

    return pl.pallas_call(
        body, name=name, grid=(1,), out_shape=SDS((S, LANES), F32),
        in_specs=[BS((S, LANES), lambda i: (0, col)), BS((1, LANES), lambda i: (0, 0))],
        out_specs=BS((S, LANES), lambda i: (0, 0)), compiler_params=_cp("arbitrary"))(proj, bias)


def _fox_prep_bwd(proj, bias, col, df, name):
    S = proj.shape[0]

    def body(a_ref, b_ref, df_ref, da_ref, db_ref):
        z = a_ref[...] + b_ref[...]
        da = _cumsum_rows(df_ref[...], reverse=True) * _sigmoid(-z)
        da_ref[...] = da.astype(BF16)
        db_ref[...] = jnp.sum(da, axis=0, keepdims=True)

    full = BS((S, LANES), lambda i: (0, 0))
    vec = BS((1, LANES), lambda i: (0, 0))
    return pl.pallas_call(
        body, name=name, grid=(1,), out_shape=(SDS((S, LANES), BF16), SDS((1, LANES), F32)),
        in_specs=[BS((S, LANES), lambda i: (0, col)), vec, full], out_specs=(full, vec),
        compiler_params=_cp("arbitrary"))(proj, bias, df)


def _fox_scores(qs, kv, fqv, fkv, i, j, t, masked):
    s = lax.dot_general(qs, kv, (_DIMS["nt"], ((), ())), preferred_element_type=F32) + fqv - fkv
    if not masked:
        return s
    row = i * t + lax.broadcasted_iota(jnp.int32, (t, t), 0)
    col = j * t + lax.broadcasted_iota(jnp.int32, (t, t), 1)
    return jnp.where(col <= row, s, NEG)


def _fox_fwd(q, k, v, fq, fk, name):
    H, S, Dh = q.shape
    t = _tile(S, 256)
    n = S // t
    scale = Dh ** -0.5

    def body(q_ref, k_ref, v_ref, fq_ref, fk_ref, o_ref, lse_ref):
        i = pl.program_id(1)
        qs, fqv = q_ref[0] * scale, fq_ref[0]

        def step(masked, j, carry):
            m, l, acc = carry
            s = _fox_scores(qs, k_ref[0, j], fqv, fk_ref[0, j], i, j, t, masked)
            mn = jnp.maximum(m, jnp.max(s, axis=-1, keepdims=True))
            al = jnp.exp(m - mn)
            p = jnp.exp(s - mn)
            l = l * al + jnp.sum(p, axis=-1, keepdims=True)
            acc = acc * al + jnp.dot(p.astype(BF16), v_ref[0, j], preferred_element_type=F32)
            return mn, l, acc

        init = (jnp.full((t, 1), NEG, F32), jnp.zeros((t, 1), F32), jnp.zeros((t, Dh), F32))
        m, l, acc = step(True, i, lax.fori_loop(0, i, functools.partial(step, False), init))
        o_ref[0] = acc / l
        lse_ref[0] = m + jnp.log(l)

    blk = BS((1, t, Dh), lambda h, i: (h, i, 0))
    col = BS((1, t, 1), lambda h, i: (h, i, 0))
    res = BS((1, n, t, Dh), lambda h, i: (h, 0, 0, 0))
    return pl.pallas_call(
        body, name=name, grid=(H, n), out_shape=(SDS((H, S, Dh), F32), SDS((H, S, 1), F32)),
        in_specs=[blk, res, res, col, BS((1, n, 1, t), lambda h, i: (h, 0, 0, 0))], out_specs=(blk, col),
        compiler_params=_cp("parallel", "arbitrary"))(q, k.reshape(H, n, t, Dh), v.reshape(H, n, t, Dh), fq, fk)


def _fox_bwd_dq(q, k, v, fq, fk, do, lse, name):
    H, S, Dh = q.shape
    t = _tile(S, 256)
    n = S // t
    scale = Dh ** -0.5

    def body(q_ref, k_ref, v_ref, fq_ref, fk_ref, do_ref, lse_ref, dq_ref, dl_ref):
        i = pl.program_id(1)
        qs, fqv, dob, lsev = q_ref[0] * scale, fq_ref[0], do_ref[0], lse_ref[0]

        def p_dp(masked, j):
            p = jnp.exp(_fox_scores(qs, k_ref[0, j], fqv, fk_ref[0, j], i, j, t, masked) - lsev)
            return p, lax.dot_general(dob, v_ref[0, j], (_DIMS["nt"], ((), ())), preferred_element_type=F32)

        def sum_step(masked, j, delta):
            p, dp = p_dp(masked, j)
            return delta + jnp.sum(p * dp, axis=-1, keepdims=True)

        delta = sum_step(True, i, lax.fori_loop(0, i, functools.partial(sum_step, False), jnp.zeros((t, 1), F32)))

        def step(masked, j, dq):
            p, dp = p_dp(masked, j)
            return dq + jnp.dot((p * (dp - delta)).astype(BF16), k_ref[0, j], preferred_element_type=F32)

        dq = step(True, i, lax.fori_loop(0, i, functools.partial(step, False), jnp.zeros((t, Dh), F32)))
        dq_ref[0] = dq * scale
        dl_ref[0] = delta

    blk = BS((1, t, Dh), lambda h, i: (h, i, 0))
    col = BS((1, t, 1), lambda h, i: (h, i, 0))
    res = BS((1, n, t, Dh), lambda h, i: (h, 0, 0, 0))
    return pl.pallas_call(
        body, name=name, grid=(H, n), out_shape=(SDS((H, S, Dh), F32), SDS((H, S, 1), F32)),
        in_specs=[blk, res, res, col, BS((1, n, 1, t), lambda h, i: (h, 0, 0, 0)), blk, col],
        out_specs=(blk, col), compiler_params=_cp("parallel", "arbitrary"),
    )(q, k.reshape(H, n, t, Dh), v.reshape(H, n, t, Dh), fq, fk, do, lse)


def _fox_bwd_dkv(q, k, v, fq, fk, do, lse, delta, name):
    H, S, Dh = q.shape
    t = _tile(S, 256)
    n = S // t
    scale = Dh ** -0.5
    tn_dims = (_DIMS["tn"], ((), ()))

    def body(q_ref, k_ref, v_ref, fq_ref, fk_ref, do_ref, lse_ref, dl_ref, dk_ref, dv_ref, dfk_ref):
        j = pl.program_id(1)
        kv, vv, fkv = k_ref[0], v_ref[0], fk_ref[0, 0]

        def step(masked, i, carry):
            dk, dv, dfk = carry
            qs, dob = q_ref[0, i] * scale, do_ref[0, i]
            p = jnp.exp(_fox_scores(qs, kv, fq_ref[0, i], fkv, i, j, t, masked) - lse_ref[0, i])
            dv = dv + lax.dot_general(p.astype(BF16), dob, tn_dims, preferred_element_type=F32)
            dp = lax.dot_general(dob, vv, (_DIMS["nt"], ((), ())), preferred_element_type=F32)
            ds = p * (dp - dl_ref[0, i])
            dk = dk + lax.dot_general(ds.astype(BF16), qs, tn_dims, preferred_element_type=F32)
            return dk, dv, dfk - jnp.sum(ds, axis=0, keepdims=True)

        init = (jnp.zeros((t, Dh), F32), jnp.zeros((t, Dh), F32), jnp.zeros((1, t), F32))
        dk, dv, dfk = lax.fori_loop(j + 1, n, functools.partial(step, False), step(True, j, init))
        dk_ref[0] = dk
        dv_ref[0] = dv
        dfk_ref[0, 0] = dfk

    blk = BS((1, t, Dh), lambda h, j: (h, j, 0))
    res = BS((1, n, t, Dh), lambda h, j: (h, 0, 0, 0))
    rcol = BS((1, n, t, 1), lambda h, j: (h, 0, 0, 0))
    frow = BS((1, 1, 1, t), lambda h, j: (h, j, 0, 0))
    return pl.pallas_call(
        body, name=name, grid=(H, n),
        out_shape=(SDS((H, S, Dh), F32), SDS((H, S, Dh), F32), SDS((H, n, 1, t), F32)),
        in_specs=[res, blk, blk, rcol, frow, res, rcol, rcol], out_specs=(blk, blk, frow),
        compiler_params=_cp("parallel", "arbitrary"),
    )(q.reshape(H, n, t, Dh), k, v, fq.reshape(H, n, t, 1), fk, do.astype(BF16).reshape(H, n, t, Dh),
      lse.reshape(H, n, t, 1), delta.reshape(H, n, t, 1))


def _sconv_fwd(proj, w, col0, width, name):
    S = proj.shape[0]
    tc = LANES
    nb, c0 = width // tc, col0 // tc

    def body(x_ref, gb_ref, gc_ref, w_ref, o_ref):
        o_ref[...] = (gb_ref[...] * _conv3(gc_ref[...] * x_ref[...], w_ref)).astype(BF16)

    return pl.pallas_call(
        body, name=name, grid=(nb,), out_shape=SDS((S, width), BF16),
        in_specs=[BS((S, tc), lambda j: (0, c0 + j)), BS((S, tc), lambda j: (0, c0 + nb + j)),
                  BS((S, tc), lambda j: (0, c0 + 2 * nb + j)), BS((3, tc), lambda j: (0, j))],
        out_specs=BS((S, tc), lambda j: (0, j)), compiler_params=_cp("parallel"))(proj, proj, proj, w)


def _sconv_bwd(proj, w, col0, width, do, name):
    S = proj.shape[0]
    tc = LANES
    nb, c0 = width // tc, col0 // tc

    def body(x_ref, gb_ref, gc_ref, w_ref, do_ref, dx_ref, dgb_ref, dgc_ref, dw_ref):
        xb, gb, gc, dov = x_ref[...], gb_ref[...], gc_ref[...], do_ref[...]
        z = gc * xb
        dgb_ref[...] = (dov * _conv3(z, w_ref)).astype(BF16)
        dz = _conv3_bwd(z, dov * gb, w_ref, dw_ref)
        dgc_ref[...] = (dz * xb).astype(BF16)
        dx_ref[...] = (dz * gc).astype(BF16)

    out = BS((S, tc), lambda j: (0, j))
    wspec = BS((3, tc), lambda j: (0, j))
    return pl.pallas_call(
        body, name=name, grid=(nb,),
        out_shape=(SDS((S, width), BF16), SDS((S, width), BF16), SDS((S, width), BF16), SDS((3, width), F32)),
        in_specs=[BS((S, tc), lambda j: (0, c0 + j)), BS((S, tc), lambda j: (0, c0 + nb + j)),
                  BS((S, tc), lambda j: (0, c0 + 2 * nb + j)), wspec, out],
        out_specs=(out, out, out, wspec), compiler_params=_cp("parallel"))(proj, proj, proj, w, do)


def _gelu(x):
    return 0.5 * x * (1.0 + jnp.tanh(0.7978845608028654 * (x + 0.044715 * x * x * x)))


def _gelu_grad(x):
    u = 0.7978845608028654 * (x + 0.044715 * x * x * x)
    th = jnp.tanh(u)
    return 0.5 * (1.0 + th) + 0.5 * x * (1.0 - th * th) * 0.7978845608028654 * (1.0 + 3.0 * 0.044715 * x * x)


def _tril_w(w_ref, g):
    r = lax.broadcasted_iota(jnp.int32, (SGU_CHUNK, SGU_CHUNK), 0)
    c = lax.broadcasted_iota(jnp.int32, (SGU_CHUNK, SGU_CHUNK), 1)
    return jnp.where(c <= r, w_ref[g], 0.0), c <= r


def _sgu_mixed(vn, w_ref, b_ref, nch, G):
    rows = []
    for ch in range(nch):
        cols = []
        for g in range(G):
            wt, _ = _tril_w(w_ref, g)
            blk = vn[ch * SGU_CHUNK:(ch + 1) * SGU_CHUNK, g * LANES:(g + 1) * LANES]
            cols.append(jnp.dot(wt.astype(BF16), blk.astype(BF16), preferred_element_type=F32) + b_ref[g])
        rows.append(jnp.concatenate(cols, axis=1))
    return jnp.concatenate(rows, axis=0)


def _sgu_fwd(proj, gn, w, b, col0, width, name):
    S = proj.shape[0]
    G = w.shape[0]
    ts = _tile(S, 512)
    nch = ts // SGU_CHUNK
    c0 = col0 // width

    def body(u_ref, v_ref, gn_ref, w_ref, b_ref, o_ref):
        cv = _gelu(v_ref[...])
        rstd = lax.rsqrt(jnp.mean(cv * cv, axis=-1, keepdims=True) + EPS)
        mixed = _sgu_mixed(cv * rstd * gn_ref[...], w_ref, b_ref, nch, G)
        o_ref[...] = (_gelu(u_ref[...]) * mixed).astype(BF16)

    return pl.pallas_call(
        body, name=name, grid=(S // ts,), out_shape=SDS((S, width), BF16),
        in_specs=[BS((ts, width), lambda i: (i, c0)), BS((ts, width), lambda i: (i, c0 + 1)),
                  BS((1, width), lambda i: (0, 0)), BS(w.shape, lambda i: (0, 0, 0)), BS(b.shape, lambda i: (0, 0, 0))],
        out_specs=BS((ts, width), lambda i: (i, 0)), compiler_params=_cp("parallel"))(proj, proj, gn, w, b)


def _sgu_bwd(proj, gn, w, b, col0, width, do, name):
    S = proj.shape[0]
    G = w.shape[0]
    ts = _tile(S, 512)
    nch = ts // SGU_CHUNK
    c0 = col0 // width

    def body(u_ref, v_ref, gn_ref, w_ref, b_ref, do_ref, du_ref, dv_ref, dw_ref, db_ref, dgn_ref):
        uin, vin, dov, gnv = u_ref[...], v_ref[...], do_ref[...], gn_ref[...]
        cu, cv = _gelu(uin), _gelu(vin)
        rstd = lax.rsqrt(jnp.mean(cv * cv, axis=-1, keepdims=True) + EPS)
        vhat = cv * rstd
        vn = vhat * gnv
        mixed = _sgu_mixed(vn, w_ref, b_ref, nch, G)
        du_ref[...] = (dov * mixed * _gelu_grad(uin)).astype(BF16)
        dmix = dov * cu

        @pl.when(pl.program_id(0) == 0)
        def _():
            dw_ref[...] = jnp.zeros_like(dw_ref)
            db_ref[...] = jnp.zeros_like(db_ref)
            dgn_ref[...] = jnp.zeros_like(dgn_ref)

        rows = []
        for ch in range(nch):
            cols = []
            for g in range(G):
                wt, mask = _tril_w(w_ref, g)
                sl = (slice(ch * SGU_CHUNK, (ch + 1) * SGU_CHUNK), slice(g * LANES, (g + 1) * LANES))
                dm, vb = dmix[sl], vn[sl].astype(BF16)
                db_ref[g] += jnp.sum(dm, axis=-1, keepdims=True)
                dwg = lax.dot_general(dm.astype(BF16), vb, (_DIMS["nt"], ((), ())), preferred_element_type=F32)
                dw_ref[g] += jnp.where(mask, dwg, 0.0)
                cols.append(lax.dot_general(wt.astype(BF16), dm.astype(BF16), (_DIMS["tn"], ((), ())),
                                            preferred_element_type=F32))
            rows.append(jnp.concatenate(cols, axis=1))
        dvn = jnp.concatenate(rows, axis=0)
        dgn_ref[...] += jnp.sum(dvn * vhat, axis=0, keepdims=True)
        dvh = dvn * gnv
        dcv = rstd * (dvh - vhat * jnp.mean(dvh * vhat, axis=-1, keepdims=True))
        dv_ref[...] = (dcv * _gelu_grad(vin)).astype(BF16)

    row = BS((ts, width), lambda i: (i, 0))
    wsp, bsp, gsp = BS(w.shape, lambda i: (0, 0, 0)), BS(b.shape, lambda i: (0, 0, 0)), BS((1, width), lambda i: (0, 0))
    return pl.pallas_call(
        body, name=name, grid=(S // ts,),
        out_shape=(SDS((S, width), BF16), SDS((S, width), BF16), SDS(w.shape, F32), SDS(b.shape, F32),
                   SDS((1, width), F32)),
        in_specs=[BS((ts, width), lambda i: (i, c0)), BS((ts, width), lambda i: (i, c0 + 1)), gsp, wsp, bsp, row],
        out_specs=(row, row, wsp, bsp, gsp), compiler_params=_cp("arbitrary"))(proj, proj, gn, w, b, do)


def _rope_tables(positions, inv, name):
    S = positions.shape[0]
    ts = _tile(S, 512, 8)
    half = ROPE_DIM // 2

    def body(p_ref, inv_ref, c_ref, sa_ref, sb_ref):
        ang = p_ref[...].astype(F32) * inv_ref[...]
        lane = lax.broadcasted_iota(jnp.int32, (ts, LANES), 1) % HEAD_DIM
        sn = jnp.sin(ang)
        c_ref[...] = jnp.cos(ang)
        sa_ref[...] = jnp.where(lane < half, -sn, 0.0)
        sb_ref[...] = jnp.where((lane >= half) & (lane < ROPE_DIM), sn, 0.0)

    out = BS((ts, LANES), lambda i: (i, 0))
    return pl.pallas_call(
        body, name=name, grid=(S // ts,), out_shape=(SDS((S, LANES), F32),) * 3,
        in_specs=[BS((ts, 1), lambda i: (i, 0)), BS((1, LANES), lambda i: (0, 0))], out_specs=(out, out, out),
        compiler_params=_cp("parallel"))(positions, inv)


def _rope_apply(x, col0, width, tables, sign, out_dtype, name):
    S = x.shape[0]
    ts = _tile(S, 512, 8)
    tc = 2 * LANES
    nb, c0 = width // tc, col0 // tc
    half = ROPE_DIM // 2

    def body(x_ref, c_ref, sa_ref, sb_ref, o_ref):
        xv = x_ref[...].astype(F32)
        wide = lambda r: jnp.concatenate([r[...], r[...]], axis=1)
        y = (xv * wide(c_ref) + pltpu.roll(xv, tc - half, 1) * (sign * wide(sa_ref))
             + pltpu.roll(xv, half, 1) * (sign * wide(sb_ref)))
        o_ref[...] = y.astype(o_ref.dtype)

    tab = BS((ts, LANES), lambda i, j: (i, 0))
    return pl.pallas_call(
        body, name=name, grid=(S // ts, nb), out_shape=SDS((S, width), out_dtype),
        in_specs=[BS((ts, tc), lambda i, j: (i, c0 + j)), tab, tab, tab], out_specs=BS((ts, tc), lambda i, j: (i, j)),
        compiler_params=_cp("parallel", "parallel"))(x, *tables)


def _dil_masks(hh, m, bpcs, hpg):
    g = hh // hpg
    bpc = jnp.where(g == 0, bpcs[0], jnp.where(g == 1, bpcs[1], bpcs[2]))
    r = lax.broadcasted_iota(jnp.int32, (DIL_SPAN, DIL_SPAN), 0)
    c = lax.broadcasted_iota(jnp.int32, (DIL_SPAN, DIL_SPAN), 1)
    return (c >= r) & (lax.rem(m, bpc) != 0), c <= r


def _nt(a, b):
    return lax.dot_general(a, b, (_DIMS["nt"], ((), ())), preferred_element_type=F32)


def _tn(a, b):
    return lax.dot_general(a, b, (_DIMS["tn"], ((), ())), preferred_element_type=F32)


def _dil_fwd(q, k, v, bpcs, hpg, name):
    HH, S, Dh = q.shape
    nb = S // DIL_SPAN
    scale = Dh ** -0.5

    def body(q_ref, kp_ref, kc_ref, vp_ref, vc_ref, o_ref, lse_ref):
        mp, mc = _dil_masks(pl.program_id(0), pl.program_id(1), bpcs, hpg)
        qv = q_ref[0]
        sp = jnp.where(mp, _nt(qv, kp_ref[0]) * scale, NEG)
        sc = jnp.where(mc, _nt(qv, kc_ref[0]) * scale, NEG)
        mx = jnp.maximum(jnp.max(sp, axis=-1, keepdims=True), jnp.max(sc, axis=-1, keepdims=True))
        pp, pc = jnp.exp(sp - mx), jnp.exp(sc - mx)
        l = jnp.sum(pp, axis=-1, keepdims=True) + jnp.sum(pc, axis=-1, keepdims=True)
        acc = (jnp.dot(pp.astype(BF16), vp_ref[0], preferred_element_type=F32)
               + jnp.dot(pc.astype(BF16), vc_ref[0], preferred_element_type=F32))
        o_ref[0] = acc / l
        lse_ref[0] = mx + jnp.log(l)

    cur = BS((1, DIL_SPAN, Dh), lambda h, m: (h, m, 0))
    prev = BS((1, DIL_SPAN, Dh), lambda h, m: (h, jnp.maximum(m - 1, 0), 0))
    return pl.pallas_call(
        body, name=name, grid=(HH, nb), out_shape=(SDS((HH, S, Dh), F32), SDS((HH, S, 1), F32)),
        in_specs=[cur, prev, cur, prev, cur], out_specs=(cur, BS((1, DIL_SPAN, 1), lambda h, m: (h, m, 0))),
        compiler_params=_cp("parallel", "parallel"))(q, k, k, v, v)


def _dil_bwd_dq(q, k, v, do, lse, coef, bpcs, hpg, name):
    HH, S, Dh = q.shape
    nb = S // DIL_SPAN
    scale = Dh ** -0.5

    def body(q_ref, kp_ref, kc_ref, vp_ref, vc_ref, do_ref, lse_ref, cf_ref, dq_ref):
        mp, mc = _dil_masks(pl.program_id(0), pl.program_id(1), bpcs, hpg)
        qv, dob, lsev, cf = q_ref[0], do_ref[0], lse_ref[0], cf_ref[0]
        pp = jnp.exp(jnp.where(mp, _nt(qv, kp_ref[0]) * scale, NEG) - lsev)
        pc = jnp.exp(jnp.where(mc, _nt(qv, kc_ref[0]) * scale, NEG) - lsev)
        dsp = pp * (_nt(dob, vp_ref[0]) + cf)
        dsc = pc * (_nt(dob, vc_ref[0]) + cf)
        dq_ref[0] = (jnp.dot(dsp.astype(BF16), kp_ref[0], preferred_element_type=F32)
                     + jnp.dot(dsc.astype(BF16), kc_ref[0], preferred_element_type=F32)) * scale

    cur = BS((1, DIL_SPAN, Dh), lambda h, m: (h, m, 0))
    prev = BS((1, DIL_SPAN, Dh), lambda h, m: (h, jnp.maximum(m - 1, 0), 0))
    col = BS((1, DIL_SPAN, 1), lambda h, m: (h, m, 0))
    return pl.pallas_call(
        body, name=name, grid=(HH, nb), out_shape=SDS((HH, S, Dh), F32),
        in_specs=[cur, prev, cur, prev, cur, cur, col, col], out_specs=cur,
        compiler_params=_cp("parallel", "parallel"))(q, k, k, v, v, do, lse, coef)


def _dil_bwd_dkv(q, k, v, do, lse, coef, bpcs, hpg, name):
    HH, S, Dh = q.shape
    nb = S // DIL_SPAN
    scale = Dh ** -0.5

    def body(k_ref, v_ref, qc_ref, qn_ref, doc_ref, don_ref, lc_ref, ln_ref, cc_ref, cn_ref, dk_ref, dv_ref):
        hh, m = pl.program_id(0), pl.program_id(1)
        _, mc = _dil_masks(hh, m, bpcs, hpg)
        mp, _ = _dil_masks(hh, m + 1, bpcs, hpg)
        mp = mp & (m + 1 < nb)
        kv, vv = k_ref[0], v_ref[0]
        pc = jnp.exp(jnp.where(mc, _nt(qc_ref[0], kv) * scale, NEG) - lc_ref[0])
        pn = jnp.exp(jnp.where(mp, _nt(qn_ref[0], kv) * scale, NEG) - ln_ref[0])
        dsc = pc * (_nt(doc_ref[0], vv) + cc_ref[0])
        dsn = pn * (_nt(don_ref[0], vv) + cn_ref[0])
        dv_ref[0] = _tn(pc.astype(BF16), doc_ref[0]) + _tn(pn.astype(BF16), don_ref[0])
        dk_ref[0] = (_tn(dsc.astype(BF16), qc_ref[0]) + _tn(dsn.astype(BF16), qn_ref[0])) * scale

    cur = BS((1, DIL_SPAN, Dh), lambda h, m: (h, m, 0))
    nxt = BS((1, DIL_SPAN, Dh), lambda h, m: (h, jnp.minimum(m + 1, nb - 1), 0))
    col = BS((1, DIL_SPAN, 1), lambda h, m: (h, m, 0))
    ncol = BS((1, DIL_SPAN, 1), lambda h, m: (h, jnp.minimum(m + 1, nb - 1), 0))
    return pl.pallas_call(
        body, name=name, grid=(HH, nb), out_shape=(SDS((HH, S, Dh), F32), SDS((HH, S, Dh), F32)),
        in_specs=[cur, cur, cur, nxt, cur, nxt, col, ncol, col, ncol], out_specs=(cur, cur),
        compiler_params=_cp("parallel", "parallel"))(k, v, q, q, do, do, lse, lse, coef, coef)


def _dil_merge(o3, lse3, name):
    G, H, S, Dh = o3.shape
    ts = _tile(S, 512, 8)

    def body(o_ref, l_ref, out_ref):
        lv = [l_ref[g, 0] for g in range(G)]
        mx = functools.reduce(jnp.maximum, lv)
        ex = [jnp.exp(v - mx) for v in lv]
        den = functools.reduce(lambda a, b: a + b, ex)
        out_ref[0] = functools.reduce(lambda a, b: a + b, [(ex[g] / den) * o_ref[g, 0] for g in range(G)])

    return pl.pallas_call(
        body, name=name, grid=(H, S // ts), out_shape=SDS((H, S, Dh), F32),
        in_specs=[BS((G, 1, ts, Dh), lambda h, i: (0, h, i, 0)), BS((G, 1, ts, 1), lambda h, i: (0, h, i, 0))],
        out_specs=BS((1, ts, Dh), lambda h, i: (h, i, 0)), compiler_params=_cp("parallel", "parallel"))(o3, lse3)


def _dil_merge_bwd(o3, lse3, do, name):
    G, H, S, Dh = o3.shape
    ts = _tile(S, 512, 8)

    def body(o_ref, l_ref, do_ref, do3_ref, cf_ref):
        lv = [l_ref[g, 0] for g in range(G)]
        mx = functools.reduce(jnp.maximum, lv)
        ex = [jnp.exp(v - mx) for v in lv]
        den = functools.reduce(lambda a, b: a + b, ex)
        wt = [e / den for e in ex]
        dov = do_ref[0]
        dw = [jnp.sum(dov * o_ref[g, 0], axis=-1, keepdims=True) for g in range(G)]
        mean = functools.reduce(lambda a, b: a + b, [wt[g] * dw[g] for g in range(G)])
        for g in range(G):
            do3_ref[g, 0] = (wt[g] * dov).astype(BF16)
            cf_ref[g, 0] = wt[g] * (dw[g] - mean) - wt[g] * dw[g]

    o_spec = BS((G, 1, ts, Dh), lambda h, i: (0, h, i, 0))
    l_spec = BS((G, 1, ts, 1), lambda h, i: (0, h, i, 0))
    return pl.pallas_call(
        body, name=name, grid=(H, S // ts), out_shape=(SDS((G, H, S, Dh), BF16), SDS((G, H, S, 1), F32)),
        in_specs=[o_spec, l_spec, BS((1, ts, Dh), lambda h, i: (h, i, 0))], out_specs=(o_spec, l_spec),
        compiler_params=_cp("parallel", "parallel"))(o3, lse3, do)


def _merge_fwd(proj, gate_col, os_, ws, name):
    S = proj.shape[0]
    D = ws[0].shape[1]
    nbr = len(os_)
    ts = _tile(S, 512, 8)
    tn = LANES * 2 if gate_col % (2 * LANES) == 0 and D % (2 * LANES) == 0 else LANES
    g0, gstep = gate_col // tn, D // tn

    def body(*refs):
        p_refs, o_refs, w_refs, out_ref = refs[:nbr], refs[nbr:2 * nbr], refs[2 * nbr:3 * nbr], refs[-1]
        acc = None
        for i in range(nbr):
            t = _sigmoid(p_refs[i][...]) * jnp.dot(o_refs[i][...], w_refs[i][...], preferred_element_type=F32)
            acc = t if acc is None else acc + t
        out_ref[...] = acc.astype(BF16)

    specs = ([BS((ts, tn), lambda s, j, i=i: (s, g0 + i * gstep + j)) for i in range(nbr)]
             + [BS((ts, o.shape[1]), lambda s, j: (s, 0)) for o in os_]
             + [BS((w.shape[0], tn), lambda s, j: (0, j)) for w in ws])
    return pl.pallas_call(
        body, name=name, grid=(S // ts, D // tn), out_shape=SDS((S, D), BF16), in_specs=specs,
        out_specs=BS((ts, tn), lambda s, j: (s, j)), compiler_params=_cp("parallel", "parallel"),
    )(*([proj] * nbr), *os_, *ws)


def _merge_bwd(proj, gate_col, os_, ws, dm, name):
    S = proj.shape[0]
    D = ws[0].shape[1]
    nbr = len(os_)
    ts = _tile(S, 512, 8)
    tn = LANES * 2 if gate_col % (2 * LANES) == 0 and D % (2 * LANES) == 0 else LANES
    g0, gstep = gate_col // tn, D // tn

    def body(*refs):
        p_refs, o_refs, w_refs = refs[:nbr], refs[nbr:2 * nbr], refs[2 * nbr:3 * nbr]
        dmv = refs[3 * nbr][...]
        dbo_refs, dgl_refs = refs[3 * nbr + 1:4 * nbr + 1], refs[4 * nbr + 1:]
        for i in range(nbr):
            gt = _sigmoid(p_refs[i][...])
            bo = jnp.dot(o_refs[i][...], w_refs[i][...], preferred_element_type=F32)
            dbo_refs[i][...] = (dmv * gt).astype(BF16)
            dgl_refs[i][...] = (dmv * bo * gt * (1.0 - gt)).astype(BF16)

    tile = BS((ts, tn), lambda s, j: (s, j))
    specs = ([BS((ts, tn), lambda s, j, i=i: (s, g0 + i * gstep + j)) for i in range(nbr)]
             + [BS((ts, o.shape[1]), lambda s, j: (s, 0)) for o in os_]
             + [BS((w.shape[0], tn), lambda s, j: (0, j)) for w in ws] + [tile])
    outs = pl.pallas_call(
        body, name=name, grid=(S // ts, D // tn), out_shape=(SDS((S, D), BF16),) * (2 * nbr), in_specs=specs,
        out_specs=(tile,) * (2 * nbr), compiler_params=_cp("parallel", "parallel"),
    )(*([proj] * nbr), *os_, *ws, dm)
    return outs[:nbr], outs[nbr:]


def _ffn_act(up, w, name):
    S, F2 = up.shape
    F = F2 // 2
    tc = LANES
    nb = F // tc

    def body(g_ref, v_ref, wg_ref, wv_ref, a_ref):
        ug = _conv3(g_ref[...], wg_ref)
        a_ref[...] = (ug * _sigmoid(ug) * _conv3(v_ref[...], wv_ref)).astype(BF16)

    return pl.pallas_call(
        body, name=name, grid=(nb,), out_shape=SDS((S, F), BF16),
        in_specs=[BS((S, tc), lambda j: (0, j)), BS((S, tc), lambda j: (0, nb + j)),
                  BS((3, tc), lambda j: (0, j)), BS((3, tc), lambda j: (0, nb + j))],
        out_specs=BS((S, tc), lambda j: (0, j)), compiler_params=_cp("parallel"))(up, up, w, w)


def _ffn_act_bwd(up, w, da, name):
    S, F2 = up.shape
    F = F2 // 2
    tc = LANES
    nb = F // tc

    def body(g_ref, v_ref, wg_ref, wv_ref, da_ref, dg_ref, dv_ref, dwg_ref, dwv_ref):
        pg, pv, dav = g_ref[...], v_ref[...], da_ref[...].astype(F32)
        ug, uv = _conv3(pg, wg_ref), _conv3(pv, wv_ref)
        sg = _sigmoid(ug)
        dg_ref[...] = _conv3_bwd(pg, dav * uv * (sg * (1.0 + ug * (1.0 - sg))), wg_ref, dwg_ref).astype(BF16)
        dv_ref[...] = _conv3_bwd(pv, dav * ug * sg, wv_ref, dwv_ref).astype(BF16)

    col = BS((S, tc), lambda j: (0, j))
    wsp = BS((3, tc), lambda j: (0, j))
    return pl.pallas_call(
        body, name=name, grid=(nb,),
        out_shape=(SDS((S, F), BF16), SDS((S, F), BF16), SDS((3, F), F32), SDS((3, F), F32)),
        in_specs=[col, BS((S, tc), lambda j: (0, nb + j)), wsp, BS((3, tc), lambda j: (0, nb + j)), col],
        out_specs=(col, col, wsp, wsp), compiler_params=_cp("parallel"))(up, up, w, w, da)


def _ple_fwd(x, pgl, pe, name):
    S, D = x.shape
    ts = _tile(S, 256, 8)

    def body(x_ref, g_ref, e_ref, o_ref):
        o_ref[...] = x_ref[...] + _sigmoid(g_ref[...]) * e_ref[...]

    row = BS((ts, D), lambda i: (i, 0))
    return pl.pallas_call(body, name=name, grid=(S // ts,), out_shape=SDS((S, D), F32), in_specs=[row] * 3,
                          out_specs=row, compiler_params=_cp("parallel"))(x, pgl, pe)


def _ple_bwd(dx, pgl, pe, name):
    S, D = dx.shape
    ts = _tile(S, 256, 8)

    def body(dx_ref, g_ref, e_ref, dg_ref, de_ref):
        dxv, sg = dx_ref[...], _sigmoid(g_ref[...])
        dg_ref[...] = (dxv * e_ref[...] * sg * (1.0 - sg)).astype(BF16)
        de_ref[...] = (dxv * sg).astype(BF16)

    row = BS((ts, D), lambda i: (i, 0))
    return pl.pallas_call(body, name=name, grid=(S // ts,), out_shape=(SDS((S, D), BF16),) * 2, in_specs=[row] * 3,
                          out_specs=(row, row), compiler_params=_cp("parallel"))(dx, pgl, pe)


def _adamw(w, g, m, v, name):
    shape = w.shape
    cols = shape[-1]
    rows = w.size // cols
    tr = _tile(rows, 256, 8)
    c1, c2 = 1.0 / (1.0 - ADAM_B1 ** ADAM_STEP), 1.0 / (1.0 - ADAM_B2 ** ADAM_STEP)

    def body(w_ref, g_ref, m_ref, v_ref, d_ref, nm_ref, nv_ref):
        gv = g_ref[...]
        nm = ADAM_B1 * m_ref[...] + (1.0 - ADAM_B1) * gv
        nv = ADAM_B2 * v_ref[...] + (1.0 - ADAM_B2) * (gv * gv)
        d_ref[...] = -ADAM_LR * ((nm * c1) / (jnp.sqrt(nv * c2) + ADAM_EPS) + ADAM_WD * w_ref[...])
        nm_ref[...] = nm
        nv_ref[...] = nv

    blk = BS((tr, cols), lambda i: (i, 0))
    outs = pl.pallas_call(
        body, name=name, grid=(rows // tr,), out_shape=(SDS((rows, cols), F32),) * 3, in_specs=[blk] * 4,
        out_specs=(blk,) * 3, compiler_params=_cp("parallel"),
    )(*(a.reshape(rows, cols) for a in (w, g, m, v)))
    return tuple(o.reshape(shape) for o in outs)


SHARDED = ("w_in", "shortconv_w", "w_br_fox", "w_br_conv", "w_br_sgu", "w_br_dil", "w_out", "w_up", "ffn_conv_w",
           "w_down", "w_ple_gate", "w_ple_proj")
REPLICATED = ("norm_mix_g", "fox_forget_b", "sgu_norm_g", "sgu_w", "sgu_b", "norm_ffn_g", "norm_ple_g")
WEIGHTS = ("norm_mix_g", "w_in", "fox_forget_b", "shortconv_w", "sgu_norm_g", "sgu_w", "sgu_b", "w_br_fox",
           "w_br_conv", "w_br_sgu", "w_br_dil", "w_out", "norm_ffn_g", "w_up", "ffn_conv_w", "w_down", "norm_ple_g",
           "w_ple_gate", "w_ple_proj", "final_norm_g")
PACK_COLS = 1024
PACK_ROWS = 256


def _pack(parts, dtype, unit):
    lead = parts[0].shape[:-1] if parts[0].ndim > 1 else ()
    flat = jnp.concatenate([p.astype(dtype) for p in parts], axis=-1)
    n = flat.shape[-1]
    rows = -(-n // (PACK_COLS * unit)) * unit
    flat = jnp.pad(flat, [(0, 0)] * len(lead) + [(0, rows * PACK_COLS - n)])
    return flat.reshape(lead + (rows, PACK_COLS))


def _to_heads(x, heads):
    S = x.shape[0]
    return x.reshape(S, heads, HEAD_DIM).transpose(1, 0, 2)


def _from_heads(x):
    H, S, Dh = x.shape
    return x.transpose(1, 0, 2).reshape(S, H * Dh)


def _to_residue(x, hpg):
    S = x.shape[0]
    gw = hpg * HEAD_DIM
    outs = []
    for g, (_, d) in enumerate(DIL_PATTERNS):
        seg = x[:, g * gw:(g + 1) * gw].reshape(S // d, d, hpg, HEAD_DIM)
        outs.append(seg.transpose(2, 1, 0, 3).reshape(hpg, S, HEAD_DIM))
    return jnp.concatenate(outs, axis=0)


def _from_residue(x, hpg):
    _, S, C = x.shape
    outs = []
    for g, (_, d) in enumerate(DIL_PATTERNS):
        seg = x[g * hpg:(g + 1) * hpg].reshape(hpg, d, S // d, C)
        outs.append(seg.transpose(0, 2, 1, 3).reshape(hpg, S, C))
    return jnp.stack(outs, axis=0)


def _groups_to_residue(x):
    G, hpg, S, C = x.shape
    outs = []
    for g, (_, d) in enumerate(DIL_PATTERNS):
        outs.append(x[g].reshape(hpg, S // d, d, C).transpose(0, 2, 1, 3).reshape(hpg, S, C))
    return jnp.concatenate(outs, axis=0)


def kernel(x, p, positions, norm_mix_g, w_in, fox_forget_b, shortconv_w, sgu_norm_g, sgu_w, sgu_b, w_br_fox, w_br_conv, w_br_sgu, w_br_dil, w_out, norm_ffn_g, w_up, ffn_conv_w, w_down, norm_ple_g, w_ple_gate, w_ple_proj, final_norm_g, loss_target, m_norm_mix_g, m_w_in, m_fox_forget_b, m_shortconv_w, m_sgu_norm_g, m_sgu_w, m_sgu_b, m_w_br_fox, m_w_br_conv, m_w_br_sgu, m_w_br_dil, m_w_out, m_norm_ffn_g, m_w_up, m_ffn_conv_w, m_w_down, m_norm_ple_g, m_w_ple_gate, m_w_ple_proj, m_final_norm_g, v_norm_mix_g, v_w_in, v_fox_forget_b, v_shortconv_w, v_sgu_norm_g, v_sgu_w, v_sgu_b, v_w_br_fox, v_w_br_conv, v_w_br_sgu, v_w_br_dil, v_w_out, v_norm_ffn_g, v_w_up, v_ffn_conv_w, v_w_down, v_norm_ple_g, v_w_ple_gate, v_w_ple_proj, v_final_norm_g):
    given = dict(locals())
    W = {n: given[n] for n in WEIGHTS}
    depth = w_in.shape[0]
    S, D = x.shape[1], x.shape[2]
    x0, target, pos = x[0], loss_target[0], positions[0].reshape(S, 1)
    FH = fox_forget_b.shape[1]
    FW = FH * HEAD_DIM
    CW = shortconv_w.shape[2] * N_DEV
    G = sgu_w.shape[1]
    SW = G * sgu_w.shape[2]
    DOUT = w_br_dil.shape[1]
    hpg = DOUT // HEAD_DIM
    NG = len(DIL_PATTERNS)
    DW = NG * DOUT
    bpcs = tuple(S // d // DIL_SPAN for _, d in DIL_PATTERNS)
    A0, B0 = 0, 3 * FW
    C0 = B0 + 3 * CW
    D0 = C0 + 2 * SW
    G0 = D0 + 3 * DW
    F0 = G0 + 4 * D
    NP = F0 + LANES
    orig = [3 * FW, FH, 3 * CW, 2 * SW, 3 * DW, 4 * D]
    o_off = [sum(orig[:i]) for i in range(len(orig) + 1)]

    me = 4 * lax.axis_index("x") + 2 * lax.axis_index("y") + lax.axis_index("c")
    n_in, o1, o2, d_in = w_in.shape[2], o_off[1], o_off[2], o_off[6]
    NPB = NP // LANES
    WB = -(-(LANES - 1 + n_in) // LANES)
    PW = (WB + 1) * LANES

    def a_of(j, clip=lambda v: min(max(v, 0), FH)):
        return n_in * j - clip(n_in * j - o1)

    def s_of(j):
        return min(a_of(j) // LANES, NPB - WB)

    def s_traced(j):
        return jnp.minimum(a_of(j, lambda v: jnp.clip(v, 0, FH)) // LANES, NPB - WB)

    def runs(j):
        lo, hi, out = n_in * j, n_in * (j + 1), []
        for a, b in ((0, o1), (o1, o2), (o2, d_in)):
            l, h = max(lo, a), min(hi, b)
            if l < h:
                dst = WB * LANES + l - o1 if a == o1 else l - (FH if l >= o2 else 0) - s_of(j) * LANES
                out.append((l - lo, h - l, dst))
        return out

    def to_window(j, shard):
        piece = jnp.zeros((shard.shape[0], PW), shard.dtype)
        for src, ln, dst in runs(j):
            piece = piece.at[:, dst:dst + ln].set(shard[:, src:src + ln])
        return piece

    def from_window(j, win):
        return jnp.concatenate([win[:, dst:dst + ln] for _, ln, dst in runs(j)], axis=1)

    def on_my_index(fn, arg):
        return lax.switch(me, [functools.partial(fn, j) for j in range(N_DEV)], arg)

    def place(axis, length):
        def view(ref, j):
            align = LANES if axis == len(ref.shape) - 1 else 16
            start = pl.multiple_of(j * length, align) if length % align == 0 else j * length
            return ref.at[tuple([slice(None)] * axis + [pl.ds(start, length)])]
        return view

    def slot(ref, j):
        return ref.at[j]

    def whole(ref, j):
        return ref

    n_up, r_down, r_sq = w_up.shape[2], w_down.shape[1], w_out.shape[1]
    n_br, n_sm, n_sc = w_br_fox.shape[2], w_br_dil.shape[2], shortconv_w.shape[2]
    FF2 = n_up * N_DEV
    cw_pad = -(-n_up // LANES) * LANES
    placed = [place(1, n_up), place(0, r_down), place(1, r_sq), place(2, n_br), place(2, n_sm)]

    def gather_layer(i):
        conv = jnp.zeros((8, cw_pad), F32).at[0:3, :n_up].set(ffn_conv_w[i]).at[3:6, :n_sc].set(shortconv_w[i])
        srcs = [on_my_index(to_window, w_in[i].astype(BF16)), w_up[i].astype(BF16), w_down[i].astype(BF16),
                jnp.stack([w_out[i], w_ple_gate[i]]).astype(BF16),
                jnp.stack([w_br_fox[i], w_br_conv[i], w_br_sgu[i]]).astype(BF16),
                jnp.stack([w_br_dil[i], w_ple_proj[i]]).astype(BF16), conv]
        shapes = [SDS((N_DEV, D, PW), BF16), SDS((D, FF2), BF16), SDS((r_down * N_DEV, D), BF16),
                  SDS((2, r_sq * N_DEV, D), BF16), SDS((3, w_br_fox.shape[1], n_br * N_DEV), BF16),
                  SDS((2, w_br_dil.shape[1], n_sm * N_DEV), BF16), SDS((N_DEV, 8, cw_pad), F32)]
        pieces, wup, wdown, sq, br, sm, cv = _comm("comm_gather_weights", srcs, shapes, [whole] * 7,
                                                   [slot] + placed + [slot])
        return {
            "w_in": _assemble(pieces, [s_of(j) for j in range(N_DEV)], WB, NPB, "assemble_w_in"),
            "w_up": wup, "w_down": wdown, "w_out": sq[0], "w_ple_gate": sq[1],
            "w_br_fox": br[0], "w_br_conv": br[1], "w_br_sgu": br[2], "w_br_dil": sm[0], "w_ple_proj": sm[1],
            "ffn_conv_w": jnp.moveaxis(cv[:, 0:3, :n_up], 0, 1).reshape(3, FF2),
            "shortconv_w": jnp.moveaxis(cv[:, 3:6, :n_sc], 0, 1).reshape(3, CW),
        }

    full = [gather_layer(i) for i in range(depth)]

    inv = ROPE_THETA ** (-jnp.arange(ROPE_DIM // 2, dtype=F32) * (2.0 / ROPE_DIM))
    inv_row = jnp.tile(jnp.concatenate([inv, inv, jnp.zeros((HEAD_DIM - ROPE_DIM,), F32)]), LANES // HEAD_DIM)
    tables = _rope_tables(pos, inv_row.reshape(1, LANES), "rope_tables")

    def pad_lanes(v):
        return jnp.pad(v, ((0, 0), (0, LANES - v.shape[1])))

    saved = []
    xc = x0
    for i in range(depth):
        fw, sv = full[i], {}
        sv["x"] = xc
        h1 = _rms_fwd(xc, norm_mix_g[i], "rms_mix")
        proj = _mm(h1, fw["w_in"], "nn", F32, "mm_in")
        sv["h1"], sv["proj"] = h1, proj
        bias = pad_lanes(fox_forget_b[i].reshape(1, FH))
        Fc = _fox_prep(proj, bias, F0 // LANES, "fox_prep")
        t = _tile(S, 256)
        Ft = Fc[:, :FH].T
        fq, fk = Ft.reshape(FH, S, 1), Ft.reshape(FH, S // t, 1, t)
        qa, ka, va = (_to_heads(proj[:, A0 + j * FW:A0 + (j + 1) * FW], FH).astype(BF16) for j in range(3))
        oa, lse_a = _fox_fwd(qa, ka, va, fq, fk, "fox_fwd")
        oa_b = _from_heads(oa).astype(BF16)
        sv.update(bias=bias, fq=fq, fk=fk, qa=qa, ka=ka, va=va, oa=oa, lse_a=lse_a, oa_b=oa_b)
        ob = _sconv_fwd(proj, fw["shortconv_w"], B0, CW, "sconv_fwd")
        sgb = sgu_b[i].reshape(G, SGU_CHUNK, 1)
        oc = _sgu_fwd(proj, sgu_norm_g[i].reshape(1, SW), sgu_w[i], sgb, C0, SW, "sgu_fwd")
        qd = _rope_apply(proj, D0, DW, tables, 1.0, BF16, "rope_q")
        kd = _rope_apply(proj, D0 + DW, DW, tables, 1.0, BF16, "rope_k")
        qr, kr = _to_residue(qd, hpg), _to_residue(kd, hpg)
        vr = _to_residue(proj[:, D0 + 2 * DW:D0 + 3 * DW].astype(BF16), hpg)
        od_r, lse_r = _dil_fwd(qr, kr, vr, bpcs, hpg, "dil_fwd")
        o3, lse3 = _from_residue(od_r, hpg), _from_residue(lse_r, hpg)
        od = _from_heads(_dil_merge(o3, lse3, "dil_merge")).astype(BF16)
        sv.update(ob=ob, oc=oc, sgb=sgb, qr=qr, kr=kr, vr=vr, lse_r=lse_r, o3=o3, lse3=lse3, od=od)
        brs = [oa_b, ob, oc, od]
        wbr = [fw["w_br_fox"], fw["w_br_conv"], fw["w_br_sgu"], fw["w_br_dil"]]
        merged = _merge_fwd(proj, G0, brs, wbr, "merge_fwd")
        x2 = _mm(merged, fw["w_out"], "nn", F32, "mm_out", res=xc)
        h2 = _rms_fwd(x2, norm_ffn_g[i], "rms_ffn")
        up = _mm(h2, fw["w_up"], "nn", F32, "mm_up")
        act = _ffn_act(up, fw["ffn_conv_w"], "ffn_act")
        x3 = _mm(act, fw["w_down"], "nn", F32, "mm_down", res=x2)
        h3 = _rms_fwd(x3, norm_ple_g[i], "rms_ple")
        pgl = _mm(h3, fw["w_ple_gate"], "nn", F32, "mm_ple_gate")
        pb = p[i, 0].astype(BF16)
        pe = _mm(pb, fw["w_ple_proj"], "nn", F32, "mm_ple_proj")
        xc = _ple_fwd(x3, pgl, pe, "ple_fwd")
        sv.update(merged=merged, x2=x2, h2=h2, up=up, act=act, x3=x3, h3=h3, pgl=pgl, pb=pb, pe=pe)
        saved.append(sv)

    loss_row, dx, dxb, dg_final = _final_loss(xc, final_norm_g, target, "final_loss")
    loss = lax.psum(loss_row[0, 0], MESH_AXES)

    grads_sh = [None] * depth
    grads_rep = [None] * depth
    for i in reversed(range(depth)):
        fw, sv = full[i], saved[i]
        gs, gr = {}, {}
        proj = sv["proj"]
        dpgl, dpe = _ple_bwd(dx, sv["pgl"], sv["pe"], "ple_bwd")
        gs["w_ple_proj"] = _mm(sv["pb"], dpe, "tn", BF16, "mm_dw_ple_proj")
        gs["w_ple_gate"] = _mm(sv["h3"], dpgl, "tn", BF16, "mm_dw_ple_gate")
        dh3 = _mm(dpgl, fw["w_ple_gate"], "nt", F32, "mm_dh3")
        dx, dxb, gr["norm_ple_g"] = _rms_bwd(sv["x3"], norm_ple_g[i], dh3, dx, "rms_ple_bwd")
        gs["w_down"] = _mm(sv["act"], dxb, "tn", BF16, "mm_dw_down")
        dact = _mm(dxb, fw["w_down"], "nt", BF16, "mm_dact")
        dug, duv, dwg, dwv = _ffn_act_bwd(sv["up"], fw["ffn_conv_w"], dact, "ffn_act_bwd")
        gs["ffn_conv_w"] = jnp.concatenate([dwg, dwv], axis=1)
        dup = jnp.concatenate([dug, duv], axis=1)
        gs["w_up"] = _mm(sv["h2"], dup, "tn", BF16, "mm_dw_up")
        dh2 = _mm(dup, fw["w_up"], "nt", F32, "mm_dh2")
        dx, dxb, gr["norm_ffn_g"] = _rms_bwd(sv["x2"], norm_ffn_g[i], dh2, dx, "rms_ffn_bwd")
        gs["w_out"] = _mm(sv["merged"], dxb, "tn", BF16, "mm_dw_out")
        dmerged = _mm(dxb, fw["w_out"], "nt", F32, "mm_dmerged")
        brs = [sv["oa_b"], sv["ob"], sv["oc"], sv["od"]]
        names = ["w_br_fox", "w_br_conv", "w_br_sgu", "w_br_dil"]
        wbr = [fw[n] for n in names]
        dbo, dgl = _merge_bwd(proj, G0, brs, wbr, dmerged, "merge_bwd")
        dos = []
        for j, n in enumerate(names):
            gs[n] = _mm(brs[j], dbo[j], "tn", BF16, "mm_dw_" + n)
            dos.append(_mm(dbo[j], wbr[j], "nt", F32, "mm_do_" + n))
        doa = _to_heads(dos[0], FH).astype(BF16)
        dqa, delta = _fox_bwd_dq(sv["qa"], sv["ka"], sv["va"], sv["fq"], sv["fk"], doa, sv["lse_a"], "fox_bwd_dq")
        dka, dva, dfk = _fox_bwd_dkv(sv["qa"], sv["ka"], sv["va"], sv["fq"], sv["fk"], doa, sv["lse_a"], delta,
                                     "fox_bwd_dkv")
        dF = pad_lanes(dfk.reshape(FH, S).T)
        daf, dbias = _fox_prep_bwd(proj, sv["bias"], F0 // LANES, dF, "fox_prep_bwd")
        gr["fox_forget_b"] = dbias[0, :FH]
        d_a = [_from_heads(t).astype(BF16) for t in (dqa, dka, dva)]
        dxb_, dgb, dgc, gs["shortconv_w"] = _sconv_bwd(proj, fw["shortconv_w"], B0, CW, dos[1], "sconv_bwd")
        du, dv, dsw, dsb, dsg = _sgu_bwd(proj, sgu_norm_g[i].reshape(1, SW), sgu_w[i], sv["sgb"], C0, SW, dos[2],
                                         "sgu_bwd")
        gr["sgu_w"], gr["sgu_b"], gr["sgu_norm_g"] = dsw, dsb.reshape(G, SGU_CHUNK), dsg[0]
        do3, cf3 = _dil_merge_bwd(sv["o3"], sv["lse3"], _to_heads(dos[3], hpg), "dil_merge_bwd")
        do_r, cf_r = _groups_to_residue(do3), _groups_to_residue(cf3)
        dq_r = _dil_bwd_dq(sv["qr"], sv["kr"], sv["vr"], do_r, sv["lse_r"], cf_r, bpcs, hpg, "dil_bwd_dq")
        dk_r, dv_r = _dil_bwd_dkv(sv["qr"], sv["kr"], sv["vr"], do_r, sv["lse_r"], cf_r, bpcs, hpg, "dil_bwd_dkv")

        def natural(t):
            return _from_heads(_from_residue(t, hpg).reshape(NG * hpg, S, HEAD_DIM))

        dqd = _rope_apply(natural(dq_r), 0, DW, tables, -1.0, BF16, "rope_q_bwd")
        dkd = _rope_apply(natural(dk_r), 0, DW, tables, -1.0, BF16, "rope_k_bwd")
        dvd = natural(dv_r).astype(BF16)
        dproj = jnp.concatenate(d_a + [dxb_, dgb, dgc, du, dv, dqd, dkd, dvd] + list(dgl) + [daf], axis=1)
        gs["w_in"] = _mm(sv["h1"], dproj, "tn", BF16, "mm_dw_in")
        dh1 = _mm(dproj, fw["w_in"], "nt", F32, "mm_dh1")
        dx, dxb, gr["norm_mix_g"] = _rms_bwd(sv["x"], norm_mix_g[i], dh1, dx, "rms_mix_bwd")
        gr["norm_mix_g"], gr["norm_ffn_g"], gr["norm_ple_g"] = (gr[n][0] for n in
                                                               ("norm_mix_g", "norm_ffn_g", "norm_ple_g"))
        grads_sh[i], grads_rep[i] = gs, gr
    grad_x = dx.reshape(1, S, D)

    def in_window(ref, j):
        return ref.at[:, pl.ds(pl.multiple_of(s_traced(j) * LANES, LANES), WB * LANES)]

    def in_flogit(ref, j):
        return ref.at[:, pl.ds(F0, LANES)]

    def sum_sources(t, name):
        lead = t.shape[1:]
        return _sum8(t.reshape(N_DEV, -1, lead[-1]), name).reshape(lead)

    def scatter_layer(i):
        gs = grads_sh[i]
        conv = jnp.zeros((N_DEV, 8, cw_pad), F32)
        conv = conv.at[:, 0:3, :n_up].set(jnp.moveaxis(gs["ffn_conv_w"].reshape(3, N_DEV, n_up), 1, 0))
        conv = conv.at[:, 3:6, :n_sc].set(jnp.moveaxis(gs["shortconv_w"].reshape(3, N_DEV, n_sc), 1, 0))
        srcs = [gs["w_in"], gs["w_in"], gs["w_up"], gs["w_down"], jnp.stack([gs["w_out"], gs["w_ple_gate"]]),
                jnp.stack([gs["w_br_fox"], gs["w_br_conv"], gs["w_br_sgu"]]),
                jnp.stack([gs["w_br_dil"], gs["w_ple_proj"]]), conv]
        stacked = lambda shape, dt=BF16: SDS((N_DEV,) + tuple(shape), dt)
        shapes = [stacked((D, WB * LANES)), stacked((D, LANES)), stacked((D, n_up)), stacked((r_down, D)),
                  stacked((2, r_sq, D)), stacked((3, w_br_fox.shape[1], n_br)), stacked((2, w_br_dil.shape[1], n_sm)),
                  stacked((8, cw_pad), F32)]
        return _comm("comm_scatter_grads", srcs, shapes, [in_window, in_flogit] + placed + [slot], [slot] * 8)

    grads = {n: [None] * depth for n in SHARDED}
    for i in range(depth):
        gw, gf, gup, gdown, gsq, gbr, gsm, gcv = scatter_layer(i)
        win = jnp.concatenate([sum_sources(gw, "sum_w_in"), sum_sources(gf, "sum_w_in_flogit")], axis=1)
        grads["w_in"][i] = on_my_index(from_window, win)
        grads["w_up"][i] = sum_sources(gup, "sum_w_up")
        grads["w_down"][i] = sum_sources(gdown, "sum_w_down")
        sq, br, sm = sum_sources(gsq, "sum_sq"), sum_sources(gbr, "sum_br"), sum_sources(gsm, "sum_sm")
        cv = sum_sources(gcv, "sum_conv")
        grads["w_out"][i], grads["w_ple_gate"][i] = sq[0], sq[1]
        grads["w_br_fox"][i], grads["w_br_conv"][i], grads["w_br_sgu"][i] = br[0], br[1], br[2]
        grads["w_br_dil"][i], grads["w_ple_proj"][i] = sm[0], sm[1]
        grads["ffn_conv_w"][i], grads["shortconv_w"][i] = cv[0:3, :n_up], cv[3:6, :n_sc]
    grads = {n: jnp.stack(v, axis=0) for n, v in grads.items()}

    rep_parts = [jnp.stack([grads_rep[i][n] for i in range(depth)], axis=0).reshape(-1) for n in REPLICATED]
    rep_parts.append(dg_final.reshape(-1))
    rep_sizes = [int(r.shape[0]) for r in rep_parts]
    rep_offs = [sum(rep_sizes[:i]) for i in range(len(rep_sizes) + 1)]
    rep = _sum8(_exchange(_pack(rep_parts, F32, 8), True, "comm_gather_small"), "sum_small").reshape(-1)
    for n, a, b in zip(REPLICATED + ("final_norm_g",), rep_offs[:-1], rep_offs[1:]):
        grads[n] = rep[a:b].reshape(W[n].shape)

    deltas, new_m, new_v = {}, {}, {}
    for n in WEIGHTS:
        deltas[n], new_m[n], new_v[n] = _adamw(W[n], grads[n], given["m_" + n], given["v_" + n], "adamw_" + n)
    return (loss, grad_x, *[grads[n] for n in WEIGHTS], *[deltas[n] for n in WEIGHTS],
            *[new_m[n] for n in WEIGHTS], *[new_v[n] for n in WEIGHTS])
```

```python
import functools

import jax
import jax.numpy as jnp
from jax import lax
from jax.experimental import pallas as pl
from jax.experimental.pallas import tpu as pltpu

F32 = jnp.float32
BF16 = jnp.bfloat16
EPS = 1e-6
NEG = -1e30
HEAD_DIM = 64
SGU_CHUNK = 128
DIL_PATTERNS = ((128, 1), (512, 4), (2048, 16))
DIL_SPAN = 128
ROPE_THETA = 500000.0
ROPE_DIM = 16
N_DEV = 8
LANES = 128
VMEM_LIMIT = 56 * 1024 * 1024
ADAM_LR, ADAM_B1, ADAM_B2, ADAM_EPS, ADAM_WD, ADAM_STEP = 0.001, 0.9, 0.999, 1e-08, 0.01, 10
MESH_AXES = ("x", "y", "c")

BS = pl.BlockSpec
SDS = jax.ShapeDtypeStruct


def _cp(*sem):
    return pltpu.CompilerParams(dimension_semantics=sem, vmem_limit_bytes=VMEM_LIMIT)


def _tile(dim, pref, unit=LANES):
    if dim % unit:
        return dim
    best, t = unit, unit
    while t <= min(dim, pref):
        if dim % t == 0:
            best = t
        t += unit
    return best


def _sigmoid(z):
    return 1.0 / (1.0 + jnp.exp(-z))


def _exchange(x, gather, name):
    shape = x.shape if not gather else (N_DEV,) + x.shape

    def body(x_ref, o_ref, send_sems, recv_sems, local_sem):
        ix, iy, ic = lax.axis_index("x"), lax.axis_index("y"), lax.axis_index("c")
        me = 4 * ix + 2 * iy + ic

        def src(j):
            return x_ref if gather else x_ref.at[j]

        local = pltpu.make_async_copy(src(me), o_ref.at[me], local_sem)
        local.start()
        sends, recvs = [], []
        for r in range(1, N_DEV):
            px = 1 - ix if (r >> 2) & 1 else ix
            py = 1 - iy if (r >> 1) & 1 else iy
            pc = 1 - ic if r & 1 else ic
            peer = 4 * px + 2 * py + pc
            sends.append(pltpu.make_async_remote_copy(
                src_ref=src(peer), dst_ref=o_ref.at[me], send_sem=send_sems.at[r - 1], recv_sem=recv_sems.at[r - 1],
                device_id=(px, py, pc), device_id_type=pl.DeviceIdType.MESH))
            recvs.append(pltpu.make_async_remote_copy(
                src_ref=src(peer), dst_ref=o_ref.at[peer], send_sem=send_sems.at[r - 1], recv_sem=recv_sems.at[r - 1],
                device_id=(px, py, pc), device_id_type=pl.DeviceIdType.MESH))
        for cp in sends:
            cp.start()
        for cp in recvs:
            cp.wait_recv()
        for cp in sends:
            cp.wait_send()
        local.wait()

    return pl.pallas_call(
        body, name=name, out_shape=SDS(shape, x.dtype),
        in_specs=[BS(memory_space=pl.ANY)], out_specs=BS(memory_space=pl.ANY),
        scratch_shapes=[pltpu.SemaphoreType.DMA((N_DEV - 1,)), pltpu.SemaphoreType.DMA((N_DEV - 1,)),
                        pltpu.SemaphoreType.DMA],
    )(x)


def _comm(name, srcs, out_shapes, src_views, dst_views):
    n = len(srcs)

    def body(*refs):
        copies = _comm_copies(refs[:n], refs[n:2 * n], refs[2 * n:], src_views, dst_views)
        _comm_start(copies)
        _comm_wait(copies)

    anyspec = BS(memory_space=pl.ANY)
    return pl.pallas_call(
        body, name=name, out_shape=tuple(out_shapes), in_specs=[anyspec] * n, out_specs=(anyspec,) * n,
        scratch_shapes=_comm_sems(n))(*srcs)


def _comm_sems(n):
    return [pltpu.SemaphoreType.DMA((N_DEV - 1, n)), pltpu.SemaphoreType.DMA((N_DEV - 1, n)),
            pltpu.SemaphoreType.DMA((n,))]


def _comm_copies(s_refs, o_refs, sems, src_views, dst_views):
    send_sems, recv_sems, local_sems = sems
    n = len(s_refs)
    ix, iy, ic = lax.axis_index("x"), lax.axis_index("y"), lax.axis_index("c")
    me = 4 * ix + 2 * iy + ic
    local = [pltpu.make_async_copy(src_views[k](s_refs[k], me), dst_views[k](o_refs[k], me), local_sems.at[k])
             for k in range(n)]
    sends, recvs = [], []
    for r in range(1, N_DEV):
        px = 1 - ix if (r >> 2) & 1 else ix
        py = 1 - iy if (r >> 1) & 1 else iy
        pc = 1 - ic if r & 1 else ic
        peer = 4 * px + 2 * py + pc
        for k in range(n):
            common = dict(send_sem=send_sems.at[r - 1, k], recv_sem=recv_sems.at[r - 1, k],
                          device_id=(px, py, pc), device_id_type=pl.DeviceIdType.MESH)
            sends.append(pltpu.make_async_remote_copy(
                src_ref=src_views[k](s_refs[k], peer), dst_ref=dst_views[k](o_refs[k], me), **common))
            recvs.append(pltpu.make_async_remote_copy(
                src_ref=src_views[k](s_refs[k], peer), dst_ref=dst_views[k](o_refs[k], peer), **common))
    return local, sends, recvs


def _comm_start(copies):
    local, sends, _ = copies
    for cp in local + sends:
        cp.start()


def _comm_wait(copies):
    local, sends, recvs = copies
    for cp in recvs:
        cp.wait_recv()
    for cp in sends:
        cp.wait_send()
    for cp in local:
        cp.wait()


def _call(body, *, name, grid, in_specs, out_specs, out_shape, args, scratch_shapes=(), sem=None, comm=None):
    single = not isinstance(out_shape, (tuple, list))
    out_shape = (out_shape,) if single else tuple(out_shape)
    out_specs = (out_specs,) if single else tuple(out_specs)
    if comm is None:
        outs = pl.pallas_call(body, name=name, grid=grid, in_specs=list(in_specs), out_specs=out_specs,
                              out_shape=out_shape, scratch_shapes=list(scratch_shapes),
                              compiler_params=_cp(*sem))(*args)
        return outs[0] if single else outs
    srcs, shapes, src_views, dst_views = comm
    n, ni, no, ns = len(srcs), len(in_specs), len(out_shape), len(scratch_shapes)

    def hosted(*refs):
        ins, c_in = refs[:ni], refs[ni:ni + n]
        outs, c_out = refs[ni + n:ni + n + no], refs[ni + n + no:ni + 2 * n + no]
        scr, sems = refs[ni + 2 * n + no:ni + 2 * n + no + ns], refs[ni + 2 * n + no + ns:]
        ids = [pl.program_id(a) for a in range(len(grid))]
        first = functools.reduce(jnp.logical_and, [i == 0 for i in ids])
        last = functools.reduce(jnp.logical_and, [i == g - 1 for i, g in zip(ids, grid)])

        @pl.when(first)
        def _():
            _comm_start(_comm_copies(c_in, c_out, sems, src_views, dst_views))

        body(*ins, *outs, *scr)

        @pl.when(last)
        def _():
            _comm_wait(_comm_copies(c_in, c_out, sems, src_views, dst_views))

    anyspec = BS(memory_space=pl.ANY)
    outs = pl.pallas_call(
        hosted, name=name, grid=grid, in_specs=list(in_specs) + [anyspec] * n,
        out_specs=out_specs + (anyspec,) * n, out_shape=out_shape + tuple(shapes),
        scratch_shapes=list(scratch_shapes) + _comm_sems(n),
        compiler_params=_cp(*(["arbitrary"] * len(grid))))(*args, *srcs)
    return (outs[0] if single else outs[:no]), outs[no:]


def _assemble(pieces, starts, wb, npb, name):
    _, K, pw = pieces.shape
    tr = _tile(K, 256, 16)

    def body(p_ref, o_ref):
        for b in range(npb - 1):
            acc = None
            for j in range(N_DEV):
                if starts[j] <= b < starts[j] + wb:
                    blk = p_ref[j, :, (b - starts[j]) * LANES:(b - starts[j] + 1) * LANES]
                    acc = blk if acc is None else acc + blk
            o_ref[:, b * LANES:(b + 1) * LANES] = acc
        acc = p_ref[0, :, wb * LANES:]
        for j in range(1, N_DEV):
            acc = acc + p_ref[j, :, wb * LANES:]
        o_ref[:, (npb - 1) * LANES:] = acc

    return pl.pallas_call(
        body, name=name, grid=(K // tr,), out_shape=SDS((K, npb * LANES), pieces.dtype),
        in_specs=[BS((N_DEV, tr, pw), lambda i: (0, i, 0))], out_specs=BS((tr, npb * LANES), lambda i: (i, 0)),
        compiler_params=_cp("parallel"))(pieces)


def _sum8(x, name):
    _, R, C = x.shape
    tr = _tile(R, 256, 16)

    def body(x_ref, o_ref):
        acc = x_ref[0].astype(F32)
        for j in range(1, N_DEV):
            acc = acc + x_ref[j].astype(F32)
        o_ref[...] = acc

    return pl.pallas_call(
        body, name=name, grid=(R // tr,), out_shape=SDS((R, C), F32),
        in_specs=[BS((N_DEV, tr, C), lambda i: (0, i, 0))], out_specs=BS((tr, C), lambda i: (i, 0)),
        compiler_params=_cp("parallel"))(x)


_DIMS = {"nn": ((1,), (0,)), "nt": ((1,), (1,)), "tn": ((0,), (0,))}


def _mm(a, b, mode, out_dtype, name, res=None, comm=None):
    if mode == "nn":
        (M, K), N = a.shape, b.shape[1]
    elif mode == "nt":
        (M, K), N = a.shape, b.shape[0]
    else:
        (K, M), N = a.shape, b.shape[1]
    tm, tn, tk = _tile(M, 1024), _tile(N, 1024), _tile(K, 1024)
    nk = K // tk
    a_spec = BS((tk, tm), lambda i, j, k: (k, i)) if mode == "tn" else BS((tm, tk), lambda i, j, k: (i, k))
    b_spec = BS((tn, tk), lambda i, j, k: (j, k)) if mode == "nt" else BS((tk, tn), lambda i, j, k: (k, j))
    o_spec = BS((tm, tn), lambda i, j, k: (i, j))
    dims = (_DIMS[mode], ((), ()))

    def body(a_ref, b_ref, *rest):
        o_ref, acc = rest[-2], rest[-1]
        k = pl.program_id(2)

        @pl.when(k == 0)
        def _():
            acc[...] = jnp.zeros_like(acc)

        acc[...] += lax.dot_general(a_ref[...].astype(BF16), b_ref[...].astype(BF16), dims,
                                    preferred_element_type=F32)

        @pl.when(k == nk - 1)
        def _():
            r = acc[...]
            if res is not None:
                r = r + rest[0][...]
            o_ref[...] = r.astype(o_ref.dtype)

    ins, specs = [a, b], [a_spec, b_spec]
    if res is not None:
        ins.append(res)
        specs.append(o_spec)
    return _call(body, name=name, grid=(M // tm, N // tn, nk), out_shape=SDS((M, N), out_dtype), in_specs=specs,
                 out_specs=o_spec, scratch_shapes=[pltpu.VMEM((tm, tn), F32)],
                 sem=("parallel", "parallel", "arbitrary"), args=ins, comm=comm)


def _rms_fwd(x, g, name):
    S, D = x.shape
    ts = _tile(S, 256, 8)

    def body(x_ref, g_ref, o_ref):
        xv = x_ref[...]
        rstd = lax.rsqrt(jnp.mean(xv * xv, axis=-1, keepdims=True) + EPS)
        o_ref[...] = (xv * rstd * g_ref[...]).astype(o_ref.dtype)

    return pl.pallas_call(
        body, name=name, grid=(S // ts,), out_shape=SDS((S, D), BF16),
        in_specs=[BS((ts, D), lambda i: (i, 0)), BS((1, D), lambda i: (0, 0))],
        out_specs=BS((ts, D), lambda i: (i, 0)), compiler_params=_cp("parallel"))(x, g.reshape(1, D))


def _rms_bwd(x, g, dh, dres, name):
    S, D = x.shape
    ts = _tile(S, 256, 8)

    def body(x_ref, g_ref, dh_ref, dres_ref, dx_ref, dxb_ref, dg_ref):
        xv = x_ref[...]
        rstd = lax.rsqrt(jnp.mean(xv * xv, axis=-1, keepdims=True) + EPS)
        xn = xv * rstd
        dhv = dh_ref[...]

        @pl.when(pl.program_id(0) == 0)
        def _():
            dg_ref[...] = jnp.zeros_like(dg_ref)

        dg_ref[...] += jnp.sum(dhv * xn, axis=0, keepdims=True)
        dhn = dhv * g_ref[...]
        dxv = rstd * (dhn - xn * jnp.mean(dhn * xn, axis=-1, keepdims=True)) + dres_ref[...]
        dx_ref[...] = dxv
        dxb_ref[...] = dxv.astype(BF16)

    row = BS((ts, D), lambda i: (i, 0))
    vec = BS((1, D), lambda i: (0, 0))
    return pl.pallas_call(
        body, name=name, grid=(S // ts,), out_shape=(SDS((S, D), F32), SDS((S, D), BF16), SDS((1, D), F32)),
        in_specs=[row, vec, row, row], out_specs=(row, row, vec), compiler_params=_cp("arbitrary"),
    )(x, g.reshape(1, D), dh, dres)


def _final_loss(x, g, target, name):
    S, D = x.shape
    ts = _tile(S, 256, 8)

    def body(x_ref, g_ref, t_ref, loss_ref, dx_ref, dxb_ref, dg_ref):
        xv = x_ref[...]
        rstd = lax.rsqrt(jnp.mean(xv * xv, axis=-1, keepdims=True) + EPS)
        xn = xv * rstd
        gv = g_ref[...]
        e = xn * gv - t_ref[...]

        @pl.when(pl.program_id(0) == 0)
        def _():
            dg_ref[...] = jnp.zeros_like(dg_ref)
            loss_ref[...] = jnp.zeros_like(loss_ref)

        loss_ref[...] += 0.5 * jnp.sum(jnp.mean(e * e, axis=-1, keepdims=True), axis=0, keepdims=True)
        dy = e * (1.0 / D)
        dg_ref[...] += jnp.sum(dy * xn, axis=0, keepdims=True)
        dhn = dy * gv
        dxv = rstd * (dhn - xn * jnp.mean(dhn * xn, axis=-1, keepdims=True))
        dx_ref[...] = dxv
        dxb_ref[...] = dxv.astype(BF16)

    row = BS((ts, D), lambda i: (i, 0))
    vec = BS((1, D), lambda i: (0, 0))
    return pl.pallas_call(
        body, name=name, grid=(S // ts,),
        out_shape=(SDS((1, LANES), F32), SDS((S, D), F32), SDS((S, D), BF16), SDS((1, D), F32)),
        in_specs=[row, vec, row], out_specs=(BS((1, LANES), lambda i: (0, 0)), row, row, vec),
        compiler_params=_cp("arbitrary"))(x, g.reshape(1, D), target)


def _shift_down(z, k):
    rows = lax.broadcasted_iota(jnp.int32, z.shape, 0)
    return jnp.where(rows >= k, pltpu.roll(z, k, 0), 0.0)


def _shift_up(z, k):
    n = z.shape[0]
    rows = lax.broadcasted_iota(jnp.int32, z.shape, 0)
    return jnp.where(rows < n - k, pltpu.roll(z, n - k, 0), 0.0)


def _cumsum_rows(z, reverse=False):
    n = z.shape[0]
    k = 1
    while k < n:
        z = z + (_shift_up(z, k) if reverse else _shift_down(z, k))
        k *= 2
    return z


def _conv3(z, w_ref):
    return w_ref[0:1, :] * _shift_down(z, 2) + w_ref[1:2, :] * _shift_down(z, 1) + w_ref[2:3, :] * z


def _conv3_bwd(z, dy, w_ref, dw_ref):
    dw_ref[0:1, :] = jnp.sum(dy * _shift_down(z, 2), axis=0, keepdims=True)
    dw_ref[1:2, :] = jnp.sum(dy * _shift_down(z, 1), axis=0, keepdims=True)
    dw_ref[2:3, :] = jnp.sum(dy * z, axis=0, keepdims=True)
    return w_ref[2:3, :] * dy + w_ref[1:2, :] * _shift_up(dy, 1) + w_ref[0:1, :] * _shift_up(dy, 2)


def _fox_prep(proj, bias, col, name):
    S = proj.shape[0]

    def body(a_ref, b_ref, f_ref):
        z = a_ref[...] + b_ref[...]
        logf = jnp.minimum(z, 0.0) - jnp.log(1.0 + jnp.exp(-jnp.abs(z)))
        f_ref[...] = _cumsum_rows(logf)

    return pl.pallas_call(
        body, name=name, grid=(1,), out_shape=SDS((S, LANES), F32),
        in_specs=[BS((S, LANES), lambda i: (0, col)), BS((1, LANES), lambda i: (0, 0))],
        out_specs=BS((S, LANES), lambda i: (0, 0)), compiler_params=_cp("arbitrary"))(proj, bias)


def _fox_prep_bwd(proj, bias, col, df, name):
    S = proj.shape[0]

    def body(a_ref, b_ref, df_ref, da_ref, db_ref):
        z = a_ref[...] + b_ref[...]
        da = _cumsum_rows(df_ref[...], reverse=True) * _sigmoid(-z)
        da_ref[...] = da.astype(BF16)
        db_ref[...] = jnp.sum(da, axis=0, keepdims=True)

    full = BS((S, LANES), lambda i: (0, 0))
    vec = BS((1, LANES), lambda i: (0, 0))
    return pl.pallas_call(
        body, name=name, grid=(1,), out_shape=(SDS((S, LANES), BF16), SDS((1, LANES), F32)),
        in_specs=[BS((S, LANES), lambda i: (0, col)), vec, full], out_specs=(full, vec),
        compiler_params=_cp("arbitrary"))(proj, bias, df)


def _fox_scores(qs, kv, fqv, fkv, i, j, t, masked):
    s = lax.dot_general(qs, kv, (_DIMS["nt"], ((), ())), preferred_element_type=F32) + fqv - fkv
    if not masked:
        return s
    row = i * t + lax.broadcasted_iota(jnp.int32, (t, t), 0)
    col = j * t + lax.broadcasted_iota(jnp.int32, (t, t), 1)
    return jnp.where(col <= row, s, NEG)


def _fox_fwd(q, k, v, fq, fk, name, comm=None):
    H, S, Dh = q.shape
    t = _tile(S, 256)
    n = S // t
    scale = Dh ** -0.5

    def body(q_ref, k_ref, v_ref, fq_ref, fk_ref, o_ref, lse_ref):
        i = pl.program_id(1)
        qs, fqv = q_ref[0] * scale, fq_ref[0]

        def step(masked, j, carry):
            m, l, acc = carry
            s = _fox_scores(qs, k_ref[0, j], fqv, fk_ref[0, j], i, j, t, masked)
            mn = jnp.maximum(m, jnp.max(s, axis=-1, keepdims=True))
            al = jnp.exp(m - mn)
            p = jnp.exp(s - mn)
            l = l * al + jnp.sum(p, axis=-1, keepdims=True)
            acc = acc * al + jnp.dot(p.astype(BF16), v_ref[0, j], preferred_element_type=F32)
            return mn, l, acc

        init = (jnp.full((t, 1), NEG, F32), jnp.zeros((t, 1), F32), jnp.zeros((t, Dh), F32))
        m, l, acc = step(True, i, lax.fori_loop(0, i, functools.partial(step, False), init))
        o_ref[0] = acc / l
        lse_ref[0] = m + jnp.log(l)

    blk = BS((1, t, Dh), lambda h, i: (h, i, 0))
    col = BS((1, t, 1), lambda h, i: (h, i, 0))
    res = BS((1, n, t, Dh), lambda h, i: (h, 0, 0, 0))
    return _call(
        body, name=name, grid=(H, n), out_shape=(SDS((H, S, Dh), F32), SDS((H, S, 1), F32)),
        in_specs=[blk, res, res, col, BS((1, n, 1, t), lambda h, i: (h, 0, 0, 0))], out_specs=(blk, col),
        sem=("parallel", "arbitrary"), args=(q, k.reshape(H, n, t, Dh), v.reshape(H, n, t, Dh), fq, fk), comm=comm)


def _fox_bwd_dq(q, k, v, fq, fk, do, lse, name, comm=None):
    H, S, Dh = q.shape
    t = _tile(S, 256)
    n = S // t
    scale = Dh ** -0.5

    def body(q_ref, k_ref, v_ref, fq_ref, fk_ref, do_ref, lse_ref, dq_ref, dl_ref):
        i = pl.program_id(1)
        qs, fqv, dob, lsev = q_ref[0] * scale, fq_ref[0], do_ref[0], lse_ref[0]

        def p_dp(masked, j):
            p = jnp.exp(_fox_scores(qs, k_ref[0, j], fqv, fk_ref[0, j], i, j, t, masked) - lsev)
            return p, lax.dot_general(dob, v_ref[0, j], (_DIMS["nt"], ((), ())), preferred_element_type=F32)

        def sum_step(masked, j, delta):
            p, dp = p_dp(masked, j)
            return delta + jnp.sum(p * dp, axis=-1, keepdims=True)

        delta = sum_step(True, i, lax.fori_loop(0, i, functools.partial(sum_step, False), jnp.zeros((t, 1), F32)))

        def step(masked, j, dq):
            p, dp = p_dp(masked, j)
            return dq + jnp.dot((p * (dp - delta)).astype(BF16), k_ref[0, j], preferred_element_type=F32)

        dq = step(True, i, lax.fori_loop(0, i, functools.partial(step, False), jnp.zeros((t, Dh), F32)))
        dq_ref[0] = dq * scale
        dl_ref[0] = delta

    blk = BS((1, t, Dh), lambda h, i: (h, i, 0))
    col = BS((1, t, 1), lambda h, i: (h, i, 0))
    res = BS((1, n, t, Dh), lambda h, i: (h, 0, 0, 0))
    return _call(
        body, name=name, grid=(H, n), out_shape=(SDS((H, S, Dh), F32), SDS((H, S, 1), F32)),
        in_specs=[blk, res, res, col, BS((1, n, 1, t), lambda h, i: (h, 0, 0, 0)), blk, col],
        out_specs=(blk, col), sem=("parallel", "arbitrary"), comm=comm,
        args=(q, k.reshape(H, n, t, Dh), v.reshape(H, n, t, Dh), fq, fk, do, lse))


def _fox_bwd_dkv(q, k, v, fq, fk, do, lse, delta, name, comm=None):
    H, S, Dh = q.shape
    t = _tile(S, 256)
    n = S // t
    scale = Dh ** -0.5
    tn_dims = (_DIMS["tn"], ((), ()))

    def body(q_ref, k_ref, v_ref, fq_ref, fk_ref, do_ref, lse_ref, dl_ref, dk_ref, dv_ref, dfk_ref):
        j = pl.program_id(1)
        kv, vv, fkv = k_ref[0], v_ref[0], fk_ref[0, 0]

        def step(masked, i, carry):
            dk, dv, dfk = carry
            qs, dob = q_ref[0, i] * scale, do_ref[0, i]
            p = jnp.exp(_fox_scores(qs, kv, fq_ref[0, i], fkv, i, j, t, masked) - lse_ref[0, i])
            dv = dv + lax.dot_general(p.astype(BF16), dob, tn_dims, preferred_element_type=F32)
            dp = lax.dot_general(dob, vv, (_DIMS["nt"], ((), ())), preferred_element_type=F32)
            ds = p * (dp - dl_ref[0, i])
            dk = dk + lax.dot_general(ds.astype(BF16), qs, tn_dims, preferred_element_type=F32)
            return dk, dv, dfk - jnp.sum(ds, axis=0, keepdims=True)

        init = (jnp.zeros((t, Dh), F32), jnp.zeros((t, Dh), F32), jnp.zeros((1, t), F32))
        dk, dv, dfk = lax.fori_loop(j + 1, n, functools.partial(step, False), step(True, j, init))
        dk_ref[0] = dk
        dv_ref[0] = dv
        dfk_ref[0, 0] = dfk

    blk = BS((1, t, Dh), lambda h, j: (h, j, 0))
    res = BS((1, n, t, Dh), lambda h, j: (h, 0, 0, 0))
    rcol = BS((1, n, t, 1), lambda h, j: (h, 0, 0, 0))
    frow = BS((1, 1, 1, t), lambda h, j: (h, j, 0, 0))
    return _call(
        body, name=name, grid=(H, n),
        out_shape=(SDS((H, S, Dh), F32), SDS((H, S, Dh), F32), SDS((H, n, 1, t), F32)),
        in_specs=[res, blk, blk, rcol, frow, res, rcol, rcol], out_specs=(blk, blk, frow),
        sem=("parallel", "arbitrary"), comm=comm,
        args=(q.reshape(H, n, t, Dh), k, v, fq.reshape(H, n, t, 1), fk, do.astype(BF16).reshape(H, n, t, Dh),
              lse.reshape(H, n, t, 1), delta.reshape(H, n, t, 1)))


def _sconv_fwd(proj, w, col0, width, name):
    S = proj.shape[0]
    tc = LANES
    nb, c0 = width // tc, col0 // tc

    def body(x_ref, gb_ref, gc_ref, w_ref, o_ref):
        o_ref[...] = (gb_ref[...] * _conv3(gc_ref[...] * x_ref[...], w_ref)).astype(BF16)

    return pl.pallas_call(
        body, name=name, grid=(nb,), out_shape=SDS((S, width), BF16),
        in_specs=[BS((S, tc), lambda j: (0, c0 + j)), BS((S, tc), lambda j: (0, c0 + nb + j)),
                  BS((S, tc), lambda j: (0, c0 + 2 * nb + j)), BS((3, tc), lambda j: (0, j))],
        out_specs=BS((S, tc), lambda j: (0, j)), compiler_params=_cp("parallel"))(proj, proj, proj, w)


def _sconv_bwd(proj, w, col0, width, do, name):
    S = proj.shape[0]
    tc = LANES
    nb, c0 = width // tc, col0 // tc

    def body(x_ref, gb_ref, gc_ref, w_ref, do_ref, dx_ref, dgb_ref, dgc_ref, dw_ref):
        xb, gb, gc, dov = x_ref[...], gb_ref[...], gc_ref[...], do_ref[...]
        z = gc * xb
        dgb_ref[...] = (dov * _conv3(z, w_ref)).astype(BF16)
        dz = _conv3_bwd(z, dov * gb, w_ref, dw_ref)
        dgc_ref[...] = (dz * xb).astype(BF16)
        dx_ref[...] = (dz * gc).astype(BF16)

    out = BS((S, tc), lambda j: (0, j))
    wspec = BS((3, tc), lambda j: (0, j))
    return pl.pallas_call(
        body, name=name, grid=(nb,),
        out_shape=(SDS((S, width), BF16), SDS((S, width), BF16), SDS((S, width), BF16), SDS((3, width), F32)),
        in_specs=[BS((S, tc), lambda j: (0, c0 + j)), BS((S, tc), lambda j: (0, c0 + nb + j)),
                  BS((S, tc), lambda j: (0, c0 + 2 * nb + j)), wspec, out],
        out_specs=(out, out, out, wspec), compiler_params=_cp("parallel"))(proj, proj, proj, w, do)


def _gelu(x):
    return 0.5 * x * (1.0 + jnp.tanh(0.7978845608028654 * (x + 0.044715 * x * x * x)))


def _gelu_grad(x):
    u = 0.7978845608028654 * (x + 0.044715 * x * x * x)
    th = jnp.tanh(u)
    return 0.5 * (1.0 + th) + 0.5 * x * (1.0 - th * th) * 0.7978845608028654 * (1.0 + 3.0 * 0.044715 * x * x)


def _tril_w(w_ref, g):
    r = lax.broadcasted_iota(jnp.int32, (SGU_CHUNK, SGU_CHUNK), 0)
    c = lax.broadcasted_iota(jnp.int32, (SGU_CHUNK, SGU_CHUNK), 1)
    return jnp.where(c <= r, w_ref[g], 0.0), c <= r


def _sgu_mixed(vn, w_ref, b_ref, nch, G):
    rows = []
    for ch in range(nch):
        cols = []
        for g in range(G):
            wt, _ = _tril_w(w_ref, g)
            blk = vn[ch * SGU_CHUNK:(ch + 1) * SGU_CHUNK, g * LANES:(g + 1) * LANES]
            cols.append(jnp.dot(wt.astype(BF16), blk.astype(BF16), preferred_element_type=F32) + b_ref[g])
        rows.append(jnp.concatenate(cols, axis=1))
    return jnp.concatenate(rows, axis=0)


def _sgu_fwd(proj, gn, w, b, col0, width, name):
    S = proj.shape[0]
    G = w.shape[0]
    ts = _tile(S, 512)
    nch = ts // SGU_CHUNK
    c0 = col0 // width

    def body(u_ref, v_ref, gn_ref, w_ref, b_ref, o_ref):
        cv = _gelu(v_ref[...])
        rstd = lax.rsqrt(jnp.mean(cv * cv, axis=-1, keepdims=True) + EPS)
        mixed = _sgu_mixed(cv * rstd * gn_ref[...], w_ref, b_ref, nch, G)
        o_ref[...] = (_gelu(u_ref[...]) * mixed).astype(BF16)

    return pl.pallas_call(
        body, name=name, grid=(S // ts,), out_shape=SDS((S, width), BF16),
        in_specs=[BS((ts, width), lambda i: (i, c0)), BS((ts, width), lambda i: (i, c0 + 1)),
                  BS((1, width), lambda i: (0, 0)), BS(w.shape, lambda i: (0, 0, 0)), BS(b.shape, lambda i: (0, 0, 0))],
        out_specs=BS((ts, width), lambda i: (i, 0)), compiler_params=_cp("parallel"))(proj, proj, gn, w, b)


def _sgu_bwd(proj, gn, w, b, col0, width, do, name):
    S = proj.shape[0]
    G = w.shape[0]
    ts = _tile(S, 512)
    nch = ts // SGU_CHUNK
    c0 = col0 // width

    def body(u_ref, v_ref, gn_ref, w_ref, b_ref, do_ref, du_ref, dv_ref, dw_ref, db_ref, dgn_ref):
        uin, vin, dov, gnv = u_ref[...], v_ref[...], do_ref[...], gn_ref[...]
        cu, cv = _gelu(uin), _gelu(vin)
        rstd = lax.rsqrt(jnp.mean(cv * cv, axis=-1, keepdims=True) + EPS)
        vhat = cv * rstd
        vn = vhat * gnv
        mixed = _sgu_mixed(vn, w_ref, b_ref, nch, G)
        du_ref[...] = (dov * mixed * _gelu_grad(uin)).astype(BF16)
        dmix = dov * cu

        @pl.when(pl.program_id(0) == 0)
        def _():
            dw_ref[...] = jnp.zeros_like(dw_ref)
            db_ref[...] = jnp.zeros_like(db_ref)
            dgn_ref[...] = jnp.zeros_like(dgn_ref)

        rows = []
        for ch in range(nch):
            cols = []
            for g in range(G):
                wt, mask = _tril_w(w_ref, g)
                sl = (slice(ch * SGU_CHUNK, (ch + 1) * SGU_CHUNK), slice(g * LANES, (g + 1) * LANES))
                dm, vb = dmix[sl], vn[sl].astype(BF16)
                db_ref[g] += jnp.sum(dm, axis=-1, keepdims=True)
                dwg = lax.dot_general(dm.astype(BF16), vb, (_DIMS["nt"], ((), ())), preferred_element_type=F32)
                dw_ref[g] += jnp.where(mask, dwg, 0.0)
                cols.append(lax.dot_general(wt.astype(BF16), dm.astype(BF16), (_DIMS["tn"], ((), ())),
                                            preferred_element_type=F32))
            rows.append(jnp.concatenate(cols, axis=1))
        dvn = jnp.concatenate(rows, axis=0)
        dgn_ref[...] += jnp.sum(dvn * vhat, axis=0, keepdims=True)
        dvh = dvn * gnv
        dcv = rstd * (dvh - vhat * jnp.mean(dvh * vhat, axis=-1, keepdims=True))
        dv_ref[...] = (dcv * _gelu_grad(vin)).astype(BF16)

    row = BS((ts, width), lambda i: (i, 0))
    wsp, bsp, gsp = BS(w.shape, lambda i: (0, 0, 0)), BS(b.shape, lambda i: (0, 0, 0)), BS((1, width), lambda i: (0, 0))
    return pl.pallas_call(
        body, name=name, grid=(S // ts,),
        out_shape=(SDS((S, width), BF16), SDS((S, width), BF16), SDS(w.shape, F32), SDS(b.shape, F32),
                   SDS((1, width), F32)),
        in_specs=[BS((ts, width), lambda i: (i, c0)), BS((ts, width), lambda i: (i, c0 + 1)), gsp, wsp, bsp, row],
        out_specs=(row, row, wsp, bsp, gsp), compiler_params=_cp("arbitrary"))(proj, proj, gn, w, b, do)


def _rope_tables(positions, inv, name):
    S = positions.shape[0]
    ts = _tile(S, 512, 8)
    half = ROPE_DIM // 2

    def body(p_ref, inv_ref, c_ref, sa_ref, sb_ref):
        ang = p_ref[...].astype(F32) * inv_ref[...]
        lane = lax.broadcasted_iota(jnp.int32, (ts, LANES), 1) % HEAD_DIM
        sn = jnp.sin(ang)
        c_ref[...] = jnp.cos(ang)
        sa_ref[...] = jnp.where(lane < half, -sn, 0.0)
        sb_ref[...] = jnp.where((lane >= half) & (lane < ROPE_DIM), sn, 0.0)

    out = BS((ts, LANES), lambda i: (i, 0))
    return pl.pallas_call(
        body, name=name, grid=(S // ts,), out_shape=(SDS((S, LANES), F32),) * 3,
        in_specs=[BS((ts, 1), lambda i: (i, 0)), BS((1, LANES), lambda i: (0, 0))], out_specs=(out, out, out),
        compiler_params=_cp("parallel"))(positions, inv)


def _rope_apply(x, col0, width, tables, sign, out_dtype, name):
    S = x.shape[0]
    ts = _tile(S, 512, 8)
    tc = 2 * LANES
    nb, c0 = width // tc, col0 // tc
    half = ROPE_DIM // 2

    def body(x_ref, c_ref, sa_ref, sb_ref, o_ref):
        xv = x_ref[...].astype(F32)
        wide = lambda r: jnp.concatenate([r[...], r[...]], axis=1)
        y = (xv * wide(c_ref) + pltpu.roll(xv, tc - half, 1) * (sign * wide(sa_ref))
             + pltpu.roll(xv, half, 1) * (sign * wide(sb_ref)))
        o_ref[...] = y.astype(o_ref.dtype)

    tab = BS((ts, LANES), lambda i, j: (i, 0))
    return pl.pallas_call(
        body, name=name, grid=(S // ts, nb), out_shape=SDS((S, width), out_dtype),
        in_specs=[BS((ts, tc), lambda i, j: (i, c0 + j)), tab, tab, tab], out_specs=BS((ts, tc), lambda i, j: (i, j)),
        compiler_params=_cp("parallel", "parallel"))(x, *tables)


def _dil_masks(hh, m, bpcs, hpg):
    g = hh // hpg
    bpc = jnp.where(g == 0, bpcs[0], jnp.where(g == 1, bpcs[1], bpcs[2]))
    r = lax.broadcasted_iota(jnp.int32, (DIL_SPAN, DIL_SPAN), 0)
    c = lax.broadcasted_iota(jnp.int32, (DIL_SPAN, DIL_SPAN), 1)
    return (c >= r) & (lax.rem(m, bpc) != 0), c <= r


def _nt(a, b):
    return lax.dot_general(a, b, (_DIMS["nt"], ((), ())), preferred_element_type=F32)


def _tn(a, b):
    return lax.dot_general(a, b, (_DIMS["tn"], ((), ())), preferred_element_type=F32)


def _dil_fwd(q, k, v, bpcs, hpg, name, comm=None):
    HH, S, Dh = q.shape
    nb = S // DIL_SPAN
    scale = Dh ** -0.5

    def body(q_ref, kp_ref, kc_ref, vp_ref, vc_ref, o_ref, lse_ref):
        mp, mc = _dil_masks(pl.program_id(0), pl.program_id(1), bpcs, hpg)
        qv = q_ref[0]
        sp = jnp.where(mp, _nt(qv, kp_ref[0]) * scale, NEG)
        sc = jnp.where(mc, _nt(qv, kc_ref[0]) * scale, NEG)
        mx = jnp.maximum(jnp.max(sp, axis=-1, keepdims=True), jnp.max(sc, axis=-1, keepdims=True))
        pp, pc = jnp.exp(sp - mx), jnp.exp(sc - mx)
        l = jnp.sum(pp, axis=-1, keepdims=True) + jnp.sum(pc, axis=-1, keepdims=True)
        acc = (jnp.dot(pp.astype(BF16), vp_ref[0], preferred_element_type=F32)
               + jnp.dot(pc.astype(BF16), vc_ref[0], preferred_element_type=F32))
        o_ref[0] = acc / l
        lse_ref[0] = mx + jnp.log(l)

    cur = BS((1, DIL_SPAN, Dh), lambda h, m: (h, m, 0))
    prev = BS((1, DIL_SPAN, Dh), lambda h, m: (h, jnp.maximum(m - 1, 0), 0))
    return _call(
        body, name=name, grid=(HH, nb), out_shape=(SDS((HH, S, Dh), F32), SDS((HH, S, 1), F32)),
        in_specs=[cur, prev, cur, prev, cur], out_specs=(cur, BS((1, DIL_SPAN, 1), lambda h, m: (h, m, 0))),
        sem=("parallel", "parallel"), args=(q, k, k, v, v), comm=comm)


def _dil_bwd_dq(q, k, v, do, lse, coef, bpcs, hpg, name, comm=None):
    HH, S, Dh = q.shape
    nb = S // DIL_SPAN
    scale = Dh ** -0.5

    def body(q_ref, kp_ref, kc_ref, vp_ref, vc_ref, do_ref, lse_ref, cf_ref, dq_ref):
        mp, mc = _dil_masks(pl.program_id(0), pl.program_id(1), bpcs, hpg)
        qv, dob, lsev, cf = q_ref[0], do_ref[0], lse_ref[0], cf_ref[0]
        pp = jnp.exp(jnp.where(mp, _nt(qv, kp_ref[0]) * scale, NEG) - lsev)
        pc = jnp.exp(jnp.where(mc, _nt(qv, kc_ref[0]) * scale, NEG) - lsev)
        dsp = pp * (_nt(dob, vp_ref[0]) + cf)
        dsc = pc * (_nt(dob, vc_ref[0]) + cf)
        dq_ref[0] = (jnp.dot(dsp.astype(BF16), kp_ref[0], preferred_element_type=F32)
                     + jnp.dot(dsc.astype(BF16), kc_ref[0], preferred_element_type=F32)) * scale

    cur = BS((1, DIL_SPAN, Dh), lambda h, m: (h, m, 0))
    prev = BS((1, DIL_SPAN, Dh), lambda h, m: (h, jnp.maximum(m - 1, 0), 0))
    col = BS((1, DIL_SPAN, 1), lambda h, m: (h, m, 0))
    return _call(
        body, name=name, grid=(HH, nb), out_shape=SDS((HH, S, Dh), F32),
        in_specs=[cur, prev, cur, prev, cur, cur, col, col], out_specs=cur,
        sem=("parallel", "parallel"), args=(q, k, k, v, v, do, lse, coef), comm=comm)


def _dil_bwd_dkv(q, k, v, do, lse, coef, bpcs, hpg, name, comm=None):
    HH, S, Dh = q.shape
    nb = S // DIL_SPAN
    scale = Dh ** -0.5

    def body(k_ref, v_ref, qc_ref, qn_ref, doc_ref, don_ref, lc_ref, ln_ref, cc_ref, cn_ref, dk_ref, dv_ref):
        hh, m = pl.program_id(0), pl.program_id(1)
        _, mc = _dil_masks(hh, m, bpcs, hpg)
        mp, _ = _dil_masks(hh, m + 1, bpcs, hpg)
        mp = mp & (m + 1 < nb)
        kv, vv = k_ref[0], v_ref[0]
        pc = jnp.exp(jnp.where(mc, _nt(qc_ref[0], kv) * scale, NEG) - lc_ref[0])
        pn = jnp.exp(jnp.where(mp, _nt(qn_ref[0], kv) * scale, NEG) - ln_ref[0])
        dsc = pc * (_nt(doc_ref[0], vv) + cc_ref[0])
        dsn = pn * (_nt(don_ref[0], vv) + cn_ref[0])
        dv_ref[0] = _tn(pc.astype(BF16), doc_ref[0]) + _tn(pn.astype(BF16), don_ref[0])
        dk_ref[0] = (_tn(dsc.astype(BF16), qc_ref[0]) + _tn(dsn.astype(BF16), qn_ref[0])) * scale

    cur = BS((1, DIL_SPAN, Dh), lambda h, m: (h, m, 0))
    nxt = BS((1, DIL_SPAN, Dh), lambda h, m: (h, jnp.minimum(m + 1, nb - 1), 0))
    col = BS((1, DIL_SPAN, 1), lambda h, m: (h, m, 0))
    ncol = BS((1, DIL_SPAN, 1), lambda h, m: (h, jnp.minimum(m + 1, nb - 1), 0))
    return _call(
        body, name=name, grid=(HH, nb), out_shape=(SDS((HH, S, Dh), F32), SDS((HH, S, Dh), F32)),
        in_specs=[cur, cur, cur, nxt, cur, nxt, col, ncol, col, ncol], out_specs=(cur, cur),
        sem=("parallel", "parallel"), args=(k, v, q, q, do, do, lse, lse, coef, coef), comm=comm)


def _dil_merge(o3, lse3, name):
    G, H, S, Dh = o3.shape
    ts = _tile(S, 512, 8)

    def body(o_ref, l_ref, out_ref):
        lv = [l_ref[g, 0] for g in range(G)]
        mx = functools.reduce(jnp.maximum, lv)
        ex = [jnp.exp(v - mx) for v in lv]
        den = functools.reduce(lambda a, b: a + b, ex)
        out_ref[0] = functools.reduce(lambda a, b: a + b, [(ex[g] / den) * o_ref[g, 0] for g in range(G)])

    return pl.pallas_call(
        body, name=name, grid=(H, S // ts), out_shape=SDS((H, S, Dh), F32),
        in_specs=[BS((G, 1, ts, Dh), lambda h, i: (0, h, i, 0)), BS((G, 1, ts, 1), lambda h, i: (0, h, i, 0))],
        out_specs=BS((1, ts, Dh), lambda h, i: (h, i, 0)), compiler_params=_cp("parallel", "parallel"))(o3, lse3)


def _dil_merge_bwd(o3, lse3, do, name):
    G, H, S, Dh = o3.shape
    ts = _tile(S, 512, 8)

    def body(o_ref, l_ref, do_ref, do3_ref, cf_ref):
        lv = [l_ref[g, 0] for g in range(G)]
        mx = functools.reduce(jnp.maximum, lv)
        ex = [jnp.exp(v - mx) for v in lv]
        den = functools.reduce(lambda a, b: a + b, ex)
        wt = [e / den for e in ex]
        dov = do_ref[0]
        dw = [jnp.sum(dov * o_ref[g, 0], axis=-1, keepdims=True) for g in range(G)]
        mean = functools.reduce(lambda a, b: a + b, [wt[g] * dw[g] for g in range(G)])
        for g in range(G):
            do3_ref[g, 0] = (wt[g] * dov).astype(BF16)
            cf_ref[g, 0] = wt[g] * (dw[g] - mean) - wt[g] * dw[g]

    o_spec = BS((G, 1, ts, Dh), lambda h, i: (0, h, i, 0))
    l_spec = BS((G, 1, ts, 1), lambda h, i: (0, h, i, 0))
    return pl.pallas_call(
        body, name=name, grid=(H, S // ts), out_shape=(SDS((G, H, S, Dh), BF16), SDS((G, H, S, 1), F32)),
        in_specs=[o_spec, l_spec, BS((1, ts, Dh), lambda h, i: (h, i, 0))], out_specs=(o_spec, l_spec),
        compiler_params=_cp("parallel", "parallel"))(o3, lse3, do)


def _merge_fwd(proj, gate_col, os_, ws, name):
    S = proj.shape[0]
    D = ws[0].shape[1]
    nbr = len(os_)
    ts = _tile(S, 512, 8)
    tn = LANES * 2 if gate_col % (2 * LANES) == 0 and D % (2 * LANES) == 0 else LANES
    g0, gstep = gate_col // tn, D // tn

    def body(*refs):
        p_refs, o_refs, w_refs, out_ref = refs[:nbr], refs[nbr:2 * nbr], refs[2 * nbr:3 * nbr], refs[-1]
        acc = None
        for i in range(nbr):
            t = _sigmoid(p_refs[i][...]) * jnp.dot(o_refs[i][...], w_refs[i][...], preferred_element_type=F32)
            acc = t if acc is None else acc + t
        out_ref[...] = acc.astype(BF16)

    specs = ([BS((ts, tn), lambda s, j, i=i: (s, g0 + i * gstep + j)) for i in range(nbr)]
             + [BS((ts, o.shape[1]), lambda s, j: (s, 0)) for o in os_]
             + [BS((w.shape[0], tn), lambda s, j: (0, j)) for w in ws])
    return pl.pallas_call(
        body, name=name, grid=(S // ts, D // tn), out_shape=SDS((S, D), BF16), in_specs=specs,
        out_specs=BS((ts, tn), lambda s, j: (s, j)), compiler_params=_cp("parallel", "parallel"),
    )(*([proj] * nbr), *os_, *ws)


def _merge_bwd(proj, gate_col, os_, ws, dm, name):
    S = proj.shape[0]
    D = ws[0].shape[1]
    nbr = len(os_)
    ts = _tile(S, 512, 8)
    tn = LANES * 2 if gate_col % (2 * LANES) == 0 and D % (2 * LANES) == 0 else LANES
    g0, gstep = gate_col // tn, D // tn

    def body(*refs):
        p_refs, o_refs, w_refs = refs[:nbr], refs[nbr:2 * nbr], refs[2 * nbr:3 * nbr]
        dmv = refs[3 * nbr][...]
        dbo_refs, dgl_refs = refs[3 * nbr + 1:4 * nbr + 1], refs[4 * nbr + 1:]
        for i in range(nbr):
            gt = _sigmoid(p_refs[i][...])
            bo = jnp.dot(o_refs[i][...], w_refs[i][...], preferred_element_type=F32)
            dbo_refs[i][...] = (dmv * gt).astype(BF16)
            dgl_refs[i][...] = (dmv * bo * gt * (1.0 - gt)).astype(BF16)

    tile = BS((ts, tn), lambda s, j: (s, j))
    specs = ([BS((ts, tn), lambda s, j, i=i: (s, g0 + i * gstep + j)) for i in range(nbr)]
             + [BS((ts, o.shape[1]), lambda s, j: (s, 0)) for o in os_]
             + [BS((w.shape[0], tn), lambda s, j: (0, j)) for w in ws] + [tile])
    outs = pl.pallas_call(
        body, name=name, grid=(S // ts, D // tn), out_shape=(SDS((S, D), BF16),) * (2 * nbr), in_specs=specs,
        out_specs=(tile,) * (2 * nbr), compiler_params=_cp("parallel", "parallel"),
    )(*([proj] * nbr), *os_, *ws, dm)
    return outs[:nbr], outs[nbr:]


def _ffn_act(up, w, name):
    S, F2 = up.shape
    F = F2 // 2
    tc = LANES
    nb = F // tc

    def body(g_ref, v_ref, wg_ref, wv_ref, a_ref):
        ug = _conv3(g_ref[...], wg_ref)
        a_ref[...] = (ug * _sigmoid(ug) * _conv3(v_ref[...], wv_ref)).astype(BF16)

    return pl.pallas_call(
        body, name=name, grid=(nb,), out_shape=SDS((S, F), BF16),
        in_specs=[BS((S, tc), lambda j: (0, j)), BS((S, tc), lambda j: (0, nb + j)),
                  BS((3, tc), lambda j: (0, j)), BS((3, tc), lambda j: (0, nb + j))],
        out_specs=BS((S, tc), lambda j: (0, j)), compiler_params=_cp("parallel"))(up, up, w, w)


def _ffn_act_bwd(up, w, da, name):
    S, F2 = up.shape
    F = F2 // 2
    tc = LANES
    nb = F // tc

    def body(g_ref, v_ref, wg_ref, wv_ref, da_ref, dg_ref, dv_ref, dwg_ref, dwv_ref):
        pg, pv, dav = g_ref[...], v_ref[...], da_ref[...].astype(F32)
        ug, uv = _conv3(pg, wg_ref), _conv3(pv, wv_ref)
        sg = _sigmoid(ug)
        dg_ref[...] = _conv3_bwd(pg, dav * uv * (sg * (1.0 + ug * (1.0 - sg))), wg_ref, dwg_ref).astype(BF16)
        dv_ref[...] = _conv3_bwd(pv, dav * ug * sg, wv_ref, dwv_ref).astype(BF16)

    col = BS((S, tc), lambda j: (0, j))
    wsp = BS((3, tc), lambda j: (0, j))
    return pl.pallas_call(
        body, name=name, grid=(nb,),
        out_shape=(SDS((S, F), BF16), SDS((S, F), BF16), SDS((3, F), F32), SDS((3, F), F32)),
        in_specs=[col, BS((S, tc), lambda j: (0, nb + j)), wsp, BS((3, tc), lambda j: (0, nb + j)), col],
        out_specs=(col, col, wsp, wsp), compiler_params=_cp("parallel"))(up, up, w, w, da)


def _ple_fwd(x, pgl, pe, name):
    S, D = x.shape
    ts = _tile(S, 256, 8)

    def body(x_ref, g_ref, e_ref, o_ref):
        o_ref[...] = x_ref[...] + _sigmoid(g_ref[...]) * e_ref[...]

    row = BS((ts, D), lambda i: (i, 0))
    return pl.pallas_call(body, name=name, grid=(S // ts,), out_shape=SDS((S, D), F32), in_specs=[row] * 3,
                          out_specs=row, compiler_params=_cp("parallel"))(x, pgl, pe)


def _ple_bwd(dx, pgl, pe, name):
    S, D = dx.shape
    ts = _tile(S, 256, 8)

    def body(dx_ref, g_ref, e_ref, dg_ref, de_ref):
        dxv, sg = dx_ref[...], _sigmoid(g_ref[...])
        dg_ref[...] = (dxv * e_ref[...] * sg * (1.0 - sg)).astype(BF16)
        de_ref[...] = (dxv * sg).astype(BF16)

    row = BS((ts, D), lambda i: (i, 0))
    return pl.pallas_call(body, name=name, grid=(S // ts,), out_shape=(SDS((S, D), BF16),) * 2, in_specs=[row] * 3,
                          out_specs=(row, row), compiler_params=_cp("parallel"))(dx, pgl, pe)


def _adamw(w, g, m, v, name):
    shape = w.shape
    cols = shape[-1]
    rows = w.size // cols
    tr = _tile(rows, 256, 8)
    c1, c2 = 1.0 / (1.0 - ADAM_B1 ** ADAM_STEP), 1.0 / (1.0 - ADAM_B2 ** ADAM_STEP)

    def body(w_ref, g_ref, m_ref, v_ref, d_ref, nm_ref, nv_ref):
        gv = g_ref[...]
        nm = ADAM_B1 * m_ref[...] + (1.0 - ADAM_B1) * gv
        nv = ADAM_B2 * v_ref[...] + (1.0 - ADAM_B2) * (gv * gv)
        d_ref[...] = -ADAM_LR * ((nm * c1) / (jnp.sqrt(nv * c2) + ADAM_EPS) + ADAM_WD * w_ref[...])
        nm_ref[...] = nm
        nv_ref[...] = nv

    blk = BS((tr, cols), lambda i: (i, 0))
    outs = pl.pallas_call(
        body, name=name, grid=(rows // tr,), out_shape=(SDS((rows, cols), F32),) * 3, in_specs=[blk] * 4,
        out_specs=(blk,) * 3, compiler_params=_cp("parallel"),
    )(*(a.reshape(rows, cols) for a in (w, g, m, v)))
    return tuple(o.reshape(shape) for o in outs)


SHARDED = ("w_in", "shortconv_w", "w_br_fox", "w_br_conv", "w_br_sgu", "w_br_dil", "w_out", "w_up", "ffn_conv_w",
           "w_down", "w_ple_gate", "w_ple_proj")
REPLICATED = ("norm_mix_g", "fox_forget_b", "sgu_norm_g", "sgu_w", "sgu_b", "norm_ffn_g", "norm_ple_g")
WEIGHTS = ("norm_mix_g", "w_in", "fox_forget_b", "shortconv_w", "sgu_norm_g", "sgu_w", "sgu_b", "w_br_fox",
           "w_br_conv", "w_br_sgu", "w_br_dil", "w_out", "norm_ffn_g", "w_up", "ffn_conv_w", "w_down", "norm_ple_g",
           "w_ple_gate", "w_ple_proj", "final_norm_g")
PACK_COLS = 1024
PACK_ROWS = 256


def _pack(parts, dtype, unit):
    lead = parts[0].shape[:-1] if parts[0].ndim > 1 else ()
    flat = jnp.concatenate([p.astype(dtype) for p in parts], axis=-1)
    n = flat.shape[-1]
    rows = -(-n // (PACK_COLS * unit)) * unit
    flat = jnp.pad(flat, [(0, 0)] * len(lead) + [(0, rows * PACK_COLS - n)])
    return flat.reshape(lead + (rows, PACK_COLS))


def _to_heads(x, heads):
    S = x.shape[0]
    return x.reshape(S, heads, HEAD_DIM).transpose(1, 0, 2)


def _from_heads(x):
    H, S, Dh = x.shape
    return x.transpose(1, 0, 2).reshape(S, H * Dh)


def _to_residue(x, hpg):
    S = x.shape[0]
    gw = hpg * HEAD_DIM
    outs = []
    for g, (_, d) in enumerate(DIL_PATTERNS):
        seg = x[:, g * gw:(g + 1) * gw].reshape(S // d, d, hpg, HEAD_DIM)
        outs.append(seg.transpose(2, 1, 0, 3).reshape(hpg, S, HEAD_DIM))
    return jnp.concatenate(outs, axis=0)


def _from_residue(x, hpg):
    _, S, C = x.shape
    outs = []
    for g, (_, d) in enumerate(DIL_PATTERNS):
        seg = x[g * hpg:(g + 1) * hpg].reshape(hpg, d, S // d, C)
        outs.append(seg.transpose(0, 2, 1, 3).reshape(hpg, S, C))
    return jnp.stack(outs, axis=0)


def _groups_to_residue(x):
    G, hpg, S, C = x.shape
    outs = []
    for g, (_, d) in enumerate(DIL_PATTERNS):
        outs.append(x[g].reshape(hpg, S // d, d, C).transpose(0, 2, 1, 3).reshape(hpg, S, C))
    return jnp.concatenate(outs, axis=0)


def kernel(x, p, positions, norm_mix_g, w_in, fox_forget_b, shortconv_w, sgu_norm_g, sgu_w, sgu_b, w_br_fox, w_br_conv, w_br_sgu, w_br_dil, w_out, norm_ffn_g, w_up, ffn_conv_w, w_down, norm_ple_g, w_ple_gate, w_ple_proj, final_norm_g, loss_target, m_norm_mix_g, m_w_in, m_fox_forget_b, m_shortconv_w, m_sgu_norm_g, m_sgu_w, m_sgu_b, m_w_br_fox, m_w_br_conv, m_w_br_sgu, m_w_br_dil, m_w_out, m_norm_ffn_g, m_w_up, m_ffn_conv_w, m_w_down, m_norm_ple_g, m_w_ple_gate, m_w_ple_proj, m_final_norm_g, v_norm_mix_g, v_w_in, v_fox_forget_b, v_shortconv_w, v_sgu_norm_g, v_sgu_w, v_sgu_b, v_w_br_fox, v_w_br_conv, v_w_br_sgu, v_w_br_dil, v_w_out, v_norm_ffn_g, v_w_up, v_ffn_conv_w, v_w_down, v_norm_ple_g, v_w_ple_gate, v_w_ple_proj, v_final_norm_g):
    given = dict(locals())
    W = {n: given[n] for n in WEIGHTS}
    depth = w_in.shape[0]
    S, D = x.shape[1], x.shape[2]
    x0, target, pos = x[0], loss_target[0], positions[0].reshape(S, 1)
    FH = fox_forget_b.shape[1]
    FW = FH * HEAD_DIM
    CW = shortconv_w.shape[2] * N_DEV
    G = sgu_w.shape[1]
    SW = G * sgu_w.shape[2]
    DOUT = w_br_dil.shape[1]
    hpg = DOUT // HEAD_DIM
    NG = len(DIL_PATTERNS)
    DW = NG * DOUT
    bpcs = tuple(S // d // DIL_SPAN for _, d in DIL_PATTERNS)
    A0, B0 = 0, 3 * FW
    C0 = B0 + 3 * CW
    D0 = C0 + 2 * SW
    G0 = D0 + 3 * DW
    F0 = G0 + 4 * D
    NP = F0 + LANES
    orig = [3 * FW, FH, 3 * CW, 2 * SW, 3 * DW, 4 * D]
    o_off = [sum(orig[:i]) for i in range(len(orig) + 1)]

    me = 4 * lax.axis_index("x") + 2 * lax.axis_index("y") + lax.axis_index("c")
    n_in, o1, o2, d_in = w_in.shape[2], o_off[1], o_off[2], o_off[6]
    NPB = NP // LANES
    WB = -(-(LANES - 1 + n_in) // LANES)
    PW = (WB + 1) * LANES

    def a_of(j, clip=lambda v: min(max(v, 0), FH)):
        return n_in * j - clip(n_in * j - o1)

    def s_of(j):
        return min(a_of(j) // LANES, NPB - WB)

    def s_traced(j):
        return jnp.minimum(a_of(j, lambda v: jnp.clip(v, 0, FH)) // LANES, NPB - WB)

    def runs(j):
        lo, hi, out = n_in * j, n_in * (j + 1), []
        for a, b in ((0, o1), (o1, o2), (o2, d_in)):
            l, h = max(lo, a), min(hi, b)
            if l < h:
                dst = WB * LANES + l - o1 if a == o1 else l - (FH if l >= o2 else 0) - s_of(j) * LANES
                out.append((l - lo, h - l, dst))
        return out

    def to_window(j, shard):
        piece = jnp.zeros((shard.shape[0], PW), shard.dtype)
        for src, ln, dst in runs(j):
            piece = piece.at[:, dst:dst + ln].set(shard[:, src:src + ln])
        return piece

    def from_window(j, win):
        return jnp.concatenate([win[:, dst:dst + ln] for _, ln, dst in runs(j)], axis=1)

    def on_my_index(fn, arg):
        return lax.switch(me, [functools.partial(fn, j) for j in range(N_DEV)], arg)

    def place(axis, length):
        def view(ref, j):
            align = LANES if axis == len(ref.shape) - 1 else 16
            start = pl.multiple_of(j * length, align) if length % align == 0 else j * length
            return ref.at[tuple([slice(None)] * axis + [pl.ds(start, length)])]
        return view

    def slot(ref, j):
        return ref.at[j]

    def whole(ref, j):
        return ref

    n_up, r_down, r_sq = w_up.shape[2], w_down.shape[1], w_out.shape[1]
    n_br, n_sm, n_sc = w_br_fox.shape[2], w_br_dil.shape[2], shortconv_w.shape[2]
    FF2 = n_up * N_DEV
    cw_pad = -(-n_up // LANES) * LANES
    placed = [place(1, n_up), place(0, r_down), place(1, r_sq), place(2, n_br), place(2, n_sm)]

    def gather_plans(i):
        conv = jnp.zeros((8, cw_pad), F32).at[0:3, :n_up].set(ffn_conv_w[i]).at[3:6, :n_sc].set(shortconv_w[i])
        srcs = [on_my_index(to_window, w_in[i].astype(BF16)), w_up[i].astype(BF16), w_down[i].astype(BF16),
                jnp.stack([w_out[i], w_ple_gate[i]]).astype(BF16),
                jnp.stack([w_br_fox[i], w_br_conv[i], w_br_sgu[i]]).astype(BF16),
                jnp.stack([w_br_dil[i], w_ple_proj[i]]).astype(BF16), conv]
        shapes = [SDS((N_DEV, D, PW), BF16), SDS((D, FF2), BF16), SDS((r_down * N_DEV, D), BF16),
                  SDS((2, r_sq * N_DEV, D), BF16), SDS((3, w_br_fox.shape[1], n_br * N_DEV), BF16),
                  SDS((2, w_br_dil.shape[1], n_sm * N_DEV), BF16), SDS((N_DEV, 8, cw_pad), F32)]
        dsts = [slot] + placed + [slot]
        return [(srcs[a:b], shapes[a:b], [whole] * (b - a), dsts[a:b]) for a, b in ((0, 1), (1, 2), (2, 3), (3, 7))]

    def gathered(parts):
        (pieces,), (wup,), (wdown,), (sq, br, sm, cv) = parts
        return {
            "w_in": _assemble(pieces, [s_of(j) for j in range(N_DEV)], WB, NPB, "assemble_w_in"),
            "w_up": wup, "w_down": wdown, "w_out": sq[0], "w_ple_gate": sq[1],
            "w_br_fox": br[0], "w_br_conv": br[1], "w_br_sgu": br[2], "w_br_dil": sm[0], "w_ple_proj": sm[1],
            "ffn_conv_w": jnp.moveaxis(cv[:, 0:3, :n_up], 0, 1).reshape(3, FF2),
            "shortconv_w": jnp.moveaxis(cv[:, 3:6, :n_sc], 0, 1).reshape(3, CW),
        }

    def one_plan(plans):
        return tuple(sum((list(plan[k]) for plan in plans), []) for k in range(4))

    first = _comm("comm_gather_weights", *one_plan(gather_plans(0)))
    full = [gathered([first[0:1], first[1:2], first[2:3], first[3:7]])]

    inv = ROPE_THETA ** (-jnp.arange(ROPE_DIM // 2, dtype=F32) * (2.0 / ROPE_DIM))
    inv_row = jnp.tile(jnp.concatenate([inv, inv, jnp.zeros((HEAD_DIM - ROPE_DIM,), F32)]), LANES // HEAD_DIM)
    tables = _rope_tables(pos, inv_row.reshape(1, LANES), "rope_tables")

    def pad_lanes(v):
        return jnp.pad(v, ((0, 0), (0, LANES - v.shape[1])))

    saved = []
    xc = x0
    for i in range(depth):
        fw, sv = full[i], {}
        sv["x"] = xc
        nxt = gather_plans(i + 1) if i + 1 < depth else [None] * 4
        got = [None] * 4

        def host(k, result, nxt=nxt, got=got):
            if nxt[k] is None:
                return result
            got[k] = result[1]
            return result[0]

        h1 = _rms_fwd(xc, norm_mix_g[i], "rms_mix")
        proj = host(1, _mm(h1, fw["w_in"], "nn", F32, "mm_in", comm=nxt[1]))
        sv["h1"], sv["proj"] = h1, proj
        bias = pad_lanes(fox_forget_b[i].reshape(1, FH))
        Fc = _fox_prep(proj, bias, F0 // LANES, "fox_prep")
        t = _tile(S, 256)
        Ft = Fc[:, :FH].T
        fq, fk = Ft.reshape(FH, S, 1), Ft.reshape(FH, S // t, 1, t)
        qa, ka, va = (_to_heads(proj[:, A0 + j * FW:A0 + (j + 1) * FW], FH).astype(BF16) for j in range(3))
        oa, lse_a = host(0, _fox_fwd(qa, ka, va, fq, fk, "fox_fwd", comm=nxt[0]))
        oa_b = _from_heads(oa).astype(BF16)
        sv.update(bias=bias, fq=fq, fk=fk, qa=qa, ka=ka, va=va, oa=oa, lse_a=lse_a, oa_b=oa_b)
        ob = _sconv_fwd(proj, fw["shortconv_w"], B0, CW, "sconv_fwd")
        sgb = sgu_b[i].reshape(G, SGU_CHUNK, 1)
        oc = _sgu_fwd(proj, sgu_norm_g[i].reshape(1, SW), sgu_w[i], sgb, C0, SW, "sgu_fwd")
        qd = _rope_apply(proj, D0, DW, tables, 1.0, BF16, "rope_q")
        kd = _rope_apply(proj, D0 + DW, DW, tables, 1.0, BF16, "rope_k")
        qr, kr = _to_residue(qd, hpg), _to_residue(kd, hpg)
        vr = _to_residue(proj[:, D0 + 2 * DW:D0 + 3 * DW].astype(BF16), hpg)
        od_r, lse_r = host(2, _dil_fwd(qr, kr, vr, bpcs, hpg, "dil_fwd", comm=nxt[2]))
        o3, lse3 = _from_residue(od_r, hpg), _from_residue(lse_r, hpg)
        od = _from_heads(_dil_merge(o3, lse3, "dil_merge")).astype(BF16)
        sv.update(ob=ob, oc=oc, sgb=sgb, qr=qr, kr=kr, vr=vr, lse_r=lse_r, o3=o3, lse3=lse3, od=od)
        brs = [oa_b, ob, oc, od]
        wbr = [fw["w_br_fox"], fw["w_br_conv"], fw["w_br_sgu"], fw["w_br_dil"]]
        merged = _merge_fwd(proj, G0, brs, wbr, "merge_fwd")
        x2 = _mm(merged, fw["w_out"], "nn", F32, "mm_out", res=xc)
        h2 = _rms_fwd(x2, norm_ffn_g[i], "rms_ffn")
        up = host(3, _mm(h2, fw["w_up"], "nn", F32, "mm_up", comm=nxt[3]))
        if i + 1 < depth:
            full.append(gathered(got))
        act = _ffn_act(up, fw["ffn_conv_w"], "ffn_act")
        x3 = _mm(act, fw["w_down"], "nn", F32, "mm_down", res=x2)
        h3 = _rms_fwd(x3, norm_ple_g[i], "rms_ple")
        pgl = _mm(h3, fw["w_ple_gate"], "nn", F32, "mm_ple_gate")
        pb = p[i, 0].astype(BF16)
        pe = _mm(pb, fw["w_ple_proj"], "nn", F32, "mm_ple_proj")
        xc = _ple_fwd(x3, pgl, pe, "ple_fwd")
        sv.update(merged=merged, x2=x2, h2=h2, up=up, act=act, x3=x3, h3=h3, pgl=pgl, pb=pb, pe=pe)
        saved.append(sv)

    loss_row, dx, dxb, dg_final = _final_loss(xc, final_norm_g, target, "final_loss")
    loss = lax.psum(loss_row[0, 0], MESH_AXES)

    def in_window(ref, j):
        return ref.at[:, pl.ds(pl.multiple_of(s_traced(j) * LANES, LANES), WB * LANES)]

    def in_flogit(ref, j):
        return ref.at[:, pl.ds(F0, LANES)]

    def scatter_plans(gs):
        conv = jnp.zeros((N_DEV, 8, cw_pad), F32)
        conv = conv.at[:, 0:3, :n_up].set(jnp.moveaxis(gs["ffn_conv_w"].reshape(3, N_DEV, n_up), 1, 0))
        conv = conv.at[:, 3:6, :n_sc].set(jnp.moveaxis(gs["shortconv_w"].reshape(3, N_DEV, n_sc), 1, 0))
        srcs = [gs["w_in"], gs["w_in"], gs["w_up"], gs["w_down"], jnp.stack([gs["w_out"], gs["w_ple_gate"]]),
                jnp.stack([gs["w_br_fox"], gs["w_br_conv"], gs["w_br_sgu"]]),
                jnp.stack([gs["w_br_dil"], gs["w_ple_proj"]]), conv]
        stacked = lambda shape, dt=BF16: SDS((N_DEV,) + tuple(shape), dt)
        shapes = [stacked((D, WB * LANES)), stacked((D, LANES)), stacked((D, n_up)), stacked((r_down, D)),
                  stacked((2, r_sq, D)), stacked((3, w_br_fox.shape[1], n_br)), stacked((2, w_br_dil.shape[1], n_sm)),
                  stacked((8, cw_pad), F32)]
        views = [in_window, in_flogit] + placed + [slot]
        return [(srcs[a:b], shapes[a:b], views[a:b], [slot] * (b - a)) for a, b in ((0, 2), (2, 3), (3, 4), (4, 8))]

    grads_sh = [None] * depth
    grads_rep = [None] * depth
    scattered = [None] * depth
    for i in reversed(range(depth)):
        fw, sv = full[i], saved[i]
        gs, gr = {}, {}
        pend = scatter_plans(grads_sh[i + 1]) if i + 1 < depth else [None] * 4
        got = [None] * 4

        def host(k, result, pend=pend, got=got):
            if pend[k] is None:
                return result
            got[k] = result[1]
            return result[0]

        proj = sv["proj"]
        dpgl, dpe = _ple_bwd(dx, sv["pgl"], sv["pe"], "ple_bwd")
        gs["w_ple_proj"] = _mm(sv["pb"], dpe, "tn", BF16, "mm_dw_ple_proj")
        gs["w_ple_gate"] = _mm(sv["h3"], dpgl, "tn", BF16, "mm_dw_ple_gate")
        dh3 = _mm(dpgl, fw["w_ple_gate"], "nt", F32, "mm_dh3")
        dx, dxb, gr["norm_ple_g"] = _rms_bwd(sv["x3"], norm_ple_g[i], dh3, dx, "rms_ple_bwd")
        gs["w_down"] = _mm(sv["act"], dxb, "tn", BF16, "mm_dw_down")
        dact = _mm(dxb, fw["w_down"], "nt", BF16, "mm_dact")
        dug, duv, dwg, dwv = _ffn_act_bwd(sv["up"], fw["ffn_conv_w"], dact, "ffn_act_bwd")
        gs["ffn_conv_w"] = jnp.concatenate([dwg, dwv], axis=1)
        dup = jnp.concatenate([dug, duv], axis=1)
        gs["w_up"] = _mm(sv["h2"], dup, "tn", BF16, "mm_dw_up")
        dh2 = _mm(dup, fw["w_up"], "nt", F32, "mm_dh2")
        dx, dxb, gr["norm_ffn_g"] = _rms_bwd(sv["x2"], norm_ffn_g[i], dh2, dx, "rms_ffn_bwd")
        gs["w_out"] = _mm(sv["merged"], dxb, "tn", BF16, "mm_dw_out")
        dmerged = _mm(dxb, fw["w_out"], "nt", F32, "mm_dmerged")
        brs = [sv["oa_b"], sv["ob"], sv["oc"], sv["od"]]
        names = ["w_br_fox", "w_br_conv", "w_br_sgu", "w_br_dil"]
        wbr = [fw[n] for n in names]
        dbo, dgl = _merge_bwd(proj, G0, brs, wbr, dmerged, "merge_bwd")
        dos = []
        for j, n in enumerate(names):
            gs[n] = _mm(brs[j], dbo[j], "tn", BF16, "mm_dw_" + n)
            dos.append(_mm(dbo[j], wbr[j], "nt", F32, "mm_do_" + n))
        doa = _to_heads(dos[0], FH).astype(BF16)
        dqa, delta = host(1, _fox_bwd_dq(sv["qa"], sv["ka"], sv["va"], sv["fq"], sv["fk"], doa, sv["lse_a"],
                                         "fox_bwd_dq", comm=pend[1]))
        dka, dva, dfk = host(0, _fox_bwd_dkv(sv["qa"], sv["ka"], sv["va"], sv["fq"], sv["fk"], doa, sv["lse_a"],
                                             delta, "fox_bwd_dkv", comm=pend[0]))
        dF = pad_lanes(dfk.reshape(FH, S).T)
        daf, dbias = _fox_prep_bwd(proj, sv["bias"], F0 // LANES, dF, "fox_prep_bwd")
        gr["fox_forget_b"] = dbias[0, :FH]
        d_a = [_from_heads(t).astype(BF16) for t in (dqa, dka, dva)]
        dxb_, dgb, dgc, gs["shortconv_w"] = _sconv_bwd(proj, fw["shortconv_w"], B0, CW, dos[1], "sconv_bwd")
        du, dv, dsw, dsb, dsg = _sgu_bwd(proj, sgu_norm_g[i].reshape(1, SW), sgu_w[i], sv["sgb"], C0, SW, dos[2],
                                         "sgu_bwd")
        gr["sgu_w"], gr["sgu_b"], gr["sgu_norm_g"] = dsw, dsb.reshape(G, SGU_CHUNK), dsg[0]
        do3, cf3 = _dil_merge_bwd(sv["o3"], sv["lse3"], _to_heads(dos[3], hpg), "dil_merge_bwd")
        do_r, cf_r = _groups_to_residue(do3), _groups_to_residue(cf3)
        dq_r = host(3, _dil_bwd_dq(sv["qr"], sv["kr"], sv["vr"], do_r, sv["lse_r"], cf_r, bpcs, hpg, "dil_bwd_dq",
                                   comm=pend[3]))
        dk_r, dv_r = host(2, _dil_bwd_dkv(sv["qr"], sv["kr"], sv["vr"], do_r, sv["lse_r"], cf_r, bpcs, hpg,
                                          "dil_bwd_dkv", comm=pend[2]))
        if i + 1 < depth:
            scattered[i + 1] = got

        def natural(t):
            return _from_heads(_from_residue(t, hpg).reshape(NG * hpg, S, HEAD_DIM))

        dqd = _rope_apply(natural(dq_r), 0, DW, tables, -1.0, BF16, "rope_q_bwd")
        dkd = _rope_apply(natural(dk_r), 0, DW, tables, -1.0, BF16, "rope_k_bwd")
        dvd = natural(dv_r).astype(BF16)
        dproj = jnp.concatenate(d_a + [dxb_, dgb, dgc, du, dv, dqd, dkd, dvd] + list(dgl) + [daf], axis=1)
        gs["w_in"] = _mm(sv["h1"], dproj, "tn", BF16, "mm_dw_in")
        dh1 = _mm(dproj, fw["w_in"], "nt", F32, "mm_dh1")
        dx, dxb, gr["norm_mix_g"] = _rms_bwd(sv["x"], norm_mix_g[i], dh1, dx, "rms_mix_bwd")
        gr["norm_mix_g"], gr["norm_ffn_g"], gr["norm_ple_g"] = (gr[n][0] for n in
                                                               ("norm_mix_g", "norm_ffn_g", "norm_ple_g"))
        grads_sh[i], grads_rep[i] = gs, gr
    grad_x = dx.reshape(1, S, D)

    last = _comm("comm_scatter_grads", *one_plan(scatter_plans(grads_sh[0])))
    scattered[0] = [last[0:2], last[2:3], last[3:4], last[4:8]]

    def sum_sources(t, name):
        lead = t.shape[1:]
        return _sum8(t.reshape(N_DEV, -1, lead[-1]), name).reshape(lead)

    grads = {n: [None] * depth for n in SHARDED}
    for i in range(depth):
        (gw, gf), (gup,), (gdown,), (gsq, gbr, gsm, gcv) = scattered[i]
        win = jnp.concatenate([sum_sources(gw, "sum_w_in"), sum_sources(gf, "sum_w_in_flogit")], axis=1)
        grads["w_in"][i] = on_my_index(from_window, win)
        grads["w_up"][i] = sum_sources(gup, "sum_w_up")
        grads["w_down"][i] = sum_sources(gdown, "sum_w_down")
        sq, br, sm = sum_sources(gsq, "sum_sq"), sum_sources(gbr, "sum_br"), sum_sources(gsm, "sum_sm")
        cv = sum_sources(gcv, "sum_conv")
        grads["w_out"][i], grads["w_ple_gate"][i] = sq[0], sq[1]
        grads["w_br_fox"][i], grads["w_br_conv"][i], grads["w_br_sgu"][i] = br[0], br[1], br[2]
        grads["w_br_dil"][i], grads["w_ple_proj"][i] = sm[0], sm[1]
        grads["ffn_conv_w"][i], grads["shortconv_w"][i] = cv[0:3, :n_up], cv[3:6, :n_sc]
    grads = {n: jnp.stack(v, axis=0) for n, v in grads.items()}

    rep_parts = [jnp.stack([grads_rep[i][n] for i in range(depth)], axis=0).reshape(-1) for n in REPLICATED]
    rep_parts.append(dg_final.reshape(-1))
    rep_sizes = [int(r.shape[0]) for r in rep_parts]
    rep_offs = [sum(rep_sizes[:i]) for i in range(len(rep_sizes) + 1)]
    rep = _sum8(_exchange(_pack(rep_parts, F32, 8), True, "comm_gather_small"), "sum_small").reshape(-1)
    for n, a, b in zip(REPLICATED + ("final_norm_g",), rep_offs[:-1], rep_offs[1:]):
        grads[n] = rep[a:b].reshape(W[n].shape)

    deltas, new_m, new_v = {}, {}, {}
    for n in WEIGHTS:
        deltas[n], new_m[n], new_v[n] = _adamw(W[n], grads[n], given["m_" + n], given["v_" + n], "adamw_" + n)
    return (loss, grad_x, *[grads[n] for n in WEIGHTS], *[deltas[n] for n in WEIGHTS],
            *[new_m[n] for n in WEIGHTS], *[new_v[n] for n in WEIGHTS])
```

```python
import functools

import jax
import jax.numpy as jnp
from jax import lax
from jax.experimental import pallas as pl
from jax.experimental.pallas import tpu as pltpu

F32 = jnp.float32
BF16 = jnp.bfloat16
EPS = 1e-6
NEG = -1e30
HEAD_DIM = 64
SGU_CHUNK = 128
DIL_PATTERNS = ((128, 1), (512, 4), (2048, 16))
DIL_SPAN = 128
ROPE_THETA = 500000.0
ROPE_DIM = 16
N_DEV = 8
LANES = 128
VMEM_LIMIT = 56 * 1024 * 1024
ADAM_LR, ADAM_B1, ADAM_B2, ADAM_EPS, ADAM_WD, ADAM_STEP = 0.001, 0.9, 0.999, 1e-08, 0.01, 10
MESH_AXES = ("x", "y", "c")

BS = pl.BlockSpec
SDS = jax.ShapeDtypeStruct


def _cp(*sem):
    return pltpu.CompilerParams(dimension_semantics=sem, vmem_limit_bytes=VMEM_LIMIT)


def _tile(dim, pref, unit=LANES):
    if dim % unit:
        return dim
    best, t = unit, unit
    while t <= min(dim, pref):
        if dim % t == 0:
            best = t
        t += unit
    return best


def _sigmoid(z):
    return 1.0 / (1.0 + jnp.exp(-z))


def _exchange(x, gather, name):
    shape = x.shape if not gather else (N_DEV,) + x.shape

    def body(x_ref, o_ref, send_sems, recv_sems, local_sem):
        ix, iy, ic = lax.axis_index("x"), lax.axis_index("y"), lax.axis_index("c")
        me = 4 * ix + 2 * iy + ic

        def src(j):
            return x_ref if gather else x_ref.at[j]

        local = pltpu.make_async_copy(src(me), o_ref.at[me], local_sem)
        local.start()
        sends, recvs = [], []
        for r in range(1, N_DEV):
            px = 1 - ix if (r >> 2) & 1 else ix
            py = 1 - iy if (r >> 1) & 1 else iy
            pc = 1 - ic if r & 1 else ic
            peer = 4 * px + 2 * py + pc
            sends.append(pltpu.make_async_remote_copy(
                src_ref=src(peer), dst_ref=o_ref.at[me], send_sem=send_sems.at[r - 1], recv_sem=recv_sems.at[r - 1],
                device_id=(px, py, pc), device_id_type=pl.DeviceIdType.MESH))
            recvs.append(pltpu.make_async_remote_copy(
                src_ref=src(peer), dst_ref=o_ref.at[peer], send_sem=send_sems.at[r - 1], recv_sem=recv_sems.at[r - 1],
                device_id=(px, py, pc), device_id_type=pl.DeviceIdType.MESH))
        for cp in sends:
            cp.start()
        for cp in recvs:
            cp.wait_recv()
        for cp in sends:
            cp.wait_send()
        local.wait()

    return pl.pallas_call(
        body, name=name, out_shape=SDS(shape, x.dtype),
        in_specs=[BS(memory_space=pl.ANY)], out_specs=BS(memory_space=pl.ANY),
        scratch_shapes=[pltpu.SemaphoreType.DMA((N_DEV - 1,)), pltpu.SemaphoreType.DMA((N_DEV - 1,)),
                        pltpu.SemaphoreType.DMA],
    )(x)


def _comm(name, srcs, out_shapes, src_views, dst_views):
    n = len(srcs)

    def body(*refs):
        copies = _comm_copies(refs[:n], refs[n:2 * n], refs[2 * n:], src_views, dst_views)
        _comm_start(copies)
        _comm_wait(copies)

    anyspec = BS(memory_space=pl.ANY)
    return pl.pallas_call(
        body, name=name, out_shape=tuple(out_shapes), in_specs=[anyspec] * n, out_specs=(anyspec,) * n,
        scratch_shapes=_comm_sems(n))(*srcs)


def _comm_sems(n):
    return [pltpu.SemaphoreType.DMA((N_DEV - 1, n)), pltpu.SemaphoreType.DMA((N_DEV - 1, n)),
            pltpu.SemaphoreType.DMA((n,))]


def _comm_copies(s_refs, o_refs, sems, src_views, dst_views):
    send_sems, recv_sems, local_sems = sems
    n = len(s_refs)
    ix, iy, ic = lax.axis_index("x"), lax.axis_index("y"), lax.axis_index("c")
    me = 4 * ix + 2 * iy + ic
    local = [pltpu.make_async_copy(src_views[k](s_refs[k], me), dst_views[k](o_refs[k], me), local_sems.at[k])
             for k in range(n)]
    sends, recvs = [], []
    for r in range(1, N_DEV):
        px = 1 - ix if (r >> 2) & 1 else ix
        py = 1 - iy if (r >> 1) & 1 else iy
        pc = 1 - ic if r & 1 else ic
        peer = 4 * px + 2 * py + pc
        for k in range(n):
            common = dict(send_sem=send_sems.at[r - 1, k], recv_sem=recv_sems.at[r - 1, k],
                          device_id=(px, py, pc), device_id_type=pl.DeviceIdType.MESH)
            sends.append(pltpu.make_async_remote_copy(
                src_ref=src_views[k](s_refs[k], peer), dst_ref=dst_views[k](o_refs[k], me), **common))
            recvs.append(pltpu.make_async_remote_copy(
                src_ref=src_views[k](s_refs[k], peer), dst_ref=dst_views[k](o_refs[k], peer), **common))
    return local, sends, recvs


def _comm_start(copies):
    local, sends, _ = copies
    for cp in local + sends:
        cp.start()


def _comm_wait(copies):
    local, sends, recvs = copies
    for cp in recvs:
        cp.wait_recv()
    for cp in sends:
        cp.wait_send()
    for cp in local:
        cp.wait()


def _call(body, *, name, grid, in_specs, out_specs, out_shape, args, scratch_shapes=(), sem=None, comm=None):
    single = not isinstance(out_shape, (tuple, list))
    out_shape = (out_shape,) if single else tuple(out_shape)
    out_specs = (out_specs,) if single else tuple(out_specs)
    if comm is None:
        outs = pl.pallas_call(body, name=name, grid=grid, in_specs=list(in_specs), out_specs=out_specs,
                              out_shape=out_shape, scratch_shapes=list(scratch_shapes),
                              compiler_params=_cp(*sem))(*args)
        return outs[0] if single else outs
    srcs, shapes, src_views, dst_views = comm
    n, ni, no, ns = len(srcs), len(in_specs), len(out_shape), len(scratch_shapes)

    def hosted(*refs):
        ins, c_in = refs[:ni], refs[ni:ni + n]
        outs, c_out = refs[ni + n:ni + n + no], refs[ni + n + no:ni + 2 * n + no]
        scr, sems = refs[ni + 2 * n + no:ni + 2 * n + no + ns], refs[ni + 2 * n + no + ns:]
        ids = [pl.program_id(a) for a in range(len(grid))]
        first = functools.reduce(jnp.logical_and, [i == 0 for i in ids])
        last = functools.reduce(jnp.logical_and, [i == g - 1 for i, g in zip(ids, grid)])

        @pl.when(first)
        def _():
            _comm_start(_comm_copies(c_in, c_out, sems, src_views, dst_views))

        body(*ins, *outs, *scr)

        @pl.when(last)
        def _():
            _comm_wait(_comm_copies(c_in, c_out, sems, src_views, dst_views))

    anyspec = BS(memory_space=pl.ANY)
    outs = pl.pallas_call(
        hosted, name=name, grid=grid, in_specs=list(in_specs) + [anyspec] * n,
        out_specs=out_specs + (anyspec,) * n, out_shape=out_shape + tuple(shapes),
        scratch_shapes=list(scratch_shapes) + _comm_sems(n),
        compiler_params=_cp(*(["arbitrary"] * len(grid))))(*args, *srcs)
    return (outs[0] if single else outs[:no]), outs[no:]


def _assemble(pieces, starts, wb, npb, name):
    _, K, pw = pieces.shape
    tr = _tile(K, 256, 16)

    def body(p_ref, o_ref):
        for b in range(npb - 1):
            acc = None
            for j in range(N_DEV):
                if starts[j] <= b < starts[j] + wb:
                    blk = p_ref[j, :, (b - starts[j]) * LANES:(b - starts[j] + 1) * LANES]
                    acc = blk if acc is None else acc + blk
            o_ref[:, b * LANES:(b + 1) * LANES] = acc
        acc = p_ref[0, :, wb * LANES:]
        for j in range(1, N_DEV):
            acc = acc + p_ref[j, :, wb * LANES:]
        o_ref[:, (npb - 1) * LANES:] = acc

    return pl.pallas_call(
        body, name=name, grid=(K // tr,), out_shape=SDS((K, npb * LANES), pieces.dtype),
        in_specs=[BS((N_DEV, tr, pw), lambda i: (0, i, 0))], out_specs=BS((tr, npb * LANES), lambda i: (i, 0)),
        compiler_params=_cp("parallel"))(pieces)


def _sum8(x, name):
    _, R, C = x.shape
    tr = _tile(R, 256, 16)

    def body(x_ref, o_ref):
        acc = x_ref[0].astype(F32)
        for j in range(1, N_DEV):
            acc = acc + x_ref[j].astype(F32)
        o_ref[...] = acc

    return pl.pallas_call(
        body, name=name, grid=(R // tr,), out_shape=SDS((R, C), F32),
        in_specs=[BS((N_DEV, tr, C), lambda i: (0, i, 0))], out_specs=BS((tr, C), lambda i: (i, 0)),
        compiler_params=_cp("parallel"))(x)


_DIMS = {"nn": ((1,), (0,)), "nt": ((1,), (1,)), "tn": ((0,), (0,))}


def _mm(a, b, mode, out_dtype, name, res=None, comm=None):
    if mode == "nn":
        (M, K), N = a.shape, b.shape[1]
    elif mode == "nt":
        (M, K), N = a.shape, b.shape[0]
    else:
        (K, M), N = a.shape, b.shape[1]
    tm, tn, tk = _tile(M, 1024), _tile(N, 1024), _tile(K, 1024)
    nk = K // tk
    a_spec = BS((tk, tm), lambda i, j, k: (k, i)) if mode == "tn" else BS((tm, tk), lambda i, j, k: (i, k))
    b_spec = BS((tn, tk), lambda i, j, k: (j, k)) if mode == "nt" else BS((tk, tn), lambda i, j, k: (k, j))
    o_spec = BS((tm, tn), lambda i, j, k: (i, j))
    dims = (_DIMS[mode], ((), ()))

    def body(a_ref, b_ref, *rest):
        o_ref, acc = rest[-2], rest[-1]
        k = pl.program_id(2)

        @pl.when(k == 0)
        def _():
            acc[...] = jnp.zeros_like(acc)

        acc[...] += lax.dot_general(a_ref[...].astype(BF16), b_ref[...].astype(BF16), dims,
                                    preferred_element_type=F32)

        @pl.when(k == nk - 1)
        def _():
            r = acc[...]
            if res is not None:
                r = r + rest[0][...]
            o_ref[...] = r.astype(o_ref.dtype)

    ins, specs = [a, b], [a_spec, b_spec]
    if res is not None:
        ins.append(res)
        specs.append(o_spec)
    return _call(body, name=name, grid=(M // tm, N // tn, nk), out_shape=SDS((M, N), out_dtype), in_specs=specs,
                 out_specs=o_spec, scratch_shapes=[pltpu.VMEM((tm, tn), F32)],
                 sem=("parallel", "parallel", "arbitrary"), args=ins, comm=comm)


def _rms_fwd(x, g, name):
    S, D = x.shape
    ts = _tile(S, 256, 8)

    def body(x_ref, g_ref, o_ref):
        xv = x_ref[...]
        rstd = lax.rsqrt(jnp.mean(xv * xv, axis=-1, keepdims=True) + EPS)
        o_ref[...] = (xv * rstd * g_ref[...]).astype(o_ref.dtype)

    return pl.pallas_call(
        body, name=name, grid=(S // ts,), out_shape=SDS((S, D), BF16),
        in_specs=[BS((ts, D), lambda i: (i, 0)), BS((1, D), lambda i: (0, 0))],
        out_specs=BS((ts, D), lambda i: (i, 0)), compiler_params=_cp("parallel"))(x, g.reshape(1, D))


def _rms_bwd(x, g, dh, dres, name):
    S, D = x.shape
    ts = _tile(S, 256, 8)

    def body(x_ref, g_ref, dh_ref, dres_ref, dx_ref, dxb_ref, dg_ref):
        xv = x_ref[...]
        rstd = lax.rsqrt(jnp.mean(xv * xv, axis=-1, keepdims=True) + EPS)
        xn = xv * rstd
        dhv = dh_ref[...]

        @pl.when(pl.program_id(0) == 0)
        def _():
            dg_ref[...] = jnp.zeros_like(dg_ref)

        dg_ref[...] += jnp.sum(dhv * xn, axis=0, keepdims=True)
        dhn = dhv * g_ref[...]
        dxv = rstd * (dhn - xn * jnp.mean(dhn * xn, axis=-1, keepdims=True)) + dres_ref[...]
        dx_ref[...] = dxv
        dxb_ref[...] = dxv.astype(BF16)

    row = BS((ts, D), lambda i: (i, 0))
    vec = BS((1, D), lambda i: (0, 0))
    return pl.pallas_call(
        body, name=name, grid=(S // ts,), out_shape=(SDS((S, D), F32), SDS((S, D), BF16), SDS((1, D), F32)),
        in_specs=[row, vec, row, row], out_specs=(row, row, vec), compiler_params=_cp("arbitrary"),
    )(x, g.reshape(1, D), dh, dres)


def _final_loss(x, g, target, name):
    S, D = x.shape
    ts = _tile(S, 256, 8)

    def body(x_ref, g_ref, t_ref, loss_ref, dx_ref, dxb_ref, dg_ref):
        xv = x_ref[...]
        rstd = lax.rsqrt(jnp.mean(xv * xv, axis=-1, keepdims=True) + EPS)
        xn = xv * rstd
        gv = g_ref[...]
        e = xn * gv - t_ref[...]

        @pl.when(pl.program_id(0) == 0)
        def _():
            dg_ref[...] = jnp.zeros_like(dg_ref)
            loss_ref[...] = jnp.zeros_like(loss_ref)

        loss_ref[...] += 0.5 * jnp.sum(jnp.mean(e * e, axis=-1, keepdims=True), axis=0, keepdims=True)
        dy = e * (1.0 / D)
        dg_ref[...] += jnp.sum(dy * xn, axis=0, keepdims=True)
        dhn = dy * gv
        dxv = rstd * (dhn - xn * jnp.mean(dhn * xn, axis=-1, keepdims=True))
        dx_ref[...] = dxv
        dxb_ref[...] = dxv.astype(BF16)

    row = BS((ts, D), lambda i: (i, 0))
    vec = BS((1, D), lambda i: (0, 0))
    return pl.pallas_call(
        body, name=name, grid=(S // ts,),
        out_shape=(SDS((1, LANES), F32), SDS((S, D), F32), SDS((S, D), BF16), SDS((1, D), F32)),
        in_specs=[row, vec, row], out_specs=(BS((1, LANES), lambda i: (0, 0)), row, row, vec),
        compiler_params=_cp("arbitrary"))(x, g.reshape(1, D), target)


def _shift_down(z, k):
    rows = lax.broadcasted_iota(jnp.int32, z.shape, 0)
    return jnp.where(rows >= k, pltpu.roll(z, k, 0), 0.0)


def _shift_up(z, k):
    n = z.shape[0]
    rows = lax.broadcasted_iota(jnp.int32, z.shape, 0)
    return jnp.where(rows < n - k, pltpu.roll(z, n - k, 0), 0.0)


def _cumsum_rows(z, reverse=False):
    n = z.shape[0]
    k = 1
    while k < n:
        z = z + (_shift_up(z, k) if reverse else _shift_down(z, k))
        k *= 2
    return z


def _conv3(z, w_ref):
    return w_ref[0:1, :] * _shift_down(z, 2) + w_ref[1:2, :] * _shift_down(z, 1) + w_ref[2:3, :] * z


def _conv3_bwd(z, dy, w_ref, dw_ref):
    dw_ref[0:1, :] = jnp.sum(dy * _shift_down(z, 2), axis=0, keepdims=True)
    dw_ref[1:2, :] = jnp.sum(dy * _shift_down(z, 1), axis=0, keepdims=True)
    dw_ref[2:3, :] = jnp.sum(dy * z, axis=0, keepdims=True)
    return w_ref[2:3, :] * dy + w_ref[1:2, :] * _shift_up(dy, 1) + w_ref[0:1, :] * _shift_up(dy, 2)


def _fox_prep(proj, bias, col, name):
    S = proj.shape[0]

    def body(a_ref, b_ref, f_ref):
        z = a_ref[...] + b_ref[...]
        logf = jnp.minimum(z, 0.0) - jnp.log(1.0 + jnp.exp(-jnp.abs(z)))
        rest = _cumsum_rows(logf)
        for k in range(3):
            piece = rest.astype(BF16)
            f_ref[k] = piece
            rest = rest - piece.astype(F32)

    return pl.pallas_call(
        body, name=name, grid=(1,), out_shape=SDS((3, S, LANES), BF16),
        in_specs=[BS((S, LANES), lambda i: (0, col)), BS((1, LANES), lambda i: (0, 0))],
        out_specs=BS((3, S, LANES), lambda i: (0, 0, 0)), compiler_params=_cp("arbitrary"))(proj, bias)


def _fox_prep_bwd(proj, bias, col, df, name):
    S = proj.shape[0]

    def body(a_ref, b_ref, df_ref, da_ref, db_ref):
        z = a_ref[...] + b_ref[...]
        da = _cumsum_rows(df_ref[...], reverse=True) * _sigmoid(-z)
        da_ref[...] = da.astype(BF16)
        db_ref[...] = jnp.sum(da, axis=0, keepdims=True)

    full = BS((S, LANES), lambda i: (0, 0))
    vec = BS((1, LANES), lambda i: (0, 0))
    return pl.pallas_call(
        body, name=name, grid=(1,), out_shape=(SDS((S, LANES), BF16), SDS((1, LANES), F32)),
        in_specs=[BS((S, LANES), lambda i: (0, col)), vec, full], out_specs=(full, vec),
        compiler_params=_cp("arbitrary"))(proj, bias, df)


def _fox_scores(qs, kv, fqv, fkv, i, j, t, masked):
    s = lax.dot_general(qs, kv, (_DIMS["nt"], ((), ())), preferred_element_type=F32) + fqv - fkv
    if not masked:
        return s
    row = i * t + lax.broadcasted_iota(jnp.int32, (t, t), 0)
    col = j * t + lax.broadcasted_iota(jnp.int32, (t, t), 1)
    return jnp.where(col <= row, s, NEG)


def _fox_fwd(q, k, v, fq, fk, name, comm=None):
    H, S, Dh = q.shape
    t = _tile(S, 256)
    n = S // t
    scale = Dh ** -0.5

    def body(q_ref, k_ref, v_ref, fq_ref, fk_ref, o_ref, lse_ref):
        i = pl.program_id(1)
        qs, fqv = q_ref[0] * scale, fq_ref[0]

        def step(masked, j, carry):
            m, l, acc = carry
            s = _fox_scores(qs, k_ref[0, j], fqv, fk_ref[0, j], i, j, t, masked)
            mn = jnp.maximum(m, jnp.max(s, axis=-1, keepdims=True))
            al = jnp.exp(m - mn)
            p = jnp.exp(s - mn)
            l = l * al + jnp.sum(p, axis=-1, keepdims=True)
            acc = acc * al + jnp.dot(p.astype(BF16), v_ref[0, j], preferred_element_type=F32)
            return mn, l, acc

        init = (jnp.full((t, 1), NEG, F32), jnp.zeros((t, 1), F32), jnp.zeros((t, Dh), F32))
        m, l, acc = step(True, i, lax.fori_loop(0, i, functools.partial(step, False), init))
        o_ref[0] = acc / l
        lse_ref[0] = m + jnp.log(l)

    blk = BS((1, t, Dh), lambda h, i: (h, i, 0))
    col = BS((1, t, 1), lambda h, i: (h, i, 0))
    res = BS((1, n, t, Dh), lambda h, i: (h, 0, 0, 0))
    return _call(
        body, name=name, grid=(H, n), out_shape=(SDS((H, S, Dh), F32), SDS((H, S, 1), F32)),
        in_specs=[blk, res, res, col, BS((1, n, 1, t), lambda h, i: (h, 0, 0, 0))], out_specs=(blk, col),
        sem=("parallel", "arbitrary"), args=(q, k.reshape(H, n, t, Dh), v.reshape(H, n, t, Dh), fq, fk), comm=comm)


def _fox_bwd_dq(q, k, v, fq, fk, do, lse, name, comm=None):
    H, S, Dh = q.shape
    t = _tile(S, 256)
    n = S // t
    scale = Dh ** -0.5

    def body(q_ref, k_ref, v_ref, fq_ref, fk_ref, do_ref, lse_ref, dq_ref, dl_ref):
        i = pl.program_id(1)
        qs, fqv, dob, lsev = q_ref[0] * scale, fq_ref[0], do_ref[0], lse_ref[0]

        def p_dp(masked, j):
            p = jnp.exp(_fox_scores(qs, k_ref[0, j], fqv, fk_ref[0, j], i, j, t, masked) - lsev)
            return p, lax.dot_general(dob, v_ref[0, j], (_DIMS["nt"], ((), ())), preferred_element_type=F32)

        def sum_step(masked, j, delta):
            p, dp = p_dp(masked, j)
            return delta + jnp.sum(p * dp, axis=-1, keepdims=True)

        delta = sum_step(True, i, lax.fori_loop(0, i, functools.partial(sum_step, False), jnp.zeros((t, 1), F32)))

        def step(masked, j, dq):
            p, dp = p_dp(masked, j)
            return dq + jnp.dot((p * (dp - delta)).astype(BF16), k_ref[0, j], preferred_element_type=F32)

        dq = step(True, i, lax.fori_loop(0, i, functools.partial(step, False), jnp.zeros((t, Dh), F32)))
        dq_ref[0] = dq * scale
        dl_ref[0] = delta

    blk = BS((1, t, Dh), lambda h, i: (h, i, 0))
    col = BS((1, t, 1), lambda h, i: (h, i, 0))
    res = BS((1, n, t, Dh), lambda h, i: (h, 0, 0, 0))
    return _call(
        body, name=name, grid=(H, n), out_shape=(SDS((H, S, Dh), F32), SDS((H, S, 1), F32)),
        in_specs=[blk, res, res, col, BS((1, n, 1, t), lambda h, i: (h, 0, 0, 0)), blk, col],
        out_specs=(blk, col), sem=("parallel", "arbitrary"), comm=comm,
        args=(q, k.reshape(H, n, t, Dh), v.reshape(H, n, t, Dh), fq, fk, do, lse))


def _fox_bwd_dkv(q, k, v, fq, fk, do, lse, delta, name, comm=None):
    H, S, Dh = q.shape
    t = _tile(S, 256)
    n = S // t
    scale = Dh ** -0.5
    tn_dims = (_DIMS["tn"], ((), ()))

    def body(q_ref, k_ref, v_ref, fq_ref, fk_ref, do_ref, lse_ref, dl_ref, dk_ref, dv_ref, dfk_ref):
        j = pl.program_id(1)
        kv, vv, fkv = k_ref[0], v_ref[0], fk_ref[0, 0]

        def step(masked, i, carry):
            dk, dv, dfk = carry
            qs, dob = q_ref[0, i] * scale, do_ref[0, i]
            p = jnp.exp(_fox_scores(qs, kv, fq_ref[0, i], fkv, i, j, t, masked) - lse_ref[0, i])
            dv = dv + lax.dot_general(p.astype(BF16), dob, tn_dims, preferred_element_type=F32)
            dp = lax.dot_general(dob, vv, (_DIMS["nt"], ((), ())), preferred_element_type=F32)
            ds = p * (dp - dl_ref[0, i])
            dk = dk + lax.dot_general(ds.astype(BF16), qs, tn_dims, preferred_element_type=F32)
            return dk, dv, dfk - jnp.sum(ds, axis=0, keepdims=True)

        init = (jnp.zeros((t, Dh), F32), jnp.zeros((t, Dh), F32), jnp.zeros((1, t), F32))
        dk, dv, dfk = lax.fori_loop(j + 1, n, functools.partial(step, False), step(True, j, init))
        dk_ref[0] = dk
        dv_ref[0] = dv
        dfk_ref[0, 0] = dfk

    blk = BS((1, t, Dh), lambda h, j: (h, j, 0))
    res = BS((1, n, t, Dh), lambda h, j: (h, 0, 0, 0))
    rcol = BS((1, n, t, 1), lambda h, j: (h, 0, 0, 0))
    frow = BS((1, 1, 1, t), lambda h, j: (h, j, 0, 0))
    return _call(
        body, name=name, grid=(H, n),
        out_shape=(SDS((H, S, Dh), F32), SDS((H, S, Dh), F32), SDS((H, n, 1, t), F32)),
        in_specs=[res, blk, blk, rcol, frow, res, rcol, rcol], out_specs=(blk, blk, frow),
        sem=("parallel", "arbitrary"), comm=comm,
        args=(q.reshape(H, n, t, Dh), k, v, fq.reshape(H, n, t, 1), fk, do.astype(BF16).reshape(H, n, t, Dh),
              lse.reshape(H, n, t, 1), delta.reshape(H, n, t, 1)))


FOX_HEADS_PER_STEP = 2


def _fox_q_scale(Dh):
    lane = lax.broadcasted_iota(jnp.int32, (1, 2 * Dh), 1)
    return jnp.where(lane < Dh, Dh ** -0.5, 1.0).astype(BF16)


def _foxt_scores(ka, qs, i, j, t, masked):
    s = lax.dot_general(ka, qs, (_DIMS["nt"], ((), ())), preferred_element_type=F32)
    if not masked:
        return s
    key = j * t + lax.broadcasted_iota(jnp.int32, (t, t), 0)
    qry = i * t + lax.broadcasted_iota(jnp.int32, (t, t), 1)
    return jnp.where(key <= qry, s, NEG)


def _foxt_fwd(q_aug, k_aug, vT, name, comm=None):
    H, S, W = q_aug.shape
    Dh = W // 2
    t = _tile(S, 256)
    n = S // t
    hb = FOX_HEADS_PER_STEP

    def body(q_ref, k_ref, v_ref, o_ref, lse_ref):
        i = pl.program_id(1)
        qs = [q_ref[h] * _fox_q_scale(Dh) for h in range(hb)]

        def step(masked, j, carry):
            out = []
            for h in range(hb):
                m, l, acc = carry[h]
                s = _foxt_scores(k_ref[h, j], qs[h], i, j, t, masked)
                mn = jnp.maximum(m, jnp.max(s, axis=0, keepdims=True))
                al = jnp.exp(m - mn)
                p = jnp.exp(s - mn)
                l = l * al + jnp.sum(p, axis=0, keepdims=True)
                acc = acc * al + jnp.dot(v_ref[h, j], p.astype(BF16), preferred_element_type=F32)
                out.append((mn, l, acc))
            return tuple(out)

        init = tuple((jnp.full((1, t), NEG, F32), jnp.zeros((1, t), F32), jnp.zeros((Dh, t), F32))
                     for _ in range(hb))
        fin = step(True, i, lax.fori_loop(0, i, functools.partial(step, False), init))
        for h in range(hb):
            m, l, acc = fin[h]
            o_ref[h] = acc / l
            lse_ref[h] = m + jnp.log(l)

    return _call(
        body, name=name, grid=(H // hb, n), out_shape=(SDS((H, Dh, S), F32), SDS((H, 1, S), F32)),
        in_specs=[BS((hb, t, W), lambda g, i: (g, i, 0)), BS((hb, n, t, W), lambda g, i: (g, 0, 0, 0)),
                  BS((hb, n, Dh, t), lambda g, i: (g, 0, 0, 0))],
        out_specs=(BS((hb, Dh, t), lambda g, i: (g, 0, i)), BS((hb, 1, t), lambda g, i: (g, 0, i))),
        sem=("parallel", "arbitrary"), args=(q_aug, k_aug.reshape(H, n, t, W), vT), comm=comm)


def _foxt_bwd_dq(q_aug, k_aug, kT, v, do, lse, name, comm=None):
    H, S, W = q_aug.shape
    Dh = W // 2
    t = _tile(S, 256)
    n = S // t
    hb = FOX_HEADS_PER_STEP

    def body(q_ref, k_ref, kt_ref, v_ref, do_ref, lse_ref, dq_ref, dl_ref):
        i = pl.program_id(1)
        qs = [q_ref[h] * _fox_q_scale(Dh) for h in range(hb)]

        def p_dp(h, masked, j):
            p = jnp.exp(_foxt_scores(k_ref[h, j], qs[h], i, j, t, masked) - lse_ref[h])
            return p, lax.dot_general(v_ref[h, j], do_ref[h], (_DIMS["nt"], ((), ())), preferred_element_type=F32)

        def sum_step(masked, j, delta):
            out = []
            for h in range(hb):
                p, dp = p_dp(h, masked, j)
                out.append(delta[h] + jnp.sum(p * dp, axis=0, keepdims=True))
            return tuple(out)

        zero = tuple(jnp.zeros((1, t), F32) for _ in range(hb))
        delta = sum_step(True, i, lax.fori_loop(0, i, functools.partial(sum_step, False), zero))

        def step(masked, j, dq):
            out = []
            for h in range(hb):
                p, dp = p_dp(h, masked, j)
                ds = (p * (dp - delta[h])).astype(BF16)
                out.append(dq[h] + jnp.dot(kt_ref[h, j], ds, preferred_element_type=F32))
            return tuple(out)

        zero = tuple(jnp.zeros((W, t), F32) for _ in range(hb))
        dq = step(True, i, lax.fori_loop(0, i, functools.partial(step, False), zero))
        for h in range(hb):
            dq_ref[h] = dq[h] * Dh ** -0.5
            dl_ref[h] = delta[h]

    row = BS((hb, 1, t), lambda g, i: (g, 0, i))
    blk = lambda w: BS((hb, t, w), lambda g, i: (g, i, 0))
    return _call(
        body, name=name, grid=(H // hb, n), out_shape=(SDS((H, W, S), F32), SDS((H, 1, S), F32)),
        in_specs=[blk(W), BS((hb, n, t, W), lambda g, i: (g, 0, 0, 0)), BS((hb, n, W, t), lambda g, i: (g, 0, 0, 0)),
                  BS((hb, n, t, Dh), lambda g, i: (g, 0, 0, 0)), blk(Dh), row],
        out_specs=(BS((hb, W, t), lambda g, i: (g, 0, i)), row), sem=("parallel", "arbitrary"), comm=comm,
        args=(q_aug, k_aug.reshape(H, n, t, W), kT, v.reshape(H, n, t, Dh), do, lse))


def _foxt_bwd_dkv(q_aug, k_aug, v, do, lse, delta, name, comm=None):
    H, S, W = q_aug.shape
    Dh = W // 2
    t = _tile(S, 256)
    n = S // t
    hb = FOX_HEADS_PER_STEP

    def body(q_ref, k_ref, v_ref, do_ref, lse_ref, dl_ref, dk_ref, dv_ref, dfk_ref):
        j = pl.program_id(1)

        def step(masked, i, carry):
            out = []
            for h in range(hb):
                dk, dv, dfk = carry[h]
                qs, dob = q_ref[h, i] * _fox_q_scale(Dh), do_ref[h, i]
                p = jnp.exp(_foxt_scores(k_ref[h], qs, i, j, t, masked) - lse_ref[h, i])
                dv = dv + jnp.dot(p.astype(BF16), dob, preferred_element_type=F32)
                dp = lax.dot_general(v_ref[h], dob, (_DIMS["nt"], ((), ())), preferred_element_type=F32)
                ds = p * (dp - dl_ref[h, i])
                dk = dk + jnp.dot(ds.astype(BF16), qs, preferred_element_type=F32)
                out.append((dk, dv, dfk - jnp.sum(ds, axis=1, keepdims=True)))
            return tuple(out)

        init = tuple((jnp.zeros((t, W), F32), jnp.zeros((t, Dh), F32), jnp.zeros((t, 1), F32)) for _ in range(hb))
        fin = lax.fori_loop(j + 1, n, functools.partial(step, False), step(True, j, init))
        for h in range(hb):
            dk_ref[h], dv_ref[h], dfk_ref[h] = fin[h]

    blk = lambda w: BS((hb, t, w), lambda g, j: (g, j, 0))
    res = lambda w: BS((hb, n, t, w), lambda g, j: (g, 0, 0, 0))
    rows = BS((hb, n, 1, t), lambda g, j: (g, 0, 0, 0))
    return _call(
        body, name=name, grid=(H // hb, n),
        out_shape=(SDS((H, S, W), F32), SDS((H, S, Dh), F32), SDS((H, S, 1), F32)),
        in_specs=[res(W), blk(W), blk(Dh), res(Dh), rows, rows], out_specs=(blk(W), blk(Dh), blk(1)),
        sem=("parallel", "arbitrary"), comm=comm,
        args=(q_aug.reshape(H, n, t, W), k_aug, v, do.reshape(H, n, t, Dh), lse.reshape(H, n, 1, t),
              delta.reshape(H, n, 1, t)))


def _sconv_fwd(proj, w, col0, width, name):
    S = proj.shape[0]
    tc = LANES
    nb, c0 = width // tc, col0 // tc

    def body(x_ref, gb_ref, gc_ref, w_ref, o_ref):
        o_ref[...] = (gb_ref[...] * _conv3(gc_ref[...] * x_ref[...], w_ref)).astype(BF16)

    return pl.pallas_call(
        body, name=name, grid=(nb,), out_shape=SDS((S, width), BF16),
        in_specs=[BS((S, tc), lambda j: (0, c0 + j)), BS((S, tc), lambda j: (0, c0 + nb + j)),
                  BS((S, tc), lambda j: (0, c0 + 2 * nb + j)), BS((3, tc), lambda j: (0, j))],
        out_specs=BS((S, tc), lambda j: (0, j)), compiler_params=_cp("parallel"))(proj, proj, proj, w)


def _sconv_bwd(proj, w, col0, width, do, name):
    S = proj.shape[0]
    tc = LANES
    nb, c0 = width // tc, col0 // tc

    def body(x_ref, gb_ref, gc_ref, w_ref, do_ref, dx_ref, dgb_ref, dgc_ref, dw_ref):
        xb, gb, gc, dov = x_ref[...], gb_ref[...], gc_ref[...], do_ref[...]
        z = gc * xb
        dgb_ref[...] = (dov * _conv3(z, w_ref)).astype(BF16)
        dz = _conv3_bwd(z, dov * gb, w_ref, dw_ref)
        dgc_ref[...] = (dz * xb).astype(BF16)
        dx_ref[...] = (dz * gc).astype(BF16)

    out = BS((S, tc), lambda j: (0, j))
    wspec = BS((3, tc), lambda j: (0, j))
    return pl.pallas_call(
        body, name=name, grid=(nb,),
        out_shape=(SDS((S, width), BF16), SDS((S, width), BF16), SDS((S, width), BF16), SDS((3, width), F32)),
        in_specs=[BS((S, tc), lambda j: (0, c0 + j)), BS((S, tc), lambda j: (0, c0 + nb + j)),
                  BS((S, tc), lambda j: (0, c0 + 2 * nb + j)), wspec, out],
        out_specs=(out, out, out, wspec), compiler_params=_cp("parallel"))(proj, proj, proj, w, do)


def _gelu(x):
    return 0.5 * x * (1.0 + jnp.tanh(0.7978845608028654 * (x + 0.044715 * x * x * x)))


def _gelu_grad(x):
    u = 0.7978845608028654 * (x + 0.044715 * x * x * x)
    th = jnp.tanh(u)
    return 0.5 * (1.0 + th) + 0.5 * x * (1.0 - th * th) * 0.7978845608028654 * (1.0 + 3.0 * 0.044715 * x * x)


def _tril_w(w_ref, g):
    r = lax.broadcasted_iota(jnp.int32, (SGU_CHUNK, SGU_CHUNK), 0)
    c = lax.broadcasted_iota(jnp.int32, (SGU_CHUNK, SGU_CHUNK), 1)
    return jnp.where(c <= r, w_ref[g], 0.0), c <= r


def _sgu_mixed(vn, w_ref, b_ref, nch, G):
    rows = []
    for ch in range(nch):
        cols = []
        for g in range(G):
            wt, _ = _tril_w(w_ref, g)
            blk = vn[ch * SGU_CHUNK:(ch + 1) * SGU_CHUNK, g * LANES:(g + 1) * LANES]
            cols.append(jnp.dot(wt.astype(BF16), blk.astype(BF16), preferred_element_type=F32) + b_ref[g])
        rows.append(jnp.concatenate(cols, axis=1))
    return jnp.concatenate(rows, axis=0)


def _sgu_fwd(proj, gn, w, b, col0, width, name):
    S = proj.shape[0]
    G = w.shape[0]
    ts = _tile(S, 512)
    nch = ts // SGU_CHUNK
    c0 = col0 // width

    def body(u_ref, v_ref, gn_ref, w_ref, b_ref, o_ref):
        cv = _gelu(v_ref[...])
        rstd = lax.rsqrt(jnp.mean(cv * cv, axis=-1, keepdims=True) + EPS)
        mixed = _sgu_mixed(cv * rstd * gn_ref[...], w_ref, b_ref, nch, G)
        o_ref[...] = (_gelu(u_ref[...]) * mixed).astype(BF16)

    return pl.pallas_call(
        body, name=name, grid=(S // ts,), out_shape=SDS((S, width), BF16),
        in_specs=[BS((ts, width), lambda i: (i, c0)), BS((ts, width), lambda i: (i, c0 + 1)),
                  BS((1, width), lambda i: (0, 0)), BS(w.shape, lambda i: (0, 0, 0)), BS(b.shape, lambda i: (0, 0, 0))],
        out_specs=BS((ts, width), lambda i: (i, 0)), compiler_params=_cp("parallel"))(proj, proj, gn, w, b)


def _sgu_bwd(proj, gn, w, b, col0, width, do, name):
    S = proj.shape[0]
    G = w.shape[0]
    ts = _tile(S, 512)
    nch = ts // SGU_CHUNK
    c0 = col0 // width

    def body(u_ref, v_ref, gn_ref, w_ref, b_ref, do_ref, du_ref, dv_ref, dw_ref, db_ref, dgn_ref):
        uin, vin, dov, gnv = u_ref[...], v_ref[...], do_ref[...], gn_ref[...]
        cu, cv = _gelu(uin), _gelu(vin)
        rstd = lax.rsqrt(jnp.mean(cv * cv, axis=-1, keepdims=True) + EPS)
        vhat = cv * rstd
        vn = vhat * gnv
        mixed = _sgu_mixed(vn, w_ref, b_ref, nch, G)
        du_ref[...] = (dov * mixed * _gelu_grad(uin)).astype(BF16)
        dmix = dov * cu

        @pl.when(pl.program_id(0) == 0)
        def _():
            dw_ref[...] = jnp.zeros_like(dw_ref)
            db_ref[...] = jnp.zeros_like(db_ref)
            dgn_ref[...] = jnp.zeros_like(dgn_ref)

        rows = []
        for ch in range(nch):
            cols = []
            for g in range(G):
                wt, mask = _tril_w(w_ref, g)
                sl = (slice(ch * SGU_CHUNK, (ch + 1) * SGU_CHUNK), slice(g * LANES, (g + 1) * LANES))
                dm, vb = dmix[sl], vn[sl].astype(BF16)
                db_ref[g] += jnp.sum(dm, axis=-1, keepdims=True)
                dwg = lax.dot_general(dm.astype(BF16), vb, (_DIMS["nt"], ((), ())), preferred_element_type=F32)
                dw_ref[g] += jnp.where(mask, dwg, 0.0)
                cols.append(lax.dot_general(wt.astype(BF16), dm.astype(BF16), (_DIMS["tn"], ((), ())),
                                            preferred_element_type=F32))
            rows.append(jnp.concatenate(cols, axis=1))
        dvn = jnp.concatenate(rows, axis=0)
        dgn_ref[...] += jnp.sum(dvn * vhat, axis=0, keepdims=True)
        dvh = dvn * gnv
        dcv = rstd * (dvh - vhat * jnp.mean(dvh * vhat, axis=-1, keepdims=True))
        dv_ref[...] = (dcv * _gelu_grad(vin)).astype(BF16)

    row = BS((ts, width), lambda i: (i, 0))
    wsp, bsp, gsp = BS(w.shape, lambda i: (0, 0, 0)), BS(b.shape, lambda i: (0, 0, 0)), BS((1, width), lambda i: (0, 0))
    return pl.pallas_call(
        body, name=name, grid=(S // ts,),
        out_shape=(SDS((S, width), BF16), SDS((S, width), BF16), SDS(w.shape, F32), SDS(b.shape, F32),
                   SDS((1, width), F32)),
        in_specs=[BS((ts, width), lambda i: (i, c0)), BS((ts, width), lambda i: (i, c0 + 1)), gsp, wsp, bsp, row],
        out_specs=(row, row, wsp, bsp, gsp), compiler_params=_cp("arbitrary"))(proj, proj, gn, w, b, do)


def _rope_tables(positions, inv, name):
    S = positions.shape[0]
    ts = _tile(S, 512, 8)
    half = ROPE_DIM // 2

    def body(p_ref, inv_ref, c_ref, sa_ref, sb_ref):
        ang = p_ref[...].astype(F32) * inv_ref[...]
        lane = lax.broadcasted_iota(jnp.int32, (ts, LANES), 1) % HEAD_DIM
        sn = jnp.sin(ang)
        c_ref[...] = jnp.cos(ang)
        sa_ref[...] = jnp.where(lane < half, -sn, 0.0)
        sb_ref[...] = jnp.where((lane >= half) & (lane < ROPE_DIM), sn, 0.0)

    out = BS((ts, LANES), lambda i: (i, 0))
    return pl.pallas_call(
        body, name=name, grid=(S // ts,), out_shape=(SDS((S, LANES), F32),) * 3,
        in_specs=[BS((ts, 1), lambda i: (i, 0)), BS((1, LANES), lambda i: (0, 0))], out_specs=(out, out, out),
        compiler_params=_cp("parallel"))(positions, inv)


def _rope_apply(x, col0, width, tables, sign, out_dtype, name):
    S = x.shape[0]
    ts = _tile(S, 512, 8)
    tc = 2 * LANES
    nb, c0 = width // tc, col0 // tc
    half = ROPE_DIM // 2

    def body(x_ref, c_ref, sa_ref, sb_ref, o_ref):
        xv = x_ref[...].astype(F32)
        wide = lambda r: jnp.concatenate([r[...], r[...]], axis=1)
        y = (xv * wide(c_ref) + pltpu.roll(xv, tc - half, 1) * (sign * wide(sa_ref))
             + pltpu.roll(xv, half, 1) * (sign * wide(sb_ref)))
        o_ref[...] = y.astype(o_ref.dtype)

    tab = BS((ts, LANES), lambda i, j: (i, 0))
    return pl.pallas_call(
        body, name=name, grid=(S // ts, nb), out_shape=SDS((S, width), out_dtype),
        in_specs=[BS((ts, tc), lambda i, j: (i, c0 + j)), tab, tab, tab], out_specs=BS((ts, tc), lambda i, j: (i, j)),
        compiler_params=_cp("parallel", "parallel"))(x, *tables)


def _dil_masks(hh, m, bpcs, hpg):
    g = hh // hpg
    bpc = jnp.where(g == 0, bpcs[0], jnp.where(g == 1, bpcs[1], bpcs[2]))
    r = lax.broadcasted_iota(jnp.int32, (DIL_SPAN, DIL_SPAN), 0)
    c = lax.broadcasted_iota(jnp.int32, (DIL_SPAN, DIL_SPAN), 1)
    return (c >= r) & (lax.rem(m, bpc) != 0), c <= r


def _nt(a, b):
    return lax.dot_general(a, b, (_DIMS["nt"], ((), ())), preferred_element_type=F32)


def _tn(a, b):
    return lax.dot_general(a, b, (_DIMS["tn"], ((), ())), preferred_element_type=F32)


def _dil_fwd(q, k, v, bpcs, hpg, name, comm=None):
    HH, S, Dh = q.shape
    nb = S // DIL_SPAN
    scale = Dh ** -0.5

    def body(q_ref, kp_ref, kc_ref, vp_ref, vc_ref, o_ref, lse_ref):
        mp, mc = _dil_masks(pl.program_id(0), pl.program_id(1), bpcs, hpg)
        qv = q_ref[0]
        sp = jnp.where(mp, _nt(qv, kp_ref[0]) * scale, NEG)
        sc = jnp.where(mc, _nt(qv, kc_ref[0]) * scale, NEG)
        mx = jnp.maximum(jnp.max(sp, axis=-1, keepdims=True), jnp.max(sc, axis=-1, keepdims=True))
        pp, pc = jnp.exp(sp - mx), jnp.exp(sc - mx)
        l = jnp.sum(pp, axis=-1, keepdims=True) + jnp.sum(pc, axis=-1, keepdims=True)
        acc = (jnp.dot(pp.astype(BF16), vp_ref[0], preferred_element_type=F32)
               + jnp.dot(pc.astype(BF16), vc_ref[0], preferred_element_type=F32))
        o_ref[0] = acc / l
        lse_ref[0] = mx + jnp.log(l)

    cur = BS((1, DIL_SPAN, Dh), lambda h, m: (h, m, 0))
    prev = BS((1, DIL_SPAN, Dh), lambda h, m: (h, jnp.maximum(m - 1, 0), 0))
    return _call(
        body, name=name, grid=(HH, nb), out_shape=(SDS((HH, S, Dh), F32), SDS((HH, S, 1), F32)),
        in_specs=[cur, prev, cur, prev, cur], out_specs=(cur, BS((1, DIL_SPAN, 1), lambda h, m: (h, m, 0))),
        sem=("parallel", "parallel"), args=(q, k, k, v, v), comm=comm)


def _dil_bwd_dq(q, k, v, do, lse, coef, bpcs, hpg, name, comm=None):
    HH, S, Dh = q.shape
    nb = S // DIL_SPAN
    scale = Dh ** -0.5

    def body(q_ref, kp_ref, kc_ref, vp_ref, vc_ref, do_ref, lse_ref, cf_ref, dq_ref):
        mp, mc = _dil_masks(pl.program_id(0), pl.program_id(1), bpcs, hpg)
        qv, dob, lsev, cf = q_ref[0], do_ref[0], lse_ref[0], cf_ref[0]
        pp = jnp.exp(jnp.where(mp, _nt(qv, kp_ref[0]) * scale, NEG) - lsev)
        pc = jnp.exp(jnp.where(mc, _nt(qv, kc_ref[0]) * scale, NEG) - lsev)
        dsp = pp * (_nt(dob, vp_ref[0]) + cf)
        dsc = pc * (_nt(dob, vc_ref[0]) + cf)
        dq_ref[0] = (jnp.dot(dsp.astype(BF16), kp_ref[0], preferred_element_type=F32)
                     + jnp.dot(dsc.astype(BF16), kc_ref[0], preferred_element_type=F32)) * scale

    cur = BS((1, DIL_SPAN, Dh), lambda h, m: (h, m, 0))
    prev = BS((1, DIL_SPAN, Dh), lambda h, m: (h, jnp.maximum(m - 1, 0), 0))
    col = BS((1, DIL_SPAN, 1), lambda h, m: (h, m, 0))
    return _call(
        body, name=name, grid=(HH, nb), out_shape=SDS((HH, S, Dh), F32),
        in_specs=[cur, prev, cur, prev, cur, cur, col, col], out_specs=cur,
        sem=("parallel", "parallel"), args=(q, k, k, v, v, do, lse, coef), comm=comm)


def _dil_bwd_dkv(q, k, v, do, lse, coef, bpcs, hpg, name, comm=None):
    HH, S, Dh = q.shape
    nb = S // DIL_SPAN
    scale = Dh ** -0.5

    def body(k_ref, v_ref, qc_ref, qn_ref, doc_ref, don_ref, lc_ref, ln_ref, cc_ref, cn_ref, dk_ref, dv_ref):
        hh, m = pl.program_id(0), pl.program_id(1)
        _, mc = _dil_masks(hh, m, bpcs, hpg)
        mp, _ = _dil_masks(hh, m + 1, bpcs, hpg)
        mp = mp & (m + 1 < nb)
        kv, vv = k_ref[0], v_ref[0]
        pc = jnp.exp(jnp.where(mc, _nt(qc_ref[0], kv) * scale, NEG) - lc_ref[0])
        pn = jnp.exp(jnp.where(mp, _nt(qn_ref[0], kv) * scale, NEG) - ln_ref[0])
        dsc = pc * (_nt(doc_ref[0], vv) + cc_ref[0])
        dsn = pn * (_nt(don_ref[0], vv) + cn_ref[0])
        dv_ref[0] = _tn(pc.astype(BF16), doc_ref[0]) + _tn(pn.astype(BF16), don_ref[0])
        dk_ref[0] = (_tn(dsc.astype(BF16), qc_ref[0]) + _tn(dsn.astype(BF16), qn_ref[0])) * scale

    cur = BS((1, DIL_SPAN, Dh), lambda h, m: (h, m, 0))
    nxt = BS((1, DIL_SPAN, Dh), lambda h, m: (h, jnp.minimum(m + 1, nb - 1), 0))
    col = BS((1, DIL_SPAN, 1), lambda h, m: (h, m, 0))
    ncol = BS((1, DIL_SPAN, 1), lambda h, m: (h, jnp.minimum(m + 1, nb - 1), 0))
    return _call(
        body, name=name, grid=(HH, nb), out_shape=(SDS((HH, S, Dh), F32), SDS((HH, S, Dh), F32)),
        in_specs=[cur, cur, cur, nxt, cur, nxt, col, ncol, col, ncol], out_specs=(cur, cur),
        sem=("parallel", "parallel"), args=(k, v, q, q, do, do, lse, lse, coef, coef), comm=comm)


def _dil_merge(o3, lse3, name):
    G, H, S, Dh = o3.shape
    ts = _tile(S, 512, 8)

    def body(o_ref, l_ref, out_ref):
        lv = [l_ref[g, 0] for g in range(G)]
        mx = functools.reduce(jnp.maximum, lv)
        ex = [jnp.exp(v - mx) for v in lv]
        den = functools.reduce(lambda a, b: a + b, ex)
        out_ref[0] = functools.reduce(lambda a, b: a + b, [(ex[g] / den) * o_ref[g, 0] for g in range(G)])

    return pl.pallas_call(
        body, name=name, grid=(H, S // ts), out_shape=SDS((H, S, Dh), F32),
        in_specs=[BS((G, 1, ts, Dh), lambda h, i: (0, h, i, 0)), BS((G, 1, ts, 1), lambda h, i: (0, h, i, 0))],
        out_specs=BS((1, ts, Dh), lambda h, i: (h, i, 0)), compiler_params=_cp("parallel", "parallel"))(o3, lse3)


def _dil_merge_bwd(o3, lse3, do, name):
    G, H, S, Dh = o3.shape
    ts = _tile(S, 512, 8)

    def body(o_ref, l_ref, do_ref, do3_ref, cf_ref):
        lv = [l_ref[g, 0] for g in range(G)]
        mx = functools.reduce(jnp.maximum, lv)
        ex = [jnp.exp(v - mx) for v in lv]
        den = functools.reduce(lambda a, b: a + b, ex)
        wt = [e / den for e in ex]
        dov = do_ref[0]
        dw = [jnp.sum(dov * o_ref[g, 0], axis=-1, keepdims=True) for g in range(G)]
        mean = functools.reduce(lambda a, b: a + b, [wt[g] * dw[g] for g in range(G)])
        for g in range(G):
            do3_ref[g, 0] = (wt[g] * dov).astype(BF16)
            cf_ref[g, 0] = wt[g] * (dw[g] - mean) - wt[g] * dw[g]

    o_spec = BS((G, 1, ts, Dh), lambda h, i: (0, h, i, 0))
    l_spec = BS((G, 1, ts, 1), lambda h, i: (0, h, i, 0))
    return pl.pallas_call(
        body, name=name, grid=(H, S // ts), out_shape=(SDS((G, H, S, Dh), BF16), SDS((G, H, S, 1), F32)),
        in_specs=[o_spec, l_spec, BS((1, ts, Dh), lambda h, i: (h, i, 0))], out_specs=(o_spec, l_spec),
        compiler_params=_cp("parallel", "parallel"))(o3, lse3, do)


def _merge_fwd(proj, gate_col, os_, ws, name):
    S = proj.shape[0]
    D = ws[0].shape[1]
    nbr = len(os_)
    ts = _tile(S, 512, 8)
    tn = LANES * 2 if gate_col % (2 * LANES) == 0 and D % (2 * LANES) == 0 else LANES
    g0, gstep = gate_col // tn, D // tn

    def body(*refs):
        p_refs, o_refs, w_refs, out_ref = refs[:nbr], refs[nbr:2 * nbr], refs[2 * nbr:3 * nbr], refs[-1]
        acc = None
        for i in range(nbr):
            t = _sigmoid(p_refs[i][...]) * jnp.dot(o_refs[i][...], w_refs[i][...], preferred_element_type=F32)
            acc = t if acc is None else acc + t
        out_ref[...] = acc.astype(BF16)

    specs = ([BS((ts, tn), lambda s, j, i=i: (s, g0 + i * gstep + j)) for i in range(nbr)]
             + [BS((ts, o.shape[1]), lambda s, j: (s, 0)) for o in os_]
             + [BS((w.shape[0], tn), lambda s, j: (0, j)) for w in ws])
    return pl.pallas_call(
        body, name=name, grid=(S // ts, D // tn), out_shape=SDS((S, D), BF16), in_specs=specs,
        out_specs=BS((ts, tn), lambda s, j: (s, j)), compiler_params=_cp("parallel", "parallel"),
    )(*([proj] * nbr), *os_, *ws)


def _merge_bwd(proj, gate_col, os_, ws, dm, name):
    S = proj.shape[0]
    D = ws[0].shape[1]
    nbr = len(os_)
    ts = _tile(S, 512, 8)
    tn = LANES * 2 if gate_col % (2 * LANES) == 0 and D % (2 * LANES) == 0 else LANES
    g0, gstep = gate_col // tn, D // tn

    def body(*refs):
        p_refs, o_refs, w_refs = refs[:nbr], refs[nbr:2 * nbr], refs[2 * nbr:3 * nbr]
        dmv = refs[3 * nbr][...]
        dbo_refs, dgl_refs = refs[3 * nbr + 1:4 * nbr + 1], refs[4 * nbr + 1:]
        for i in range(nbr):
            gt = _sigmoid(p_refs[i][...])
            bo = jnp.dot(o_refs[i][...], w_refs[i][...], preferred_element_type=F32)
            dbo_refs[i][...] = (dmv * gt).astype(BF16)
            dgl_refs[i][...] = (dmv * bo * gt * (1.0 - gt)).astype(BF16)

    tile = BS((ts, tn), lambda s, j: (s, j))
    specs = ([BS((ts, tn), lambda s, j, i=i: (s, g0 + i * gstep + j)) for i in range(nbr)]
             + [BS((ts, o.shape[1]), lambda s, j: (s, 0)) for o in os_]
             + [BS((w.shape[0], tn), lambda s, j: (0, j)) for w in ws] + [tile])
    outs = pl.pallas_call(
        body, name=name, grid=(S // ts, D // tn), out_shape=(SDS((S, D), BF16),) * (2 * nbr), in_specs=specs,
        out_specs=(tile,) * (2 * nbr), compiler_params=_cp("parallel", "parallel"),
    )(*([proj] * nbr), *os_, *ws, dm)
    return outs[:nbr], outs[nbr:]


def _ffn_act(up, w, name):
    S, F2 = up.shape
    F = F2 // 2
    tc = LANES
    nb = F // tc

    def body(g_ref, v_ref, wg_ref, wv_ref, a_ref):
        ug = _conv3(g_ref[...], wg_ref)
        a_ref[...] = (ug * _sigmoid(ug) * _conv3(v_ref[...], wv_ref)).astype(BF16)

    return pl.pallas_call(
        body, name=name, grid=(nb,), out_shape=SDS((S, F), BF16),
        in_specs=[BS((S, tc), lambda j: (0, j)), BS((S, tc), lambda j: (0, nb + j)),
                  BS((3, tc), lambda j: (0, j)), BS((3, tc), lambda j: (0, nb + j))],
        out_specs=BS((S, tc), lambda j: (0, j)), compiler_params=_cp("parallel"))(up, up, w, w)


def _ffn_act_bwd(up, w, da, name):
    S, F2 = up.shape
    F = F2 // 2
    tc = LANES
    nb = F // tc

    def body(g_ref, v_ref, wg_ref, wv_ref, da_ref, dg_ref, dv_ref, dwg_ref, dwv_ref):
        pg, pv, dav = g_ref[...], v_ref[...], da_ref[...].astype(F32)
        ug, uv = _conv3(pg, wg_ref), _conv3(pv, wv_ref)
        sg = _sigmoid(ug)
        dg_ref[...] = _conv3_bwd(pg, dav * uv * (sg * (1.0 + ug * (1.0 - sg))), wg_ref, dwg_ref).astype(BF16)
        dv_ref[...] = _conv3_bwd(pv, dav * ug * sg, wv_ref, dwv_ref).astype(BF16)

    col = BS((S, tc), lambda j: (0, j))
    wsp = BS((3, tc), lambda j: (0, j))
    return pl.pallas_call(
        body, name=name, grid=(nb,),
        out_shape=(SDS((S, F), BF16), SDS((S, F), BF16), SDS((3, F), F32), SDS((3, F), F32)),
        in_specs=[col, BS((S, tc), lambda j: (0, nb + j)), wsp, BS((3, tc), lambda j: (0, nb + j)), col],
        out_specs=(col, col, wsp, wsp), compiler_params=_cp("parallel"))(up, up, w, w, da)


def _ple_fwd(x, pgl, pe, name):
    S, D = x.shape
    ts = _tile(S, 256, 8)

    def body(x_ref, g_ref, e_ref, o_ref):
        o_ref[...] = x_ref[...] + _sigmoid(g_ref[...]) * e_ref[...]

    row = BS((ts, D), lambda i: (i, 0))
    return pl.pallas_call(body, name=name, grid=(S // ts,), out_shape=SDS((S, D), F32), in_specs=[row] * 3,
                          out_specs=row, compiler_params=_cp("parallel"))(x, pgl, pe)


def _ple_bwd(dx, pgl, pe, name):
    S, D = dx.shape
    ts = _tile(S, 256, 8)

    def body(dx_ref, g_ref, e_ref, dg_ref, de_ref):
        dxv, sg = dx_ref[...], _sigmoid(g_ref[...])
        dg_ref[...] = (dxv * e_ref[...] * sg * (1.0 - sg)).astype(BF16)
        de_ref[...] = (dxv * sg).astype(BF16)

    row = BS((ts, D), lambda i: (i, 0))
    return pl.pallas_call(body, name=name, grid=(S // ts,), out_shape=(SDS((S, D), BF16),) * 2, in_specs=[row] * 3,
                          out_specs=(row, row), compiler_params=_cp("parallel"))(dx, pgl, pe)


def _adamw(w, g, m, v, name):
    shape = w.shape
    cols = shape[-1]
    rows = w.size // cols
    tr = _tile(rows, 256, 8)
    c1, c2 = 1.0 / (1.0 - ADAM_B1 ** ADAM_STEP), 1.0 / (1.0 - ADAM_B2 ** ADAM_STEP)

    def body(w_ref, g_ref, m_ref, v_ref, d_ref, nm_ref, nv_ref):
        gv = g_ref[...]
        nm = ADAM_B1 * m_ref[...] + (1.0 - ADAM_B1) * gv
        nv = ADAM_B2 * v_ref[...] + (1.0 - ADAM_B2) * (gv * gv)
        d_ref[...] = -ADAM_LR * ((nm * c1) / (jnp.sqrt(nv * c2) + ADAM_EPS) + ADAM_WD * w_ref[...])
        nm_ref[...] = nm
        nv_ref[...] = nv

    blk = BS((tr, cols), lambda i: (i, 0))
    outs = pl.pallas_call(
        body, name=name, grid=(rows // tr,), out_shape=(SDS((rows, cols), F32),) * 3, in_specs=[blk] * 4,
        out_specs=(blk,) * 3, compiler_params=_cp("parallel"),
    )(*(a.reshape(rows, cols) for a in (w, g, m, v)))
    return tuple(o.reshape(shape) for o in outs)


SHARDED = ("w_in", "shortconv_w", "w_br_fox", "w_br_conv", "w_br_sgu", "w_br_dil", "w_out", "w_up", "ffn_conv_w",
           "w_down", "w_ple_gate", "w_ple_proj")
REPLICATED = ("norm_mix_g", "fox_forget_b", "sgu_norm_g", "sgu_w", "sgu_b", "norm_ffn_g", "norm_ple_g")
WEIGHTS = ("norm_mix_g", "w_in", "fox_forget_b", "shortconv_w", "sgu_norm_g", "sgu_w", "sgu_b", "w_br_fox",
           "w_br_conv", "w_br_sgu", "w_br_dil", "w_out", "norm_ffn_g", "w_up", "ffn_conv_w", "w_down", "norm_ple_g",
           "w_ple_gate", "w_ple_proj", "final_norm_g")
PACK_COLS = 1024
PACK_ROWS = 256


def _pack(parts, dtype, unit):
    lead = parts[0].shape[:-1] if parts[0].ndim > 1 else ()
    flat = jnp.concatenate([p.astype(dtype) for p in parts], axis=-1)
    n = flat.shape[-1]
    rows = -(-n // (PACK_COLS * unit)) * unit
    flat = jnp.pad(flat, [(0, 0)] * len(lead) + [(0, rows * PACK_COLS - n)])
    return flat.reshape(lead + (rows, PACK_COLS))


def _to_heads(x, heads):
    S = x.shape[0]
    return x.reshape(S, heads, HEAD_DIM).transpose(1, 0, 2)


def _from_heads(x):
    H, S, Dh = x.shape
    return x.transpose(1, 0, 2).reshape(S, H * Dh)


def _to_residue(x, hpg):
    S = x.shape[0]
    gw = hpg * HEAD_DIM
    outs = []
    for g, (_, d) in enumerate(DIL_PATTERNS):
        seg = x[:, g * gw:(g + 1) * gw].reshape(S // d, d, hpg, HEAD_DIM)
        outs.append(seg.transpose(2, 1, 0, 3).reshape(hpg, S, HEAD_DIM))
    return jnp.concatenate(outs, axis=0)


def _from_residue(x, hpg):
    _, S, C = x.shape
    outs = []
    for g, (_, d) in enumerate(DIL_PATTERNS):
        seg = x[g * hpg:(g + 1) * hpg].reshape(hpg, d, S // d, C)
        outs.append(seg.transpose(0, 2, 1, 3).reshape(hpg, S, C))
    return jnp.stack(outs, axis=0)


def _groups_to_residue(x):
    G, hpg, S, C = x.shape
    outs = []
    for g, (_, d) in enumerate(DIL_PATTERNS):
        outs.append(x[g].reshape(hpg, S // d, d, C).transpose(0, 2, 1, 3).reshape(hpg, S, C))
    return jnp.concatenate(outs, axis=0)


def kernel(x, p, positions, norm_mix_g, w_in, fox_forget_b, shortconv_w, sgu_norm_g, sgu_w, sgu_b, w_br_fox, w_br_conv, w_br_sgu, w_br_dil, w_out, norm_ffn_g, w_up, ffn_conv_w, w_down, norm_ple_g, w_ple_gate, w_ple_proj, final_norm_g, loss_target, m_norm_mix_g, m_w_in, m_fox_forget_b, m_shortconv_w, m_sgu_norm_g, m_sgu_w, m_sgu_b, m_w_br_fox, m_w_br_conv, m_w_br_sgu, m_w_br_dil, m_w_out, m_norm_ffn_g, m_w_up, m_ffn_conv_w, m_w_down, m_norm_ple_g, m_w_ple_gate, m_w_ple_proj, m_final_norm_g, v_norm_mix_g, v_w_in, v_fox_forget_b, v_shortconv_w, v_sgu_norm_g, v_sgu_w, v_sgu_b, v_w_br_fox, v_w_br_conv, v_w_br_sgu, v_w_br_dil, v_w_out, v_norm_ffn_g, v_w_up, v_ffn_conv_w, v_w_down, v_norm_ple_g, v_w_ple_gate, v_w_ple_proj, v_final_norm_g):
    given = dict(locals())
    W = {n: given[n] for n in WEIGHTS}
    depth = w_in.shape[0]
    S, D = x.shape[1], x.shape[2]
    x0, target, pos = x[0], loss_target[0], positions[0].reshape(S, 1)
    FH = fox_forget_b.shape[1]
    FW = FH * HEAD_DIM
    CW = shortconv_w.shape[2] * N_DEV
    G = sgu_w.shape[1]
    SW = G * sgu_w.shape[2]
    DOUT = w_br_dil.shape[1]
    hpg = DOUT // HEAD_DIM
    NG = len(DIL_PATTERNS)
    DW = NG * DOUT
    bpcs = tuple(S // d // DIL_SPAN for _, d in DIL_PATTERNS)
    A0, B0 = 0, 3 * FW
    C0 = B0 + 3 * CW
    D0 = C0 + 2 * SW
    G0 = D0 + 3 * DW
    F0 = G0 + 4 * D
    NP = F0 + LANES
    orig = [3 * FW, FH, 3 * CW, 2 * SW, 3 * DW, 4 * D]
    o_off = [sum(orig[:i]) for i in range(len(orig) + 1)]

    me = 4 * lax.axis_index("x") + 2 * lax.axis_index("y") + lax.axis_index("c")
    n_in, o1, o2, d_in = w_in.shape[2], o_off[1], o_off[2], o_off[6]
    NPB = NP // LANES
    WB = -(-(LANES - 1 + n_in) // LANES)
    PW = (WB + 1) * LANES

    def a_of(j, clip=lambda v: min(max(v, 0), FH)):
        return n_in * j - clip(n_in * j - o1)

    def s_of(j):
        return min(a_of(j) // LANES, NPB - WB)

    def s_traced(j):
        return jnp.minimum(a_of(j, lambda v: jnp.clip(v, 0, FH)) // LANES, NPB - WB)

    def runs(j):
        lo, hi, out = n_in * j, n_in * (j + 1), []
        for a, b in ((0, o1), (o1, o2), (o2, d_in)):
            l, h = max(lo, a), min(hi, b)
            if l < h:
                dst = WB * LANES + l - o1 if a == o1 else l - (FH if l >= o2 else 0) - s_of(j) * LANES
                out.append((l - lo, h - l, dst))
        return out

    def to_window(j, shard):
        piece = jnp.zeros((shard.shape[0], PW), shard.dtype)
        for src, ln, dst in runs(j):
            piece = piece.at[:, dst:dst + ln].set(shard[:, src:src + ln])
        return piece

    def from_window(j, win):
        return jnp.concatenate([win[:, dst:dst + ln] for _, ln, dst in runs(j)], axis=1)

    def on_my_index(fn, arg):
        return lax.switch(me, [functools.partial(fn, j) for j in range(N_DEV)], arg)

    def place(axis, length):
        def view(ref, j):
            align = LANES if axis == len(ref.shape) - 1 else 16
            start = pl.multiple_of(j * length, align) if length % align == 0 else j * length
            return ref.at[tuple([slice(None)] * axis + [pl.ds(start, length)])]
        return view

    def slot(ref, j):
        return ref.at[j]

    def whole(ref, j):
        return ref

    n_up, r_down, r_sq = w_up.shape[2], w_down.shape[1], w_out.shape[1]
    n_br, n_sm, n_sc = w_br_fox.shape[2], w_br_dil.shape[2], shortconv_w.shape[2]
    FF2 = n_up * N_DEV
    cw_pad = -(-n_up // LANES) * LANES
    placed = [place(1, n_up), place(0, r_down), place(1, r_sq), place(2, n_br), place(2, n_sm)]

    def gather_plans(i):
        conv = jnp.zeros((8, cw_pad), F32).at[0:3, :n_up].set(ffn_conv_w[i]).at[3:6, :n_sc].set(shortconv_w[i])
        srcs = [on_my_index(to_window, w_in[i].astype(BF16)), w_up[i].astype(BF16), w_down[i].astype(BF16),
                jnp.stack([w_out[i], w_ple_gate[i]]).astype(BF16),
                jnp.stack([w_br_fox[i], w_br_conv[i], w_br_sgu[i]]).astype(BF16),
                jnp.stack([w_br_dil[i], w_ple_proj[i]]).astype(BF16), conv]
        shapes = [SDS((N_DEV, D, PW), BF16), SDS((D, FF2), BF16), SDS((r_down * N_DEV, D), BF16),
                  SDS((2, r_sq * N_DEV, D), BF16), SDS((3, w_br_fox.shape[1], n_br * N_DEV), BF16),
                  SDS((2, w_br_dil.shape[1], n_sm * N_DEV), BF16), SDS((N_DEV, 8, cw_pad), F32)]
        dsts = [slot] + placed + [slot]
        return [(srcs[a:b], shapes[a:b], [whole] * (b - a), dsts[a:b]) for a, b in ((0, 1), (1, 2), (2, 3), (3, 7))]

    def gathered(parts):
        (pieces,), (wup,), (wdown,), (sq, br, sm, cv) = parts
        return {
            "w_in": _assemble(pieces, [s_of(j) for j in range(N_DEV)], WB, NPB, "assemble_w_in"),
            "w_up": wup, "w_down": wdown, "w_out": sq[0], "w_ple_gate": sq[1],
            "w_br_fox": br[0], "w_br_conv": br[1], "w_br_sgu": br[2], "w_br_dil": sm[0], "w_ple_proj": sm[1],
            "ffn_conv_w": jnp.moveaxis(cv[:, 0:3, :n_up], 0, 1).reshape(3, FF2),
            "shortconv_w": jnp.moveaxis(cv[:, 3:6, :n_sc], 0, 1).reshape(3, CW),
        }

    def one_plan(plans):
        return tuple(sum((list(plan[k]) for plan in plans), []) for k in range(4))

    first = _comm("comm_gather_weights", *one_plan(gather_plans(0)))
    full = [gathered([first[0:1], first[1:2], first[2:3], first[3:7]])]

    inv = ROPE_THETA ** (-jnp.arange(ROPE_DIM // 2, dtype=F32) * (2.0 / ROPE_DIM))
    inv_row = jnp.tile(jnp.concatenate([inv, inv, jnp.zeros((HEAD_DIM - ROPE_DIM,), F32)]), LANES // HEAD_DIM)
    tables = _rope_tables(pos, inv_row.reshape(1, LANES), "rope_tables")

    def pad_lanes(v):
        return jnp.pad(v, ((0, 0), (0, LANES - v.shape[1])))

    saved = []
    xc = x0
    for i in range(depth):
        fw, sv = full[i], {}
        sv["x"] = xc
        nxt = gather_plans(i + 1) if i + 1 < depth else [None] * 4
        got = [None] * 4

        def host(k, result, nxt=nxt, got=got):
            if nxt[k] is None:
                return result
            got[k] = result[1]
            return result[0]

        h1 = _rms_fwd(xc, norm_mix_g[i], "rms_mix")
        proj = host(1, _mm(h1, fw["w_in"], "nn", F32, "mm_in", comm=nxt[1]))
        sv["h1"], sv["proj"] = h1, proj
        bias = pad_lanes(fox_forget_b[i].reshape(1, FH))
        Fp = _fox_prep(proj, bias, F0 // LANES, "fox_prep")[:, :, :FH].transpose(2, 1, 0)
        t = _tile(S, 256)
        qa, ka, va = (_to_heads(proj[:, A0 + j * FW:A0 + (j + 1) * FW], FH).astype(BF16) for j in range(3))
        ones3, zpad = jnp.ones((FH, S, 3), BF16), jnp.zeros((FH, S, HEAD_DIM - 6), BF16)
        q_aug = jnp.concatenate([qa, Fp, ones3, zpad], axis=-1)
        k_aug = jnp.concatenate([ka, ones3, -Fp, zpad], axis=-1)

        def blocks_t(a):
            return a.reshape(FH, S // t, t, a.shape[-1]).transpose(0, 1, 3, 2)

        kT = blocks_t(k_aug)
        oaT, lse_a = host(0, _foxt_fwd(q_aug, k_aug, blocks_t(va), "fox_fwd", comm=nxt[0]))
        oa_b = oaT.transpose(2, 0, 1).reshape(S, FW).astype(BF16)
        sv.update(bias=bias, q_aug=q_aug, k_aug=k_aug, kT=kT, va=va, lse_a=lse_a, oa_b=oa_b)
        ob = _sconv_fwd(proj, fw["shortconv_w"], B0, CW, "sconv_fwd")
        sgb = sgu_b[i].reshape(G, SGU_CHUNK, 1)
        oc = _sgu_fwd(proj, sgu_norm_g[i].reshape(1, SW), sgu_w[i], sgb, C0, SW, "sgu_fwd")
        qd = _rope_apply(proj, D0, DW, tables, 1.0, BF16, "rope_q")
        kd = _rope_apply(proj, D0 + DW, DW, tables, 1.0, BF16, "rope_k")
        qr, kr = _to_residue(qd, hpg), _to_residue(kd, hpg)
        vr = _to_residue(proj[:, D0 + 2 * DW:D0 + 3 * DW].astype(BF16), hpg)
        od_r, lse_r = host(2, _dil_fwd(qr, kr, vr, bpcs, hpg, "dil_fwd", comm=nxt[2]))
        o3, lse3 = _from_residue(od_r, hpg), _from_residue(lse_r, hpg)
        od = _from_heads(_dil_merge(o3, lse3, "dil_merge")).astype(BF16)
        sv.update(ob=ob, oc=oc, sgb=sgb, qr=qr, kr=kr, vr=vr, lse_r=lse_r, o3=o3, lse3=lse3, od=od)
        brs = [oa_b, ob, oc, od]
        wbr = [fw["w_br_fox"], fw["w_br_conv"], fw["w_br_sgu"], fw["w_br_dil"]]
        merged = _merge_fwd(proj, G0, brs, wbr, "merge_fwd")
        x2 = _mm(merged, fw["w_out"], "nn", F32, "mm_out", res=xc)
        h2 = _rms_fwd(x2, norm_ffn_g[i], "rms_ffn")
        up = host(3, _mm(h2, fw["w_up"], "nn", F32, "mm_up", comm=nxt[3]))
        if i + 1 < depth:
            full.append(gathered(got))
        act = _ffn_act(up, fw["ffn_conv_w"], "ffn_act")
        x3 = _mm(act, fw["w_down"], "nn", F32, "mm_down", res=x2)
        h3 = _rms_fwd(x3, norm_ple_g[i], "rms_ple")
        pgl = _mm(h3, fw["w_ple_gate"], "nn", F32, "mm_ple_gate")
        pb = p[i, 0].astype(BF16)
        pe = _mm(pb, fw["w_ple_proj"], "nn", F32, "mm_ple_proj")
        xc = _ple_fwd(x3, pgl, pe, "ple_fwd")
        sv.update(merged=merged, x2=x2, h2=h2, up=up, act=act, x3=x3, h3=h3, pgl=pgl, pb=pb, pe=pe)
        saved.append(sv)

    loss_row, dx, dxb, dg_final = _final_loss(xc, final_norm_g, target, "final_loss")
    loss = lax.psum(loss_row[0, 0], MESH_AXES)

    def in_window(ref, j):
        return ref.at[:, pl.ds(pl.multiple_of(s_traced(j) * LANES, LANES), WB * LANES)]

    def in_flogit(ref, j):
        return ref.at[:, pl.ds(F0, LANES)]

    def scatter_plans(gs):
        conv = jnp.zeros((N_DEV, 8, cw_pad), F32)
        conv = conv.at[:, 0:3, :n_up].set(jnp.moveaxis(gs["ffn_conv_w"].reshape(3, N_DEV, n_up), 1, 0))
        conv = conv.at[:, 3:6, :n_sc].set(jnp.moveaxis(gs["shortconv_w"].reshape(3, N_DEV, n_sc), 1, 0))
        srcs = [gs["w_in"], gs["w_in"], gs["w_up"], gs["w_down"], jnp.stack([gs["w_out"], gs["w_ple_gate"]]),
                jnp.stack([gs["w_br_fox"], gs["w_br_conv"], gs["w_br_sgu"]]),
                jnp.stack([gs["w_br_dil"], gs["w_ple_proj"]]), conv]
        stacked = lambda shape, dt=BF16: SDS((N_DEV,) + tuple(shape), dt)
        shapes = [stacked((D, WB * LANES)), stacked((D, LANES)), stacked((D, n_up)), stacked((r_down, D)),
                  stacked((2, r_sq, D)), stacked((3, w_br_fox.shape[1], n_br)), stacked((2, w_br_dil.shape[1], n_sm)),
                  stacked((8, cw_pad), F32)]
        views = [in_window, in_flogit] + placed + [slot]
        return [(srcs[a:b], shapes[a:b], views[a:b], [slot] * (b - a)) for a, b in ((0, 2), (2, 3), (3, 4), (4, 8))]

    grads_sh = [None] * depth
    grads_rep = [None] * depth
    scattered = [None] * depth
    for i in reversed(range(depth)):
        fw, sv = full[i], saved[i]
        gs, gr = {}, {}
        pend = scatter_plans(grads_sh[i + 1]) if i + 1 < depth else [None] * 4
        got = [None] * 4

        def host(k, result, pend=pend, got=got):
            if pend[k] is None:
                return result
            got[k] = result[1]
            return result[0]

        proj = sv["proj"]
        dpgl, dpe = _ple_bwd(dx, sv["pgl"], sv["pe"], "ple_bwd")
        gs["w_ple_proj"] = _mm(sv["pb"], dpe, "tn", BF16, "mm_dw_ple_proj")
        gs["w_ple_gate"] = _mm(sv["h3"], dpgl, "tn", BF16, "mm_dw_ple_gate")
        dh3 = _mm(dpgl, fw["w_ple_gate"], "nt", F32, "mm_dh3")
        dx, dxb, gr["norm_ple_g"] = _rms_bwd(sv["x3"], norm_ple_g[i], dh3, dx, "rms_ple_bwd")
        gs["w_down"] = _mm(sv["act"], dxb, "tn", BF16, "mm_dw_down")
        dact = _mm(dxb, fw["w_down"], "nt", BF16, "mm_dact")
        dug, duv, dwg, dwv = _ffn_act_bwd(sv["up"], fw["ffn_conv_w"], dact, "ffn_act_bwd")
        gs["ffn_conv_w"] = jnp.concatenate([dwg, dwv], axis=1)
        dup = jnp.concatenate([dug, duv], axis=1)
        gs["w_up"] = _mm(sv["h2"], dup, "tn", BF16, "mm_dw_up")
        dh2 = _mm(dup, fw["w_up"], "nt", F32, "mm_dh2")
        dx, dxb, gr["norm_ffn_g"] = _rms_bwd(sv["x2"], norm_ffn_g[i], dh2, dx, "rms_ffn_bwd")
        gs["w_out"] = _mm(sv["merged"], dxb, "tn", BF16, "mm_dw_out")
        dmerged = _mm(dxb, fw["w_out"], "nt", F32, "mm_dmerged")
        brs = [sv["oa_b"], sv["ob"], sv["oc"], sv["od"]]
        names = ["w_br_fox", "w_br_conv", "w_br_sgu", "w_br_dil"]
        wbr = [fw[n] for n in names]
        dbo, dgl = _merge_bwd(proj, G0, brs, wbr, dmerged, "merge_bwd")
        dos = []
        for j, n in enumerate(names):
            gs[n] = _mm(brs[j], dbo[j], "tn", BF16, "mm_dw_" + n)
            dos.append(_mm(dbo[j], wbr[j], "nt", F32, "mm_do_" + n))
        doa = _to_heads(dos[0], FH).astype(BF16)
        dqT, delta = host(1, _foxt_bwd_dq(sv["q_aug"], sv["k_aug"], sv["kT"], sv["va"], doa, sv["lse_a"],
                                          "fox_bwd_dq", comm=pend[1]))
        dk_aug, dva, dfk = host(0, _foxt_bwd_dkv(sv["q_aug"], sv["k_aug"], sv["va"], doa, sv["lse_a"], delta,
                                                 "fox_bwd_dkv", comm=pend[0]))
        dF = pad_lanes(dfk[:, :, 0].T)
        daf, dbias = _fox_prep_bwd(proj, sv["bias"], F0 // LANES, dF, "fox_prep_bwd")
        gr["fox_forget_b"] = dbias[0, :FH]
        d_a = [dqT[:, :HEAD_DIM].transpose(2, 0, 1).reshape(S, FW).astype(BF16),
               _from_heads(dk_aug[:, :, :HEAD_DIM]).astype(BF16), _from_heads(dva).astype(BF16)]
        dxb_, dgb, dgc, gs["shortconv_w"] = _sconv_bwd(proj, fw["shortconv_w"], B0, CW, dos[1], "sconv_bwd")
        du, dv, dsw, dsb, dsg = _sgu_bwd(proj, sgu_norm_g[i].reshape(1, SW), sgu_w[i], sv["sgb"], C0, SW, dos[2],
                                         "sgu_bwd")
        gr["sgu_w"], gr["sgu_b"], gr["sgu_norm_g"] = dsw, dsb.reshape(G, SGU_CHUNK), dsg[0]
        do3, cf3 = _dil_merge_bwd(sv["o3"], sv["lse3"], _to_heads(dos[3], hpg), "dil_merge_bwd")
        do_r, cf_r = _groups_to_residue(do3), _groups_to_residue(cf3)
        dq_r = host(3, _dil_bwd_dq(sv["qr"], sv["kr"], sv["vr"], do_r, sv["lse_r"], cf_r, bpcs, hpg, "dil_bwd_dq",
                                   comm=pend[3]))
        dk_r, dv_r = host(2, _dil_bwd_dkv(sv["qr"], sv["kr"], sv["vr"], do_r, sv["lse_r"], cf_r, bpcs, hpg,
                                          "dil_bwd_dkv", comm=pend[2]))
        if i + 1 < depth:
            scattered[i + 1] = got

        def natural(t):
            return _from_heads(_from_residue(t, hpg).reshape(NG * hpg, S, HEAD_DIM))

        dqd = _rope_apply(natural(dq_r), 0, DW, tables, -1.0, BF16, "rope_q_bwd")
        dkd = _rope_apply(natural(dk_r), 0, DW, tables, -1.0, BF16, "rope_k_bwd")
        dvd = natural(dv_r).astype(BF16)
        dproj = jnp.concatenate(d_a + [dxb_, dgb, dgc, du, dv, dqd, dkd, dvd] + list(dgl) + [daf], axis=1)
        gs["w_in"] = _mm(sv["h1"], dproj, "tn", BF16, "mm_dw_in")
        dh1 = _mm(dproj, fw["w_in"], "nt", F32, "mm_dh1")
        dx, dxb, gr["norm_mix_g"] = _rms_bwd(sv["x"], norm_mix_g[i], dh1, dx, "rms_mix_bwd")
        gr["norm_mix_g"], gr["norm_ffn_g"], gr["norm_ple_g"] = (gr[n][0] for n in
                                                               ("norm_mix_g", "norm_ffn_g", "norm_ple_g"))
        grads_sh[i], grads_rep[i] = gs, gr
    grad_x = dx.reshape(1, S, D)

    last = _comm("comm_scatter_grads", *one_plan(scatter_plans(grads_sh[0])))
    scattered[0] = [last[0:2], last[2:3], last[3:4], last[4:8]]

    def sum_sources(t, name):
        lead = t.shape[1:]
        return _sum8(t.reshape(N_DEV, -1, lead[-1]), name).reshape(lead)

    grads = {n: [None] * depth for n in SHARDED}
    for i in range(depth):
        (gw, gf), (gup,), (gdown,), (gsq, gbr, gsm, gcv) = scattered[i]
        win = jnp.concatenate([sum_sources(gw, "sum_w_in"), sum_sources(gf, "sum_w_in_flogit")], axis=1)
        grads["w_in"][i] = on_my_index(from_window, win)
        grads["w_up"][i] = sum_sources(gup, "sum_w_up")
        grads["w_down"][i] = sum_sources(gdown, "sum_w_down")
        sq, br, sm = sum_sources(gsq, "sum_sq"), sum_sources(gbr, "sum_br"), sum_sources(gsm, "sum_sm")
        cv = sum_sources(gcv, "sum_conv")
        grads["w_out"][i], grads["w_ple_gate"][i] = sq[0], sq[1]
        grads["w_br_fox"][i], grads["w_br_conv"][i], grads["w_br_sgu"][i] = br[0], br[1], br[2]
        grads["w_br_dil"][i], grads["w_ple_proj"][i] = sm[0], sm[1]
        grads["ffn_conv_w"][i], grads["shortconv_w"][i] = cv[0:3, :n_up], cv[3:6, :n_sc]
    grads = {n: jnp.stack(v, axis=0) for n, v in grads.items()}

    rep_parts = [jnp.stack([grads_rep[i][n] for i in range(depth)], axis=0).reshape(-1) for n in REPLICATED]
    rep_parts.append(dg_final.reshape(-1))
    rep_sizes = [int(r.shape[0]) for r in rep_parts]
    rep_offs = [sum(rep_sizes[:i]) for i in range(len(rep_sizes) + 1)]
    rep = _sum8(_exchange(_pack(rep_parts, F32, 8), True, "comm_gather_small"), "sum_small").reshape(-1)
    for n, a, b in zip(REPLICATED + ("final_norm_g",), rep_offs[:-1], rep_offs[1:]):
        grads[n] = rep[a:b].reshape(W[n].shape)

    deltas, new_m, new_v = {}, {}, {}
    for n in WEIGHTS:
        deltas[n], new_m[n], new_v[n] = _adamw(W[n], grads[n], given["m_" + n], given["v_" + n], "adamw_" + n)
    return (loss, grad_x, *[grads[n] for n in WEIGHTS], *[deltas[n] for n in WEIGHTS],
            *[new_m[n] for n in WEIGHTS], *[new_v[n] for n in WEIGHTS])
```

```python
import functools

import jax
import jax.numpy as jnp
from jax import lax
from jax.experimental import pallas as pl
from jax.experimental.pallas import tpu as pltpu

F32 = jnp.float32
BF16 = jnp.bfloat16
EPS = 1e-6
NEG = -1e30
HEAD_DIM = 64
SGU_CHUNK = 128
DIL_PATTERNS = ((128, 1), (512, 4), (2048, 16))
DIL_SPAN = 128
ROPE_THETA = 500000.0
ROPE_DIM = 16
N_DEV = 8
LANES = 128
VMEM_LIMIT = 56 * 1024 * 1024
ADAM_LR, ADAM_B1, ADAM_B2, ADAM_EPS, ADAM_WD, ADAM_STEP = 0.001, 0.9, 0.999, 1e-08, 0.01, 10
MESH_AXES = ("x", "y", "c")

BS = pl.BlockSpec
SDS = jax.ShapeDtypeStruct


def _cp(*sem):
    return pltpu.CompilerParams(dimension_semantics=sem, vmem_limit_bytes=VMEM_LIMIT)


def _tile(dim, pref, unit=LANES):
    if dim % unit:
        return dim
    best, t = unit, unit
    while t <= min(dim, pref):
        if dim % t == 0:
            best = t
        t += unit
    return best


def _sigmoid(z):
    return 1.0 / (1.0 + jnp.exp(-z))


def _exchange(x, gather, name):
    shape = x.shape if not gather else (N_DEV,) + x.shape

    def body(x_ref, o_ref, send_sems, recv_sems, local_sem):
        ix, iy, ic = lax.axis_index("x"), lax.axis_index("y"), lax.axis_index("c")
        me = 4 * ix + 2 * iy + ic

        def src(j):
            return x_ref if gather else x_ref.at[j]

        local = pltpu.make_async_copy(src(me), o_ref.at[me], local_sem)
        local.start()
        sends, recvs = [], []
        for r in range(1, N_DEV):
            px = 1 - ix if (r >> 2) & 1 else ix
            py = 1 - iy if (r >> 1) & 1 else iy
            pc = 1 - ic if r & 1 else ic
            peer = 4 * px + 2 * py + pc
            sends.append(pltpu.make_async_remote_copy(
                src_ref=src(peer), dst_ref=o_ref.at[me], send_sem=send_sems.at[r - 1], recv_sem=recv_sems.at[r - 1],
                device_id=(px, py, pc), device_id_type=pl.DeviceIdType.MESH))
            recvs.append(pltpu.make_async_remote_copy(
                src_ref=src(peer), dst_ref=o_ref.at[peer], send_sem=send_sems.at[r - 1], recv_sem=recv_sems.at[r - 1],
                device_id=(px, py, pc), device_id_type=pl.DeviceIdType.MESH))
        for cp in sends:
            cp.start()
        for cp in recvs:
            cp.wait_recv()
        for cp in sends:
            cp.wait_send()
        local.wait()

    return pl.pallas_call(
        body, name=name, out_shape=SDS(shape, x.dtype),
        in_specs=[BS(memory_space=pl.ANY)], out_specs=BS(memory_space=pl.ANY),
        scratch_shapes=[pltpu.SemaphoreType.DMA((N_DEV - 1,)), pltpu.SemaphoreType.DMA((N_DEV - 1,)),
                        pltpu.SemaphoreType.DMA],
    )(x)


def _comm(name, srcs, out_shapes, src_views, dst_views, relay=False):
    n = len(srcs)

    def body(*refs):
        copies = _comm_copies(refs[:n], refs[n:2 * n], refs[2 * n:], src_views, dst_views, relay)
        _comm_start(copies)
        _comm_wait(copies)

    anyspec = BS(memory_space=pl.ANY)
    return pl.pallas_call(
        body, name=name, out_shape=tuple(out_shapes), in_specs=[anyspec] * n, out_specs=(anyspec,) * n,
        scratch_shapes=_comm_sems(n))(*srcs)


def _comm_sems(n):
    return [pltpu.SemaphoreType.DMA((N_DEV - 1, n)), pltpu.SemaphoreType.DMA((N_DEV - 1, n)),
            pltpu.SemaphoreType.DMA((n,))]


SIBLING = 1
OTHER_CHIP_SAME_CORE = (2, 4, 6)


def _comm_copies(s_refs, o_refs, sems, src_views, dst_views, relay=False):
    send_sems, recv_sems, local_sems = sems
    n = len(s_refs)
    ix, iy, ic = lax.axis_index("x"), lax.axis_index("y"), lax.axis_index("c")
    me = 4 * ix + 2 * iy + ic

    def at(r):
        px = 1 - ix if (r >> 2) & 1 else ix
        py = 1 - iy if (r >> 1) & 1 else iy
        pc = 1 - ic if r & 1 else ic
        return (px, py, pc), 4 * px + 2 * py + pc

    def copy(r, k, src, dst, to):
        return pltpu.make_async_remote_copy(src_ref=src, dst_ref=dst, send_sem=send_sems.at[r - 1, k],
                                            recv_sem=recv_sems.at[r - 1, k], device_id=to,
                                            device_id_type=pl.DeviceIdType.MESH)

    local = [pltpu.make_async_copy(src_views[k](s_refs[k], me), dst_views[k](o_refs[k], me), local_sems.at[k])
             for k in range(n)]
    sends, passed, arrivals = [], [], []
    for r in range(1, N_DEV):
        to, peer = at(r)
        for k in range(n):
            arrival = copy(r, k, src_views[k](s_refs[k], peer), dst_views[k](o_refs[k], peer), to)
            direct = not relay or r == SIBLING or r in OTHER_CHIP_SAME_CORE
            if direct:
                sends.append(copy(r, k, src_views[k](s_refs[k], peer), dst_views[k](o_refs[k], me), to))
            if relay and r in OTHER_CHIP_SAME_CORE:
                landed = dst_views[k](o_refs[k], peer)
                passed.append((arrival, copy(r ^ SIBLING, k, landed, landed, at(SIBLING)[0])))
            else:
                arrivals.append(arrival)
    return local, sends, passed, arrivals


def _comm_start(copies):
    local, sends, _, _ = copies
    for cp in local + sends:
        cp.start()


def _comm_wait(copies):
    local, sends, passed, arrivals = copies
    for arrival, onward in passed:
        arrival.wait_recv()
        onward.start()
    for cp in arrivals:
        cp.wait_recv()
    for cp in sends + [onward for _, onward in passed]:
        cp.wait_send()
    for cp in local:
        cp.wait()


def _call(body, *, name, grid, in_specs, out_specs, out_shape, args, scratch_shapes=(), sem=None, comm=None):
    single = not isinstance(out_shape, (tuple, list))
    out_shape = (out_shape,) if single else tuple(out_shape)
    out_specs = (out_specs,) if single else tuple(out_specs)
    if comm is None:
        outs = pl.pallas_call(body, name=name, grid=grid, in_specs=list(in_specs), out_specs=out_specs,
                              out_shape=out_shape, scratch_shapes=list(scratch_shapes),
                              compiler_params=_cp(*sem))(*args)
        return outs[0] if single else outs
    srcs, shapes, src_views, dst_views, relay = comm
    n, ni, no, ns = len(srcs), len(in_specs), len(out_shape), len(scratch_shapes)

    def hosted(*refs):
        ins, c_in = refs[:ni], refs[ni:ni + n]
        outs, c_out = refs[ni + n:ni + n + no], refs[ni + n + no:ni + 2 * n + no]
        scr, sems = refs[ni + 2 * n + no:ni + 2 * n + no + ns], refs[ni + 2 * n + no + ns:]
        ids = [pl.program_id(a) for a in range(len(grid))]
        first = functools.reduce(jnp.logical_and, [i == 0 for i in ids])
        last = functools.reduce(jnp.logical_and, [i == g - 1 for i, g in zip(ids, grid)])

        @pl.when(first)
        def _():
            _comm_start(_comm_copies(c_in, c_out, sems, src_views, dst_views, relay))

        body(*ins, *outs, *scr)

        @pl.when(last)
        def _():
            _comm_wait(_comm_copies(c_in, c_out, sems, src_views, dst_views, relay))

    anyspec = BS(memory_space=pl.ANY)
    outs = pl.pallas_call(
        hosted, name=name, grid=grid, in_specs=list(in_specs) + [anyspec] * n,
        out_specs=out_specs + (anyspec,) * n, out_shape=out_shape + tuple(shapes),
        scratch_shapes=list(scratch_shapes) + _comm_sems(n),
        compiler_params=_cp(*(["arbitrary"] * len(grid))))(*args, *srcs)
    return (outs[0] if single else outs[:no]), outs[no:]


def _assemble(pieces, starts, wb, npb, name):
    _, K, pw = pieces.shape
    tr = _tile(K, 256, 16)

    def body(p_ref, o_ref):
        for b in range(npb - 1):
            acc = None
            for j in range(N_DEV):
                if starts[j] <= b < starts[j] + wb:
                    blk = p_ref[j, :, (b - starts[j]) * LANES:(b - starts[j] + 1) * LANES]
                    acc = blk if acc is None else acc + blk
            o_ref[:, b * LANES:(b + 1) * LANES] = acc
        acc = p_ref[0, :, wb * LANES:]
        for j in range(1, N_DEV):
            acc = acc + p_ref[j, :, wb * LANES:]
        o_ref[:, (npb - 1) * LANES:] = acc

    return pl.pallas_call(
        body, name=name, grid=(K // tr,), out_shape=SDS((K, npb * LANES), pieces.dtype),
        in_specs=[BS((N_DEV, tr, pw), lambda i: (0, i, 0))], out_specs=BS((tr, npb * LANES), lambda i: (i, 0)),
        compiler_params=_cp("parallel"))(pieces)


def _sum8(x, name):
    _, R, C = x.shape
    tr = _tile(R, 256, 16)

    def body(x_ref, o_ref):
        acc = x_ref[0].astype(F32)
        for j in range(1, N_DEV):
            acc = acc + x_ref[j].astype(F32)
        o_ref[...] = acc

    return pl.pallas_call(
        body, name=name, grid=(R // tr,), out_shape=SDS((R, C), F32),
        in_specs=[BS((N_DEV, tr, C), lambda i: (0, i, 0))], out_specs=BS((tr, C), lambda i: (i, 0)),
        compiler_params=_cp("parallel"))(x)


_DIMS = {"nn": ((1,), (0,)), "nt": ((1,), (1,)), "tn": ((0,), (0,))}


def _mm(a, b, mode, out_dtype, name, res=None, comm=None):
    if mode == "nn":
        (M, K), N = a.shape, b.shape[1]
    elif mode == "nt":
        (M, K), N = a.shape, b.shape[0]
    else:
        (K, M), N = a.shape, b.shape[1]
    tm, tn, tk = _tile(M, 1024), _tile(N, 1024), _tile(K, 1024)
    nk = K // tk
    a_spec = BS((tk, tm), lambda i, j, k: (k, i)) if mode == "tn" else BS((tm, tk), lambda i, j, k: (i, k))
    b_spec = BS((tn, tk), lambda i, j, k: (j, k)) if mode == "nt" else BS((tk, tn), lambda i, j, k: (k, j))
    o_spec = BS((tm, tn), lambda i, j, k: (i, j))
    dims = (_DIMS[mode], ((), ()))

    def body(a_ref, b_ref, *rest):
        o_ref, acc = rest[-2], rest[-1]
        k = pl.program_id(2)

        @pl.when(k == 0)
        def _():
            acc[...] = jnp.zeros_like(acc)

        acc[...] += lax.dot_general(a_ref[...].astype(BF16), b_ref[...].astype(BF16), dims,
                                    preferred_element_type=F32)

        @pl.when(k == nk - 1)
        def _():
            r = acc[...]
            if res is not None:
                r = r + rest[0][...]
            o_ref[...] = r.astype(o_ref.dtype)

    ins, specs = [a, b], [a_spec, b_spec]
    if res is not None:
        ins.append(res)
        specs.append(o_spec)
    return _call(body, name=name, grid=(M // tm, N // tn, nk), out_shape=SDS((M, N), out_dtype), in_specs=specs,
                 out_specs=o_spec, scratch_shapes=[pltpu.VMEM((tm, tn), F32)],
                 sem=("parallel", "parallel", "arbitrary"), args=ins, comm=comm)


def _rms_fwd(x, g, name):
    S, D = x.shape
    ts = _tile(S, 256, 8)

    def body(x_ref, g_ref, o_ref):
        xv = x_ref[...]
        rstd = lax.rsqrt(jnp.mean(xv * xv, axis=-1, keepdims=True) + EPS)
        o_ref[...] = (xv * rstd * g_ref[...]).astype(o_ref.dtype)

    return pl.pallas_call(
        body, name=name, grid=(S // ts,), out_shape=SDS((S, D), BF16),
        in_specs=[BS((ts, D), lambda i: (i, 0)), BS((1, D), lambda i: (0, 0))],
        out_specs=BS((ts, D), lambda i: (i, 0)), compiler_params=_cp("parallel"))(x, g.reshape(1, D))


def _rms_bwd(x, g, dh, dres, name):
    S, D = x.shape
    ts = _tile(S, 256, 8)

    def body(x_ref, g_ref, dh_ref, dres_ref, dx_ref, dxb_ref, dg_ref):
        xv = x_ref[...]
        rstd = lax.rsqrt(jnp.mean(xv * xv, axis=-1, keepdims=True) + EPS)
        xn = xv * rstd
        dhv = dh_ref[...]

        @pl.when(pl.program_id(0) == 0)
        def _():
            dg_ref[...] = jnp.zeros_like(dg_ref)

        dg_ref[...] += jnp.sum(dhv * xn, axis=0, keepdims=True)
        dhn = dhv * g_ref[...]
        dxv = rstd * (dhn - xn * jnp.mean(dhn * xn, axis=-1, keepdims=True)) + dres_ref[...]
        dx_ref[...] = dxv
        dxb_ref[...] = dxv.astype(BF16)

    row = BS((ts, D), lambda i: (i, 0))
    vec = BS((1, D), lambda i: (0, 0))
    return pl.pallas_call(
        body, name=name, grid=(S // ts,), out_shape=(SDS((S, D), F32), SDS((S, D), BF16), SDS((1, D), F32)),
        in_specs=[row, vec, row, row], out_specs=(row, row, vec), compiler_params=_cp("arbitrary"),
    )(x, g.reshape(1, D), dh, dres)


def _final_loss(x, g, target, name):
    S, D = x.shape
    ts = _tile(S, 256, 8)

    def body(x_ref, g_ref, t_ref, loss_ref, dx_ref, dxb_ref, dg_ref):
        xv = x_ref[...]
        rstd = lax.rsqrt(jnp.mean(xv * xv, axis=-1, keepdims=True) + EPS)
        xn = xv * rstd
        gv = g_ref[...]
        e = xn * gv - t_ref[...]

        @pl.when(pl.program_id(0) == 0)
        def _():
            dg_ref[...] = jnp.zeros_like(dg_ref)
            loss_ref[...] = jnp.zeros_like(loss_ref)

        loss_ref[...] += 0.5 * jnp.sum(jnp.mean(e * e, axis=-1, keepdims=True), axis=0, keepdims=True)
        dy = e * (1.0 / D)
        dg_ref[...] += jnp.sum(dy * xn, axis=0, keepdims=True)
        dhn = dy * gv
        dxv = rstd * (dhn - xn * jnp.mean(dhn * xn, axis=-1, keepdims=True))
        dx_ref[...] = dxv
        dxb_ref[...] = dxv.astype(BF16)

    row = BS((ts, D), lambda i: (i, 0))
    vec = BS((1, D), lambda i: (0, 0))
    return pl.pallas_call(
        body, name=name, grid=(S // ts,),
        out_shape=(SDS((1, LANES), F32), SDS((S, D), F32), SDS((S, D), BF16), SDS((1, D), F32)),
        in_specs=[row, vec, row], out_specs=(BS((1, LANES), lambda i: (0, 0)), row, row, vec),
        compiler_params=_cp("arbitrary"))(x, g.reshape(1, D), target)


def _shift_down(z, k):
    rows = lax.broadcasted_iota(jnp.int32, z.shape, 0)
    return jnp.where(rows >= k, pltpu.roll(z, k, 0), 0.0)


def _shift_up(z, k):
    n = z.shape[0]
    rows = lax.broadcasted_iota(jnp.int32, z.shape, 0)
    return jnp.where(rows < n - k, pltpu.roll(z, n - k, 0), 0.0)


def _cumsum_rows(z, reverse=False):
    n = z.shape[0]
    k = 1
    while k < n:
        z = z + (_shift_up(z, k) if reverse else _shift_down(z, k))
        k *= 2
    return z


def _conv3(z, w_ref):
    return w_ref[0:1, :] * _shift_down(z, 2) + w_ref[1:2, :] * _shift_down(z, 1) + w_ref[2:3, :] * z


def _conv3_bwd(z, dy, w_ref, dw_ref):
    dw_ref[0:1, :] = jnp.sum(dy * _shift_down(z, 2), axis=0, keepdims=True)
    dw_ref[1:2, :] = jnp.sum(dy * _shift_down(z, 1), axis=0, keepdims=True)
    dw_ref[2:3, :] = jnp.sum(dy * z, axis=0, keepdims=True)
    return w_ref[2:3, :] * dy + w_ref[1:2, :] * _shift_up(dy, 1) + w_ref[0:1, :] * _shift_up(dy, 2)


def _fox_prep(proj, bias, col, name):
    S = proj.shape[0]

    def body(a_ref, b_ref, f_ref):
        z = a_ref[...] + b_ref[...]
        logf = jnp.minimum(z, 0.0) - jnp.log(1.0 + jnp.exp(-jnp.abs(z)))
        rest = _cumsum_rows(logf)
        for k in range(3):
            piece = rest.astype(BF16)
            f_ref[k] = piece
            rest = rest - piece.astype(F32)

    return pl.pallas_call(
        body, name=name, grid=(1,), out_shape=SDS((3, S, LANES), BF16),
        in_specs=[BS((S, LANES), lambda i: (0, col)), BS((1, LANES), lambda i: (0, 0))],
        out_specs=BS((3, S, LANES), lambda i: (0, 0, 0)), compiler_params=_cp("arbitrary"))(proj, bias)


def _fox_prep_bwd(proj, bias, col, df, name):
    S = proj.shape[0]

    def body(a_ref, b_ref, df_ref, da_ref, db_ref):
        z = a_ref[...] + b_ref[...]
        da = _cumsum_rows(df_ref[...], reverse=True) * _sigmoid(-z)
        da_ref[...] = da.astype(BF16)
        db_ref[...] = jnp.sum(da, axis=0, keepdims=True)

    full = BS((S, LANES), lambda i: (0, 0))
    vec = BS((1, LANES), lambda i: (0, 0))
    return pl.pallas_call(
        body, name=name, grid=(1,), out_shape=(SDS((S, LANES), BF16), SDS((1, LANES), F32)),
        in_specs=[BS((S, LANES), lambda i: (0, col)), vec, full], out_specs=(full, vec),
        compiler_params=_cp("arbitrary"))(proj, bias, df)


def _fox_scores(qs, kv, fqv, fkv, i, j, t, masked):
    s = lax.dot_general(qs, kv, (_DIMS["nt"], ((), ())), preferred_element_type=F32) + fqv - fkv
    if not masked:
        return s
    row = i * t + lax.broadcasted_iota(jnp.int32, (t, t), 0)
    col = j * t + lax.broadcasted_iota(jnp.int32, (t, t), 1)
    return jnp.where(col <= row, s, NEG)


def _fox_fwd(q, k, v, fq, fk, name, comm=None):
    H, S, Dh = q.shape
    t = _tile(S, 256)
    n = S // t
    scale = Dh ** -0.5

    def body(q_ref, k_ref, v_ref, fq_ref, fk_ref, o_ref, lse_ref):
        i = pl.program_id(1)
        qs, fqv = q_ref[0] * scale, fq_ref[0]

        def step(masked, j, carry):
            m, l, acc = carry
            s = _fox_scores(qs, k_ref[0, j], fqv, fk_ref[0, j], i, j, t, masked)
            mn = jnp.maximum(m, jnp.max(s, axis=-1, keepdims=True))
            al = jnp.exp(m - mn)
            p = jnp.exp(s - mn)
            l = l * al + jnp.sum(p, axis=-1, keepdims=True)
            acc = acc * al + jnp.dot(p.astype(BF16), v_ref[0, j], preferred_element_type=F32)
            return mn, l, acc

        init = (jnp.full((t, 1), NEG, F32), jnp.zeros((t, 1), F32), jnp.zeros((t, Dh), F32))
        m, l, acc = step(True, i, lax.fori_loop(0, i, functools.partial(step, False), init))
        o_ref[0] = acc / l
        lse_ref[0] = m + jnp.log(l)

    blk = BS((1, t, Dh), lambda h, i: (h, i, 0))
    col = BS((1, t, 1), lambda h, i: (h, i, 0))
    res = BS((1, n, t, Dh), lambda h, i: (h, 0, 0, 0))
    return _call(
        body, name=name, grid=(H, n), out_shape=(SDS((H, S, Dh), F32), SDS((H, S, 1), F32)),
        in_specs=[blk, res, res, col, BS((1, n, 1, t), lambda h, i: (h, 0, 0, 0))], out_specs=(blk, col),
        sem=("parallel", "arbitrary"), args=(q, k.reshape(H, n, t, Dh), v.reshape(H, n, t, Dh), fq, fk), comm=comm)


def _fox_bwd_dq(q, k, v, fq, fk, do, lse, name, comm=None):
    H, S, Dh = q.shape
    t = _tile(S, 256)
    n = S // t
    scale = Dh ** -0.5

    def body(q_ref, k_ref, v_ref, fq_ref, fk_ref, do_ref, lse_ref, dq_ref, dl_ref):
        i = pl.program_id(1)
        qs, fqv, dob, lsev = q_ref[0] * scale, fq_ref[0], do_ref[0], lse_ref[0]

        def p_dp(masked, j):
            p = jnp.exp(_fox_scores(qs, k_ref[0, j], fqv, fk_ref[0, j], i, j, t, masked) - lsev)
            return p, lax.dot_general(dob, v_ref[0, j], (_DIMS["nt"], ((), ())), preferred_element_type=F32)

        def sum_step(masked, j, delta):
            p, dp = p_dp(masked, j)
            return delta + jnp.sum(p * dp, axis=-1, keepdims=True)

        delta = sum_step(True, i, lax.fori_loop(0, i, functools.partial(sum_step, False), jnp.zeros((t, 1), F32)))

        def step(masked, j, dq):
            p, dp = p_dp(masked, j)
            return dq + jnp.dot((p * (dp - delta)).astype(BF16), k_ref[0, j], preferred_element_type=F32)

        dq = step(True, i, lax.fori_loop(0, i, functools.partial(step, False), jnp.zeros((t, Dh), F32)))
        dq_ref[0] = dq * scale
        dl_ref[0] = delta

    blk = BS((1, t, Dh), lambda h, i: (h, i, 0))
    col = BS((1, t, 1), lambda h, i: (h, i, 0))
    res = BS((1, n, t, Dh), lambda h, i: (h, 0, 0, 0))
    return _call(
        body, name=name, grid=(H, n), out_shape=(SDS((H, S, Dh), F32), SDS((H, S, 1), F32)),
        in_specs=[blk, res, res, col, BS((1, n, 1, t), lambda h, i: (h, 0, 0, 0)), blk, col],
        out_specs=(blk, col), sem=("parallel", "arbitrary"), comm=comm,
        args=(q, k.reshape(H, n, t, Dh), v.reshape(H, n, t, Dh), fq, fk, do, lse))


def _fox_bwd_dkv(q, k, v, fq, fk, do, lse, delta, name, comm=None):
    H, S, Dh = q.shape
    t = _tile(S, 256)
    n = S // t
    scale = Dh ** -0.5
    tn_dims = (_DIMS["tn"], ((), ()))

    def body(q_ref, k_ref, v_ref, fq_ref, fk_ref, do_ref, lse_ref, dl_ref, dk_ref, dv_ref, dfk_ref):
        j = pl.program_id(1)
        kv, vv, fkv = k_ref[0], v_ref[0], fk_ref[0, 0]

        def step(masked, i, carry):
            dk, dv, dfk = carry
            qs, dob = q_ref[0, i] * scale, do_ref[0, i]
            p = jnp.exp(_fox_scores(qs, kv, fq_ref[0, i], fkv, i, j, t, masked) - lse_ref[0, i])
            dv = dv + lax.dot_general(p.astype(BF16), dob, tn_dims, preferred_element_type=F32)
            dp = lax.dot_general(dob, vv, (_DIMS["nt"], ((), ())), preferred_element_type=F32)
            ds = p * (dp - dl_ref[0, i])
            dk = dk + lax.dot_general(ds.astype(BF16), qs, tn_dims, preferred_element_type=F32)
            return dk, dv, dfk - jnp.sum(ds, axis=0, keepdims=True)

        init = (jnp.zeros((t, Dh), F32), jnp.zeros((t, Dh), F32), jnp.zeros((1, t), F32))
        dk, dv, dfk = lax.fori_loop(j + 1, n, functools.partial(step, False), step(True, j, init))
        dk_ref[0] = dk
        dv_ref[0] = dv
        dfk_ref[0, 0] = dfk

    blk = BS((1, t, Dh), lambda h, j: (h, j, 0))
    res = BS((1, n, t, Dh), lambda h, j: (h, 0, 0, 0))
    rcol = BS((1, n, t, 1), lambda h, j: (h, 0, 0, 0))
    frow = BS((1, 1, 1, t), lambda h, j: (h, j, 0, 0))
    return _call(
        body, name=name, grid=(H, n),
        out_shape=(SDS((H, S, Dh), F32), SDS((H, S, Dh), F32), SDS((H, n, 1, t), F32)),
        in_specs=[res, blk, blk, rcol, frow, res, rcol, rcol], out_specs=(blk, blk, frow),
        sem=("parallel", "arbitrary"), comm=comm,
        args=(q.reshape(H, n, t, Dh), k, v, fq.reshape(H, n, t, 1), fk, do.astype(BF16).reshape(H, n, t, Dh),
              lse.reshape(H, n, t, 1), delta.reshape(H, n, t, 1)))


FOX_HEADS_PER_STEP = 2


def _fox_q_scale(Dh):
    lane = lax.broadcasted_iota(jnp.int32, (1, 2 * Dh), 1)
    return jnp.where(lane < Dh, Dh ** -0.5, 1.0).astype(BF16)


def _foxt_scores(ka, qs, i, j, t, masked):
    s = lax.dot_general(ka, qs, (_DIMS["nt"], ((), ())), preferred_element_type=F32)
    if not masked:
        return s
    key = j * t + lax.broadcasted_iota(jnp.int32, (t, t), 0)
    qry = i * t + lax.broadcasted_iota(jnp.int32, (t, t), 1)
    return jnp.where(key <= qry, s, NEG)


def _foxt_fwd(q_aug, k_aug, vT, name, comm=None):
    H, S, W = q_aug.shape
    Dh = W // 2
    t = _tile(S, 256)
    n = S // t
    hb = FOX_HEADS_PER_STEP

    def body(q_ref, k_ref, v_ref, o_ref, lse_ref):
        i = pl.program_id(1)
        qs = [q_ref[h] * _fox_q_scale(Dh) for h in range(hb)]

        def step(masked, j, carry):
            out = []
            for h in range(hb):
                m, l, acc = carry[h]
                s = _foxt_scores(k_ref[h, j], qs[h], i, j, t, masked)
                mn = jnp.maximum(m, jnp.max(s, axis=0, keepdims=True))
                al = jnp.exp(m - mn)
                p = jnp.exp(s - mn)
                l = l * al + jnp.sum(p, axis=0, keepdims=True)
                acc = acc * al + jnp.dot(v_ref[h, j], p.astype(BF16), preferred_element_type=F32)
                out.append((mn, l, acc))
            return tuple(out)

        init = tuple((jnp.full((1, t), NEG, F32), jnp.zeros((1, t), F32), jnp.zeros((Dh, t), F32))
                     for _ in range(hb))
        fin = step(True, i, lax.fori_loop(0, i, functools.partial(step, False), init))
        for h in range(hb):
            m, l, acc = fin[h]
            o_ref[h] = acc / l
            lse_ref[h] = m + jnp.log(l)

    return _call(
        body, name=name, grid=(H // hb, n), out_shape=(SDS((H, Dh, S), F32), SDS((H, 1, S), F32)),
        in_specs=[BS((hb, t, W), lambda g, i: (g, i, 0)), BS((hb, n, t, W), lambda g, i: (g, 0, 0, 0)),
                  BS((hb, n, Dh, t), lambda g, i: (g, 0, 0, 0))],
        out_specs=(BS((hb, Dh, t), lambda g, i: (g, 0, i)), BS((hb, 1, t), lambda g, i: (g, 0, i))),
        sem=("parallel", "arbitrary"), args=(q_aug, k_aug.reshape(H, n, t, W), vT), comm=comm)


def _foxt_bwd_dq(q_aug, k_aug, kT, v, do, lse, name, comm=None):
    H, S, W = q_aug.shape
    Dh = W // 2
    t = _tile(S, 256)
    n = S // t
    hb = FOX_HEADS_PER_STEP

    def body(q_ref, k_ref, kt_ref, v_ref, do_ref, lse_ref, dq_ref, dl_ref):
        i = pl.program_id(1)
        qs = [q_ref[h] * _fox_q_scale(Dh) for h in range(hb)]

        def p_dp(h, masked, j):
            p = jnp.exp(_foxt_scores(k_ref[h, j], qs[h], i, j, t, masked) - lse_ref[h])
            return p, lax.dot_general(v_ref[h, j], do_ref[h], (_DIMS["nt"], ((), ())), preferred_element_type=F32)

        def sum_step(masked, j, delta):
            out = []
            for h in range(hb):
                p, dp = p_dp(h, masked, j)
                out.append(delta[h] + jnp.sum(p * dp, axis=0, keepdims=True))
            return tuple(out)

        zero = tuple(jnp.zeros((1, t), F32) for _ in range(hb))
        delta = sum_step(True, i, lax.fori_loop(0, i, functools.partial(sum_step, False), zero))

        def step(masked, j, dq):
            out = []
            for h in range(hb):
                p, dp = p_dp(h, masked, j)
                ds = (p * (dp - delta[h])).astype(BF16)
                out.append(dq[h] + jnp.dot(kt_ref[h, j], ds, preferred_element_type=F32))
            return tuple(out)

        zero = tuple(jnp.zeros((W, t), F32) for _ in range(hb))
        dq = step(True, i, lax.fori_loop(0, i, functools.partial(step, False), zero))
        for h in range(hb):
            dq_ref[h] = dq[h] * Dh ** -0.5
            dl_ref[h] = delta[h]

    row = BS((hb, 1, t), lambda g, i: (g, 0, i))
    blk = lambda w: BS((hb, t, w), lambda g, i: (g, i, 0))
    return _call(
        body, name=name, grid=(H // hb, n), out_shape=(SDS((H, W, S), F32), SDS((H, 1, S), F32)),
        in_specs=[blk(W), BS((hb, n, t, W), lambda g, i: (g, 0, 0, 0)), BS((hb, n, W, t), lambda g, i: (g, 0, 0, 0)),
                  BS((hb, n, t, Dh), lambda g, i: (g, 0, 0, 0)), blk(Dh), row],
        out_specs=(BS((hb, W, t), lambda g, i: (g, 0, i)), row), sem=("parallel", "arbitrary"), comm=comm,
        args=(q_aug, k_aug.reshape(H, n, t, W), kT, v.reshape(H, n, t, Dh), do, lse))


def _foxt_bwd_dkv(q_aug, k_aug, v, do, lse, delta, name, comm=None):
    H, S, W = q_aug.shape
    Dh = W // 2
    t = _tile(S, 256)
    n = S // t
    hb = FOX_HEADS_PER_STEP

    def body(q_ref, k_ref, v_ref, do_ref, lse_ref, dl_ref, dk_ref, dv_ref, dfk_ref):
        j = pl.program_id(1)

        def step(masked, i, carry):
            out = []
            for h in range(hb):
                dk, dv, dfk = carry[h]
                qs, dob = q_ref[h, i] * _fox_q_scale(Dh), do_ref[h, i]
                p = jnp.exp(_foxt_scores(k_ref[h], qs, i, j, t, masked) - lse_ref[h, i])
                dv = dv + jnp.dot(p.astype(BF16), dob, preferred_element_type=F32)
                dp = lax.dot_general(v_ref[h], dob, (_DIMS["nt"], ((), ())), preferred_element_type=F32)
                ds = p * (dp - dl_ref[h, i])
                dk = dk + jnp.dot(ds.astype(BF16), qs, preferred_element_type=F32)
                out.append((dk, dv, dfk - jnp.sum(ds, axis=1, keepdims=True)))
            return tuple(out)

        init = tuple((jnp.zeros((t, W), F32), jnp.zeros((t, Dh), F32), jnp.zeros((t, 1), F32)) for _ in range(hb))
        fin = lax.fori_loop(j + 1, n, functools.partial(step, False), step(True, j, init))
        for h in range(hb):
            dk_ref[h], dv_ref[h], dfk_ref[h] = fin[h]

    blk = lambda w: BS((hb, t, w), lambda g, j: (g, j, 0))
    res = lambda w: BS((hb, n, t, w), lambda g, j: (g, 0, 0, 0))
    rows = BS((hb, n, 1, t), lambda g, j: (g, 0, 0, 0))
    return _call(
        body, name=name, grid=(H // hb, n),
        out_shape=(SDS((H, S, W), F32), SDS((H, S, Dh), F32), SDS((H, S, 1), F32)),
        in_specs=[res(W), blk(W), blk(Dh), res(Dh), rows, rows], out_specs=(blk(W), blk(Dh), blk(1)),
        sem=("parallel", "arbitrary"), comm=comm,
        args=(q_aug.reshape(H, n, t, W), k_aug, v, do.reshape(H, n, t, Dh), lse.reshape(H, n, 1, t),
              delta.reshape(H, n, 1, t)))


def _sconv_fwd(proj, w, col0, width, name):
    S = proj.shape[0]
    tc = LANES
    nb, c0 = width // tc, col0 // tc

    def body(x_ref, gb_ref, gc_ref, w_ref, o_ref):
        o_ref[...] = (gb_ref[...] * _conv3(gc_ref[...] * x_ref[...], w_ref)).astype(BF16)

    return pl.pallas_call(
        body, name=name, grid=(nb,), out_shape=SDS((S, width), BF16),
        in_specs=[BS((S, tc), lambda j: (0, c0 + j)), BS((S, tc), lambda j: (0, c0 + nb + j)),
                  BS((S, tc), lambda j: (0, c0 + 2 * nb + j)), BS((3, tc), lambda j: (0, j))],
        out_specs=BS((S, tc), lambda j: (0, j)), compiler_params=_cp("parallel"))(proj, proj, proj, w)


def _sconv_bwd(proj, w, col0, width, do, name):
    S = proj.shape[0]
    tc = LANES
    nb, c0 = width // tc, col0 // tc

    def body(x_ref, gb_ref, gc_ref, w_ref, do_ref, dx_ref, dgb_ref, dgc_ref, dw_ref):
        xb, gb, gc, dov = x_ref[...], gb_ref[...], gc_ref[...], do_ref[...]
        z = gc * xb
        dgb_ref[...] = (dov * _conv3(z, w_ref)).astype(BF16)
        dz = _conv3_bwd(z, dov * gb, w_ref, dw_ref)
        dgc_ref[...] = (dz * xb).astype(BF16)
        dx_ref[...] = (dz * gc).astype(BF16)

    out = BS((S, tc), lambda j: (0, j))
    wspec = BS((3, tc), lambda j: (0, j))
    return pl.pallas_call(
        body, name=name, grid=(nb,),
        out_shape=(SDS((S, width), BF16), SDS((S, width), BF16), SDS((S, width), BF16), SDS((3, width), F32)),
        in_specs=[BS((S, tc), lambda j: (0, c0 + j)), BS((S, tc), lambda j: (0, c0 + nb + j)),
                  BS((S, tc), lambda j: (0, c0 + 2 * nb + j)), wspec, out],
        out_specs=(out, out, out, wspec), compiler_params=_cp("parallel"))(proj, proj, proj, w, do)


def _gelu(x):
    return 0.5 * x * (1.0 + jnp.tanh(0.7978845608028654 * (x + 0.044715 * x * x * x)))


def _gelu_grad(x):
    u = 0.7978845608028654 * (x + 0.044715 * x * x * x)
    th = jnp.tanh(u)
    return 0.5 * (1.0 + th) + 0.5 * x * (1.0 - th * th) * 0.7978845608028654 * (1.0 + 3.0 * 0.044715 * x * x)


def _tril_w(w_ref, g):
    r = lax.broadcasted_iota(jnp.int32, (SGU_CHUNK, SGU_CHUNK), 0)
    c = lax.broadcasted_iota(jnp.int32, (SGU_CHUNK, SGU_CHUNK), 1)
    return jnp.where(c <= r, w_ref[g], 0.0), c <= r


def _sgu_mixed(vn, w_ref, b_ref, nch, G):
    rows = []
    for ch in range(nch):
        cols = []
        for g in range(G):
            wt, _ = _tril_w(w_ref, g)
            blk = vn[ch * SGU_CHUNK:(ch + 1) * SGU_CHUNK, g * LANES:(g + 1) * LANES]
            cols.append(jnp.dot(wt.astype(BF16), blk.astype(BF16), preferred_element_type=F32) + b_ref[g])
        rows.append(jnp.concatenate(cols, axis=1))
    return jnp.concatenate(rows, axis=0)


def _sgu_fwd(proj, gn, w, b, col0, width, name):
    S = proj.shape[0]
    G = w.shape[0]
    ts = _tile(S, 512)
    nch = ts // SGU_CHUNK
    c0 = col0 // width

    def body(u_ref, v_ref, gn_ref, w_ref, b_ref, o_ref):
        cv = _gelu(v_ref[...])
        rstd = lax.rsqrt(jnp.mean(cv * cv, axis=-1, keepdims=True) + EPS)
        mixed = _sgu_mixed(cv * rstd * gn_ref[...], w_ref, b_ref, nch, G)
        o_ref[...] = (_gelu(u_ref[...]) * mixed).astype(BF16)

    return pl.pallas_call(
        body, name=name, grid=(S // ts,), out_shape=SDS((S, width), BF16),
        in_specs=[BS((ts, width), lambda i: (i, c0)), BS((ts, width), lambda i: (i, c0 + 1)),
                  BS((1, width), lambda i: (0, 0)), BS(w.shape, lambda i: (0, 0, 0)), BS(b.shape, lambda i: (0, 0, 0))],
        out_specs=BS((ts, width), lambda i: (i, 0)), compiler_params=_cp("parallel"))(proj, proj, gn, w, b)


def _sgu_bwd(proj, gn, w, b, col0, width, do, name):
    S = proj.shape[0]
    G = w.shape[0]
    ts = _tile(S, 512)
    nch = ts // SGU_CHUNK
    c0 = col0 // width

    def body(u_ref, v_ref, gn_ref, w_ref, b_ref, do_ref, du_ref, dv_ref, dw_ref, db_ref, dgn_ref):
        uin, vin, dov, gnv = u_ref[...], v_ref[...], do_ref[...], gn_ref[...]
        cu, cv = _gelu(uin), _gelu(vin)
        rstd = lax.rsqrt(jnp.mean(cv * cv, axis=-1, keepdims=True) + EPS)
        vhat = cv * rstd
        vn = vhat * gnv
        mixed = _sgu_mixed(vn, w_ref, b_ref, nch, G)
        du_ref[...] = (dov * mixed * _gelu_grad(uin)).astype(BF16)
        dmix = dov * cu

        @pl.when(pl.program_id(0) == 0)
        def _():
            dw_ref[...] = jnp.zeros_like(dw_ref)
            db_ref[...] = jnp.zeros_like(db_ref)
            dgn_ref[...] = jnp.zeros_like(dgn_ref)

        rows = []
        for ch in range(nch):
            cols = []
            for g in range(G):
                wt, mask = _tril_w(w_ref, g)
                sl = (slice(ch * SGU_CHUNK, (ch + 1) * SGU_CHUNK), slice(g * LANES, (g + 1) * LANES))
                dm, vb = dmix[sl], vn[sl].astype(BF16)
                db_ref[g] += jnp.sum(dm, axis=-1, keepdims=True)
                dwg = lax.dot_general(dm.astype(BF16), vb, (_DIMS["nt"], ((), ())), preferred_element_type=F32)
                dw_ref[g] += jnp.where(mask, dwg, 0.0)
                cols.append(lax.dot_general(wt.astype(BF16), dm.astype(BF16), (_DIMS["tn"], ((), ())),
                                            preferred_element_type=F32))
            rows.append(jnp.concatenate(cols, axis=1))
        dvn = jnp.concatenate(rows, axis=0)
        dgn_ref[...] += jnp.sum(dvn * vhat, axis=0, keepdims=True)
        dvh = dvn * gnv
        dcv = rstd * (dvh - vhat * jnp.mean(dvh * vhat, axis=-1, keepdims=True))
        dv_ref[...] = (dcv * _gelu_grad(vin)).astype(BF16)

    row = BS((ts, width), lambda i: (i, 0))
    wsp, bsp, gsp = BS(w.shape, lambda i: (0, 0, 0)), BS(b.shape, lambda i: (0, 0, 0)), BS((1, width), lambda i: (0, 0))
    return pl.pallas_call(
        body, name=name, grid=(S // ts,),
        out_shape=(SDS((S, width), BF16), SDS((S, width), BF16), SDS(w.shape, F32), SDS(b.shape, F32),
                   SDS((1, width), F32)),
        in_specs=[BS((ts, width), lambda i: (i, c0)), BS((ts, width), lambda i: (i, c0 + 1)), gsp, wsp, bsp, row],
        out_specs=(row, row, wsp, bsp, gsp), compiler_params=_cp("arbitrary"))(proj, proj, gn, w, b, do)


def _rope_tables(positions, inv, name):
    S = positions.shape[0]
    ts = _tile(S, 512, 8)
    half = ROPE_DIM // 2

    def body(p_ref, inv_ref, c_ref, sa_ref, sb_ref):
        ang = p_ref[...].astype(F32) * inv_ref[...]
        lane = lax.broadcasted_iota(jnp.int32, (ts, LANES), 1) % HEAD_DIM
        sn = jnp.sin(ang)
        c_ref[...] = jnp.cos(ang)
        sa_ref[...] = jnp.where(lane < half, -sn, 0.0)
        sb_ref[...] = jnp.where((lane >= half) & (lane < ROPE_DIM), sn, 0.0)

    out = BS((ts, LANES), lambda i: (i, 0))
    return pl.pallas_call(
        body, name=name, grid=(S // ts,), out_shape=(SDS((S, LANES), F32),) * 3,
        in_specs=[BS((ts, 1), lambda i: (i, 0)), BS((1, LANES), lambda i: (0, 0))], out_specs=(out, out, out),
        compiler_params=_cp("parallel"))(positions, inv)


def _rope_apply(x, col0, width, tables, sign, out_dtype, name):
    S = x.shape[0]
    ts = _tile(S, 512, 8)
    tc = 2 * LANES
    nb, c0 = width // tc, col0 // tc
    half = ROPE_DIM // 2

    def body(x_ref, c_ref, sa_ref, sb_ref, o_ref):
        xv = x_ref[...].astype(F32)
        wide = lambda r: jnp.concatenate([r[...], r[...]], axis=1)
        y = (xv * wide(c_ref) + pltpu.roll(xv, tc - half, 1) * (sign * wide(sa_ref))
             + pltpu.roll(xv, half, 1) * (sign * wide(sb_ref)))
        o_ref[...] = y.astype(o_ref.dtype)

    tab = BS((ts, LANES), lambda i, j: (i, 0))
    return pl.pallas_call(
        body, name=name, grid=(S // ts, nb), out_shape=SDS((S, width), out_dtype),
        in_specs=[BS((ts, tc), lambda i, j: (i, c0 + j)), tab, tab, tab], out_specs=BS((ts, tc), lambda i, j: (i, j)),
        compiler_params=_cp("parallel", "parallel"))(x, *tables)


def _dil_masks(hh, m, bpcs, hpg):
    g = hh // hpg
    bpc = jnp.where(g == 0, bpcs[0], jnp.where(g == 1, bpcs[1], bpcs[2]))
    r = lax.broadcasted_iota(jnp.int32, (DIL_SPAN, DIL_SPAN), 0)
    c = lax.broadcasted_iota(jnp.int32, (DIL_SPAN, DIL_SPAN), 1)
    return (c >= r) & (lax.rem(m, bpc) != 0), c <= r


def _nt(a, b):
    return lax.dot_general(a, b, (_DIMS["nt"], ((), ())), preferred_element_type=F32)


def _tn(a, b):
    return lax.dot_general(a, b, (_DIMS["tn"], ((), ())), preferred_element_type=F32)


def _dil_fwd(q, k, v, bpcs, hpg, name, comm=None):
    HH, S, Dh = q.shape
    nb = S // DIL_SPAN
    scale = Dh ** -0.5

    def body(q_ref, kp_ref, kc_ref, vp_ref, vc_ref, o_ref, lse_ref):
        mp, mc = _dil_masks(pl.program_id(0), pl.program_id(1), bpcs, hpg)
        qv = q_ref[0]
        sp = jnp.where(mp, _nt(qv, kp_ref[0]) * scale, NEG)
        sc = jnp.where(mc, _nt(qv, kc_ref[0]) * scale, NEG)
        mx = jnp.maximum(jnp.max(sp, axis=-1, keepdims=True), jnp.max(sc, axis=-1, keepdims=True))
        pp, pc = jnp.exp(sp - mx), jnp.exp(sc - mx)
        l = jnp.sum(pp, axis=-1, keepdims=True) + jnp.sum(pc, axis=-1, keepdims=True)
        acc = (jnp.dot(pp.astype(BF16), vp_ref[0], preferred_element_type=F32)
               + jnp.dot(pc.astype(BF16), vc_ref[0], preferred_element_type=F32))
        o_ref[0] = acc / l
        lse_ref[0] = mx + jnp.log(l)

    cur = BS((1, DIL_SPAN, Dh), lambda h, m: (h, m, 0))
    prev = BS((1, DIL_SPAN, Dh), lambda h, m: (h, jnp.maximum(m - 1, 0), 0))
    return _call(
        body, name=name, grid=(HH, nb), out_shape=(SDS((HH, S, Dh), F32), SDS((HH, S, 1), F32)),
        in_specs=[cur, prev, cur, prev, cur], out_specs=(cur, BS((1, DIL_SPAN, 1), lambda h, m: (h, m, 0))),
        sem=("parallel", "parallel"), args=(q, k, k, v, v), comm=comm)


def _dil_bwd_dq(q, k, v, do, lse, coef, bpcs, hpg, name, comm=None):
    HH, S, Dh = q.shape
    nb = S // DIL_SPAN
    scale = Dh ** -0.5

    def body(q_ref, kp_ref, kc_ref, vp_ref, vc_ref, do_ref, lse_ref, cf_ref, dq_ref):
        mp, mc = _dil_masks(pl.program_id(0), pl.program_id(1), bpcs, hpg)
        qv, dob, lsev, cf = q_ref[0], do_ref[0], lse_ref[0], cf_ref[0]
        pp = jnp.exp(jnp.where(mp, _nt(qv, kp_ref[0]) * scale, NEG) - lsev)
        pc = jnp.exp(jnp.where(mc, _nt(qv, kc_ref[0]) * scale, NEG) - lsev)
        dsp = pp * (_nt(dob, vp_ref[0]) + cf)
        dsc = pc * (_nt(dob, vc_ref[0]) + cf)
        dq_ref[0] = (jnp.dot(dsp.astype(BF16), kp_ref[0], preferred_element_type=F32)
                     + jnp.dot(dsc.astype(BF16), kc_ref[0], preferred_element_type=F32)) * scale

    cur = BS((1, DIL_SPAN, Dh), lambda h, m: (h, m, 0))
    prev = BS((1, DIL_SPAN, Dh), lambda h, m: (h, jnp.maximum(m - 1, 0), 0))
    col = BS((1, DIL_SPAN, 1), lambda h, m: (h, m, 0))
    return _call(
        body, name=name, grid=(HH, nb), out_shape=SDS((HH, S, Dh), F32),
        in_specs=[cur, prev, cur, prev, cur, cur, col, col], out_specs=cur,
        sem=("parallel", "parallel"), args=(q, k, k, v, v, do, lse, coef), comm=comm)


def _dil_bwd_dkv(q, k, v, do, lse, coef, bpcs, hpg, name, comm=None):
    HH, S, Dh = q.shape
    nb = S // DIL_SPAN
    scale = Dh ** -0.5

    def body(k_ref, v_ref, qc_ref, qn_ref, doc_ref, don_ref, lc_ref, ln_ref, cc_ref, cn_ref, dk_ref, dv_ref):
        hh, m = pl.program_id(0), pl.program_id(1)
        _, mc = _dil_masks(hh, m, bpcs, hpg)
        mp, _ = _dil_masks(hh, m + 1, bpcs, hpg)
        mp = mp & (m + 1 < nb)
        kv, vv = k_ref[0], v_ref[0]
        pc = jnp.exp(jnp.where(mc, _nt(qc_ref[0], kv) * scale, NEG) - lc_ref[0])
        pn = jnp.exp(jnp.where(mp, _nt(qn_ref[0], kv) * scale, NEG) - ln_ref[0])
        dsc = pc * (_nt(doc_ref[0], vv) + cc_ref[0])
        dsn = pn * (_nt(don_ref[0], vv) + cn_ref[0])
        dv_ref[0] = _tn(pc.astype(BF16), doc_ref[0]) + _tn(pn.astype(BF16), don_ref[0])
        dk_ref[0] = (_tn(dsc.astype(BF16), qc_ref[0]) + _tn(dsn.astype(BF16), qn_ref[0])) * scale

    cur = BS((1, DIL_SPAN, Dh), lambda h, m: (h, m, 0))
    nxt = BS((1, DIL_SPAN, Dh), lambda h, m: (h, jnp.minimum(m + 1, nb - 1), 0))
    col = BS((1, DIL_SPAN, 1), lambda h, m: (h, m, 0))
    ncol = BS((1, DIL_SPAN, 1), lambda h, m: (h, jnp.minimum(m + 1, nb - 1), 0))
    return _call(
        body, name=name, grid=(HH, nb), out_shape=(SDS((HH, S, Dh), F32), SDS((HH, S, Dh), F32)),
        in_specs=[cur, cur, cur, nxt, cur, nxt, col, ncol, col, ncol], out_specs=(cur, cur),
        sem=("parallel", "parallel"), args=(k, v, q, q, do, do, lse, lse, coef, coef), comm=comm)


def _dil_merge(o3, lse3, name):
    G, H, S, Dh = o3.shape
    ts = _tile(S, 512, 8)

    def body(o_ref, l_ref, out_ref):
        lv = [l_ref[g, 0] for g in range(G)]
        mx = functools.reduce(jnp.maximum, lv)
        ex = [jnp.exp(v - mx) for v in lv]
        den = functools.reduce(lambda a, b: a + b, ex)
        out_ref[0] = functools.reduce(lambda a, b: a + b, [(ex[g] / den) * o_ref[g, 0] for g in range(G)])

    return pl.pallas_call(
        body, name=name, grid=(H, S // ts), out_shape=SDS((H, S, Dh), F32),
        in_specs=[BS((G, 1, ts, Dh), lambda h, i: (0, h, i, 0)), BS((G, 1, ts, 1), lambda h, i: (0, h, i, 0))],
        out_specs=BS((1, ts, Dh), lambda h, i: (h, i, 0)), compiler_params=_cp("parallel", "parallel"))(o3, lse3)


def _dil_merge_bwd(o3, lse3, do, name):
    G, H, S, Dh = o3.shape
    ts = _tile(S, 512, 8)

    def body(o_ref, l_ref, do_ref, do3_ref, cf_ref):
        lv = [l_ref[g, 0] for g in range(G)]
        mx = functools.reduce(jnp.maximum, lv)
        ex = [jnp.exp(v - mx) for v in lv]
        den = functools.reduce(lambda a, b: a + b, ex)
        wt = [e / den for e in ex]
        dov = do_ref[0]
        dw = [jnp.sum(dov * o_ref[g, 0], axis=-1, keepdims=True) for g in range(G)]
        mean = functools.reduce(lambda a, b: a + b, [wt[g] * dw[g] for g in range(G)])
        for g in range(G):
            do3_ref[g, 0] = (wt[g] * dov).astype(BF16)
            cf_ref[g, 0] = wt[g] * (dw[g] - mean) - wt[g] * dw[g]

    o_spec = BS((G, 1, ts, Dh), lambda h, i: (0, h, i, 0))
    l_spec = BS((G, 1, ts, 1), lambda h, i: (0, h, i, 0))
    return pl.pallas_call(
        body, name=name, grid=(H, S // ts), out_shape=(SDS((G, H, S, Dh), BF16), SDS((G, H, S, 1), F32)),
        in_specs=[o_spec, l_spec, BS((1, ts, Dh), lambda h, i: (h, i, 0))], out_specs=(o_spec, l_spec),
        compiler_params=_cp("parallel", "parallel"))(o3, lse3, do)


def _merge_fwd(proj, gate_col, os_, ws, name):
    S = proj.shape[0]
    D = ws[0].shape[1]
    nbr = len(os_)
    ts = _tile(S, 512, 8)
    tn = LANES * 2 if gate_col % (2 * LANES) == 0 and D % (2 * LANES) == 0 else LANES
    g0, gstep = gate_col // tn, D // tn

    def body(*refs):
        p_refs, o_refs, w_refs, out_ref = refs[:nbr], refs[nbr:2 * nbr], refs[2 * nbr:3 * nbr], refs[-1]
        acc = None
        for i in range(nbr):
            t = _sigmoid(p_refs[i][...]) * jnp.dot(o_refs[i][...], w_refs[i][...], preferred_element_type=F32)
            acc = t if acc is None else acc + t
        out_ref[...] = acc.astype(BF16)

    specs = ([BS((ts, tn), lambda s, j, i=i: (s, g0 + i * gstep + j)) for i in range(nbr)]
             + [BS((ts, o.shape[1]), lambda s, j: (s, 0)) for o in os_]
             + [BS((w.shape[0], tn), lambda s, j: (0, j)) for w in ws])
    return pl.pallas_call(
        body, name=name, grid=(S // ts, D // tn), out_shape=SDS((S, D), BF16), in_specs=specs,
        out_specs=BS((ts, tn), lambda s, j: (s, j)), compiler_params=_cp("parallel", "parallel"),
    )(*([proj] * nbr), *os_, *ws)


def _merge_bwd(proj, gate_col, os_, ws, dm, name):
    S = proj.shape[0]
    D = ws[0].shape[1]
    nbr = len(os_)
    ts = _tile(S, 512, 8)
    tn = LANES * 2 if gate_col % (2 * LANES) == 0 and D % (2 * LANES) == 0 else LANES
    g0, gstep = gate_col // tn, D // tn

    def body(*refs):
        p_refs, o_refs, w_refs = refs[:nbr], refs[nbr:2 * nbr], refs[2 * nbr:3 * nbr]
        dmv = refs[3 * nbr][...]
        dbo_refs, dgl_refs = refs[3 * nbr + 1:4 * nbr + 1], refs[4 * nbr + 1:]
        for i in range(nbr):
            gt = _sigmoid(p_refs[i][...])
            bo = jnp.dot(o_refs[i][...], w_refs[i][...], preferred_element_type=F32)
            dbo_refs[i][...] = (dmv * gt).astype(BF16)
            dgl_refs[i][...] = (dmv * bo * gt * (1.0 - gt)).astype(BF16)

    tile = BS((ts, tn), lambda s, j: (s, j))
    specs = ([BS((ts, tn), lambda s, j, i=i: (s, g0 + i * gstep + j)) for i in range(nbr)]
             + [BS((ts, o.shape[1]), lambda s, j: (s, 0)) for o in os_]
             + [BS((w.shape[0], tn), lambda s, j: (0, j)) for w in ws] + [tile])
    outs = pl.pallas_call(
        body, name=name, grid=(S // ts, D // tn), out_shape=(SDS((S, D), BF16),) * (2 * nbr), in_specs=specs,
        out_specs=(tile,) * (2 * nbr), compiler_params=_cp("parallel", "parallel"),
    )(*([proj] * nbr), *os_, *ws, dm)
    return outs[:nbr], outs[nbr:]


def _ffn_act(up, w, name):
    S, F2 = up.shape
    F = F2 // 2
    tc = LANES
    nb = F // tc

    def body(g_ref, v_ref, wg_ref, wv_ref, a_ref):
        ug = _conv3(g_ref[...], wg_ref)
        a_ref[...] = (ug * _sigmoid(ug) * _conv3(v_ref[...], wv_ref)).astype(BF16)

    return pl.pallas_call(
        body, name=name, grid=(nb,), out_shape=SDS((S, F), BF16),
        in_specs=[BS((S, tc), lambda j: (0, j)), BS((S, tc), lambda j: (0, nb + j)),
                  BS((3, tc), lambda j: (0, j)), BS((3, tc), lambda j: (0, nb + j))],
        out_specs=BS((S, tc), lambda j: (0, j)), compiler_params=_cp("parallel"))(up, up, w, w)


def _ffn_act_bwd(up, w, da, name):
    S, F2 = up.shape
    F = F2 // 2
    tc = LANES
    nb = F // tc

    def body(g_ref, v_ref, wg_ref, wv_ref, da_ref, dg_ref, dv_ref, dwg_ref, dwv_ref):
        pg, pv, dav = g_ref[...], v_ref[...], da_ref[...].astype(F32)
        ug, uv = _conv3(pg, wg_ref), _conv3(pv, wv_ref)
        sg = _sigmoid(ug)
        dg_ref[...] = _conv3_bwd(pg, dav * uv * (sg * (1.0 + ug * (1.0 - sg))), wg_ref, dwg_ref).astype(BF16)
        dv_ref[...] = _conv3_bwd(pv, dav * ug * sg, wv_ref, dwv_ref).astype(BF16)

    col = BS((S, tc), lambda j: (0, j))
    wsp = BS((3, tc), lambda j: (0, j))
    return pl.pallas_call(
        body, name=name, grid=(nb,),
        out_shape=(SDS((S, F), BF16), SDS((S, F), BF16), SDS((3, F), F32), SDS((3, F), F32)),
        in_specs=[col, BS((S, tc), lambda j: (0, nb + j)), wsp, BS((3, tc), lambda j: (0, nb + j)), col],
        out_specs=(col, col, wsp, wsp), compiler_params=_cp("parallel"))(up, up, w, w, da)


def _ple_fwd(x, pgl, pe, name):
    S, D = x.shape
    ts = _tile(S, 256, 8)

    def body(x_ref, g_ref, e_ref, o_ref):
        o_ref[...] = x_ref[...] + _sigmoid(g_ref[...]) * e_ref[...]

    row = BS((ts, D), lambda i: (i, 0))
    return pl.pallas_call(body, name=name, grid=(S // ts,), out_shape=SDS((S, D), F32), in_specs=[row] * 3,
                          out_specs=row, compiler_params=_cp("parallel"))(x, pgl, pe)


def _ple_bwd(dx, pgl, pe, name):
    S, D = dx.shape
    ts = _tile(S, 256, 8)

    def body(dx_ref, g_ref, e_ref, dg_ref, de_ref):
        dxv, sg = dx_ref[...], _sigmoid(g_ref[...])
        dg_ref[...] = (dxv * e_ref[...] * sg * (1.0 - sg)).astype(BF16)
        de_ref[...] = (dxv * sg).astype(BF16)

    row = BS((ts, D), lambda i: (i, 0))
    return pl.pallas_call(body, name=name, grid=(S // ts,), out_shape=(SDS((S, D), BF16),) * 2, in_specs=[row] * 3,
                          out_specs=(row, row), compiler_params=_cp("parallel"))(dx, pgl, pe)


def _adamw(w, g, m, v, name):
    shape = w.shape
    cols = shape[-1]
    rows = w.size // cols
    tr = _tile(rows, 256, 8)
    c1, c2 = 1.0 / (1.0 - ADAM_B1 ** ADAM_STEP), 1.0 / (1.0 - ADAM_B2 ** ADAM_STEP)

    def body(w_ref, g_ref, m_ref, v_ref, d_ref, nm_ref, nv_ref):
        gv = g_ref[...]
        nm = ADAM_B1 * m_ref[...] + (1.0 - ADAM_B1) * gv
        nv = ADAM_B2 * v_ref[...] + (1.0 - ADAM_B2) * (gv * gv)
        d_ref[...] = -ADAM_LR * ((nm * c1) / (jnp.sqrt(nv * c2) + ADAM_EPS) + ADAM_WD * w_ref[...])
        nm_ref[...] = nm
        nv_ref[...] = nv

    blk = BS((tr, cols), lambda i: (i, 0))
    outs = pl.pallas_call(
        body, name=name, grid=(rows // tr,), out_shape=(SDS((rows, cols), F32),) * 3, in_specs=[blk] * 4,
        out_specs=(blk,) * 3, compiler_params=_cp("parallel"),
    )(*(a.reshape(rows, cols) for a in (w, g, m, v)))
    return tuple(o.reshape(shape) for o in outs)


SHARDED = ("w_in", "shortconv_w", "w_br_fox", "w_br_conv", "w_br_sgu", "w_br_dil", "w_out", "w_up", "ffn_conv_w",
           "w_down", "w_ple_gate", "w_ple_proj")
REPLICATED = ("norm_mix_g", "fox_forget_b", "sgu_norm_g", "sgu_w", "sgu_b", "norm_ffn_g", "norm_ple_g")
WEIGHTS = ("norm_mix_g", "w_in", "fox_forget_b", "shortconv_w", "sgu_norm_g", "sgu_w", "sgu_b", "w_br_fox",
           "w_br_conv", "w_br_sgu", "w_br_dil", "w_out", "norm_ffn_g", "w_up", "ffn_conv_w", "w_down", "norm_ple_g",
           "w_ple_gate", "w_ple_proj", "final_norm_g")
PACK_COLS = 1024
PACK_ROWS = 256


def _pack(parts, dtype, unit):
    lead = parts[0].shape[:-1] if parts[0].ndim > 1 else ()
    flat = jnp.concatenate([p.astype(dtype) for p in parts], axis=-1)
    n = flat.shape[-1]
    rows = -(-n // (PACK_COLS * unit)) * unit
    flat = jnp.pad(flat, [(0, 0)] * len(lead) + [(0, rows * PACK_COLS - n)])
    return flat.reshape(lead + (rows, PACK_COLS))


def _to_heads(x, heads):
    S = x.shape[0]
    return x.reshape(S, heads, HEAD_DIM).transpose(1, 0, 2)


def _from_heads(x):
    H, S, Dh = x.shape
    return x.transpose(1, 0, 2).reshape(S, H * Dh)


def _to_residue(x, hpg):
    S = x.shape[0]
    gw = hpg * HEAD_DIM
    outs = []
    for g, (_, d) in enumerate(DIL_PATTERNS):
        seg = x[:, g * gw:(g + 1) * gw].reshape(S // d, d, hpg, HEAD_DIM)
        outs.append(seg.transpose(2, 1, 0, 3).reshape(hpg, S, HEAD_DIM))
    return jnp.concatenate(outs, axis=0)


def _from_residue(x, hpg):
    _, S, C = x.shape
    outs = []
    for g, (_, d) in enumerate(DIL_PATTERNS):
        seg = x[g * hpg:(g + 1) * hpg].reshape(hpg, d, S // d, C)
        outs.append(seg.transpose(0, 2, 1, 3).reshape(hpg, S, C))
    return jnp.stack(outs, axis=0)


def _groups_to_residue(x):
    G, hpg, S, C = x.shape
    outs = []
    for g, (_, d) in enumerate(DIL_PATTERNS):
        outs.append(x[g].reshape(hpg, S // d, d, C).transpose(0, 2, 1, 3).reshape(hpg, S, C))
    return jnp.concatenate(outs, axis=0)


def kernel(x, p, positions, norm_mix_g, w_in, fox_forget_b, shortconv_w, sgu_norm_g, sgu_w, sgu_b, w_br_fox, w_br_conv, w_br_sgu, w_br_dil, w_out, norm_ffn_g, w_up, ffn_conv_w, w_down, norm_ple_g, w_ple_gate, w_ple_proj, final_norm_g, loss_target, m_norm_mix_g, m_w_in, m_fox_forget_b, m_shortconv_w, m_sgu_norm_g, m_sgu_w, m_sgu_b, m_w_br_fox, m_w_br_conv, m_w_br_sgu, m_w_br_dil, m_w_out, m_norm_ffn_g, m_w_up, m_ffn_conv_w, m_w_down, m_norm_ple_g, m_w_ple_gate, m_w_ple_proj, m_final_norm_g, v_norm_mix_g, v_w_in, v_fox_forget_b, v_shortconv_w, v_sgu_norm_g, v_sgu_w, v_sgu_b, v_w_br_fox, v_w_br_conv, v_w_br_sgu, v_w_br_dil, v_w_out, v_norm_ffn_g, v_w_up, v_ffn_conv_w, v_w_down, v_norm_ple_g, v_w_ple_gate, v_w_ple_proj, v_final_norm_g):
    given = dict(locals())
    W = {n: given[n] for n in WEIGHTS}
    depth = w_in.shape[0]
    S, D = x.shape[1], x.shape[2]
    x0, target, pos = x[0], loss_target[0], positions[0].reshape(S, 1)
    FH = fox_forget_b.shape[1]
    FW = FH * HEAD_DIM
    CW = shortconv_w.shape[2] * N_DEV
    G = sgu_w.shape[1]
    SW = G * sgu_w.shape[2]
    DOUT = w_br_dil.shape[1]
    hpg = DOUT // HEAD_DIM
    NG = len(DIL_PATTERNS)
    DW = NG * DOUT
    bpcs = tuple(S // d // DIL_SPAN for _, d in DIL_PATTERNS)
    A0, B0 = 0, 3 * FW
    C0 = B0 + 3 * CW
    D0 = C0 + 2 * SW
    G0 = D0 + 3 * DW
    F0 = G0 + 4 * D
    NP = F0 + LANES
    orig = [3 * FW, FH, 3 * CW, 2 * SW, 3 * DW, 4 * D]
    o_off = [sum(orig[:i]) for i in range(len(orig) + 1)]

    me = 4 * lax.axis_index("x") + 2 * lax.axis_index("y") + lax.axis_index("c")
    n_in, o1, o2, d_in = w_in.shape[2], o_off[1], o_off[2], o_off[6]
    NPB = NP // LANES
    WB = -(-(LANES - 1 + n_in) // LANES)
    PW = (WB + 1) * LANES

    def a_of(j, clip=lambda v: min(max(v, 0), FH)):
        return n_in * j - clip(n_in * j - o1)

    def s_of(j):
        return min(a_of(j) // LANES, NPB - WB)

    def s_traced(j):
        return jnp.minimum(a_of(j, lambda v: jnp.clip(v, 0, FH)) // LANES, NPB - WB)

    def runs(j):
        lo, hi, out = n_in * j, n_in * (j + 1), []
        for a, b in ((0, o1), (o1, o2), (o2, d_in)):
            l, h = max(lo, a), min(hi, b)
            if l < h:
                dst = WB * LANES + l - o1 if a == o1 else l - (FH if l >= o2 else 0) - s_of(j) * LANES
                out.append((l - lo, h - l, dst))
        return out

    def to_window(j, shard):
        piece = jnp.zeros((shard.shape[0], PW), shard.dtype)
        for src, ln, dst in runs(j):
            piece = piece.at[:, dst:dst + ln].set(shard[:, src:src + ln])
        return piece

    def from_window(j, win):
        return jnp.concatenate([win[:, dst:dst + ln] for _, ln, dst in runs(j)], axis=1)

    def on_my_index(fn, arg):
        return lax.switch(me, [functools.partial(fn, j) for j in range(N_DEV)], arg)

    def place(axis, length):
        def view(ref, j):
            align = LANES if axis == len(ref.shape) - 1 else 16
            start = pl.multiple_of(j * length, align) if length % align == 0 else j * length
            return ref.at[tuple([slice(None)] * axis + [pl.ds(start, length)])]
        return view

    def slot(ref, j):
        return ref.at[j]

    def whole(ref, j):
        return ref

    n_up, r_down, r_sq = w_up.shape[2], w_down.shape[1], w_out.shape[1]
    n_br, n_sm, n_sc = w_br_fox.shape[2], w_br_dil.shape[2], shortconv_w.shape[2]
    FF2 = n_up * N_DEV
    cw_pad = -(-n_up // LANES) * LANES
    placed = [place(1, n_up), place(0, r_down), place(1, r_sq), place(2, n_br), place(2, n_sm)]

    def gather_plans(i):
        conv = jnp.zeros((8, cw_pad), F32).at[0:3, :n_up].set(ffn_conv_w[i]).at[3:6, :n_sc].set(shortconv_w[i])
        srcs = [on_my_index(to_window, w_in[i].astype(BF16)), w_up[i].astype(BF16), w_down[i].astype(BF16),
                jnp.stack([w_out[i], w_ple_gate[i]]).astype(BF16),
                jnp.stack([w_br_fox[i], w_br_conv[i], w_br_sgu[i]]).astype(BF16),
                jnp.stack([w_br_dil[i], w_ple_proj[i]]).astype(BF16), conv]
        shapes = [SDS((N_DEV, D, PW), BF16), SDS((D, FF2), BF16), SDS((r_down * N_DEV, D), BF16),
                  SDS((2, r_sq * N_DEV, D), BF16), SDS((3, w_br_fox.shape[1], n_br * N_DEV), BF16),
                  SDS((2, w_br_dil.shape[1], n_sm * N_DEV), BF16), SDS((N_DEV, 8, cw_pad), F32)]
        dsts = [slot] + placed + [slot]
        return [(srcs[a:b], shapes[a:b], [whole] * (b - a), dsts[a:b], True)
                for a, b in ((0, 1), (1, 2), (2, 3), (3, 7))]

    def gathered(parts):
        (pieces,), (wup,), (wdown,), (sq, br, sm, cv) = parts
        return {
            "w_in": _assemble(pieces, [s_of(j) for j in range(N_DEV)], WB, NPB, "assemble_w_in"),
            "w_up": wup, "w_down": wdown, "w_out": sq[0], "w_ple_gate": sq[1],
            "w_br_fox": br[0], "w_br_conv": br[1], "w_br_sgu": br[2], "w_br_dil": sm[0], "w_ple_proj": sm[1],
            "ffn_conv_w": jnp.moveaxis(cv[:, 0:3, :n_up], 0, 1).reshape(3, FF2),
            "shortconv_w": jnp.moveaxis(cv[:, 3:6, :n_sc], 0, 1).reshape(3, CW),
        }

    def one_plan(plans):
        return tuple(sum((list(plan[k]) for plan in plans), []) for k in range(4)) + (plans[0][4],)

    first = _comm("comm_gather_weights", *one_plan(gather_plans(0)))
    full = [gathered([first[0:1], first[1:2], first[2:3], first[3:7]])]

    inv = ROPE_THETA ** (-jnp.arange(ROPE_DIM // 2, dtype=F32) * (2.0 / ROPE_DIM))
    inv_row = jnp.tile(jnp.concatenate([inv, inv, jnp.zeros((HEAD_DIM - ROPE_DIM,), F32)]), LANES // HEAD_DIM)
    tables = _rope_tables(pos, inv_row.reshape(1, LANES), "rope_tables")

    def pad_lanes(v):
        return jnp.pad(v, ((0, 0), (0, LANES - v.shape[1])))

    saved = []
    xc = x0
    for i in range(depth):
        fw, sv = full[i], {}
        sv["x"] = xc
        nxt = gather_plans(i + 1) if i + 1 < depth else [None] * 4
        got = [None] * 4

        def host(k, result, nxt=nxt, got=got):
            if nxt[k] is None:
                return result
            got[k] = result[1]
            return result[0]

        h1 = _rms_fwd(xc, norm_mix_g[i], "rms_mix")
        proj = host(0, _mm(h1, fw["w_in"], "nn", F32, "mm_in", comm=nxt[0]))
        sv["h1"], sv["proj"] = h1, proj
        bias = pad_lanes(fox_forget_b[i].reshape(1, FH))
        Fp = _fox_prep(proj, bias, F0 // LANES, "fox_prep")[:, :, :FH].transpose(2, 1, 0)
        t = _tile(S, 256)
        qa, ka, va = (_to_heads(proj[:, A0 + j * FW:A0 + (j + 1) * FW], FH).astype(BF16) for j in range(3))
        ones3, zpad = jnp.ones((FH, S, 3), BF16), jnp.zeros((FH, S, HEAD_DIM - 6), BF16)
        q_aug = jnp.concatenate([qa, Fp, ones3, zpad], axis=-1)
        k_aug = jnp.concatenate([ka, ones3, -Fp, zpad], axis=-1)

        def blocks_t(a):
            return a.reshape(FH, S // t, t, a.shape[-1]).transpose(0, 1, 3, 2)

        kT = blocks_t(k_aug)
        oaT, lse_a = host(1, _foxt_fwd(q_aug, k_aug, blocks_t(va), "fox_fwd", comm=nxt[1]))
        oa_b = oaT.transpose(2, 0, 1).reshape(S, FW).astype(BF16)
        sv.update(bias=bias, q_aug=q_aug, k_aug=k_aug, kT=kT, va=va, lse_a=lse_a, oa_b=oa_b)
        ob = _sconv_fwd(proj, fw["shortconv_w"], B0, CW, "sconv_fwd")
        sgb = sgu_b[i].reshape(G, SGU_CHUNK, 1)
        oc = _sgu_fwd(proj, sgu_norm_g[i].reshape(1, SW), sgu_w[i], sgb, C0, SW, "sgu_fwd")
        qd = _rope_apply(proj, D0, DW, tables, 1.0, BF16, "rope_q")
        kd = _rope_apply(proj, D0 + DW, DW, tables, 1.0, BF16, "rope_k")
        qr, kr = _to_residue(qd, hpg), _to_residue(kd, hpg)
        vr = _to_residue(proj[:, D0 + 2 * DW:D0 + 3 * DW].astype(BF16), hpg)
        od_r, lse_r = host(2, _dil_fwd(qr, kr, vr, bpcs, hpg, "dil_fwd", comm=nxt[2]))
        o3, lse3 = _from_residue(od_r, hpg), _from_residue(lse_r, hpg)
        od = _from_heads(_dil_merge(o3, lse3, "dil_merge")).astype(BF16)
        sv.update(ob=ob, oc=oc, sgb=sgb, qr=qr, kr=kr, vr=vr, lse_r=lse_r, o3=o3, lse3=lse3, od=od)
        brs = [oa_b, ob, oc, od]
        wbr = [fw["w_br_fox"], fw["w_br_conv"], fw["w_br_sgu"], fw["w_br_dil"]]
        merged = _merge_fwd(proj, G0, brs, wbr, "merge_fwd")
        x2 = _mm(merged, fw["w_out"], "nn", F32, "mm_out", res=xc)
        h2 = _rms_fwd(x2, norm_ffn_g[i], "rms_ffn")
        up = host(3, _mm(h2, fw["w_up"], "nn", F32, "mm_up", comm=nxt[3]))
        if i + 1 < depth:
            full.append(gathered(got))
        act = _ffn_act(up, fw["ffn_conv_w"], "ffn_act")
        x3 = _mm(act, fw["w_down"], "nn", F32, "mm_down", res=x2)
        h3 = _rms_fwd(x3, norm_ple_g[i], "rms_ple")
        pgl = _mm(h3, fw["w_ple_gate"], "nn", F32, "mm_ple_gate")
        pb = p[i, 0].astype(BF16)
        pe = _mm(pb, fw["w_ple_proj"], "nn", F32, "mm_ple_proj")
        xc = _ple_fwd(x3, pgl, pe, "ple_fwd")
        sv.update(merged=merged, x2=x2, h2=h2, up=up, act=act, x3=x3, h3=h3, pgl=pgl, pb=pb, pe=pe)
        saved.append(sv)

    loss_row, dx, dxb, dg_final = _final_loss(xc, final_norm_g, target, "final_loss")
    loss = lax.psum(loss_row[0, 0], MESH_AXES)

    def in_window(half):
        def view(ref, j):
            return ref.at[pl.ds(half * (D // 2), D // 2),
                          pl.ds(pl.multiple_of(s_traced(j) * LANES, LANES), WB * LANES)]
        return view

    def in_flogit(ref, j):
        return ref.at[:, pl.ds(F0, LANES)]

    def scatter_plans(gs):
        conv = jnp.zeros((N_DEV, 8, cw_pad), F32)
        conv = conv.at[:, 0:3, :n_up].set(jnp.moveaxis(gs["ffn_conv_w"].reshape(3, N_DEV, n_up), 1, 0))
        conv = conv.at[:, 3:6, :n_sc].set(jnp.moveaxis(gs["shortconv_w"].reshape(3, N_DEV, n_sc), 1, 0))
        srcs = [gs["w_in"], gs["w_in"], gs["w_up"], gs["w_down"], jnp.stack([gs["w_out"], gs["w_ple_gate"]]),
                jnp.stack([gs["w_br_fox"], gs["w_br_conv"], gs["w_br_sgu"]]),
                jnp.stack([gs["w_br_dil"], gs["w_ple_proj"]]), conv, gs["w_in"]]
        stacked = lambda shape, dt=BF16: SDS((N_DEV,) + tuple(shape), dt)
        shapes = [stacked((D // 2, WB * LANES)), stacked((D, LANES)), stacked((D, n_up)), stacked((r_down, D)),
                  stacked((2, r_sq, D)), stacked((3, w_br_fox.shape[1], n_br)), stacked((2, w_br_dil.shape[1], n_sm)),
                  stacked((8, cw_pad), F32), stacked((D // 2, WB * LANES))]
        views = [in_window(0), in_flogit] + placed + [slot, in_window(1)]
        return [(srcs[a:b], shapes[a:b], views[a:b], [slot] * (b - a), False)
                for a, b in ((0, 2), (2, 3), (3, 4), (4, 8), (8, 9))]

    grads_sh = [None] * depth
    grads_rep = [None] * depth
    scattered = [None] * depth
    for i in reversed(range(depth)):
        fw, sv = full[i], saved[i]
        gs, gr = {}, {}
        pend = scatter_plans(grads_sh[i + 1]) if i + 1 < depth else [None] * 5
        got = [None] * 5

        def host(k, result, pend=pend, got=got):
            if pend[k] is None:
                return result
            got[k] = result[1]
            return result[0]

        proj = sv["proj"]
        dpgl, dpe = _ple_bwd(dx, sv["pgl"], sv["pe"], "ple_bwd")
        gs["w_ple_proj"] = _mm(sv["pb"], dpe, "tn", BF16, "mm_dw_ple_proj")
        gs["w_ple_gate"] = _mm(sv["h3"], dpgl, "tn", BF16, "mm_dw_ple_gate")
        dh3 = _mm(dpgl, fw["w_ple_gate"], "nt", F32, "mm_dh3")
        dx, dxb, gr["norm_ple_g"] = _rms_bwd(sv["x3"], norm_ple_g[i], dh3, dx, "rms_ple_bwd")
        gs["w_down"] = _mm(sv["act"], dxb, "tn", BF16, "mm_dw_down")
        dact = _mm(dxb, fw["w_down"], "nt", BF16, "mm_dact")
        dug, duv, dwg, dwv = _ffn_act_bwd(sv["up"], fw["ffn_conv_w"], dact, "ffn_act_bwd")
        gs["ffn_conv_w"] = jnp.concatenate([dwg, dwv], axis=1)
        dup = jnp.concatenate([dug, duv], axis=1)
        gs["w_up"] = _mm(sv["h2"], dup, "tn", BF16, "mm_dw_up")
        dh2 = _mm(dup, fw["w_up"], "nt", F32, "mm_dh2")
        dx, dxb, gr["norm_ffn_g"] = _rms_bwd(sv["x2"], norm_ffn_g[i], dh2, dx, "rms_ffn_bwd")
        gs["w_out"] = _mm(sv["merged"], dxb, "tn", BF16, "mm_dw_out")
        dmerged = _mm(dxb, fw["w_out"], "nt", F32, "mm_dmerged")
        brs = [sv["oa_b"], sv["ob"], sv["oc"], sv["od"]]
        names = ["w_br_fox", "w_br_conv", "w_br_sgu", "w_br_dil"]
        wbr = [fw[n] for n in names]
        dbo, dgl = _merge_bwd(proj, G0, brs, wbr, dmerged, "merge_bwd")
        dos = []
        for j, n in enumerate(names):
            gs[n] = _mm(brs[j], dbo[j], "tn", BF16, "mm_dw_" + n)
            dos.append(_mm(dbo[j], wbr[j], "nt", F32, "mm_do_" + n))
        doa = _to_heads(dos[0], FH).astype(BF16)
        dqT, delta = host(1, _foxt_bwd_dq(sv["q_aug"], sv["k_aug"], sv["kT"], sv["va"], doa, sv["lse_a"],
                                          "fox_bwd_dq", comm=pend[1]))
        dk_aug, dva, dfk = host(0, _foxt_bwd_dkv(sv["q_aug"], sv["k_aug"], sv["va"], doa, sv["lse_a"], delta,
                                                 "fox_bwd_dkv", comm=pend[0]))
        dF = pad_lanes(dfk[:, :, 0].T)
        daf, dbias = _fox_prep_bwd(proj, sv["bias"], F0 // LANES, dF, "fox_prep_bwd")
        gr["fox_forget_b"] = dbias[0, :FH]
        d_a = [dqT[:, :HEAD_DIM].transpose(2, 0, 1).reshape(S, FW).astype(BF16),
               _from_heads(dk_aug[:, :, :HEAD_DIM]).astype(BF16), _from_heads(dva).astype(BF16)]
        dxb_, dgb, dgc, gs["shortconv_w"] = _sconv_bwd(proj, fw["shortconv_w"], B0, CW, dos[1], "sconv_bwd")
        du, dv, dsw, dsb, dsg = _sgu_bwd(proj, sgu_norm_g[i].reshape(1, SW), sgu_w[i], sv["sgb"], C0, SW, dos[2],
                                         "sgu_bwd")
        gr["sgu_w"], gr["sgu_b"], gr["sgu_norm_g"] = dsw, dsb.reshape(G, SGU_CHUNK), dsg[0]
        do3, cf3 = _dil_merge_bwd(sv["o3"], sv["lse3"], _to_heads(dos[3], hpg), "dil_merge_bwd")
        do_r, cf_r = _groups_to_residue(do3), _groups_to_residue(cf3)
        dq_r = host(3, _dil_bwd_dq(sv["qr"], sv["kr"], sv["vr"], do_r, sv["lse_r"], cf_r, bpcs, hpg, "dil_bwd_dq",
                                   comm=pend[3]))
        dk_r, dv_r = host(2, _dil_bwd_dkv(sv["qr"], sv["kr"], sv["vr"], do_r, sv["lse_r"], cf_r, bpcs, hpg,
                                          "dil_bwd_dkv", comm=pend[2]))

        def natural(t):
            return _from_heads(_from_residue(t, hpg).reshape(NG * hpg, S, HEAD_DIM))

        dqd = _rope_apply(natural(dq_r), 0, DW, tables, -1.0, BF16, "rope_q_bwd")
        dkd = _rope_apply(natural(dk_r), 0, DW, tables, -1.0, BF16, "rope_k_bwd")
        dvd = natural(dv_r).astype(BF16)
        dproj = jnp.concatenate(d_a + [dxb_, dgb, dgc, du, dv, dqd, dkd, dvd] + list(dgl) + [daf], axis=1)
        gs["w_in"] = _mm(sv["h1"], dproj, "tn", BF16, "mm_dw_in")
        dh1 = host(4, _mm(dproj, fw["w_in"], "nt", F32, "mm_dh1", comm=pend[4]))
        if i + 1 < depth:
            scattered[i + 1] = got
        dx, dxb, gr["norm_mix_g"] = _rms_bwd(sv["x"], norm_mix_g[i], dh1, dx, "rms_mix_bwd")
        gr["norm_mix_g"], gr["norm_ffn_g"], gr["norm_ple_g"] = (gr[n][0] for n in
                                                               ("norm_mix_g", "norm_ffn_g", "norm_ple_g"))
        grads_sh[i], grads_rep[i] = gs, gr
    grad_x = dx.reshape(1, S, D)

    last = _comm("comm_scatter_grads", *one_plan(scatter_plans(grads_sh[0])))
    scattered[0] = [last[0:2], last[2:3], last[3:4], last[4:8], last[8:9]]

    def sum_sources(t, name):
        lead = t.shape[1:]
        return _sum8(t.reshape(N_DEV, -1, lead[-1]), name).reshape(lead)

    grads = {n: [None] * depth for n in SHARDED}
    for i in range(depth):
        (gw, gf), (gup,), (gdown,), (gsq, gbr, gsm, gcv), (gw2,) = scattered[i]
        rows = jnp.concatenate([sum_sources(gw, "sum_w_in"), sum_sources(gw2, "sum_w_in")], axis=0)
        win = jnp.concatenate([rows, sum_sources(gf, "sum_w_in_flogit")], axis=1)
        grads["w_in"][i] = on_my_index(from_window, win)
        grads["w_up"][i] = sum_sources(gup, "sum_w_up")
        grads["w_down"][i] = sum_sources(gdown, "sum_w_down")
        sq, br, sm = sum_sources(gsq, "sum_sq"), sum_sources(gbr, "sum_br"), sum_sources(gsm, "sum_sm")
        cv = sum_sources(gcv, "sum_conv")
        grads["w_out"][i], grads["w_ple_gate"][i] = sq[0], sq[1]
        grads["w_br_fox"][i], grads["w_br_conv"][i], grads["w_br_sgu"][i] = br[0], br[1], br[2]
        grads["w_br_dil"][i], grads["w_ple_proj"][i] = sm[0], sm[1]
        grads["ffn_conv_w"][i], grads["shortconv_w"][i] = cv[0:3, :n_up], cv[3:6, :n_sc]
    grads = {n: jnp.stack(v, axis=0) for n, v in grads.items()}

    rep_parts = [jnp.stack([grads_rep[i][n] for i in range(depth)], axis=0).reshape(-1) for n in REPLICATED]
    rep_parts.append(dg_final.reshape(-1))
    rep_sizes = [int(r.shape[0]) for r in rep_parts]
    rep_offs = [sum(rep_sizes[:i]) for i in range(len(rep_sizes) + 1)]
    rep = _sum8(_exchange(_pack(rep_parts, F32, 8), True, "comm_gather_small"), "sum_small").reshape(-1)
    for n, a, b in zip(REPLICATED + ("final_norm_g",), rep_offs[:-1], rep_offs[1:]):
        grads[n] = rep[a:b].reshape(W[n].shape)

    deltas, new_m, new_v = {}, {}, {}
    for n in WEIGHTS:
        deltas[n], new_m[n], new_v[n] = _adamw(W[n], grads[n], given["m_" + n], given["v_" + n], "adamw_" + n)
    return (loss, grad_x, *[grads[n] for n in WEIGHTS], *[deltas[n] for n in WEIGHTS],
            *[new_m[n] for n in WEIGHTS], *[new_v[n] for n in WEIGHTS])
```

```python
import functools

import jax
import jax.numpy as jnp
from jax import lax
from jax.experimental import pallas as pl
from jax.experimental.pallas import tpu as pltpu

F32 = jnp.float32
BF16 = jnp.bfloat16
EPS = 1e-6
NEG = -1e30
HEAD_DIM = 64
SGU_CHUNK = 128
DIL_PATTERNS = ((128, 1), (512, 4), (2048, 16))
DIL_SPAN = 128
ROPE_THETA = 500000.0
ROPE_DIM = 16
N_DEV = 8
LANES = 128
VMEM_LIMIT = 56 * 1024 * 1024
ADAM_LR, ADAM_B1, ADAM_B2, ADAM_EPS, ADAM_WD, ADAM_STEP = 0.001, 0.9, 0.999, 1e-08, 0.01, 10
MESH_AXES = ("x", "y", "c")

BS = pl.BlockSpec
SDS = jax.ShapeDtypeStruct


def _cp(*sem):
    return pltpu.CompilerParams(dimension_semantics=sem, vmem_limit_bytes=VMEM_LIMIT)


def _tile(dim, pref, unit=LANES):
    if dim % unit:
        return dim
    best, t = unit, unit
    while t <= min(dim, pref):
        if dim % t == 0:
            best = t
        t += unit
    return best


def _sigmoid(z):
    return 1.0 / (1.0 + jnp.exp(-z))


def _exchange(x, gather, name):
    shape = x.shape if not gather else (N_DEV,) + x.shape

    def body(x_ref, o_ref, send_sems, recv_sems, local_sem):
        ix, iy, ic = lax.axis_index("x"), lax.axis_index("y"), lax.axis_index("c")
        me = 4 * ix + 2 * iy + ic

        def src(j):
            return x_ref if gather else x_ref.at[j]

        local = pltpu.make_async_copy(src(me), o_ref.at[me], local_sem)
        local.start()
        sends, recvs = [], []
        for r in range(1, N_DEV):
            px = 1 - ix if (r >> 2) & 1 else ix
            py = 1 - iy if (r >> 1) & 1 else iy
            pc = 1 - ic if r & 1 else ic
            peer = 4 * px + 2 * py + pc
            sends.append(pltpu.make_async_remote_copy(
                src_ref=src(peer), dst_ref=o_ref.at[me], send_sem=send_sems.at[r - 1], recv_sem=recv_sems.at[r - 1],
                device_id=(px, py, pc), device_id_type=pl.DeviceIdType.MESH))
            recvs.append(pltpu.make_async_remote_copy(
                src_ref=src(peer), dst_ref=o_ref.at[peer], send_sem=send_sems.at[r - 1], recv_sem=recv_sems.at[r - 1],
                device_id=(px, py, pc), device_id_type=pl.DeviceIdType.MESH))
        for cp in sends:
            cp.start()
        for cp in recvs:
            cp.wait_recv()
        for cp in sends:
            cp.wait_send()
        local.wait()

    return pl.pallas_call(
        body, name=name, out_shape=SDS(shape, x.dtype),
        in_specs=[BS(memory_space=pl.ANY)], out_specs=BS(memory_space=pl.ANY),
        scratch_shapes=[pltpu.SemaphoreType.DMA((N_DEV - 1,)), pltpu.SemaphoreType.DMA((N_DEV - 1,)),
                        pltpu.SemaphoreType.DMA],
    )(x)


def _comm(name, srcs, out_shapes, src_views, dst_views, relay=False):
    n = len(srcs)

    def body(*refs):
        copies = _comm_copies(refs[:n], refs[n:2 * n], refs[2 * n:], src_views, dst_views, relay)
        _comm_start(copies)
        _comm_wait(copies)

    anyspec = BS(memory_space=pl.ANY)
    return pl.pallas_call(
        body, name=name, out_shape=tuple(out_shapes), in_specs=[anyspec] * n, out_specs=(anyspec,) * n,
        scratch_shapes=_comm_sems(n))(*srcs)


def _comm_sems(n):
    return [pltpu.SemaphoreType.DMA((N_DEV - 1, n)), pltpu.SemaphoreType.DMA((N_DEV - 1, n)),
            pltpu.SemaphoreType.DMA((n,))]


SIBLING = 1
OTHER_CHIP_SAME_CORE = (2, 4, 6)


def _comm_copies(s_refs, o_refs, sems, src_views, dst_views, relay=False):
    send_sems, recv_sems, local_sems = sems
    n = len(s_refs)
    ix, iy, ic = lax.axis_index("x"), lax.axis_index("y"), lax.axis_index("c")
    me = 4 * ix + 2 * iy + ic

    def at(r):
        px = 1 - ix if (r >> 2) & 1 else ix
        py = 1 - iy if (r >> 1) & 1 else iy
        pc = 1 - ic if r & 1 else ic
        return (px, py, pc), 4 * px + 2 * py + pc

    def copy(r, k, src, dst, to):
        return pltpu.make_async_remote_copy(src_ref=src, dst_ref=dst, send_sem=send_sems.at[r - 1, k],
                                            recv_sem=recv_sems.at[r - 1, k], device_id=to,
                                            device_id_type=pl.DeviceIdType.MESH)

    local = [pltpu.make_async_copy(src_views[k](s_refs[k], me), dst_views[k](o_refs[k], me), local_sems.at[k])
             for k in range(n)]
    sends, passed, arrivals = [], [], []
    for r in range(1, N_DEV):
        to, peer = at(r)
        for k in range(n):
            arrival = copy(r, k, src_views[k](s_refs[k], peer), dst_views[k](o_refs[k], peer), to)
            direct = not relay or r == SIBLING or r in OTHER_CHIP_SAME_CORE
            if direct:
                sends.append(copy(r, k, src_views[k](s_refs[k], peer), dst_views[k](o_refs[k], me), to))
            if relay and r in OTHER_CHIP_SAME_CORE:
                landed = dst_views[k](o_refs[k], peer)
                passed.append((arrival, copy(r ^ SIBLING, k, landed, landed, at(SIBLING)[0])))
            else:
                arrivals.append(arrival)
    return local, sends, passed, arrivals


def _comm_start(copies):
    local, sends, _, _ = copies
    for cp in local + sends:
        cp.start()


def _comm_wait(copies):
    local, sends, passed, arrivals = copies
    for arrival, onward in passed:
        arrival.wait_recv()
        onward.start()
    for cp in arrivals:
        cp.wait_recv()
    for cp in sends + [onward for _, onward in passed]:
        cp.wait_send()
    for cp in local:
        cp.wait()


def _call(body, *, name, grid, in_specs, out_specs, out_shape, args, scratch_shapes=(), sem=None, comm=None):
    single = not isinstance(out_shape, (tuple, list))
    out_shape = (out_shape,) if single else tuple(out_shape)
    out_specs = (out_specs,) if single else tuple(out_specs)
    if comm is None:
        outs = pl.pallas_call(body, name=name, grid=grid, in_specs=list(in_specs), out_specs=out_specs,
                              out_shape=out_shape, scratch_shapes=list(scratch_shapes),
                              compiler_params=_cp(*sem))(*args)
        return outs[0] if single else outs
    srcs, shapes, src_views, dst_views, relay = comm
    n, ni, no, ns = len(srcs), len(in_specs), len(out_shape), len(scratch_shapes)

    def hosted(*refs):
        ins, c_in = refs[:ni], refs[ni:ni + n]
        outs, c_out = refs[ni + n:ni + n + no], refs[ni + n + no:ni + 2 * n + no]
        scr, sems = refs[ni + 2 * n + no:ni + 2 * n + no + ns], refs[ni + 2 * n + no + ns:]
        ids = [pl.program_id(a) for a in range(len(grid))]
        first = functools.reduce(jnp.logical_and, [i == 0 for i in ids])
        last = functools.reduce(jnp.logical_and, [i == g - 1 for i, g in zip(ids, grid)])

        @pl.when(first)
        def _():
            _comm_start(_comm_copies(c_in, c_out, sems, src_views, dst_views, relay))

        body(*ins, *outs, *scr)

        @pl.when(last)
        def _():
            _comm_wait(_comm_copies(c_in, c_out, sems, src_views, dst_views, relay))

    anyspec = BS(memory_space=pl.ANY)
    outs = pl.pallas_call(
        hosted, name=name, grid=grid, in_specs=list(in_specs) + [anyspec] * n,
        out_specs=out_specs + (anyspec,) * n, out_shape=out_shape + tuple(shapes),
        scratch_shapes=list(scratch_shapes) + _comm_sems(n),
        compiler_params=_cp(*(["arbitrary"] * len(grid))))(*args, *srcs)
    return (outs[0] if single else outs[:no]), outs[no:]


def _assemble(pieces, starts, wb, npb, name):
    _, K, pw = pieces.shape
    tr = _tile(K, 256, 16)

    def body(p_ref, o_ref):
        for b in range(npb - 1):
            acc = None
            for j in range(N_DEV):
                if starts[j] <= b < starts[j] + wb:
                    blk = p_ref[j, :, (b - starts[j]) * LANES:(b - starts[j] + 1) * LANES]
                    acc = blk if acc is None else acc + blk
            o_ref[:, b * LANES:(b + 1) * LANES] = acc
        acc = p_ref[0, :, wb * LANES:]
        for j in range(1, N_DEV):
            acc = acc + p_ref[j, :, wb * LANES:]
        o_ref[:, (npb - 1) * LANES:] = acc

    return pl.pallas_call(
        body, name=name, grid=(K // tr,), out_shape=SDS((K, npb * LANES), pieces.dtype),
        in_specs=[BS((N_DEV, tr, pw), lambda i: (0, i, 0))], out_specs=BS((tr, npb * LANES), lambda i: (i, 0)),
        compiler_params=_cp("parallel"))(pieces)


def _sum8(x, name):
    _, R, C = x.shape
    tr = _tile(R, 256, 16)

    def body(x_ref, o_ref):
        acc = x_ref[0].astype(F32)
        for j in range(1, N_DEV):
            acc = acc + x_ref[j].astype(F32)
        o_ref[...] = acc

    return pl.pallas_call(
        body, name=name, grid=(R // tr,), out_shape=SDS((R, C), F32),
        in_specs=[BS((N_DEV, tr, C), lambda i: (0, i, 0))], out_specs=BS((tr, C), lambda i: (i, 0)),
        compiler_params=_cp("parallel"))(x)


_DIMS = {"nn": ((1,), (0,)), "nt": ((1,), (1,)), "tn": ((0,), (0,))}


def _mm(a, b, mode, out_dtype, name, res=None, comm=None):
    if mode == "nn":
        (M, K), N = a.shape, b.shape[1]
    elif mode == "nt":
        (M, K), N = a.shape, b.shape[0]
    else:
        (K, M), N = a.shape, b.shape[1]
    tm, tn, tk = _tile(M, 1024), _tile(N, 1024), _tile(K, 1024)
    nk = K // tk
    a_spec = BS((tk, tm), lambda i, j, k: (k, i)) if mode == "tn" else BS((tm, tk), lambda i, j, k: (i, k))
    b_spec = BS((tn, tk), lambda i, j, k: (j, k)) if mode == "nt" else BS((tk, tn), lambda i, j, k: (k, j))
    o_spec = BS((tm, tn), lambda i, j, k: (i, j))
    dims = (_DIMS[mode], ((), ()))

    def body(a_ref, b_ref, *rest):
        o_ref, acc = rest[-2], rest[-1]
        k = pl.program_id(2)

        @pl.when(k == 0)
        def _():
            acc[...] = jnp.zeros_like(acc)

        acc[...] += lax.dot_general(a_ref[...].astype(BF16), b_ref[...].astype(BF16), dims,
                                    preferred_element_type=F32)

        @pl.when(k == nk - 1)
        def _():
            r = acc[...]
            if res is not None:
                r = r + rest[0][...]
            o_ref[...] = r.astype(o_ref.dtype)

    ins, specs = [a, b], [a_spec, b_spec]
    if res is not None:
        ins.append(res)
        specs.append(o_spec)
    return _call(body, name=name, grid=(M // tm, N // tn, nk), out_shape=SDS((M, N), out_dtype), in_specs=specs,
                 out_specs=o_spec, scratch_shapes=[pltpu.VMEM((tm, tn), F32)],
                 sem=("parallel", "parallel", "arbitrary"), args=ins, comm=comm)


def _rms_fwd(x, g, name):
    S, D = x.shape
    ts = _tile(S, 256, 8)

    def body(x_ref, g_ref, o_ref):
        xv = x_ref[...]
        rstd = lax.rsqrt(jnp.mean(xv * xv, axis=-1, keepdims=True) + EPS)
        o_ref[...] = (xv * rstd * g_ref[...]).astype(o_ref.dtype)

    return pl.pallas_call(
        body, name=name, grid=(S // ts,), out_shape=SDS((S, D), BF16),
        in_specs=[BS((ts, D), lambda i: (i, 0)), BS((1, D), lambda i: (0, 0))],
        out_specs=BS((ts, D), lambda i: (i, 0)), compiler_params=_cp("parallel"))(x, g.reshape(1, D))


def _rms_bwd(x, g, dh, dres, name):
    S, D = x.shape
    ts = _tile(S, 256, 8)

    def body(x_ref, g_ref, dh_ref, dres_ref, dx_ref, dxb_ref, dg_ref):
        xv = x_ref[...]
        rstd = lax.rsqrt(jnp.mean(xv * xv, axis=-1, keepdims=True) + EPS)
        xn = xv * rstd
        dhv = dh_ref[...]

        @pl.when(pl.program_id(0) == 0)
        def _():
            dg_ref[...] = jnp.zeros_like(dg_ref)

        dg_ref[...] += jnp.sum(dhv * xn, axis=0, keepdims=True)
        dhn = dhv * g_ref[...]
        dxv = rstd * (dhn - xn * jnp.mean(dhn * xn, axis=-1, keepdims=True)) + dres_ref[...]
        dx_ref[...] = dxv
        dxb_ref[...] = dxv.astype(BF16)

    row = BS((ts, D), lambda i: (i, 0))
    vec = BS((1, D), lambda i: (0, 0))
    return pl.pallas_call(
        body, name=name, grid=(S // ts,), out_shape=(SDS((S, D), F32), SDS((S, D), BF16), SDS((1, D), F32)),
        in_specs=[row, vec, row, row], out_specs=(row, row, vec), compiler_params=_cp("arbitrary"),
    )(x, g.reshape(1, D), dh, dres)


def _final_loss(x, g, target, name):
    S, D = x.shape
    ts = _tile(S, 256, 8)

    def body(x_ref, g_ref, t_ref, loss_ref, dx_ref, dxb_ref, dg_ref):
        xv = x_ref[...]
        rstd = lax.rsqrt(jnp.mean(xv * xv, axis=-1, keepdims=True) + EPS)
        xn = xv * rstd
        gv = g_ref[...]
        e = xn * gv - t_ref[...]

        @pl.when(pl.program_id(0) == 0)
        def _():
            dg_ref[...] = jnp.zeros_like(dg_ref)
            loss_ref[...] = jnp.zeros_like(loss_ref)

        loss_ref[...] += 0.5 * jnp.sum(jnp.mean(e * e, axis=-1, keepdims=True), axis=0, keepdims=True)
        dy = e * (1.0 / D)
        dg_ref[...] += jnp.sum(dy * xn, axis=0, keepdims=True)
        dhn = dy * gv
        dxv = rstd * (dhn - xn * jnp.mean(dhn * xn, axis=-1, keepdims=True))
        dx_ref[...] = dxv
        dxb_ref[...] = dxv.astype(BF16)

    row = BS((ts, D), lambda i: (i, 0))
    vec = BS((1, D), lambda i: (0, 0))
    return pl.pallas_call(
        body, name=name, grid=(S // ts,),
        out_shape=(SDS((1, LANES), F32), SDS((S, D), F32), SDS((S, D), BF16), SDS((1, D), F32)),
        in_specs=[row, vec, row], out_specs=(BS((1, LANES), lambda i: (0, 0)), row, row, vec),
        compiler_params=_cp("arbitrary"))(x, g.reshape(1, D), target)


def _shift_down(z, k):
    rows = lax.broadcasted_iota(jnp.int32, z.shape, 0)
    return jnp.where(rows >= k, pltpu.roll(z, k, 0), 0.0)


def _shift_up(z, k):
    n = z.shape[0]
    rows = lax.broadcasted_iota(jnp.int32, z.shape, 0)
    return jnp.where(rows < n - k, pltpu.roll(z, n - k, 0), 0.0)


def _cumsum_rows(z, reverse=False):
    n = z.shape[0]
    k = 1
    while k < n:
        z = z + (_shift_up(z, k) if reverse else _shift_down(z, k))
        k *= 2
    return z


def _conv3(z, w_ref):
    return w_ref[0:1, :] * _shift_down(z, 2) + w_ref[1:2, :] * _shift_down(z, 1) + w_ref[2:3, :] * z


def _conv3_bwd(z, dy, w_ref, dw_ref):
    dw_ref[0:1, :] = jnp.sum(dy * _shift_down(z, 2), axis=0, keepdims=True)
    dw_ref[1:2, :] = jnp.sum(dy * _shift_down(z, 1), axis=0, keepdims=True)
    dw_ref[2:3, :] = jnp.sum(dy * z, axis=0, keepdims=True)
    return w_ref[2:3, :] * dy + w_ref[1:2, :] * _shift_up(dy, 1) + w_ref[0:1, :] * _shift_up(dy, 2)


def _fox_prep(proj, bias, col, name):
    S = proj.shape[0]

    def body(a_ref, b_ref, f_ref):
        z = a_ref[...] + b_ref[...]
        logf = jnp.minimum(z, 0.0) - jnp.log(1.0 + jnp.exp(-jnp.abs(z)))
        rest = _cumsum_rows(logf)
        for k in range(3):
            piece = rest.astype(BF16)
            f_ref[k] = piece
            rest = rest - piece.astype(F32)

    return pl.pallas_call(
        body, name=name, grid=(1,), out_shape=SDS((3, S, LANES), BF16),
        in_specs=[BS((S, LANES), lambda i: (0, col)), BS((1, LANES), lambda i: (0, 0))],
        out_specs=BS((3, S, LANES), lambda i: (0, 0, 0)), compiler_params=_cp("arbitrary"))(proj, bias)


def _fox_prep_bwd(proj, bias, col, df, name):
    S = proj.shape[0]

    def body(a_ref, b_ref, df_ref, da_ref, db_ref):
        z = a_ref[...] + b_ref[...]
        da = _cumsum_rows(df_ref[...], reverse=True) * _sigmoid(-z)
        da_ref[...] = da.astype(BF16)
        db_ref[...] = jnp.sum(da, axis=0, keepdims=True)

    full = BS((S, LANES), lambda i: (0, 0))
    vec = BS((1, LANES), lambda i: (0, 0))
    return pl.pallas_call(
        body, name=name, grid=(1,), out_shape=(SDS((S, LANES), BF16), SDS((1, LANES), F32)),
        in_specs=[BS((S, LANES), lambda i: (0, col)), vec, full], out_specs=(full, vec),
        compiler_params=_cp("arbitrary"))(proj, bias, df)


def _fox_scores(qs, kv, fqv, fkv, i, j, t, masked):
    s = lax.dot_general(qs, kv, (_DIMS["nt"], ((), ())), preferred_element_type=F32) + fqv - fkv
    if not masked:
        return s
    row = i * t + lax.broadcasted_iota(jnp.int32, (t, t), 0)
    col = j * t + lax.broadcasted_iota(jnp.int32, (t, t), 1)
    return jnp.where(col <= row, s, NEG)


def _fox_fwd(q, k, v, fq, fk, name, comm=None):
    H, S, Dh = q.shape
    t = _tile(S, 256)
    n = S // t
    scale = Dh ** -0.5

    def body(q_ref, k_ref, v_ref, fq_ref, fk_ref, o_ref, lse_ref):
        i = pl.program_id(1)
        qs, fqv = q_ref[0] * scale, fq_ref[0]

        def step(masked, j, carry):
            m, l, acc = carry
            s = _fox_scores(qs, k_ref[0, j], fqv, fk_ref[0, j], i, j, t, masked)
            mn = jnp.maximum(m, jnp.max(s, axis=-1, keepdims=True))
            al = jnp.exp(m - mn)
            p = jnp.exp(s - mn)
            l = l * al + jnp.sum(p, axis=-1, keepdims=True)
            acc = acc * al + jnp.dot(p.astype(BF16), v_ref[0, j], preferred_element_type=F32)
            return mn, l, acc

        init = (jnp.full((t, 1), NEG, F32), jnp.zeros((t, 1), F32), jnp.zeros((t, Dh), F32))
        m, l, acc = step(True, i, lax.fori_loop(0, i, functools.partial(step, False), init))
        o_ref[0] = acc / l
        lse_ref[0] = m + jnp.log(l)

    blk = BS((1, t, Dh), lambda h, i: (h, i, 0))
    col = BS((1, t, 1), lambda h, i: (h, i, 0))
    res = BS((1, n, t, Dh), lambda h, i: (h, 0, 0, 0))
    return _call(
        body, name=name, grid=(H, n), out_shape=(SDS((H, S, Dh), F32), SDS((H, S, 1), F32)),
        in_specs=[blk, res, res, col, BS((1, n, 1, t), lambda h, i: (h, 0, 0, 0))], out_specs=(blk, col),
        sem=("parallel", "arbitrary"), args=(q, k.reshape(H, n, t, Dh), v.reshape(H, n, t, Dh), fq, fk), comm=comm)


def _fox_bwd_dq(q, k, v, fq, fk, do, lse, name, comm=None):
    H, S, Dh = q.shape
    t = _tile(S, 256)
    n = S // t
    scale = Dh ** -0.5

    def body(q_ref, k_ref, v_ref, fq_ref, fk_ref, do_ref, lse_ref, dq_ref, dl_ref):
        i = pl.program_id(1)
        qs, fqv, dob, lsev = q_ref[0] * scale, fq_ref[0], do_ref[0], lse_ref[0]

        def p_dp(masked, j):
            p = jnp.exp(_fox_scores(qs, k_ref[0, j], fqv, fk_ref[0, j], i, j, t, masked) - lsev)
            return p, lax.dot_general(dob, v_ref[0, j], (_DIMS["nt"], ((), ())), preferred_element_type=F32)

        def sum_step(masked, j, delta):
            p, dp = p_dp(masked, j)
            return delta + jnp.sum(p * dp, axis=-1, keepdims=True)

        delta = sum_step(True, i, lax.fori_loop(0, i, functools.partial(sum_step, False), jnp.zeros((t, 1), F32)))

        def step(masked, j, dq):
            p, dp = p_dp(masked, j)
            return dq + jnp.dot((p * (dp - delta)).astype(BF16), k_ref[0, j], preferred_element_type=F32)

        dq = step(True, i, lax.fori_loop(0, i, functools.partial(step, False), jnp.zeros((t, Dh), F32)))
        dq_ref[0] = dq * scale
        dl_ref[0] = delta

    blk = BS((1, t, Dh), lambda h, i: (h, i, 0))
    col = BS((1, t, 1), lambda h, i: (h, i, 0))
    res = BS((1, n, t, Dh), lambda h, i: (h, 0, 0, 0))
    return _call(
        body, name=name, grid=(H, n), out_shape=(SDS((H, S, Dh), F32), SDS((H, S, 1), F32)),
        in_specs=[blk, res, res, col, BS((1, n, 1, t), lambda h, i: (h, 0, 0, 0)), blk, col],
        out_specs=(blk, col), sem=("parallel", "arbitrary"), comm=comm,
        args=(q, k.reshape(H, n, t, Dh), v.reshape(H, n, t, Dh), fq, fk, do, lse))


def _fox_bwd_dkv(q, k, v, fq, fk, do, lse, delta, name, comm=None):
    H, S, Dh = q.shape
    t = _tile(S, 256)
    n = S // t
    scale = Dh ** -0.5
    tn_dims = (_DIMS["tn"], ((), ()))

    def body(q_ref, k_ref, v_ref, fq_ref, fk_ref, do_ref, lse_ref, dl_ref, dk_ref, dv_ref, dfk_ref):
        j = pl.program_id(1)
        kv, vv, fkv = k_ref[0], v_ref[0], fk_ref[0, 0]

        def step(masked, i, carry):
            dk, dv, dfk = carry
            qs, dob = q_ref[0, i] * scale, do_ref[0, i]
            p = jnp.exp(_fox_scores(qs, kv, fq_ref[0, i], fkv, i, j, t, masked) - lse_ref[0, i])
            dv = dv + lax.dot_general(p.astype(BF16), dob, tn_dims, preferred_element_type=F32)
            dp = lax.dot_general(dob, vv, (_DIMS["nt"], ((), ())), preferred_element_type=F32)
            ds = p * (dp - dl_ref[0, i])
            dk = dk + lax.dot_general(ds.astype(BF16), qs, tn_dims, preferred_element_type=F32)
            return dk, dv, dfk - jnp.sum(ds, axis=0, keepdims=True)

        init = (jnp.zeros((t, Dh), F32), jnp.zeros((t, Dh), F32), jnp.zeros((1, t), F32))
        dk, dv, dfk = lax.fori_loop(j + 1, n, functools.partial(step, False), step(True, j, init))
        dk_ref[0] = dk
        dv_ref[0] = dv
        dfk_ref[0, 0] = dfk

    blk = BS((1, t, Dh), lambda h, j: (h, j, 0))
    res = BS((1, n, t, Dh), lambda h, j: (h, 0, 0, 0))
    rcol = BS((1, n, t, 1), lambda h, j: (h, 0, 0, 0))
    frow = BS((1, 1, 1, t), lambda h, j: (h, j, 0, 0))
    return _call(
        body, name=name, grid=(H, n),
        out_shape=(SDS((H, S, Dh), F32), SDS((H, S, Dh), F32), SDS((H, n, 1, t), F32)),
        in_specs=[res, blk, blk, rcol, frow, res, rcol, rcol], out_specs=(blk, blk, frow),
        sem=("parallel", "arbitrary"), comm=comm,
        args=(q.reshape(H, n, t, Dh), k, v, fq.reshape(H, n, t, 1), fk, do.astype(BF16).reshape(H, n, t, Dh),
              lse.reshape(H, n, t, 1), delta.reshape(H, n, t, 1)))


FOX_HEADS_PER_STEP = 2


def _fox_q_scale(Dh):
    lane = lax.broadcasted_iota(jnp.int32, (1, 2 * Dh), 1)
    return jnp.where(lane < Dh, Dh ** -0.5, 1.0).astype(BF16)


def _foxt_scores(ka, qs, i, j, t, masked):
    s = lax.dot_general(ka, qs, (_DIMS["nt"], ((), ())), preferred_element_type=F32)
    if not masked:
        return s
    key = j * t + lax.broadcasted_iota(jnp.int32, (t, t), 0)
    qry = i * t + lax.broadcasted_iota(jnp.int32, (t, t), 1)
    return jnp.where(key <= qry, s, NEG)


def _foxt_fwd(q_aug, k_aug, vT, name, comm=None):
    H, S, W = q_aug.shape
    Dh = W // 2
    t = _tile(S, 256)
    n = S // t
    hb = FOX_HEADS_PER_STEP

    def body(q_ref, k_ref, v_ref, o_ref, lse_ref):
        i = pl.program_id(1)
        qs = [q_ref[h] * _fox_q_scale(Dh) for h in range(hb)]

        def step(masked, j, carry):
            out = []
            for h in range(hb):
                m, l, acc = carry[h]
                s = _foxt_scores(k_ref[h, j], qs[h], i, j, t, masked)
                mn = jnp.maximum(m, jnp.max(s, axis=0, keepdims=True))
                al = jnp.exp(m - mn)
                p = jnp.exp(s - mn)
                l = l * al + jnp.sum(p, axis=0, keepdims=True)
                acc = acc * al + jnp.dot(v_ref[h, j], p.astype(BF16), preferred_element_type=F32)
                out.append((mn, l, acc))
            return tuple(out)

        init = tuple((jnp.full((1, t), NEG, F32), jnp.zeros((1, t), F32), jnp.zeros((Dh, t), F32))
                     for _ in range(hb))
        fin = step(True, i, lax.fori_loop(0, i, functools.partial(step, False), init))
        for h in range(hb):
            m, l, acc = fin[h]
            o_ref[h] = acc / l
            lse_ref[h] = m + jnp.log(l)

    return _call(
        body, name=name, grid=(H // hb, n), out_shape=(SDS((H, Dh, S), F32), SDS((H, 1, S), F32)),
        in_specs=[BS((hb, t, W), lambda g, i: (g, i, 0)), BS((hb, n, t, W), lambda g, i: (g, 0, 0, 0)),
                  BS((hb, n, Dh, t), lambda g, i: (g, 0, 0, 0))],
        out_specs=(BS((hb, Dh, t), lambda g, i: (g, 0, i)), BS((hb, 1, t), lambda g, i: (g, 0, i))),
        sem=("parallel", "arbitrary"), args=(q_aug, k_aug.reshape(H, n, t, W), vT), comm=comm)


def _foxt_bwd_dq(q_aug, k_aug, kT, v, do, lse, name, comm=None):
    H, S, W = q_aug.shape
    Dh = W // 2
    t = _tile(S, 256)
    n = S // t
    hb = FOX_HEADS_PER_STEP

    def body(q_ref, k_ref, kt_ref, v_ref, do_ref, lse_ref, dq_ref, dl_ref):
        i = pl.program_id(1)
        qs = [q_ref[h] * _fox_q_scale(Dh) for h in range(hb)]

        def p_dp(h, masked, j):
            p = jnp.exp(_foxt_scores(k_ref[h, j], qs[h], i, j, t, masked) - lse_ref[h])
            return p, lax.dot_general(v_ref[h, j], do_ref[h], (_DIMS["nt"], ((), ())), preferred_element_type=F32)

        def sum_step(masked, j, delta):
            out = []
            for h in range(hb):
                p, dp = p_dp(h, masked, j)
                out.append(delta[h] + jnp.sum(p * dp, axis=0, keepdims=True))
            return tuple(out)

        zero = tuple(jnp.zeros((1, t), F32) for _ in range(hb))
        delta = sum_step(True, i, lax.fori_loop(0, i, functools.partial(sum_step, False), zero))

        def step(masked, j, dq):
            out = []
            for h in range(hb):
                p, dp = p_dp(h, masked, j)
                ds = (p * (dp - delta[h])).astype(BF16)
                out.append(dq[h] + jnp.dot(kt_ref[h, j], ds, preferred_element_type=F32))
            return tuple(out)

        zero = tuple(jnp.zeros((W, t), F32) for _ in range(hb))
        dq = step(True, i, lax.fori_loop(0, i, functools.partial(step, False), zero))
        for h in range(hb):
            dq_ref[h] = dq[h] * Dh ** -0.5
            dl_ref[h] = delta[h]

    row = BS((hb, 1, t), lambda g, i: (g, 0, i))
    blk = lambda w: BS((hb, t, w), lambda g, i: (g, i, 0))
    return _call(
        body, name=name, grid=(H // hb, n), out_shape=(SDS((H, W, S), F32), SDS((H, 1, S), F32)),
        in_specs=[blk(W), BS((hb, n, t, W), lambda g, i: (g, 0, 0, 0)), BS((hb, n, W, t), lambda g, i: (g, 0, 0, 0)),
                  BS((hb, n, t, Dh), lambda g, i: (g, 0, 0, 0)), blk(Dh), row],
        out_specs=(BS((hb, W, t), lambda g, i: (g, 0, i)), row), sem=("parallel", "arbitrary"), comm=comm,
        args=(q_aug, k_aug.reshape(H, n, t, W), kT, v.reshape(H, n, t, Dh), do, lse))


def _foxt_bwd_dkv(q_aug, k_aug, v, do, lse, delta, name, comm=None):
    H, S, W = q_aug.shape
    Dh = W // 2
    t = _tile(S, 256)
    n = S // t
    hb = FOX_HEADS_PER_STEP

    def body(q_ref, k_ref, v_ref, do_ref, lse_ref, dl_ref, dk_ref, dv_ref, dfk_ref):
        j = pl.program_id(1)

        def step(masked, i, carry):
            out = []
            for h in range(hb):
                dk, dv, dfk = carry[h]
                qs, dob = q_ref[h, i] * _fox_q_scale(Dh), do_ref[h, i]
                p = jnp.exp(_foxt_scores(k_ref[h], qs, i, j, t, masked) - lse_ref[h, i])
                dv = dv + jnp.dot(p.astype(BF16), dob, preferred_element_type=F32)
                dp = lax.dot_general(v_ref[h], dob, (_DIMS["nt"], ((), ())), preferred_element_type=F32)
                ds = p * (dp - dl_ref[h, i])
                dk = dk + jnp.dot(ds.astype(BF16), qs, preferred_element_type=F32)
                out.append((dk, dv, dfk - jnp.sum(ds, axis=1, keepdims=True)))
            return tuple(out)

        init = tuple((jnp.zeros((t, W), F32), jnp.zeros((t, Dh), F32), jnp.zeros((t, 1), F32)) for _ in range(hb))
        fin = lax.fori_loop(j + 1, n, functools.partial(step, False), step(True, j, init))
        for h in range(hb):
            dk_ref[h], dv_ref[h], dfk_ref[h] = fin[h]

    blk = lambda w: BS((hb, t, w), lambda g, j: (g, j, 0))
    res = lambda w: BS((hb, n, t, w), lambda g, j: (g, 0, 0, 0))
    rows = BS((hb, n, 1, t), lambda g, j: (g, 0, 0, 0))
    return _call(
        body, name=name, grid=(H // hb, n),
        out_shape=(SDS((H, S, W), F32), SDS((H, S, Dh), F32), SDS((H, S, 1), F32)),
        in_specs=[res(W), blk(W), blk(Dh), res(Dh), rows, rows], out_specs=(blk(W), blk(Dh), blk(1)),
        sem=("parallel", "arbitrary"), comm=comm,
        args=(q_aug.reshape(H, n, t, W), k_aug, v, do.reshape(H, n, t, Dh), lse.reshape(H, n, 1, t),
              delta.reshape(H, n, 1, t)))


def _sconv_fwd(proj, w, col0, width, name):
    S = proj.shape[0]
    tc = LANES
    nb, c0 = width // tc, col0 // tc

    def body(x_ref, gb_ref, gc_ref, w_ref, o_ref):
        o_ref[...] = (gb_ref[...] * _conv3(gc_ref[...] * x_ref[...], w_ref)).astype(BF16)

    return pl.pallas_call(
        body, name=name, grid=(nb,), out_shape=SDS((S, width), BF16),
        in_specs=[BS((S, tc), lambda j: (0, c0 + j)), BS((S, tc), lambda j: (0, c0 + nb + j)),
                  BS((S, tc), lambda j: (0, c0 + 2 * nb + j)), BS((3, tc), lambda j: (0, j))],
        out_specs=BS((S, tc), lambda j: (0, j)), compiler_params=_cp("parallel"))(proj, proj, proj, w)


def _sconv_bwd(proj, w, col0, width, do, name):
    S = proj.shape[0]
    tc = LANES
    nb, c0 = width // tc, col0 // tc

    def body(x_ref, gb_ref, gc_ref, w_ref, do_ref, dx_ref, dgb_ref, dgc_ref, dw_ref):
        xb, gb, gc, dov = x_ref[...], gb_ref[...], gc_ref[...], do_ref[...]
        z = gc * xb
        dgb_ref[...] = (dov * _conv3(z, w_ref)).astype(BF16)
        dz = _conv3_bwd(z, dov * gb, w_ref, dw_ref)
        dgc_ref[...] = (dz * xb).astype(BF16)
        dx_ref[...] = (dz * gc).astype(BF16)

    out = BS((S, tc), lambda j: (0, j))
    wspec = BS((3, tc), lambda j: (0, j))
    return pl.pallas_call(
        body, name=name, grid=(nb,),
        out_shape=(SDS((S, width), BF16), SDS((S, width), BF16), SDS((S, width), BF16), SDS((3, width), F32)),
        in_specs=[BS((S, tc), lambda j: (0, c0 + j)), BS((S, tc), lambda j: (0, c0 + nb + j)),
                  BS((S, tc), lambda j: (0, c0 + 2 * nb + j)), wspec, out],
        out_specs=(out, out, out, wspec), compiler_params=_cp("parallel"))(proj, proj, proj, w, do)


def _gelu(x):
    return 0.5 * x * (1.0 + jnp.tanh(0.7978845608028654 * (x + 0.044715 * x * x * x)))


def _gelu_grad(x):
    u = 0.7978845608028654 * (x + 0.044715 * x * x * x)
    th = jnp.tanh(u)
    return 0.5 * (1.0 + th) + 0.5 * x * (1.0 - th * th) * 0.7978845608028654 * (1.0 + 3.0 * 0.044715 * x * x)


def _tril_w(w_ref, g):
    r = lax.broadcasted_iota(jnp.int32, (SGU_CHUNK, SGU_CHUNK), 0)
    c = lax.broadcasted_iota(jnp.int32, (SGU_CHUNK, SGU_CHUNK), 1)
    return jnp.where(c <= r, w_ref[g], 0.0), c <= r


def _sgu_mixed(vn, w_ref, b_ref, nch, G):
    rows = []
    for ch in range(nch):
        cols = []
        for g in range(G):
            wt, _ = _tril_w(w_ref, g)
            blk = vn[ch * SGU_CHUNK:(ch + 1) * SGU_CHUNK, g * LANES:(g + 1) * LANES]
            cols.append(jnp.dot(wt.astype(BF16), blk.astype(BF16), preferred_element_type=F32) + b_ref[g])
        rows.append(jnp.concatenate(cols, axis=1))
    return jnp.concatenate(rows, axis=0)


def _sgu_fwd(proj, gn, w, b, col0, width, name):
    S = proj.shape[0]
    G = w.shape[0]
    ts = _tile(S, 512)
    nch = ts // SGU_CHUNK
    c0 = col0 // width

    def body(u_ref, v_ref, gn_ref, w_ref, b_ref, o_ref):
        cv = _gelu(v_ref[...])
        rstd = lax.rsqrt(jnp.mean(cv * cv, axis=-1, keepdims=True) + EPS)
        mixed = _sgu_mixed(cv * rstd * gn_ref[...], w_ref, b_ref, nch, G)
        o_ref[...] = (_gelu(u_ref[...]) * mixed).astype(BF16)

    return pl.pallas_call(
        body, name=name, grid=(S // ts,), out_shape=SDS((S, width), BF16),
        in_specs=[BS((ts, width), lambda i: (i, c0)), BS((ts, width), lambda i: (i, c0 + 1)),
                  BS((1, width), lambda i: (0, 0)), BS(w.shape, lambda i: (0, 0, 0)), BS(b.shape, lambda i: (0, 0, 0))],
        out_specs=BS((ts, width), lambda i: (i, 0)), compiler_params=_cp("parallel"))(proj, proj, gn, w, b)


def _sgu_bwd(proj, gn, w, b, col0, width, do, name):
    S = proj.shape[0]
    G = w.shape[0]
    ts = _tile(S, 512)
    nch = ts // SGU_CHUNK
    c0 = col0 // width

    def body(u_ref, v_ref, gn_ref, w_ref, b_ref, do_ref, du_ref, dv_ref, dw_ref, db_ref, dgn_ref):
        uin, vin, dov, gnv = u_ref[...], v_ref[...], do_ref[...], gn_ref[...]
        cu, cv = _gelu(uin), _gelu(vin)
        rstd = lax.rsqrt(jnp.mean(cv * cv, axis=-1, keepdims=True) + EPS)
        vhat = cv * rstd
        vn = vhat * gnv
        mixed = _sgu_mixed(vn, w_ref, b_ref, nch, G)
        du_ref[...] = (dov * mixed * _gelu_grad(uin)).astype(BF16)
        dmix = dov * cu

        @pl.when(pl.program_id(0) == 0)
        def _():
            dw_ref[...] = jnp.zeros_like(dw_ref)
            db_ref[...] = jnp.zeros_like(db_ref)
            dgn_ref[...] = jnp.zeros_like(dgn_ref)

        rows = []
        for ch in range(nch):
            cols = []
            for g in range(G):
                wt, mask = _tril_w(w_ref, g)
                sl = (slice(ch * SGU_CHUNK, (ch + 1) * SGU_CHUNK), slice(g * LANES, (g + 1) * LANES))
                dm, vb = dmix[sl], vn[sl].astype(BF16)
                db_ref[g] += jnp.sum(dm, axis=-1, keepdims=True)
                dwg = lax.dot_general(dm.astype(BF16), vb, (_DIMS["nt"], ((), ())), preferred_element_type=F32)
                dw_ref[g] += jnp.where(mask, dwg, 0.0)
                cols.append(lax.dot_general(wt.astype(BF16), dm.astype(BF16), (_DIMS["tn"], ((), ())),
                                            preferred_element_type=F32))
            rows.append(jnp.concatenate(cols, axis=1))
        dvn = jnp.concatenate(rows, axis=0)
        dgn_ref[...] += jnp.sum(dvn * vhat, axis=0, keepdims=True)
        dvh = dvn * gnv
        dcv = rstd * (dvh - vhat * jnp.mean(dvh * vhat, axis=-1, keepdims=True))
        dv_ref[...] = (dcv * _gelu_grad(vin)).astype(BF16)

    row = BS((ts, width), lambda i: (i, 0))
    wsp, bsp, gsp = BS(w.shape, lambda i: (0, 0, 0)), BS(b.shape, lambda i: (0, 0, 0)), BS((1, width), lambda i: (0, 0))
    return pl.pallas_call(
        body, name=name, grid=(S // ts,),
        out_shape=(SDS((S, width), BF16), SDS((S, width), BF16), SDS(w.shape, F32), SDS(b.shape, F32),
                   SDS((1, width), F32)),
        in_specs=[BS((ts, width), lambda i: (i, c0)), BS((ts, width), lambda i: (i, c0 + 1)), gsp, wsp, bsp, row],
        out_specs=(row, row, wsp, bsp, gsp), compiler_params=_cp("arbitrary"))(proj, proj, gn, w, b, do)


def _rope_tables(positions, inv, name):
    S = positions.shape[0]
    ts = _tile(S, 512, 8)
    half = ROPE_DIM // 2

    def body(p_ref, inv_ref, c_ref, sa_ref, sb_ref):
        ang = p_ref[...].astype(F32) * inv_ref[...]
        lane = lax.broadcasted_iota(jnp.int32, (ts, LANES), 1) % HEAD_DIM
        sn = jnp.sin(ang)
        c_ref[...] = jnp.cos(ang)
        sa_ref[...] = jnp.where(lane < half, -sn, 0.0)
        sb_ref[...] = jnp.where((lane >= half) & (lane < ROPE_DIM), sn, 0.0)

    out = BS((ts, LANES), lambda i: (i, 0))
    return pl.pallas_call(
        body, name=name, grid=(S // ts,), out_shape=(SDS((S, LANES), F32),) * 3,
        in_specs=[BS((ts, 1), lambda i: (i, 0)), BS((1, LANES), lambda i: (0, 0))], out_specs=(out, out, out),
        compiler_params=_cp("parallel"))(positions, inv)


def _rope_apply(x, col0, width, tables, sign, out_dtype, name):
    S = x.shape[0]
    ts = _tile(S, 512, 8)
    tc = 2 * LANES
    nb, c0 = width // tc, col0 // tc
    half = ROPE_DIM // 2

    def body(x_ref, c_ref, sa_ref, sb_ref, o_ref):
        xv = x_ref[...].astype(F32)
        wide = lambda r: jnp.concatenate([r[...], r[...]], axis=1)
        y = (xv * wide(c_ref) + pltpu.roll(xv, tc - half, 1) * (sign * wide(sa_ref))
             + pltpu.roll(xv, half, 1) * (sign * wide(sb_ref)))
        o_ref[...] = y.astype(o_ref.dtype)

    tab = BS((ts, LANES), lambda i, j: (i, 0))
    return pl.pallas_call(
        body, name=name, grid=(S // ts, nb), out_shape=SDS((S, width), out_dtype),
        in_specs=[BS((ts, tc), lambda i, j: (i, c0 + j)), tab, tab, tab], out_specs=BS((ts, tc), lambda i, j: (i, j)),
        compiler_params=_cp("parallel", "parallel"))(x, *tables)


def _dil_masks(hh, m, bpcs, hpg):
    g = hh // hpg
    bpc = jnp.where(g == 0, bpcs[0], jnp.where(g == 1, bpcs[1], bpcs[2]))
    r = lax.broadcasted_iota(jnp.int32, (DIL_SPAN, DIL_SPAN), 0)
    c = lax.broadcasted_iota(jnp.int32, (DIL_SPAN, DIL_SPAN), 1)
    return (c >= r) & (lax.rem(m, bpc) != 0), c <= r


def _nt(a, b):
    return lax.dot_general(a, b, (_DIMS["nt"], ((), ())), preferred_element_type=F32)


def _tn(a, b):
    return lax.dot_general(a, b, (_DIMS["tn"], ((), ())), preferred_element_type=F32)


def _dil_fwd(q, k, v, bpcs, hpg, name, comm=None):
    HH, S, Dh = q.shape
    nb = S // DIL_SPAN
    scale = Dh ** -0.5

    def body(q_ref, kp_ref, kc_ref, vp_ref, vc_ref, o_ref, lse_ref):
        mp, mc = _dil_masks(pl.program_id(0) * hpg, pl.program_id(1), bpcs, hpg)
        for h in range(hpg):
            qv = q_ref[h]
            sp = jnp.where(mp, _nt(qv, kp_ref[h]) * scale, NEG)
            sc = jnp.where(mc, _nt(qv, kc_ref[h]) * scale, NEG)
            mx = jnp.maximum(jnp.max(sp, axis=-1, keepdims=True), jnp.max(sc, axis=-1, keepdims=True))
            pp, pc = jnp.exp(sp - mx), jnp.exp(sc - mx)
            l = jnp.sum(pp, axis=-1, keepdims=True) + jnp.sum(pc, axis=-1, keepdims=True)
            acc = (jnp.dot(pp.astype(BF16), vp_ref[h], preferred_element_type=F32)
                   + jnp.dot(pc.astype(BF16), vc_ref[h], preferred_element_type=F32))
            o_ref[h] = acc / l
            lse_ref[h] = mx + jnp.log(l)

    cur = BS((hpg, DIL_SPAN, Dh), lambda g, m: (g, m, 0))
    prev = BS((hpg, DIL_SPAN, Dh), lambda g, m: (g, jnp.maximum(m - 1, 0), 0))
    return _call(
        body, name=name, grid=(HH // hpg, nb), out_shape=(SDS((HH, S, Dh), F32), SDS((HH, S, 1), F32)),
        in_specs=[cur, prev, cur, prev, cur], out_specs=(cur, BS((hpg, DIL_SPAN, 1), lambda g, m: (g, m, 0))),
        sem=("parallel", "parallel"), args=(q, k, k, v, v), comm=comm)


def _dil_bwd_dq(q, k, v, do, lse, coef, bpcs, hpg, name, comm=None):
    HH, S, Dh = q.shape
    nb = S // DIL_SPAN
    scale = Dh ** -0.5

    def body(q_ref, kp_ref, kc_ref, vp_ref, vc_ref, do_ref, lse_ref, cf_ref, dq_ref):
        mp, mc = _dil_masks(pl.program_id(0) * hpg, pl.program_id(1), bpcs, hpg)
        for h in range(hpg):
            qv, dob, lsev, cf = q_ref[h], do_ref[h], lse_ref[h], cf_ref[h]
            pp = jnp.exp(jnp.where(mp, _nt(qv, kp_ref[h]) * scale, NEG) - lsev)
            pc = jnp.exp(jnp.where(mc, _nt(qv, kc_ref[h]) * scale, NEG) - lsev)
            dsp = pp * (_nt(dob, vp_ref[h]) + cf)
            dsc = pc * (_nt(dob, vc_ref[h]) + cf)
            dq_ref[h] = (jnp.dot(dsp.astype(BF16), kp_ref[h], preferred_element_type=F32)
                         + jnp.dot(dsc.astype(BF16), kc_ref[h], preferred_element_type=F32)) * scale

    cur = BS((hpg, DIL_SPAN, Dh), lambda g, m: (g, m, 0))
    prev = BS((hpg, DIL_SPAN, Dh), lambda g, m: (g, jnp.maximum(m - 1, 0), 0))
    col = BS((hpg, DIL_SPAN, 1), lambda g, m: (g, m, 0))
    return _call(
        body, name=name, grid=(HH // hpg, nb), out_shape=SDS((HH, S, Dh), F32),
        in_specs=[cur, prev, cur, prev, cur, cur, col, col], out_specs=cur,
        sem=("parallel", "parallel"), args=(q, k, k, v, v, do, lse, coef), comm=comm)


def _dil_bwd_dkv(q, k, v, do, lse, coef, bpcs, hpg, name, comm=None):
    HH, S, Dh = q.shape
    nb = S // DIL_SPAN
    scale = Dh ** -0.5

    def body(k_ref, v_ref, qc_ref, qn_ref, doc_ref, don_ref, lc_ref, ln_ref, cc_ref, cn_ref, dk_ref, dv_ref):
        hh, m = pl.program_id(0) * hpg, pl.program_id(1)
        _, mc = _dil_masks(hh, m, bpcs, hpg)
        mp, _ = _dil_masks(hh, m + 1, bpcs, hpg)
        mp = mp & (m + 1 < nb)
        for h in range(hpg):
            kv, vv = k_ref[h], v_ref[h]
            pc = jnp.exp(jnp.where(mc, _nt(qc_ref[h], kv) * scale, NEG) - lc_ref[h])
            pn = jnp.exp(jnp.where(mp, _nt(qn_ref[h], kv) * scale, NEG) - ln_ref[h])
            dsc = pc * (_nt(doc_ref[h], vv) + cc_ref[h])
            dsn = pn * (_nt(don_ref[h], vv) + cn_ref[h])
            dv_ref[h] = _tn(pc.astype(BF16), doc_ref[h]) + _tn(pn.astype(BF16), don_ref[h])
            dk_ref[h] = (_tn(dsc.astype(BF16), qc_ref[h]) + _tn(dsn.astype(BF16), qn_ref[h])) * scale

    cur = BS((hpg, DIL_SPAN, Dh), lambda g, m: (g, m, 0))
    nxt = BS((hpg, DIL_SPAN, Dh), lambda g, m: (g, jnp.minimum(m + 1, nb - 1), 0))
    col = BS((hpg, DIL_SPAN, 1), lambda g, m: (g, m, 0))
    ncol = BS((hpg, DIL_SPAN, 1), lambda g, m: (g, jnp.minimum(m + 1, nb - 1), 0))
    return _call(
        body, name=name, grid=(HH // hpg, nb), out_shape=(SDS((HH, S, Dh), F32), SDS((HH, S, Dh), F32)),
        in_specs=[cur, cur, cur, nxt, cur, nxt, col, ncol, col, ncol], out_specs=(cur, cur),
        sem=("parallel", "parallel"), args=(k, v, q, q, do, do, lse, lse, coef, coef), comm=comm)


def _dil_merge(o3, lse3, name):
    G, H, S, Dh = o3.shape
    ts = _tile(S, 512, 8)

    def body(o_ref, l_ref, out_ref):
        lv = [l_ref[g, 0] for g in range(G)]
        mx = functools.reduce(jnp.maximum, lv)
        ex = [jnp.exp(v - mx) for v in lv]
        den = functools.reduce(lambda a, b: a + b, ex)
        out_ref[0] = functools.reduce(lambda a, b: a + b, [(ex[g] / den) * o_ref[g, 0] for g in range(G)])

    return pl.pallas_call(
        body, name=name, grid=(H, S // ts), out_shape=SDS((H, S, Dh), F32),
        in_specs=[BS((G, 1, ts, Dh), lambda h, i: (0, h, i, 0)), BS((G, 1, ts, 1), lambda h, i: (0, h, i, 0))],
        out_specs=BS((1, ts, Dh), lambda h, i: (h, i, 0)), compiler_params=_cp("parallel", "parallel"))(o3, lse3)


def _dil_merge_bwd(o3, lse3, do, name):
    G, H, S, Dh = o3.shape
    ts = _tile(S, 512, 8)

    def body(o_ref, l_ref, do_ref, do3_ref, cf_ref):
        lv = [l_ref[g, 0] for g in range(G)]
        mx = functools.reduce(jnp.maximum, lv)
        ex = [jnp.exp(v - mx) for v in lv]
        den = functools.reduce(lambda a, b: a + b, ex)
        wt = [e / den for e in ex]
        dov = do_ref[0]
        dw = [jnp.sum(dov * o_ref[g, 0], axis=-1, keepdims=True) for g in range(G)]
        mean = functools.reduce(lambda a, b: a + b, [wt[g] * dw[g] for g in range(G)])
        for g in range(G):
            do3_ref[g, 0] = (wt[g] * dov).astype(BF16)
            cf_ref[g, 0] = wt[g] * (dw[g] - mean) - wt[g] * dw[g]

    o_spec = BS((G, 1, ts, Dh), lambda h, i: (0, h, i, 0))
    l_spec = BS((G, 1, ts, 1), lambda h, i: (0, h, i, 0))
    return pl.pallas_call(
        body, name=name, grid=(H, S // ts), out_shape=(SDS((G, H, S, Dh), BF16), SDS((G, H, S, 1), F32)),
        in_specs=[o_spec, l_spec, BS((1, ts, Dh), lambda h, i: (h, i, 0))], out_specs=(o_spec, l_spec),
        compiler_params=_cp("parallel", "parallel"))(o3, lse3, do)


def _merge_fwd(proj, gate_col, os_, ws, name):
    S = proj.shape[0]
    D = ws[0].shape[1]
    nbr = len(os_)
    ts = _tile(S, 512, 8)
    tn = LANES * 2 if gate_col % (2 * LANES) == 0 and D % (2 * LANES) == 0 else LANES
    g0, gstep = gate_col // tn, D // tn

    def body(*refs):
        p_refs, o_refs, w_refs, out_ref = refs[:nbr], refs[nbr:2 * nbr], refs[2 * nbr:3 * nbr], refs[-1]
        acc = None
        for i in range(nbr):
            t = _sigmoid(p_refs[i][...]) * jnp.dot(o_refs[i][...], w_refs[i][...], preferred_element_type=F32)
            acc = t if acc is None else acc + t
        out_ref[...] = acc.astype(BF16)

    specs = ([BS((ts, tn), lambda s, j, i=i: (s, g0 + i * gstep + j)) for i in range(nbr)]
             + [BS((ts, o.shape[1]), lambda s, j: (s, 0)) for o in os_]
             + [BS((w.shape[0], tn), lambda s, j: (0, j)) for w in ws])
    return pl.pallas_call(
        body, name=name, grid=(S // ts, D // tn), out_shape=SDS((S, D), BF16), in_specs=specs,
        out_specs=BS((ts, tn), lambda s, j: (s, j)), compiler_params=_cp("parallel", "parallel"),
    )(*([proj] * nbr), *os_, *ws)


def _merge_bwd(proj, gate_col, os_, ws, dm, name):
    S = proj.shape[0]
    D = ws[0].shape[1]
    nbr = len(os_)
    ts = _tile(S, 512, 8)
    tn = LANES * 2 if gate_col % (2 * LANES) == 0 and D % (2 * LANES) == 0 else LANES
    g0, gstep = gate_col // tn, D // tn

    def body(*refs):
        p_refs, o_refs, w_refs = refs[:nbr], refs[nbr:2 * nbr], refs[2 * nbr:3 * nbr]
        dmv = refs[3 * nbr][...]
        dbo_refs, dgl_refs = refs[3 * nbr + 1:4 * nbr + 1], refs[4 * nbr + 1:]
        for i in range(nbr):
            gt = _sigmoid(p_refs[i][...])
            bo = jnp.dot(o_refs[i][...], w_refs[i][...], preferred_element_type=F32)
            dbo_refs[i][...] = (dmv * gt).astype(BF16)
            dgl_refs[i][...] = (dmv * bo * gt * (1.0 - gt)).astype(BF16)

    tile = BS((ts, tn), lambda s, j: (s, j))
    specs = ([BS((ts, tn), lambda s, j, i=i: (s, g0 + i * gstep + j)) for i in range(nbr)]
             + [BS((ts, o.shape[1]), lambda s, j: (s, 0)) for o in os_]
             + [BS((w.shape[0], tn), lambda s, j: (0, j)) for w in ws] + [tile])
    outs = pl.pallas_call(
        body, name=name, grid=(S // ts, D // tn), out_shape=(SDS((S, D), BF16),) * (2 * nbr), in_specs=specs,
        out_specs=(tile,) * (2 * nbr), compiler_params=_cp("parallel", "parallel"),
    )(*([proj] * nbr), *os_, *ws, dm)
    return outs[:nbr], outs[nbr:]


def _ffn_act(up, w, name):
    S, F2 = up.shape
    F = F2 // 2
    tc = LANES
    nb = F // tc

    def body(g_ref, v_ref, wg_ref, wv_ref, a_ref):
        ug = _conv3(g_ref[...], wg_ref)
        a_ref[...] = (ug * _sigmoid(ug) * _conv3(v_ref[...], wv_ref)).astype(BF16)

    return pl.pallas_call(
        body, name=name, grid=(nb,), out_shape=SDS((S, F), BF16),
        in_specs=[BS((S, tc), lambda j: (0, j)), BS((S, tc), lambda j: (0, nb + j)),
                  BS((3, tc), lambda j: (0, j)), BS((3, tc), lambda j: (0, nb + j))],
        out_specs=BS((S, tc), lambda j: (0, j)), compiler_params=_cp("parallel"))(up, up, w, w)


def _ffn_act_bwd(up, w, da, name):
    S, F2 = up.shape
    F = F2 // 2
    tc = LANES
    nb = F // tc

    def body(g_ref, v_ref, wg_ref, wv_ref, da_ref, dg_ref, dv_ref, dwg_ref, dwv_ref):
        pg, pv, dav = g_ref[...], v_ref[...], da_ref[...].astype(F32)
        ug, uv = _conv3(pg, wg_ref), _conv3(pv, wv_ref)
        sg = _sigmoid(ug)
        dg_ref[...] = _conv3_bwd(pg, dav * uv * (sg * (1.0 + ug * (1.0 - sg))), wg_ref, dwg_ref).astype(BF16)
        dv_ref[...] = _conv3_bwd(pv, dav * ug * sg, wv_ref, dwv_ref).astype(BF16)

    col = BS((S, tc), lambda j: (0, j))
    wsp = BS((3, tc), lambda j: (0, j))
    return pl.pallas_call(
        body, name=name, grid=(nb,),
        out_shape=(SDS((S, F), BF16), SDS((S, F), BF16), SDS((3, F), F32), SDS((3, F), F32)),
        in_specs=[col, BS((S, tc), lambda j: (0, nb + j)), wsp, BS((3, tc), lambda j: (0, nb + j)), col],
        out_specs=(col, col, wsp, wsp), compiler_params=_cp("parallel"))(up, up, w, w, da)


def _ple_fwd(x, pgl, pe, name):
    S, D = x.shape
    ts = _tile(S, 256, 8)

    def body(x_ref, g_ref, e_ref, o_ref):
        o_ref[...] = x_ref[...] + _sigmoid(g_ref[...]) * e_ref[...]

    row = BS((ts, D), lambda i: (i, 0))
    return pl.pallas_call(body, name=name, grid=(S // ts,), out_shape=SDS((S, D), F32), in_specs=[row] * 3,
                          out_specs=row, compiler_params=_cp("parallel"))(x, pgl, pe)


def _ple_bwd(dx, pgl, pe, name):
    S, D = dx.shape
    ts = _tile(S, 256, 8)

    def body(dx_ref, g_ref, e_ref, dg_ref, de_ref):
        dxv, sg = dx_ref[...], _sigmoid(g_ref[...])
        dg_ref[...] = (dxv * e_ref[...] * sg * (1.0 - sg)).astype(BF16)
        de_ref[...] = (dxv * sg).astype(BF16)

    row = BS((ts, D), lambda i: (i, 0))
    return pl.pallas_call(body, name=name, grid=(S // ts,), out_shape=(SDS((S, D), BF16),) * 2, in_specs=[row] * 3,
                          out_specs=(row, row), compiler_params=_cp("parallel"))(dx, pgl, pe)


def _adamw(w, g, m, v, name):
    shape = w.shape
    cols = shape[-1]
    rows = w.size // cols
    tr = _tile(rows, 256, 8)
    c1, c2 = 1.0 / (1.0 - ADAM_B1 ** ADAM_STEP), 1.0 / (1.0 - ADAM_B2 ** ADAM_STEP)

    def body(w_ref, g_ref, m_ref, v_ref, d_ref, nm_ref, nv_ref):
        gv = g_ref[...]
        nm = ADAM_B1 * m_ref[...] + (1.0 - ADAM_B1) * gv
        nv = ADAM_B2 * v_ref[...] + (1.0 - ADAM_B2) * (gv * gv)
        d_ref[...] = -ADAM_LR * ((nm * c1) / (jnp.sqrt(nv * c2) + ADAM_EPS) + ADAM_WD * w_ref[...])
        nm_ref[...] = nm
        nv_ref[...] = nv

    blk = BS((tr, cols), lambda i: (i, 0))
    outs = pl.pallas_call(
        body, name=name, grid=(rows // tr,), out_shape=(SDS((rows, cols), F32),) * 3, in_specs=[blk] * 4,
        out_specs=(blk,) * 3, compiler_params=_cp("parallel"),
    )(*(a.reshape(rows, cols) for a in (w, g, m, v)))
    return tuple(o.reshape(shape) for o in outs)


SHARDED = ("w_in", "shortconv_w", "w_br_fox", "w_br_conv", "w_br_sgu", "w_br_dil", "w_out", "w_up", "ffn_conv_w",
           "w_down", "w_ple_gate", "w_ple_proj")
REPLICATED = ("norm_mix_g", "fox_forget_b", "sgu_norm_g", "sgu_w", "sgu_b", "norm_ffn_g", "norm_ple_g")
WEIGHTS = ("norm_mix_g", "w_in", "fox_forget_b", "shortconv_w", "sgu_norm_g", "sgu_w", "sgu_b", "w_br_fox",
           "w_br_conv", "w_br_sgu", "w_br_dil", "w_out", "norm_ffn_g", "w_up", "ffn_conv_w", "w_down", "norm_ple_g",
           "w_ple_gate", "w_ple_proj", "final_norm_g")
PACK_COLS = 1024
PACK_ROWS = 256


def _pack(parts, dtype, unit):
    lead = parts[0].shape[:-1] if parts[0].ndim > 1 else ()
    flat = jnp.concatenate([p.astype(dtype) for p in parts], axis=-1)
    n = flat.shape[-1]
    rows = -(-n // (PACK_COLS * unit)) * unit
    flat = jnp.pad(flat, [(0, 0)] * len(lead) + [(0, rows * PACK_COLS - n)])
    return flat.reshape(lead + (rows, PACK_COLS))


def _to_heads(x, heads):
    S = x.shape[0]
    return x.reshape(S, heads, HEAD_DIM).transpose(1, 0, 2)


def _from_heads(x):
    H, S, Dh = x.shape
    return x.transpose(1, 0, 2).reshape(S, H * Dh)


def _to_residue(x, hpg):
    S = x.shape[0]
    gw = hpg * HEAD_DIM
    outs = []
    for g, (_, d) in enumerate(DIL_PATTERNS):
        seg = x[:, g * gw:(g + 1) * gw].reshape(S // d, d, hpg, HEAD_DIM)
        outs.append(seg.transpose(2, 1, 0, 3).reshape(hpg, S, HEAD_DIM))
    return jnp.concatenate(outs, axis=0)


def _from_residue(x, hpg):
    _, S, C = x.shape
    outs = []
    for g, (_, d) in enumerate(DIL_PATTERNS):
        seg = x[g * hpg:(g + 1) * hpg].reshape(hpg, d, S // d, C)
        outs.append(seg.transpose(0, 2, 1, 3).reshape(hpg, S, C))
    return jnp.stack(outs, axis=0)


def _groups_to_residue(x):
    G, hpg, S, C = x.shape
    outs = []
    for g, (_, d) in enumerate(DIL_PATTERNS):
        outs.append(x[g].reshape(hpg, S // d, d, C).transpose(0, 2, 1, 3).reshape(hpg, S, C))
    return jnp.concatenate(outs, axis=0)


def kernel(x, p, positions, norm_mix_g, w_in, fox_forget_b, shortconv_w, sgu_norm_g, sgu_w, sgu_b, w_br_fox, w_br_conv, w_br_sgu, w_br_dil, w_out, norm_ffn_g, w_up, ffn_conv_w, w_down, norm_ple_g, w_ple_gate, w_ple_proj, final_norm_g, loss_target, m_norm_mix_g, m_w_in, m_fox_forget_b, m_shortconv_w, m_sgu_norm_g, m_sgu_w, m_sgu_b, m_w_br_fox, m_w_br_conv, m_w_br_sgu, m_w_br_dil, m_w_out, m_norm_ffn_g, m_w_up, m_ffn_conv_w, m_w_down, m_norm_ple_g, m_w_ple_gate, m_w_ple_proj, m_final_norm_g, v_norm_mix_g, v_w_in, v_fox_forget_b, v_shortconv_w, v_sgu_norm_g, v_sgu_w, v_sgu_b, v_w_br_fox, v_w_br_conv, v_w_br_sgu, v_w_br_dil, v_w_out, v_norm_ffn_g, v_w_up, v_ffn_conv_w, v_w_down, v_norm_ple_g, v_w_ple_gate, v_w_ple_proj, v_final_norm_g):
    given = dict(locals())
    W = {n: given[n] for n in WEIGHTS}
    depth = w_in.shape[0]
    S, D = x.shape[1], x.shape[2]
    x0, target, pos = x[0], loss_target[0], positions[0].reshape(S, 1)
    FH = fox_forget_b.shape[1]
    FW = FH * HEAD_DIM
    CW = shortconv_w.shape[2] * N_DEV
    G = sgu_w.shape[1]
    SW = G * sgu_w.shape[2]
    DOUT = w_br_dil.shape[1]
    hpg = DOUT // HEAD_DIM
    NG = len(DIL_PATTERNS)
    DW = NG * DOUT
    bpcs = tuple(S // d // DIL_SPAN for _, d in DIL_PATTERNS)
    A0, B0 = 0, 3 * FW
    C0 = B0 + 3 * CW
    D0 = C0 + 2 * SW
    G0 = D0 + 3 * DW
    F0 = G0 + 4 * D
    NP = F0 + LANES
    orig = [3 * FW, FH, 3 * CW, 2 * SW, 3 * DW, 4 * D]
    o_off = [sum(orig[:i]) for i in range(len(orig) + 1)]

    me = 4 * lax.axis_index("x") + 2 * lax.axis_index("y") + lax.axis_index("c")
    n_in, o1, o2, d_in = w_in.shape[2], o_off[1], o_off[2], o_off[6]
    NPB = NP // LANES
    WB = -(-(LANES - 1 + n_in) // LANES)
    PW = (WB + 1) * LANES

    def a_of(j, clip=lambda v: min(max(v, 0), FH)):
        return n_in * j - clip(n_in * j - o1)

    def s_of(j):
        return min(a_of(j) // LANES, NPB - WB)

    def s_traced(j):
        return jnp.minimum(a_of(j, lambda v: jnp.clip(v, 0, FH)) // LANES, NPB - WB)

    def runs(j):
        lo, hi, out = n_in * j, n_in * (j + 1), []
        for a, b in ((0, o1), (o1, o2), (o2, d_in)):
            l, h = max(lo, a), min(hi, b)
            if l < h:
                dst = WB * LANES + l - o1 if a == o1 else l - (FH if l >= o2 else 0) - s_of(j) * LANES
                out.append((l - lo, h - l, dst))
        return out

    def to_window(j, shard):
        piece = jnp.zeros((shard.shape[0], PW), shard.dtype)
        for src, ln, dst in runs(j):
            piece = piece.at[:, dst:dst + ln].set(shard[:, src:src + ln])
        return piece

    def from_window(j, win):
        return jnp.concatenate([win[:, dst:dst + ln] for _, ln, dst in runs(j)], axis=1)

    def on_my_index(fn, arg):
        return lax.switch(me, [functools.partial(fn, j) for j in range(N_DEV)], arg)

    def place(axis, length):
        def view(ref, j):
            align = LANES if axis == len(ref.shape) - 1 else 16
            start = pl.multiple_of(j * length, align) if length % align == 0 else j * length
            return ref.at[tuple([slice(None)] * axis + [pl.ds(start, length)])]
        return view

    def slot(ref, j):
        return ref.at[j]

    def whole(ref, j):
        return ref

    n_up, r_down, r_sq = w_up.shape[2], w_down.shape[1], w_out.shape[1]
    n_br, n_sm, n_sc = w_br_fox.shape[2], w_br_dil.shape[2], shortconv_w.shape[2]
    FF2 = n_up * N_DEV
    cw_pad = -(-n_up // LANES) * LANES
    placed = [place(1, n_up), place(0, r_down), place(1, r_sq), place(2, n_br), place(2, n_sm)]

    def gather_plans(i):
        conv = jnp.zeros((8, cw_pad), F32).at[0:3, :n_up].set(ffn_conv_w[i]).at[3:6, :n_sc].set(shortconv_w[i])
        srcs = [on_my_index(to_window, w_in[i].astype(BF16)), w_up[i].astype(BF16), w_down[i].astype(BF16),
                jnp.stack([w_out[i], w_ple_gate[i]]).astype(BF16),
                jnp.stack([w_br_fox[i], w_br_conv[i], w_br_sgu[i]]).astype(BF16),
                jnp.stack([w_br_dil[i], w_ple_proj[i]]).astype(BF16), conv]
        shapes = [SDS((N_DEV, D, PW), BF16), SDS((D, FF2), BF16), SDS((r_down * N_DEV, D), BF16),
                  SDS((2, r_sq * N_DEV, D), BF16), SDS((3, w_br_fox.shape[1], n_br * N_DEV), BF16),
                  SDS((2, w_br_dil.shape[1], n_sm * N_DEV), BF16), SDS((N_DEV, 8, cw_pad), F32)]
        dsts = [slot] + placed + [slot]
        return [(srcs[a:b], shapes[a:b], [whole] * (b - a), dsts[a:b], True)
                for a, b in ((0, 1), (1, 2), (2, 3), (3, 7))]

    def gathered(parts):
        (pieces,), (wup,), (wdown,), (sq, br, sm, cv) = parts
        return {
            "w_in": _assemble(pieces, [s_of(j) for j in range(N_DEV)], WB, NPB, "assemble_w_in"),
            "w_up": wup, "w_down": wdown, "w_out": sq[0], "w_ple_gate": sq[1],
            "w_br_fox": br[0], "w_br_conv": br[1], "w_br_sgu": br[2], "w_br_dil": sm[0], "w_ple_proj": sm[1],
            "ffn_conv_w": jnp.moveaxis(cv[:, 0:3, :n_up], 0, 1).reshape(3, FF2),
            "shortconv_w": jnp.moveaxis(cv[:, 3:6, :n_sc], 0, 1).reshape(3, CW),
        }

    def one_plan(plans):
        return tuple(sum((list(plan[k]) for plan in plans), []) for k in range(4)) + (plans[0][4],)

    first = _comm("comm_gather_weights", *one_plan(gather_plans(0)))
    full = [gathered([first[0:1], first[1:2], first[2:3], first[3:7]])]

    inv = ROPE_THETA ** (-jnp.arange(ROPE_DIM // 2, dtype=F32) * (2.0 / ROPE_DIM))
    inv_row = jnp.tile(jnp.concatenate([inv, inv, jnp.zeros((HEAD_DIM - ROPE_DIM,), F32)]), LANES // HEAD_DIM)
    tables = _rope_tables(pos, inv_row.reshape(1, LANES), "rope_tables")

    def pad_lanes(v):
        return jnp.pad(v, ((0, 0), (0, LANES - v.shape[1])))

    saved = []
    xc = x0
    for i in range(depth):
        fw, sv = full[i], {}
        sv["x"] = xc
        nxt = gather_plans(i + 1) if i + 1 < depth else [None] * 4
        got = [None] * 4

        def host(k, result, nxt=nxt, got=got):
            if nxt[k] is None:
                return result
            got[k] = result[1]
            return result[0]

        h1 = _rms_fwd(xc, norm_mix_g[i], "rms_mix")
        proj = host(0, _mm(h1, fw["w_in"], "nn", F32, "mm_in", comm=nxt[0]))
        sv["h1"], sv["proj"] = h1, proj
        bias = pad_lanes(fox_forget_b[i].reshape(1, FH))
        Fp = _fox_prep(proj, bias, F0 // LANES, "fox_prep")[:, :, :FH].transpose(2, 1, 0)
        t = _tile(S, 256)
        qa, ka, va = (_to_heads(proj[:, A0 + j * FW:A0 + (j + 1) * FW], FH).astype(BF16) for j in range(3))
        ones3, zpad = jnp.ones((FH, S, 3), BF16), jnp.zeros((FH, S, HEAD_DIM - 6), BF16)
        q_aug = jnp.concatenate([qa, Fp, ones3, zpad], axis=-1)
        k_aug = jnp.concatenate([ka, ones3, -Fp, zpad], axis=-1)

        def blocks_t(a):
            return a.reshape(FH, S // t, t, a.shape[-1]).transpose(0, 1, 3, 2)

        kT = blocks_t(k_aug)
        oaT, lse_a = host(1, _foxt_fwd(q_aug, k_aug, blocks_t(va), "fox_fwd", comm=nxt[1]))
        oa_b = oaT.transpose(2, 0, 1).reshape(S, FW).astype(BF16)
        sv.update(bias=bias, q_aug=q_aug, k_aug=k_aug, kT=kT, va=va, lse_a=lse_a, oa_b=oa_b)
        ob = _sconv_fwd(proj, fw["shortconv_w"], B0, CW, "sconv_fwd")
        sgb = sgu_b[i].reshape(G, SGU_CHUNK, 1)
        oc = _sgu_fwd(proj, sgu_norm_g[i].reshape(1, SW), sgu_w[i], sgb, C0, SW, "sgu_fwd")
        qd = _rope_apply(proj, D0, DW, tables, 1.0, BF16, "rope_q")
        kd = _rope_apply(proj, D0 + DW, DW, tables, 1.0, BF16, "rope_k")
        qr, kr = _to_residue(qd, hpg), _to_residue(kd, hpg)
        vr = _to_residue(proj[:, D0 + 2 * DW:D0 + 3 * DW].astype(BF16), hpg)
        od_r, lse_r = host(2, _dil_fwd(qr, kr, vr, bpcs, hpg, "dil_fwd", comm=nxt[2]))
        o3, lse3 = _from_residue(od_r, hpg), _from_residue(lse_r, hpg)
        od = _from_heads(_dil_merge(o3, lse3, "dil_merge")).astype(BF16)
        sv.update(ob=ob, oc=oc, sgb=sgb, qr=qr, kr=kr, vr=vr, lse_r=lse_r, o3=o3, lse3=lse3, od=od)
        brs = [oa_b, ob, oc, od]
        wbr = [fw["w_br_fox"], fw["w_br_conv"], fw["w_br_sgu"], fw["w_br_dil"]]
        merged = _merge_fwd(proj, G0, brs, wbr, "merge_fwd")
        x2 = _mm(merged, fw["w_out"], "nn", F32, "mm_out", res=xc)
        h2 = _rms_fwd(x2, norm_ffn_g[i], "rms_ffn")
        up = host(3, _mm(h2, fw["w_up"], "nn", F32, "mm_up", comm=nxt[3]))
        if i + 1 < depth:
            full.append(gathered(got))
        act = _ffn_act(up, fw["ffn_conv_w"], "ffn_act")
        x3 = _mm(act, fw["w_down"], "nn", F32, "mm_down", res=x2)
        h3 = _rms_fwd(x3, norm_ple_g[i], "rms_ple")
        pgl = _mm(h3, fw["w_ple_gate"], "nn", F32, "mm_ple_gate")
        pb = p[i, 0].astype(BF16)
        pe = _mm(pb, fw["w_ple_proj"], "nn", F32, "mm_ple_proj")
        xc = _ple_fwd(x3, pgl, pe, "ple_fwd")
        sv.update(merged=merged, x2=x2, h2=h2, up=up, act=act, x3=x3, h3=h3, pgl=pgl, pb=pb, pe=pe)
        saved.append(sv)

    loss_row, dx, dxb, dg_final = _final_loss(xc, final_norm_g, target, "final_loss")
    loss = lax.psum(loss_row[0, 0], MESH_AXES)

    def in_window(half):
        def view(ref, j):
            return ref.at[pl.ds(half * (D // 2), D // 2),
                          pl.ds(pl.multiple_of(s_traced(j) * LANES, LANES), WB * LANES)]
        return view

    def in_flogit(ref, j):
        return ref.at[:, pl.ds(F0, LANES)]

    def scatter_plans(gs, parts=(0, 1, 2, 3, 4)):
        stacked = lambda shape, dt=BF16: SDS((N_DEV,) + tuple(shape), dt)

        def build(part):
            if part == 0:
                srcs, views = [gs["w_in"], gs["w_in"]], [in_window(0), in_flogit]
                shapes = [stacked((D // 2, WB * LANES)), stacked((D, LANES))]
            elif part == 1:
                srcs, views, shapes = [gs["w_up"]], [placed[0]], [stacked((D, n_up))]
            elif part == 2:
                srcs, views, shapes = [gs["w_down"]], [placed[1]], [stacked((r_down, D))]
            elif part == 3:
                conv = jnp.zeros((N_DEV, 8, cw_pad), F32)
                conv = conv.at[:, 0:3, :n_up].set(jnp.moveaxis(gs["ffn_conv_w"].reshape(3, N_DEV, n_up), 1, 0))
                conv = conv.at[:, 3:6, :n_sc].set(jnp.moveaxis(gs["shortconv_w"].reshape(3, N_DEV, n_sc), 1, 0))
                srcs = [jnp.stack([gs["w_out"], gs["w_ple_gate"]]),
                        jnp.stack([gs["w_br_fox"], gs["w_br_conv"], gs["w_br_sgu"]]),
                        jnp.stack([gs["w_br_dil"], gs["w_ple_proj"]]), conv]
                views = placed[2:] + [slot]
                shapes = [stacked((2, r_sq, D)), stacked((3, w_br_fox.shape[1], n_br)),
                          stacked((2, w_br_dil.shape[1], n_sm)), stacked((8, cw_pad), F32)]
            else:
                srcs, views, shapes = [gs["w_in"]], [in_window(1)], [stacked((D // 2, WB * LANES))]
            return srcs, shapes, views, [slot] * len(srcs), False

        return [build(part) for part in parts]

    grads_sh = [None] * depth
    grads_rep = [None] * depth
    scattered = [None] * depth
    for i in reversed(range(depth)):
        fw, sv = full[i], saved[i]
        gs, gr = {}, {}
        pend = scatter_plans(grads_sh[i + 1]) if i + 1 < depth else [None] * 5
        got = [None] * 5

        def host(k, result, pend=pend, got=got):
            if pend[k] is None:
                return result
            got[k] = result[1]
            return result[0]

        proj = sv["proj"]
        dpgl, dpe = _ple_bwd(dx, sv["pgl"], sv["pe"], "ple_bwd")
        gs["w_ple_proj"] = _mm(sv["pb"], dpe, "tn", BF16, "mm_dw_ple_proj")
        gs["w_ple_gate"] = _mm(sv["h3"], dpgl, "tn", BF16, "mm_dw_ple_gate")
        dh3 = _mm(dpgl, fw["w_ple_gate"], "nt", F32, "mm_dh3")
        dx, dxb, gr["norm_ple_g"] = _rms_bwd(sv["x3"], norm_ple_g[i], dh3, dx, "rms_ple_bwd")
        gs["w_down"] = _mm(sv["act"], dxb, "tn", BF16, "mm_dw_down")
        dact = _mm(dxb, fw["w_down"], "nt", BF16, "mm_dact")
        dug, duv, dwg, dwv = _ffn_act_bwd(sv["up"], fw["ffn_conv_w"], dact, "ffn_act_bwd")
        gs["ffn_conv_w"] = jnp.concatenate([dwg, dwv], axis=1)
        dup = jnp.concatenate([dug, duv], axis=1)
        gs["w_up"] = _mm(sv["h2"], dup, "tn", BF16, "mm_dw_up")
        own_down, own_up = scatter_plans(gs, (2, 1)) if i == 0 else (None, None)
        dh2 = _mm(dup, fw["w_up"], "nt", F32, "mm_dh2", comm=own_down)
        if i == 0:
            dh2, own_down = dh2
        dx, dxb, gr["norm_ffn_g"] = _rms_bwd(sv["x2"], norm_ffn_g[i], dh2, dx, "rms_ffn_bwd")
        gs["w_out"] = _mm(sv["merged"], dxb, "tn", BF16, "mm_dw_out")
        dmerged = _mm(dxb, fw["w_out"], "nt", F32, "mm_dmerged")
        brs = [sv["oa_b"], sv["ob"], sv["oc"], sv["od"]]
        names = ["w_br_fox", "w_br_conv", "w_br_sgu", "w_br_dil"]
        wbr = [fw[n] for n in names]
        dbo, dgl = _merge_bwd(proj, G0, brs, wbr, dmerged, "merge_bwd")
        dos = []
        for j, n in enumerate(names):
            gs[n] = _mm(brs[j], dbo[j], "tn", BF16, "mm_dw_" + n)
            dos.append(_mm(dbo[j], wbr[j], "nt", F32, "mm_do_" + n))
        doa = _to_heads(dos[0], FH).astype(BF16)
        dqT, delta = host(1, _foxt_bwd_dq(sv["q_aug"], sv["k_aug"], sv["kT"], sv["va"], doa, sv["lse_a"],
                                          "fox_bwd_dq", comm=pend[1]))
        dk_aug, dva, dfk = host(0, _foxt_bwd_dkv(sv["q_aug"], sv["k_aug"], sv["va"], doa, sv["lse_a"], delta,
                                                 "fox_bwd_dkv", comm=pend[0]))
        dF = pad_lanes(dfk[:, :, 0].T)
        daf, dbias = _fox_prep_bwd(proj, sv["bias"], F0 // LANES, dF, "fox_prep_bwd")
        gr["fox_forget_b"] = dbias[0, :FH]
        d_a = [dqT[:, :HEAD_DIM].transpose(2, 0, 1).reshape(S, FW).astype(BF16),
               _from_heads(dk_aug[:, :, :HEAD_DIM]).astype(BF16), _from_heads(dva).astype(BF16)]
        dxb_, dgb, dgc, gs["shortconv_w"] = _sconv_bwd(proj, fw["shortconv_w"], B0, CW, dos[1], "sconv_bwd")
        du, dv, dsw, dsb, dsg = _sgu_bwd(proj, sgu_norm_g[i].reshape(1, SW), sgu_w[i], sv["sgb"], C0, SW, dos[2],
                                         "sgu_bwd")
        gr["sgu_w"], gr["sgu_b"], gr["sgu_norm_g"] = dsw, dsb.reshape(G, SGU_CHUNK), dsg[0]
        do3, cf3 = _dil_merge_bwd(sv["o3"], sv["lse3"], _to_heads(dos[3], hpg), "dil_merge_bwd")
        do_r, cf_r = _groups_to_residue(do3), _groups_to_residue(cf3)
        dq_r = host(3, _dil_bwd_dq(sv["qr"], sv["kr"], sv["vr"], do_r, sv["lse_r"], cf_r, bpcs, hpg, "dil_bwd_dq",
                                   comm=pend[3]))
        dk_r, dv_r = host(2, _dil_bwd_dkv(sv["qr"], sv["kr"], sv["vr"], do_r, sv["lse_r"], cf_r, bpcs, hpg,
                                          "dil_bwd_dkv", comm=pend[2]))

        def natural(t):
            return _from_heads(_from_residue(t, hpg).reshape(NG * hpg, S, HEAD_DIM))

        dqd = _rope_apply(natural(dq_r), 0, DW, tables, -1.0, BF16, "rope_q_bwd")
        dkd = _rope_apply(natural(dk_r), 0, DW, tables, -1.0, BF16, "rope_k_bwd")
        dvd = natural(dv_r).astype(BF16)
        dproj = jnp.concatenate(d_a + [dxb_, dgb, dgc, du, dv, dqd, dkd, dvd] + list(dgl) + [daf], axis=1)
        gs["w_in"] = _mm(sv["h1"], dproj, "tn", BF16, "mm_dw_in", comm=own_up)
        if i == 0:
            gs["w_in"], own_up = gs["w_in"]
        dh1 = host(4, _mm(dproj, fw["w_in"], "nt", F32, "mm_dh1", comm=pend[4]))
        if i + 1 < depth:
            scattered[i + 1] = got
        dx, dxb, gr["norm_mix_g"] = _rms_bwd(sv["x"], norm_mix_g[i], dh1, dx, "rms_mix_bwd")
        gr["norm_mix_g"], gr["norm_ffn_g"], gr["norm_ple_g"] = (gr[n][0] for n in
                                                               ("norm_mix_g", "norm_ffn_g", "norm_ple_g"))
        grads_sh[i], grads_rep[i] = gs, gr
    grad_x = dx.reshape(1, S, D)

    last = _comm("comm_scatter_grads", *one_plan(scatter_plans(grads_sh[0], (0, 3, 4))))
    scattered[0] = [last[0:2], own_up, own_down, last[2:6], last[6:7]]

    def sum_sources(t, name):
        lead = t.shape[1:]
        return _sum8(t.reshape(N_DEV, -1, lead[-1]), name).reshape(lead)

    grads = {n: [None] * depth for n in SHARDED}
    for i in range(depth):
        (gw, gf), (gup,), (gdown,), (gsq, gbr, gsm, gcv), (gw2,) = scattered[i]
        rows = jnp.concatenate([sum_sources(gw, "sum_w_in"), sum_sources(gw2, "sum_w_in")], axis=0)
        win = jnp.concatenate([rows, sum_sources(gf, "sum_w_in_flogit")], axis=1)
        grads["w_in"][i] = on_my_index(from_window, win)
        grads["w_up"][i] = sum_sources(gup, "sum_w_up")
        grads["w_down"][i] = sum_sources(gdown, "sum_w_down")
        sq, br, sm = sum_sources(gsq, "sum_sq"), sum_sources(gbr, "sum_br"), sum_sources(gsm, "sum_sm")
        cv = sum_sources(gcv, "sum_conv")
        grads["w_out"][i], grads["w_ple_gate"][i] = sq[0], sq[1]
        grads["w_br_fox"][i], grads["w_br_conv"][i], grads["w_br_sgu"][i] = br[0], br[1], br[2]
        grads["w_br_dil"][i], grads["w_ple_proj"][i] = sm[0], sm[1]
        grads["ffn_conv_w"][i], grads["shortconv_w"][i] = cv[0:3, :n_up], cv[3:6, :n_sc]
    grads = {n: jnp.stack(v, axis=0) for n, v in grads.items()}

    rep_parts = [jnp.stack([grads_rep[i][n] for i in range(depth)], axis=0).reshape(-1) for n in REPLICATED]
    rep_parts.append(dg_final.reshape(-1))
    rep_sizes = [int(r.shape[0]) for r in rep_parts]
    rep_offs = [sum(rep_sizes[:i]) for i in range(len(rep_sizes) + 1)]
    rep = _sum8(_exchange(_pack(rep_parts, F32, 8), True, "comm_gather_small"), "sum_small").reshape(-1)
    for n, a, b in zip(REPLICATED + ("final_norm_g",), rep_offs[:-1], rep_offs[1:]):
        grads[n] = rep[a:b].reshape(W[n].shape)

    deltas, new_m, new_v = {}, {}, {}
    for n in WEIGHTS:
        deltas[n], new_m[n], new_v[n] = _adamw(W[n], grads[n], given["m_" + n], given["v_" + n], "adamw_" + n)
    return (loss, grad_x, *[grads[n] for n in WEIGHTS], *[deltas[n] for n in WEIGHTS],
            *[new_m[n] for n in WEIGHTS], *[new_v[n] for n in WEIGHTS])
```

```python
import functools

import jax
import jax.numpy as jnp
from jax import lax
from jax.experimental import pallas as pl
from jax.experimental.pallas import tpu as pltpu

F32 = jnp.float32
BF16 = jnp.bfloat16
EPS = 1e-6
NEG = -1e30
HEAD_DIM = 64
SGU_CHUNK = 128
DIL_PATTERNS = ((128, 1), (512, 4), (2048, 16))
DIL_SPAN = 128
ROPE_THETA = 500000.0
ROPE_DIM = 16
N_DEV = 8
LANES = 128
VMEM_LIMIT = 56 * 1024 * 1024
ADAM_LR, ADAM_B1, ADAM_B2, ADAM_EPS, ADAM_WD, ADAM_STEP = 0.001, 0.9, 0.999, 1e-08, 0.01, 10
MESH_AXES = ("x", "y", "c")

BS = pl.BlockSpec
SDS = jax.ShapeDtypeStruct


def _cp(*sem):
    return pltpu.CompilerParams(dimension_semantics=sem, vmem_limit_bytes=VMEM_LIMIT)


def _tile(dim, pref, unit=LANES):
    if dim % unit:
        return dim
    best, t = unit, unit
    while t <= min(dim, pref):
        if dim % t == 0:
            best = t
        t += unit
    return best


def _sigmoid(z):
    return 1.0 / (1.0 + jnp.exp(-z))


def _exchange(x, gather, name):
    shape = x.shape if not gather else (N_DEV,) + x.shape

    def body(x_ref, o_ref, send_sems, recv_sems, local_sem):
        ix, iy, ic = lax.axis_index("x"), lax.axis_index("y"), lax.axis_index("c")
        me = 4 * ix + 2 * iy + ic

        def src(j):
            return x_ref if gather else x_ref.at[j]

        local = pltpu.make_async_copy(src(me), o_ref.at[me], local_sem)
        local.start()
        sends, recvs = [], []
        for r in range(1, N_DEV):
            px = 1 - ix if (r >> 2) & 1 else ix
            py = 1 - iy if (r >> 1) & 1 else iy
            pc = 1 - ic if r & 1 else ic
            peer = 4 * px + 2 * py + pc
            sends.append(pltpu.make_async_remote_copy(
                src_ref=src(peer), dst_ref=o_ref.at[me], send_sem=send_sems.at[r - 1], recv_sem=recv_sems.at[r - 1],
                device_id=(px, py, pc), device_id_type=pl.DeviceIdType.MESH))
            recvs.append(pltpu.make_async_remote_copy(
                src_ref=src(peer), dst_ref=o_ref.at[peer], send_sem=send_sems.at[r - 1], recv_sem=recv_sems.at[r - 1],
                device_id=(px, py, pc), device_id_type=pl.DeviceIdType.MESH))
        for cp in sends:
            cp.start()
        for cp in recvs:
            cp.wait_recv()
        for cp in sends:
            cp.wait_send()
        local.wait()

    return pl.pallas_call(
        body, name=name, out_shape=SDS(shape, x.dtype),
        in_specs=[BS(memory_space=pl.ANY)], out_specs=BS(memory_space=pl.ANY),
        scratch_shapes=[pltpu.SemaphoreType.DMA((N_DEV - 1,)), pltpu.SemaphoreType.DMA((N_DEV - 1,)),
                        pltpu.SemaphoreType.DMA],
    )(x)


def _comm(name, srcs, out_shapes, src_views, dst_views, relay=False):
    n = len(srcs)

    def body(*refs):
        copies = _comm_copies(refs[:n], refs[n:2 * n], refs[2 * n:], src_views, dst_views, relay)
        _comm_start(copies)
        _comm_wait(copies)

    anyspec = BS(memory_space=pl.ANY)
    return pl.pallas_call(
        body, name=name, out_shape=tuple(out_shapes), in_specs=[anyspec] * n, out_specs=(anyspec,) * n,
        scratch_shapes=_comm_sems(n))(*srcs)


def _comm_sems(n):
    return [pltpu.SemaphoreType.DMA((N_DEV - 1, n)), pltpu.SemaphoreType.DMA((N_DEV - 1, n)),
            pltpu.SemaphoreType.DMA((n,))]


SIBLING = 1
OTHER_CHIP_SAME_CORE = (2, 4, 6)


def _comm_copies(s_refs, o_refs, sems, src_views, dst_views, relay=False):
    send_sems, recv_sems, local_sems = sems
    n = len(s_refs)
    ix, iy, ic = lax.axis_index("x"), lax.axis_index("y"), lax.axis_index("c")
    me = 4 * ix + 2 * iy + ic

    def at(r):
        px = 1 - ix if (r >> 2) & 1 else ix
        py = 1 - iy if (r >> 1) & 1 else iy
        pc = 1 - ic if r & 1 else ic
        return (px, py, pc), 4 * px + 2 * py + pc

    def copy(r, k, src, dst, to):
        return pltpu.make_async_remote_copy(src_ref=src, dst_ref=dst, send_sem=send_sems.at[r - 1, k],
                                            recv_sem=recv_sems.at[r - 1, k], device_id=to,
                                            device_id_type=pl.DeviceIdType.MESH)

    local = [pltpu.make_async_copy(src_views[k](s_refs[k], me), dst_views[k](o_refs[k], me), local_sems.at[k])
             for k in range(n)]
    sends, passed, arrivals = [], [], []
    for r in range(1, N_DEV):
        to, peer = at(r)
        for k in range(n):
            arrival = copy(r, k, src_views[k](s_refs[k], peer), dst_views[k](o_refs[k], peer), to)
            direct = not relay or r == SIBLING or r in OTHER_CHIP_SAME_CORE
            if direct:
                sends.append(copy(r, k, src_views[k](s_refs[k], peer), dst_views[k](o_refs[k], me), to))
            if relay and r in OTHER_CHIP_SAME_CORE:
                landed = dst_views[k](o_refs[k], peer)
                passed.append((arrival, copy(r ^ SIBLING, k, landed, landed, at(SIBLING)[0])))
            else:
                arrivals.append(arrival)
    return local, sends, passed, arrivals


def _comm_start(copies):
    local, sends, _, _ = copies
    for cp in local + sends:
        cp.start()


def _comm_wait(copies):
    local, sends, passed, arrivals = copies
    for arrival, onward in passed:
        arrival.wait_recv()
        onward.start()
    for cp in arrivals:
        cp.wait_recv()
    for cp in sends + [onward for _, onward in passed]:
        cp.wait_send()
    for cp in local:
        cp.wait()


def _call(body, *, name, grid, in_specs, out_specs, out_shape, args, scratch_shapes=(), sem=None, comm=None):
    single = not isinstance(out_shape, (tuple, list))
    out_shape = (out_shape,) if single else tuple(out_shape)
    out_specs = (out_specs,) if single else tuple(out_specs)
    if comm is None:
        outs = pl.pallas_call(body, name=name, grid=grid, in_specs=list(in_specs), out_specs=out_specs,
                              out_shape=out_shape, scratch_shapes=list(scratch_shapes),
                              compiler_params=_cp(*sem))(*args)
        return outs[0] if single else outs
    srcs, shapes, src_views, dst_views, relay = comm
    n, ni, no, ns = len(srcs), len(in_specs), len(out_shape), len(scratch_shapes)

    def hosted(*refs):
        ins, c_in = refs[:ni], refs[ni:ni + n]
        outs, c_out = refs[ni + n:ni + n + no], refs[ni + n + no:ni + 2 * n + no]
        scr, sems = refs[ni + 2 * n + no:ni + 2 * n + no + ns], refs[ni + 2 * n + no + ns:]
        ids = [pl.program_id(a) for a in range(len(grid))]
        first = functools.reduce(jnp.logical_and, [i == 0 for i in ids])
        last = functools.reduce(jnp.logical_and, [i == g - 1 for i, g in zip(ids, grid)])

        @pl.when(first)
        def _():
            _comm_start(_comm_copies(c_in, c_out, sems, src_views, dst_views, relay))

        body(*ins, *outs, *scr)

        @pl.when(last)
        def _():
            _comm_wait(_comm_copies(c_in, c_out, sems, src_views, dst_views, relay))

    anyspec = BS(memory_space=pl.ANY)
    outs = pl.pallas_call(
        hosted, name=name, grid=grid, in_specs=list(in_specs) + [anyspec] * n,
        out_specs=out_specs + (anyspec,) * n, out_shape=out_shape + tuple(shapes),
        scratch_shapes=list(scratch_shapes) + _comm_sems(n),
        compiler_params=_cp(*(["arbitrary"] * len(grid))))(*args, *srcs)
    return (outs[0] if single else outs[:no]), outs[no:]


def _assemble(pieces, starts, wb, npb, name):
    _, K, pw = pieces.shape
    tr = _tile(K, 256, 16)

    def body(p_ref, o_ref):
        for b in range(npb - 1):
            acc = None
            for j in range(N_DEV):
                if starts[j] <= b < starts[j] + wb:
                    blk = p_ref[j, :, (b - starts[j]) * LANES:(b - starts[j] + 1) * LANES]
                    acc = blk if acc is None else acc + blk
            o_ref[:, b * LANES:(b + 1) * LANES] = acc
        acc = p_ref[0, :, wb * LANES:]
        for j in range(1, N_DEV):
            acc = acc + p_ref[j, :, wb * LANES:]
        o_ref[:, (npb - 1) * LANES:] = acc

    return pl.pallas_call(
        body, name=name, grid=(K // tr,), out_shape=SDS((K, npb * LANES), pieces.dtype),
        in_specs=[BS((N_DEV, tr, pw), lambda i: (0, i, 0))], out_specs=BS((tr, npb * LANES), lambda i: (i, 0)),
        compiler_params=_cp("parallel"))(pieces)


def _sum8(x, name):
    _, R, C = x.shape
    tr = _tile(R, 256, 16)

    def body(x_ref, o_ref):
        acc = x_ref[0].astype(F32)
        for j in range(1, N_DEV):
            acc = acc + x_ref[j].astype(F32)
        o_ref[...] = acc

    return pl.pallas_call(
        body, name=name, grid=(R // tr,), out_shape=SDS((R, C), F32),
        in_specs=[BS((N_DEV, tr, C), lambda i: (0, i, 0))], out_specs=BS((tr, C), lambda i: (i, 0)),
        compiler_params=_cp("parallel"))(x)


_DIMS = {"nn": ((1,), (0,)), "nt": ((1,), (1,)), "tn": ((0,), (0,))}


def _mm(a, b, mode, out_dtype, name, res=None, comm=None):
    if mode == "nn":
        (M, K), N = a.shape, b.shape[1]
    elif mode == "nt":
        (M, K), N = a.shape, b.shape[0]
    else:
        (K, M), N = a.shape, b.shape[1]
    tm, tn, tk = _tile(M, 1024), _tile(N, 1024), _tile(K, 2048)
    nk = K // tk
    a_spec = BS((tk, tm), lambda i, j, k: (k, i)) if mode == "tn" else BS((tm, tk), lambda i, j, k: (i, k))
    b_spec = BS((tn, tk), lambda i, j, k: (j, k)) if mode == "nt" else BS((tk, tn), lambda i, j, k: (k, j))
    o_spec = BS((tm, tn), lambda i, j, k: (i, j))
    dims = (_DIMS[mode], ((), ()))

    def body(a_ref, b_ref, *rest):
        o_ref, acc = rest[-2], rest[-1]
        k = pl.program_id(2)

        @pl.when(k == 0)
        def _():
            acc[...] = jnp.zeros_like(acc)

        acc[...] += lax.dot_general(a_ref[...].astype(BF16), b_ref[...].astype(BF16), dims,
                                    preferred_element_type=F32)

        @pl.when(k == nk - 1)
        def _():
            r = acc[...]
            if res is not None:
                r = r + rest[0][...]
            o_ref[...] = r.astype(o_ref.dtype)

    ins, specs = [a, b], [a_spec, b_spec]
    if res is not None:
        ins.append(res)
        specs.append(o_spec)
    return _call(body, name=name, grid=(M // tm, N // tn, nk), out_shape=SDS((M, N), out_dtype), in_specs=specs,
                 out_specs=o_spec, scratch_shapes=[pltpu.VMEM((tm, tn), F32)],
                 sem=("parallel", "parallel", "arbitrary"), args=ins, comm=comm)


def _rms_fwd(x, g, name):
    S, D = x.shape
    ts = _tile(S, 256, 8)

    def body(x_ref, g_ref, o_ref):
        xv = x_ref[...]
        rstd = lax.rsqrt(jnp.mean(xv * xv, axis=-1, keepdims=True) + EPS)
        o_ref[...] = (xv * rstd * g_ref[...]).astype(o_ref.dtype)

    return pl.pallas_call(
        body, name=name, grid=(S // ts,), out_shape=SDS((S, D), BF16),
        in_specs=[BS((ts, D), lambda i: (i, 0)), BS((1, D), lambda i: (0, 0))],
        out_specs=BS((ts, D), lambda i: (i, 0)), compiler_params=_cp("parallel"))(x, g.reshape(1, D))


def _rms_bwd(x, g, dh, dres, name):
    S, D = x.shape
    ts = _tile(S, 256, 8)

    def body(x_ref, g_ref, dh_ref, dres_ref, dx_ref, dxb_ref, dg_ref):
        xv = x_ref[...]
        rstd = lax.rsqrt(jnp.mean(xv * xv, axis=-1, keepdims=True) + EPS)
        xn = xv * rstd
        dhv = dh_ref[...]

        @pl.when(pl.program_id(0) == 0)
        def _():
            dg_ref[...] = jnp.zeros_like(dg_ref)

        dg_ref[...] += jnp.sum(dhv * xn, axis=0, keepdims=True)
        dhn = dhv * g_ref[...]
        dxv = rstd * (dhn - xn * jnp.mean(dhn * xn, axis=-1, keepdims=True)) + dres_ref[...]
        dx_ref[...] = dxv
        dxb_ref[...] = dxv.astype(BF16)

    row = BS((ts, D), lambda i: (i, 0))
    vec = BS((1, D), lambda i: (0, 0))
    return pl.pallas_call(
        body, name=name, grid=(S // ts,), out_shape=(SDS((S, D), F32), SDS((S, D), BF16), SDS((1, D), F32)),
        in_specs=[row, vec, row, row], out_specs=(row, row, vec), compiler_params=_cp("arbitrary"),
    )(x, g.reshape(1, D), dh, dres)


def _final_loss(x, g, target, name):
    S, D = x.shape
    ts = _tile(S, 256, 8)

    def body(x_ref, g_ref, t_ref, loss_ref, dx_ref, dxb_ref, dg_ref):
        xv = x_ref[...]
        rstd = lax.rsqrt(jnp.mean(xv * xv, axis=-1, keepdims=True) + EPS)
        xn = xv * rstd
        gv = g_ref[...]
        e = xn * gv - t_ref[...]

        @pl.when(pl.program_id(0) == 0)
        def _():
            dg_ref[...] = jnp.zeros_like(dg_ref)
            loss_ref[...] = jnp.zeros_like(loss_ref)

        loss_ref[...] += 0.5 * jnp.sum(jnp.mean(e * e, axis=-1, keepdims=True), axis=0, keepdims=True)
        dy = e * (1.0 / D)
        dg_ref[...] += jnp.sum(dy * xn, axis=0, keepdims=True)
        dhn = dy * gv
        dxv = rstd * (dhn - xn * jnp.mean(dhn * xn, axis=-1, keepdims=True))
        dx_ref[...] = dxv
        dxb_ref[...] = dxv.astype(BF16)

    row = BS((ts, D), lambda i: (i, 0))
    vec = BS((1, D), lambda i: (0, 0))
    return pl.pallas_call(
        body, name=name, grid=(S // ts,),
        out_shape=(SDS((1, LANES), F32), SDS((S, D), F32), SDS((S, D), BF16), SDS((1, D), F32)),
        in_specs=[row, vec, row], out_specs=(BS((1, LANES), lambda i: (0, 0)), row, row, vec),
        compiler_params=_cp("arbitrary"))(x, g.reshape(1, D), target)


def _shift_down(z, k):
    rows = lax.broadcasted_iota(jnp.int32, z.shape, 0)
    return jnp.where(rows >= k, pltpu.roll(z, k, 0), 0.0)


def _shift_up(z, k):
    n = z.shape[0]
    rows = lax.broadcasted_iota(jnp.int32, z.shape, 0)
    return jnp.where(rows < n - k, pltpu.roll(z, n - k, 0), 0.0)


def _cumsum_rows(z, reverse=False):
    n = z.shape[0]
    k = 1
    while k < n:
        z = z + (_shift_up(z, k) if reverse else _shift_down(z, k))
        k *= 2
    return z


def _conv3(z, w_ref):
    return w_ref[0:1, :] * _shift_down(z, 2) + w_ref[1:2, :] * _shift_down(z, 1) + w_ref[2:3, :] * z


def _conv3_bwd(z, dy, w_ref, dw_ref):
    dw_ref[0:1, :] = jnp.sum(dy * _shift_down(z, 2), axis=0, keepdims=True)
    dw_ref[1:2, :] = jnp.sum(dy * _shift_down(z, 1), axis=0, keepdims=True)
    dw_ref[2:3, :] = jnp.sum(dy * z, axis=0, keepdims=True)
    return w_ref[2:3, :] * dy + w_ref[1:2, :] * _shift_up(dy, 1) + w_ref[0:1, :] * _shift_up(dy, 2)


def _fox_prep(proj, bias, col, name):
    S = proj.shape[0]

    def body(a_ref, b_ref, f_ref):
        z = a_ref[...] + b_ref[...]
        logf = jnp.minimum(z, 0.0) - jnp.log(1.0 + jnp.exp(-jnp.abs(z)))
        rest = _cumsum_rows(logf)
        for k in range(3):
            piece = rest.astype(BF16)
            f_ref[k] = piece
            rest = rest - piece.astype(F32)

    return pl.pallas_call(
        body, name=name, grid=(1,), out_shape=SDS((3, S, LANES), BF16),
        in_specs=[BS((S, LANES), lambda i: (0, col)), BS((1, LANES), lambda i: (0, 0))],
        out_specs=BS((3, S, LANES), lambda i: (0, 0, 0)), compiler_params=_cp("arbitrary"))(proj, bias)


def _fox_prep_bwd(proj, bias, col, df, name):
    S = proj.shape[0]

    def body(a_ref, b_ref, df_ref, da_ref, db_ref):
        z = a_ref[...] + b_ref[...]
        da = _cumsum_rows(df_ref[...], reverse=True) * _sigmoid(-z)
        da_ref[...] = da.astype(BF16)
        db_ref[...] = jnp.sum(da, axis=0, keepdims=True)

    full = BS((S, LANES), lambda i: (0, 0))
    vec = BS((1, LANES), lambda i: (0, 0))
    return pl.pallas_call(
        body, name=name, grid=(1,), out_shape=(SDS((S, LANES), BF16), SDS((1, LANES), F32)),
        in_specs=[BS((S, LANES), lambda i: (0, col)), vec, full], out_specs=(full, vec),
        compiler_params=_cp("arbitrary"))(proj, bias, df)


def _fox_scores(qs, kv, fqv, fkv, i, j, t, masked):
    s = lax.dot_general(qs, kv, (_DIMS["nt"], ((), ())), preferred_element_type=F32) + fqv - fkv
    if not masked:
        return s
    row = i * t + lax.broadcasted_iota(jnp.int32, (t, t), 0)
    col = j * t + lax.broadcasted_iota(jnp.int32, (t, t), 1)
    return jnp.where(col <= row, s, NEG)


def _fox_fwd(q, k, v, fq, fk, name, comm=None):
    H, S, Dh = q.shape
    t = _tile(S, 256)
    n = S // t
    scale = Dh ** -0.5

    def body(q_ref, k_ref, v_ref, fq_ref, fk_ref, o_ref, lse_ref):
        i = pl.program_id(1)
        qs, fqv = q_ref[0] * scale, fq_ref[0]

        def step(masked, j, carry):
            m, l, acc = carry
            s = _fox_scores(qs, k_ref[0, j], fqv, fk_ref[0, j], i, j, t, masked)
            mn = jnp.maximum(m, jnp.max(s, axis=-1, keepdims=True))
            al = jnp.exp(m - mn)
            p = jnp.exp(s - mn)
            l = l * al + jnp.sum(p, axis=-1, keepdims=True)
            acc = acc * al + jnp.dot(p.astype(BF16), v_ref[0, j], preferred_element_type=F32)
            return mn, l, acc

        init = (jnp.full((t, 1), NEG, F32), jnp.zeros((t, 1), F32), jnp.zeros((t, Dh), F32))
        m, l, acc = step(True, i, lax.fori_loop(0, i, functools.partial(step, False), init))
        o_ref[0] = acc / l
        lse_ref[0] = m + jnp.log(l)

    blk = BS((1, t, Dh), lambda h, i: (h, i, 0))
    col = BS((1, t, 1), lambda h, i: (h, i, 0))
    res = BS((1, n, t, Dh), lambda h, i: (h, 0, 0, 0))
    return _call(
        body, name=name, grid=(H, n), out_shape=(SDS((H, S, Dh), F32), SDS((H, S, 1), F32)),
        in_specs=[blk, res, res, col, BS((1, n, 1, t), lambda h, i: (h, 0, 0, 0))], out_specs=(blk, col),
        sem=("parallel", "arbitrary"), args=(q, k.reshape(H, n, t, Dh), v.reshape(H, n, t, Dh), fq, fk), comm=comm)


def _fox_bwd_dq(q, k, v, fq, fk, do, lse, name, comm=None):
    H, S, Dh = q.shape
    t = _tile(S, 256)
    n = S // t
    scale = Dh ** -0.5

    def body(q_ref, k_ref, v_ref, fq_ref, fk_ref, do_ref, lse_ref, dq_ref, dl_ref):
        i = pl.program_id(1)
        qs, fqv, dob, lsev = q_ref[0] * scale, fq_ref[0], do_ref[0], lse_ref[0]

        def p_dp(masked, j):
            p = jnp.exp(_fox_scores(qs, k_ref[0, j], fqv, fk_ref[0, j], i, j, t, masked) - lsev)
            return p, lax.dot_general(dob, v_ref[0, j], (_DIMS["nt"], ((), ())), preferred_element_type=F32)

        def sum_step(masked, j, delta):
            p, dp = p_dp(masked, j)
            return delta + jnp.sum(p * dp, axis=-1, keepdims=True)

        delta = sum_step(True, i, lax.fori_loop(0, i, functools.partial(sum_step, False), jnp.zeros((t, 1), F32)))

        def step(masked, j, dq):
            p, dp = p_dp(masked, j)
            return dq + jnp.dot((p * (dp - delta)).astype(BF16), k_ref[0, j], preferred_element_type=F32)

        dq = step(True, i, lax.fori_loop(0, i, functools.partial(step, False), jnp.zeros((t, Dh), F32)))
        dq_ref[0] = dq * scale
        dl_ref[0] = delta

    blk = BS((1, t, Dh), lambda h, i: (h, i, 0))
    col = BS((1, t, 1), lambda h, i: (h, i, 0))
    res = BS((1, n, t, Dh), lambda h, i: (h, 0, 0, 0))
    return _call(
        body, name=name, grid=(H, n), out_shape=(SDS((H, S, Dh), F32), SDS((H, S, 1), F32)),
        in_specs=[blk, res, res, col, BS((1, n, 1, t), lambda h, i: (h, 0, 0, 0)), blk, col],
        out_specs=(blk, col), sem=("parallel", "arbitrary"), comm=comm,
        args=(q, k.reshape(H, n, t, Dh), v.reshape(H, n, t, Dh), fq, fk, do, lse))


def _fox_bwd_dkv(q, k, v, fq, fk, do, lse, delta, name, comm=None):
    H, S, Dh = q.shape
    t = _tile(S, 256)
    n = S // t
    scale = Dh ** -0.5
    tn_dims = (_DIMS["tn"], ((), ()))

    def body(q_ref, k_ref, v_ref, fq_ref, fk_ref, do_ref, lse_ref, dl_ref, dk_ref, dv_ref, dfk_ref):
        j = pl.program_id(1)
        kv, vv, fkv = k_ref[0], v_ref[0], fk_ref[0, 0]

        def step(masked, i, carry):
            dk, dv, dfk = carry
            qs, dob = q_ref[0, i] * scale, do_ref[0, i]
            p = jnp.exp(_fox_scores(qs, kv, fq_ref[0, i], fkv, i, j, t, masked) - lse_ref[0, i])
            dv = dv + lax.dot_general(p.astype(BF16), dob, tn_dims, preferred_element_type=F32)
            dp = lax.dot_general(dob, vv, (_DIMS["nt"], ((), ())), preferred_element_type=F32)
            ds = p * (dp - dl_ref[0, i])
            dk = dk + lax.dot_general(ds.astype(BF16), qs, tn_dims, preferred_element_type=F32)
            return dk, dv, dfk - jnp.sum(ds, axis=0, keepdims=True)

        init = (jnp.zeros((t, Dh), F32), jnp.zeros((t, Dh), F32), jnp.zeros((1, t), F32))
        dk, dv, dfk = lax.fori_loop(j + 1, n, functools.partial(step, False), step(True, j, init))
        dk_ref[0] = dk
        dv_ref[0] = dv
        dfk_ref[0, 0] = dfk

    blk = BS((1, t, Dh), lambda h, j: (h, j, 0))
    res = BS((1, n, t, Dh), lambda h, j: (h, 0, 0, 0))
    rcol = BS((1, n, t, 1), lambda h, j: (h, 0, 0, 0))
    frow = BS((1, 1, 1, t), lambda h, j: (h, j, 0, 0))
    return _call(
        body, name=name, grid=(H, n),
        out_shape=(SDS((H, S, Dh), F32), SDS((H, S, Dh), F32), SDS((H, n, 1, t), F32)),
        in_specs=[res, blk, blk, rcol, frow, res, rcol, rcol], out_specs=(blk, blk, frow),
        sem=("parallel", "arbitrary"), comm=comm,
        args=(q.reshape(H, n, t, Dh), k, v, fq.reshape(H, n, t, 1), fk, do.astype(BF16).reshape(H, n, t, Dh),
              lse.reshape(H, n, t, 1), delta.reshape(H, n, t, 1)))


FOX_HEADS_PER_STEP = 2


def _fox_q_scale(Dh):
    lane = lax.broadcasted_iota(jnp.int32, (1, 2 * Dh), 1)
    return jnp.where(lane < Dh, Dh ** -0.5, 1.0).astype(BF16)


def _foxt_scores(ka, qs, i, j, t, masked):
    s = lax.dot_general(ka, qs, (_DIMS["nt"], ((), ())), preferred_element_type=F32)
    if not masked:
        return s
    key = j * t + lax.broadcasted_iota(jnp.int32, (t, t), 0)
    qry = i * t + lax.broadcasted_iota(jnp.int32, (t, t), 1)
    return jnp.where(key <= qry, s, NEG)


def _foxt_fwd(q_aug, k_aug, vT, name, comm=None):
    H, S, W = q_aug.shape
    Dh = W // 2
    t = _tile(S, 256)
    n = S // t
    hb = FOX_HEADS_PER_STEP

    def body(q_ref, k_ref, v_ref, o_ref, lse_ref):
        i = pl.program_id(1)
        qs = [q_ref[h] * _fox_q_scale(Dh) for h in range(hb)]

        def step(masked, j, carry):
            out = []
            for h in range(hb):
                m, l, acc = carry[h]
                s = _foxt_scores(k_ref[h, j], qs[h], i, j, t, masked)
                mn = jnp.maximum(m, jnp.max(s, axis=0, keepdims=True))
                al = jnp.exp(m - mn)
                p = jnp.exp(s - mn)
                l = l * al + jnp.sum(p, axis=0, keepdims=True)
                acc = acc * al + jnp.dot(v_ref[h, j], p.astype(BF16), preferred_element_type=F32)
                out.append((mn, l, acc))
            return tuple(out)

        init = tuple((jnp.full((1, t), NEG, F32), jnp.zeros((1, t), F32), jnp.zeros((Dh, t), F32))
                     for _ in range(hb))
        fin = step(True, i, lax.fori_loop(0, i, functools.partial(step, False), init))
        for h in range(hb):
            m, l, acc = fin[h]
            o_ref[h] = acc / l
            lse_ref[h] = m + jnp.log(l)

    return _call(
        body, name=name, grid=(H // hb, n), out_shape=(SDS((H, Dh, S), F32), SDS((H, 1, S), F32)),
        in_specs=[BS((hb, t, W), lambda g, i: (g, i, 0)), BS((hb, n, t, W), lambda g, i: (g, 0, 0, 0)),
                  BS((hb, n, Dh, t), lambda g, i: (g, 0, 0, 0))],
        out_specs=(BS((hb, Dh, t), lambda g, i: (g, 0, i)), BS((hb, 1, t), lambda g, i: (g, 0, i))),
        sem=("parallel", "arbitrary"), args=(q_aug, k_aug.reshape(H, n, t, W), vT), comm=comm)


def _foxt_bwd_dq(q_aug, k_aug, kT, v, do, lse, name, comm=None):
    H, S, W = q_aug.shape
    Dh = W // 2
    t = _tile(S, 256)
    n = S // t
    hb = FOX_HEADS_PER_STEP

    def body(q_ref, k_ref, kt_ref, v_ref, do_ref, lse_ref, dq_ref, dl_ref):
        i = pl.program_id(1)
        qs = [q_ref[h] * _fox_q_scale(Dh) for h in range(hb)]

        def p_dp(h, masked, j):
            p = jnp.exp(_foxt_scores(k_ref[h, j], qs[h], i, j, t, masked) - lse_ref[h])
            return p, lax.dot_general(v_ref[h, j], do_ref[h], (_DIMS["nt"], ((), ())), preferred_element_type=F32)

        def sum_step(masked, j, delta):
            out = []
            for h in range(hb):
                p, dp = p_dp(h, masked, j)
                out.append(delta[h] + jnp.sum(p * dp, axis=0, keepdims=True))
            return tuple(out)

        zero = tuple(jnp.zeros((1, t), F32) for _ in range(hb))
        delta = sum_step(True, i, lax.fori_loop(0, i, functools.partial(sum_step, False), zero))

        def step(masked, j, dq):
            out = []
            for h in range(hb):
                p, dp = p_dp(h, masked, j)
                ds = (p * (dp - delta[h])).astype(BF16)
                out.append(dq[h] + jnp.dot(kt_ref[h, j], ds, preferred_element_type=F32))
            return tuple(out)

        zero = tuple(jnp.zeros((W, t), F32) for _ in range(hb))
        dq = step(True, i, lax.fori_loop(0, i, functools.partial(step, False), zero))
        for h in range(hb):
            dq_ref[h] = dq[h] * Dh ** -0.5
            dl_ref[h] = delta[h]

    row = BS((hb, 1, t), lambda g, i: (g, 0, i))
    blk = lambda w: BS((hb, t, w), lambda g, i: (g, i, 0))
    return _call(
        body, name=name, grid=(H // hb, n), out_shape=(SDS((H, W, S), F32), SDS((H, 1, S), F32)),
        in_specs=[blk(W), BS((hb, n, t, W), lambda g, i: (g, 0, 0, 0)), BS((hb, n, W, t), lambda g, i: (g, 0, 0, 0)),
                  BS((hb, n, t, Dh), lambda g, i: (g, 0, 0, 0)), blk(Dh), row],
        out_specs=(BS((hb, W, t), lambda g, i: (g, 0, i)), row), sem=("parallel", "arbitrary"), comm=comm,
        args=(q_aug, k_aug.reshape(H, n, t, W), kT, v.reshape(H, n, t, Dh), do, lse))


def _foxt_bwd_dkv(q_aug, k_aug, v, do, lse, delta, name, comm=None):
    H, S, W = q_aug.shape
    Dh = W // 2
    t = _tile(S, 256)
    n = S // t
    hb = FOX_HEADS_PER_STEP

    def body(q_ref, k_ref, v_ref, do_ref, lse_ref, dl_ref, dk_ref, dv_ref, dfk_ref):
        j = pl.program_id(1)

        def step(masked, i, carry):
            out = []
            for h in range(hb):
                dk, dv, dfk = carry[h]
                qs, dob = q_ref[h, i] * _fox_q_scale(Dh), do_ref[h, i]
                p = jnp.exp(_foxt_scores(k_ref[h], qs, i, j, t, masked) - lse_ref[h, i])
                dv = dv + jnp.dot(p.astype(BF16), dob, preferred_element_type=F32)
                dp = lax.dot_general(v_ref[h], dob, (_DIMS["nt"], ((), ())), preferred_element_type=F32)
                ds = p * (dp - dl_ref[h, i])
                dk = dk + jnp.dot(ds.astype(BF16), qs, preferred_element_type=F32)
                out.append((dk, dv, dfk - jnp.sum(ds, axis=1, keepdims=True)))
            return tuple(out)

        init = tuple((jnp.zeros((t, W), F32), jnp.zeros((t, Dh), F32), jnp.zeros((t, 1), F32)) for _ in range(hb))
        fin = lax.fori_loop(j + 1, n, functools.partial(step, False), step(True, j, init))
        for h in range(hb):
            dk_ref[h], dv_ref[h], dfk_ref[h] = fin[h]

    blk = lambda w: BS((hb, t, w), lambda g, j: (g, j, 0))
    res = lambda w: BS((hb, n, t, w), lambda g, j: (g, 0, 0, 0))
    rows = BS((hb, n, 1, t), lambda g, j: (g, 0, 0, 0))
    return _call(
        body, name=name, grid=(H // hb, n),
        out_shape=(SDS((H, S, W), F32), SDS((H, S, Dh), F32), SDS((H, S, 1), F32)),
        in_specs=[res(W), blk(W), blk(Dh), res(Dh), rows, rows], out_specs=(blk(W), blk(Dh), blk(1)),
        sem=("parallel", "arbitrary"), comm=comm,
        args=(q_aug.reshape(H, n, t, W), k_aug, v, do.reshape(H, n, t, Dh), lse.reshape(H, n, 1, t),
              delta.reshape(H, n, 1, t)))


def _sconv_fwd(proj, w, col0, width, name):
    S = proj.shape[0]
    tc = LANES
    nb, c0 = width // tc, col0 // tc

    def body(x_ref, gb_ref, gc_ref, w_ref, o_ref):
        o_ref[...] = (gb_ref[...] * _conv3(gc_ref[...] * x_ref[...], w_ref)).astype(BF16)

    return pl.pallas_call(
        body, name=name, grid=(nb,), out_shape=SDS((S, width), BF16),
        in_specs=[BS((S, tc), lambda j: (0, c0 + j)), BS((S, tc), lambda j: (0, c0 + nb + j)),
                  BS((S, tc), lambda j: (0, c0 + 2 * nb + j)), BS((3, tc), lambda j: (0, j))],
        out_specs=BS((S, tc), lambda j: (0, j)), compiler_params=_cp("parallel"))(proj, proj, proj, w)


def _sconv_bwd(proj, w, col0, width, do, name):
    S = proj.shape[0]
    tc = LANES
    nb, c0 = width // tc, col0 // tc

    def body(x_ref, gb_ref, gc_ref, w_ref, do_ref, dx_ref, dgb_ref, dgc_ref, dw_ref):
        xb, gb, gc, dov = x_ref[...], gb_ref[...], gc_ref[...], do_ref[...]
        z = gc * xb
        dgb_ref[...] = (dov * _conv3(z, w_ref)).astype(BF16)
        dz = _conv3_bwd(z, dov * gb, w_ref, dw_ref)
        dgc_ref[...] = (dz * xb).astype(BF16)
        dx_ref[...] = (dz * gc).astype(BF16)

    out = BS((S, tc), lambda j: (0, j))
    wspec = BS((3, tc), lambda j: (0, j))
    return pl.pallas_call(
        body, name=name, grid=(nb,),
        out_shape=(SDS((S, width), BF16), SDS((S, width), BF16), SDS((S, width), BF16), SDS((3, width), F32)),
        in_specs=[BS((S, tc), lambda j: (0, c0 + j)), BS((S, tc), lambda j: (0, c0 + nb + j)),
                  BS((S, tc), lambda j: (0, c0 + 2 * nb + j)), wspec, out],
        out_specs=(out, out, out, wspec), compiler_params=_cp("parallel"))(proj, proj, proj, w, do)


def _gelu(x):
    return 0.5 * x * (1.0 + jnp.tanh(0.7978845608028654 * (x + 0.044715 * x * x * x)))


def _gelu_grad(x):
    u = 0.7978845608028654 * (x + 0.044715 * x * x * x)
    th = jnp.tanh(u)
    return 0.5 * (1.0 + th) + 0.5 * x * (1.0 - th * th) * 0.7978845608028654 * (1.0 + 3.0 * 0.044715 * x * x)


def _tril_w(w_ref, g):
    r = lax.broadcasted_iota(jnp.int32, (SGU_CHUNK, SGU_CHUNK), 0)
    c = lax.broadcasted_iota(jnp.int32, (SGU_CHUNK, SGU_CHUNK), 1)
    return jnp.where(c <= r, w_ref[g], 0.0), c <= r


def _sgu_mixed(vn, w_ref, b_ref, nch, G):
    rows = []
    for ch in range(nch):
        cols = []
        for g in range(G):
            wt, _ = _tril_w(w_ref, g)
            blk = vn[ch * SGU_CHUNK:(ch + 1) * SGU_CHUNK, g * LANES:(g + 1) * LANES]
            cols.append(jnp.dot(wt.astype(BF16), blk.astype(BF16), preferred_element_type=F32) + b_ref[g])
        rows.append(jnp.concatenate(cols, axis=1))
    return jnp.concatenate(rows, axis=0)


def _sgu_fwd(proj, gn, w, b, col0, width, name):
    S = proj.shape[0]
    G = w.shape[0]
    ts = _tile(S, 512)
    nch = ts // SGU_CHUNK
    c0 = col0 // width

    def body(u_ref, v_ref, gn_ref, w_ref, b_ref, o_ref):
        cv = _gelu(v_ref[...])
        rstd = lax.rsqrt(jnp.mean(cv * cv, axis=-1, keepdims=True) + EPS)
        mixed = _sgu_mixed(cv * rstd * gn_ref[...], w_ref, b_ref, nch, G)
        o_ref[...] = (_gelu(u_ref[...]) * mixed).astype(BF16)

    return pl.pallas_call(
        body, name=name, grid=(S // ts,), out_shape=SDS((S, width), BF16),
        in_specs=[BS((ts, width), lambda i: (i, c0)), BS((ts, width), lambda i: (i, c0 + 1)),
                  BS((1, width), lambda i: (0, 0)), BS(w.shape, lambda i: (0, 0, 0)), BS(b.shape, lambda i: (0, 0, 0))],
        out_specs=BS((ts, width), lambda i: (i, 0)), compiler_params=_cp("parallel"))(proj, proj, gn, w, b)


def _sgu_bwd(proj, gn, w, b, col0, width, do, name):
    S = proj.shape[0]
    G = w.shape[0]
    ts = _tile(S, 512)
    nch = ts // SGU_CHUNK
    c0 = col0 // width

    def body(u_ref, v_ref, gn_ref, w_ref, b_ref, do_ref, du_ref, dv_ref, dw_ref, db_ref, dgn_ref):
        uin, vin, dov, gnv = u_ref[...], v_ref[...], do_ref[...], gn_ref[...]
        cu, cv = _gelu(uin), _gelu(vin)
        rstd = lax.rsqrt(jnp.mean(cv * cv, axis=-1, keepdims=True) + EPS)
        vhat = cv * rstd
        vn = vhat * gnv
        mixed = _sgu_mixed(vn, w_ref, b_ref, nch, G)
        du_ref[...] = (dov * mixed * _gelu_grad(uin)).astype(BF16)
        dmix = dov * cu

        @pl.when(pl.program_id(0) == 0)
        def _():
            dw_ref[...] = jnp.zeros_like(dw_ref)
            db_ref[...] = jnp.zeros_like(db_ref)
            dgn_ref[...] = jnp.zeros_like(dgn_ref)

        rows = []
        for ch in range(nch):
            cols = []
            for g in range(G):
                wt, mask = _tril_w(w_ref, g)
                sl = (slice(ch * SGU_CHUNK, (ch + 1) * SGU_CHUNK), slice(g * LANES, (g + 1) * LANES))
                dm, vb = dmix[sl], vn[sl].astype(BF16)
                db_ref[g] += jnp.sum(dm, axis=-1, keepdims=True)
                dwg = lax.dot_general(dm.astype(BF16), vb, (_DIMS["nt"], ((), ())), preferred_element_type=F32)
                dw_ref[g] += jnp.where(mask, dwg, 0.0)
                cols.append(lax.dot_general(wt.astype(BF16), dm.astype(BF16), (_DIMS["tn"], ((), ())),
                                            preferred_element_type=F32))
            rows.append(jnp.concatenate(cols, axis=1))
        dvn = jnp.concatenate(rows, axis=0)
        dgn_ref[...] += jnp.sum(dvn * vhat, axis=0, keepdims=True)
        dvh = dvn * gnv
        dcv = rstd * (dvh - vhat * jnp.mean(dvh * vhat, axis=-1, keepdims=True))
        dv_ref[...] = (dcv * _gelu_grad(vin)).astype(BF16)

    row = BS((ts, width), lambda i: (i, 0))
    wsp, bsp, gsp = BS(w.shape, lambda i: (0, 0, 0)), BS(b.shape, lambda i: (0, 0, 0)), BS((1, width), lambda i: (0, 0))
    return pl.pallas_call(
        body, name=name, grid=(S // ts,),
        out_shape=(SDS((S, width), BF16), SDS((S, width), BF16), SDS(w.shape, F32), SDS(b.shape, F32),
                   SDS((1, width), F32)),
        in_specs=[BS((ts, width), lambda i: (i, c0)), BS((ts, width), lambda i: (i, c0 + 1)), gsp, wsp, bsp, row],
        out_specs=(row, row, wsp, bsp, gsp), compiler_params=_cp("arbitrary"))(proj, proj, gn, w, b, do)


def _rope_tables(positions, inv, name):
    S = positions.shape[0]
    ts = _tile(S, 512, 8)
    half = ROPE_DIM // 2

    def body(p_ref, inv_ref, c_ref, sa_ref, sb_ref):
        ang = p_ref[...].astype(F32) * inv_ref[...]
        lane = lax.broadcasted_iota(jnp.int32, (ts, LANES), 1) % HEAD_DIM
        sn = jnp.sin(ang)
        c_ref[...] = jnp.cos(ang)
        sa_ref[...] = jnp.where(lane < half, -sn, 0.0)
        sb_ref[...] = jnp.where((lane >= half) & (lane < ROPE_DIM), sn, 0.0)

    out = BS((ts, LANES), lambda i: (i, 0))
    return pl.pallas_call(
        body, name=name, grid=(S // ts,), out_shape=(SDS((S, LANES), F32),) * 3,
        in_specs=[BS((ts, 1), lambda i: (i, 0)), BS((1, LANES), lambda i: (0, 0))], out_specs=(out, out, out),
        compiler_params=_cp("parallel"))(positions, inv)


def _rope_apply(x, col0, width, tables, sign, out_dtype, name):
    S = x.shape[0]
    ts = _tile(S, 512, 8)
    tc = 2 * LANES
    nb, c0 = width // tc, col0 // tc
    half = ROPE_DIM // 2

    def body(x_ref, c_ref, sa_ref, sb_ref, o_ref):
        xv = x_ref[...].astype(F32)
        wide = lambda r: jnp.concatenate([r[...], r[...]], axis=1)
        y = (xv * wide(c_ref) + pltpu.roll(xv, tc - half, 1) * (sign * wide(sa_ref))
             + pltpu.roll(xv, half, 1) * (sign * wide(sb_ref)))
        o_ref[...] = y.astype(o_ref.dtype)

    tab = BS((ts, LANES), lambda i, j: (i, 0))
    return pl.pallas_call(
        body, name=name, grid=(S // ts, nb), out_shape=SDS((S, width), out_dtype),
        in_specs=[BS((ts, tc), lambda i, j: (i, c0 + j)), tab, tab, tab], out_specs=BS((ts, tc), lambda i, j: (i, j)),
        compiler_params=_cp("parallel", "parallel"))(x, *tables)


def _dil_masks(hh, m, bpcs, hpg):
    g = hh // hpg
    bpc = jnp.where(g == 0, bpcs[0], jnp.where(g == 1, bpcs[1], bpcs[2]))
    r = lax.broadcasted_iota(jnp.int32, (DIL_SPAN, DIL_SPAN), 0)
    c = lax.broadcasted_iota(jnp.int32, (DIL_SPAN, DIL_SPAN), 1)
    return (c >= r) & (lax.rem(m, bpc) != 0), c <= r


def _nt(a, b):
    return lax.dot_general(a, b, (_DIMS["nt"], ((), ())), preferred_element_type=F32)


def _tn(a, b):
    return lax.dot_general(a, b, (_DIMS["tn"], ((), ())), preferred_element_type=F32)


def _dil_fwd(q, k, v, bpcs, hpg, name, comm=None):
    HH, S, Dh = q.shape
    nb = S // DIL_SPAN
    scale = Dh ** -0.5

    def body(q_ref, kp_ref, kc_ref, vp_ref, vc_ref, o_ref, lse_ref):
        mp, mc = _dil_masks(pl.program_id(0) * hpg, pl.program_id(1), bpcs, hpg)
        for h in range(hpg):
            qv = q_ref[h]
            sp = jnp.where(mp, _nt(qv, kp_ref[h]) * scale, NEG)
            sc = jnp.where(mc, _nt(qv, kc_ref[h]) * scale, NEG)
            mx = jnp.maximum(jnp.max(sp, axis=-1, keepdims=True), jnp.max(sc, axis=-1, keepdims=True))
            pp, pc = jnp.exp(sp - mx), jnp.exp(sc - mx)
            l = jnp.sum(pp, axis=-1, keepdims=True) + jnp.sum(pc, axis=-1, keepdims=True)
            acc = (jnp.dot(pp.astype(BF16), vp_ref[h], preferred_element_type=F32)
                   + jnp.dot(pc.astype(BF16), vc_ref[h], preferred_element_type=F32))
            o_ref[h] = acc / l
            lse_ref[h] = mx + jnp.log(l)

    cur = BS((hpg, DIL_SPAN, Dh), lambda g, m: (g, m, 0))
    prev = BS((hpg, DIL_SPAN, Dh), lambda g, m: (g, jnp.maximum(m - 1, 0), 0))
    return _call(
        body, name=name, grid=(HH // hpg, nb), out_shape=(SDS((HH, S, Dh), F32), SDS((HH, S, 1), F32)),
        in_specs=[cur, prev, cur, prev, cur], out_specs=(cur, BS((hpg, DIL_SPAN, 1), lambda g, m: (g, m, 0))),
        sem=("parallel", "parallel"), args=(q, k, k, v, v), comm=comm)


def _dil_bwd_dq(q, k, v, do, lse, coef, bpcs, hpg, name, comm=None):
    HH, S, Dh = q.shape
    nb = S // DIL_SPAN
    scale = Dh ** -0.5

    def body(q_ref, kp_ref, kc_ref, vp_ref, vc_ref, do_ref, lse_ref, cf_ref, dq_ref):
        mp, mc = _dil_masks(pl.program_id(0) * hpg, pl.program_id(1), bpcs, hpg)
        for h in range(hpg):
            qv, dob, lsev, cf = q_ref[h], do_ref[h], lse_ref[h], cf_ref[h]
            pp = jnp.exp(jnp.where(mp, _nt(qv, kp_ref[h]) * scale, NEG) - lsev)
            pc = jnp.exp(jnp.where(mc, _nt(qv, kc_ref[h]) * scale, NEG) - lsev)
            dsp = pp * (_nt(dob, vp_ref[h]) + cf)
            dsc = pc * (_nt(dob, vc_ref[h]) + cf)
            dq_ref[h] = (jnp.dot(dsp.astype(BF16), kp_ref[h], preferred_element_type=F32)
                         + jnp.dot(dsc.astype(BF16), kc_ref[h], preferred_element_type=F32)) * scale

    cur = BS((hpg, DIL_SPAN, Dh), lambda g, m: (g, m, 0))
    prev = BS((hpg, DIL_SPAN, Dh), lambda g, m: (g, jnp.maximum(m - 1, 0), 0))
    col = BS((hpg, DIL_SPAN, 1), lambda g, m: (g, m, 0))
    return _call(
        body, name=name, grid=(HH // hpg, nb), out_shape=SDS((HH, S, Dh), F32),
        in_specs=[cur, prev, cur, prev, cur, cur, col, col], out_specs=cur,
        sem=("parallel", "parallel"), args=(q, k, k, v, v, do, lse, coef), comm=comm)


def _dil_bwd_dkv(q, k, v, do, lse, coef, bpcs, hpg, name, comm=None):
    HH, S, Dh = q.shape
    nb = S // DIL_SPAN
    scale = Dh ** -0.5

    def body(k_ref, v_ref, qc_ref, qn_ref, doc_ref, don_ref, lc_ref, ln_ref, cc_ref, cn_ref, dk_ref, dv_ref):
        hh, m = pl.program_id(0) * hpg, pl.program_id(1)
        _, mc = _dil_masks(hh, m, bpcs, hpg)
        mp, _ = _dil_masks(hh, m + 1, bpcs, hpg)
        mp = mp & (m + 1 < nb)
        for h in range(hpg):
            kv, vv = k_ref[h], v_ref[h]
            pc = jnp.exp(jnp.where(mc, _nt(qc_ref[h], kv) * scale, NEG) - lc_ref[h])
            pn = jnp.exp(jnp.where(mp, _nt(qn_ref[h], kv) * scale, NEG) - ln_ref[h])
            dsc = pc * (_nt(doc_ref[h], vv) + cc_ref[h])
            dsn = pn * (_nt(don_ref[h], vv) + cn_ref[h])
            dv_ref[h] = _tn(pc.astype(BF16), doc_ref[h]) + _tn(pn.astype(BF16), don_ref[h])
            dk_ref[h] = (_tn(dsc.astype(BF16), qc_ref[h]) + _tn(dsn.astype(BF16), qn_ref[h])) * scale

    cur = BS((hpg, DIL_SPAN, Dh), lambda g, m: (g, m, 0))
    nxt = BS((hpg, DIL_SPAN, Dh), lambda g, m: (g, jnp.minimum(m + 1, nb - 1), 0))
    col = BS((hpg, DIL_SPAN, 1), lambda g, m: (g, m, 0))
    ncol = BS((hpg, DIL_SPAN, 1), lambda g, m: (g, jnp.minimum(m + 1, nb - 1), 0))
    return _call(
        body, name=name, grid=(HH // hpg, nb), out_shape=(SDS((HH, S, Dh), F32), SDS((HH, S, Dh), F32)),
        in_specs=[cur, cur, cur, nxt, cur, nxt, col, ncol, col, ncol], out_specs=(cur, cur),
        sem=("parallel", "parallel"), args=(k, v, q, q, do, do, lse, lse, coef, coef), comm=comm)


def _dil_merge(o3, lse3, name):
    G, H, S, Dh = o3.shape
    ts = _tile(S, 512, 8)

    def body(o_ref, l_ref, out_ref):
        lv = [l_ref[g, 0] for g in range(G)]
        mx = functools.reduce(jnp.maximum, lv)
        ex = [jnp.exp(v - mx) for v in lv]
        den = functools.reduce(lambda a, b: a + b, ex)
        out_ref[0] = functools.reduce(lambda a, b: a + b, [(ex[g] / den) * o_ref[g, 0] for g in range(G)])

    return pl.pallas_call(
        body, name=name, grid=(H, S // ts), out_shape=SDS((H, S, Dh), F32),
        in_specs=[BS((G, 1, ts, Dh), lambda h, i: (0, h, i, 0)), BS((G, 1, ts, 1), lambda h, i: (0, h, i, 0))],
        out_specs=BS((1, ts, Dh), lambda h, i: (h, i, 0)), compiler_params=_cp("parallel", "parallel"))(o3, lse3)


def _dil_merge_bwd(o3, lse3, do, name):
    G, H, S, Dh = o3.shape
    ts = _tile(S, 512, 8)

    def body(o_ref, l_ref, do_ref, do3_ref, cf_ref):
        lv = [l_ref[g, 0] for g in range(G)]
        mx = functools.reduce(jnp.maximum, lv)
        ex = [jnp.exp(v - mx) for v in lv]
        den = functools.reduce(lambda a, b: a + b, ex)
        wt = [e / den for e in ex]
        dov = do_ref[0]
        dw = [jnp.sum(dov * o_ref[g, 0], axis=-1, keepdims=True) for g in range(G)]
        mean = functools.reduce(lambda a, b: a + b, [wt[g] * dw[g] for g in range(G)])
        for g in range(G):
            do3_ref[g, 0] = (wt[g] * dov).astype(BF16)
            cf_ref[g, 0] = wt[g] * (dw[g] - mean) - wt[g] * dw[g]

    o_spec = BS((G, 1, ts, Dh), lambda h, i: (0, h, i, 0))
    l_spec = BS((G, 1, ts, 1), lambda h, i: (0, h, i, 0))
    return pl.pallas_call(
        body, name=name, grid=(H, S // ts), out_shape=(SDS((G, H, S, Dh), BF16), SDS((G, H, S, 1), F32)),
        in_specs=[o_spec, l_spec, BS((1, ts, Dh), lambda h, i: (h, i, 0))], out_specs=(o_spec, l_spec),
        compiler_params=_cp("parallel", "parallel"))(o3, lse3, do)


def _merge_fwd(proj, gate_col, os_, ws, name):
    S = proj.shape[0]
    D = ws[0].shape[1]
    nbr = len(os_)
    ts = _tile(S, 512, 8)
    tn = LANES * 2 if gate_col % (2 * LANES) == 0 and D % (2 * LANES) == 0 else LANES
    g0, gstep = gate_col // tn, D // tn

    def body(*refs):
        p_refs, o_refs, w_refs, out_ref = refs[:nbr], refs[nbr:2 * nbr], refs[2 * nbr:3 * nbr], refs[-1]
        acc = None
        for i in range(nbr):
            t = _sigmoid(p_refs[i][...]) * jnp.dot(o_refs[i][...], w_refs[i][...], preferred_element_type=F32)
            acc = t if acc is None else acc + t
        out_ref[...] = acc.astype(BF16)

    specs = ([BS((ts, tn), lambda s, j, i=i: (s, g0 + i * gstep + j)) for i in range(nbr)]
             + [BS((ts, o.shape[1]), lambda s, j: (s, 0)) for o in os_]
             + [BS((w.shape[0], tn), lambda s, j: (0, j)) for w in ws])
    return pl.pallas_call(
        body, name=name, grid=(S // ts, D // tn), out_shape=SDS((S, D), BF16), in_specs=specs,
        out_specs=BS((ts, tn), lambda s, j: (s, j)), compiler_params=_cp("parallel", "parallel"),
    )(*([proj] * nbr), *os_, *ws)


def _merge_bwd(proj, gate_col, os_, ws, dm, name):
    S = proj.shape[0]
    D = ws[0].shape[1]
    nbr = len(os_)
    ts = _tile(S, 512, 8)
    tn = LANES * 2 if gate_col % (2 * LANES) == 0 and D % (2 * LANES) == 0 else LANES
    g0, gstep = gate_col // tn, D // tn

    def body(*refs):
        p_refs, o_refs, w_refs = refs[:nbr], refs[nbr:2 * nbr], refs[2 * nbr:3 * nbr]
        dmv = refs[3 * nbr][...]
        dbo_refs, dgl_refs = refs[3 * nbr + 1:4 * nbr + 1], refs[4 * nbr + 1:]
        for i in range(nbr):
            gt = _sigmoid(p_refs[i][...])
            bo = jnp.dot(o_refs[i][...], w_refs[i][...], preferred_element_type=F32)
            dbo_refs[i][...] = (dmv * gt).astype(BF16)
            dgl_refs[i][...] = (dmv * bo * gt * (1.0 - gt)).astype(BF16)

    tile = BS((ts, tn), lambda s, j: (s, j))
    specs = ([BS((ts, tn), lambda s, j, i=i: (s, g0 + i * gstep + j)) for i in range(nbr)]
             + [BS((ts, o.shape[1]), lambda s, j: (s, 0)) for o in os_]
             + [BS((w.shape[0], tn), lambda s, j: (0, j)) for w in ws] + [tile])
    outs = pl.pallas_call(
        body, name=name, grid=(S // ts, D // tn), out_shape=(SDS((S, D), BF16),) * (2 * nbr), in_specs=specs,
        out_specs=(tile,) * (2 * nbr), compiler_params=_cp("parallel", "parallel"),
    )(*([proj] * nbr), *os_, *ws, dm)
    return outs[:nbr], outs[nbr:]


def _ffn_act(up, w, name):
    S, F2 = up.shape
    F = F2 // 2
    tc = LANES
    nb = F // tc

    def body(g_ref, v_ref, wg_ref, wv_ref, a_ref):
        ug = _conv3(g_ref[...], wg_ref)
        a_ref[...] = (ug * _sigmoid(ug) * _conv3(v_ref[...], wv_ref)).astype(BF16)

    return pl.pallas_call(
        body, name=name, grid=(nb,), out_shape=SDS((S, F), BF16),
        in_specs=[BS((S, tc), lambda j: (0, j)), BS((S, tc), lambda j: (0, nb + j)),
                  BS((3, tc), lambda j: (0, j)), BS((3, tc), lambda j: (0, nb + j))],
        out_specs=BS((S, tc), lambda j: (0, j)), compiler_params=_cp("parallel"))(up, up, w, w)


def _ffn_act_bwd(up, w, da, name):
    S, F2 = up.shape
    F = F2 // 2
    tc = LANES
    nb = F // tc

    def body(g_ref, v_ref, wg_ref, wv_ref, da_ref, dg_ref, dv_ref, dwg_ref, dwv_ref):
        pg, pv, dav = g_ref[...], v_ref[...], da_ref[...].astype(F32)
        ug, uv = _conv3(pg, wg_ref), _conv3(pv, wv_ref)
        sg = _sigmoid(ug)
        dg_ref[...] = _conv3_bwd(pg, dav * uv * (sg * (1.0 + ug * (1.0 - sg))), wg_ref, dwg_ref).astype(BF16)
        dv_ref[...] = _conv3_bwd(pv, dav * ug * sg, wv_ref, dwv_ref).astype(BF16)

    col = BS((S, tc), lambda j: (0, j))
    wsp = BS((3, tc), lambda j: (0, j))
    return pl.pallas_call(
        body, name=name, grid=(nb,),
        out_shape=(SDS((S, F), BF16), SDS((S, F), BF16), SDS((3, F), F32), SDS((3, F), F32)),
        in_specs=[col, BS((S, tc), lambda j: (0, nb + j)), wsp, BS((3, tc), lambda j: (0, nb + j)), col],
        out_specs=(col, col, wsp, wsp), compiler_params=_cp("parallel"))(up, up, w, w, da)


def _ple_fwd(x, pgl, pe, name):
    S, D = x.shape
    ts = _tile(S, 256, 8)

    def body(x_ref, g_ref, e_ref, o_ref):
        o_ref[...] = x_ref[...] + _sigmoid(g_ref[...]) * e_ref[...]

    row = BS((ts, D), lambda i: (i, 0))
    return pl.pallas_call(body, name=name, grid=(S // ts,), out_shape=SDS((S, D), F32), in_specs=[row] * 3,
                          out_specs=row, compiler_params=_cp("parallel"))(x, pgl, pe)


def _ple_bwd(dx, pgl, pe, name):
    S, D = dx.shape
    ts = _tile(S, 256, 8)

    def body(dx_ref, g_ref, e_ref, dg_ref, de_ref):
        dxv, sg = dx_ref[...], _sigmoid(g_ref[...])
        dg_ref[...] = (dxv * e_ref[...] * sg * (1.0 - sg)).astype(BF16)
        de_ref[...] = (dxv * sg).astype(BF16)

    row = BS((ts, D), lambda i: (i, 0))
    return pl.pallas_call(body, name=name, grid=(S // ts,), out_shape=(SDS((S, D), BF16),) * 2, in_specs=[row] * 3,
                          out_specs=(row, row), compiler_params=_cp("parallel"))(dx, pgl, pe)


def _adamw(w, g, m, v, name):
    shape = w.shape
    cols = shape[-1]
    rows = w.size // cols
    tr = _tile(rows, 256, 8)
    c1, c2 = 1.0 / (1.0 - ADAM_B1 ** ADAM_STEP), 1.0 / (1.0 - ADAM_B2 ** ADAM_STEP)

    def body(w_ref, g_ref, m_ref, v_ref, d_ref, nm_ref, nv_ref):
        gv = g_ref[...]
        nm = ADAM_B1 * m_ref[...] + (1.0 - ADAM_B1) * gv
        nv = ADAM_B2 * v_ref[...] + (1.0 - ADAM_B2) * (gv * gv)
        d_ref[...] = -ADAM_LR * ((nm * c1) / (jnp.sqrt(nv * c2) + ADAM_EPS) + ADAM_WD * w_ref[...])
        nm_ref[...] = nm
        nv_ref[...] = nv

    blk = BS((tr, cols), lambda i: (i, 0))
    outs = pl.pallas_call(
        body, name=name, grid=(rows // tr,), out_shape=(SDS((rows, cols), F32),) * 3, in_specs=[blk] * 4,
        out_specs=(blk,) * 3, compiler_params=_cp("parallel"),
    )(*(a.reshape(rows, cols) for a in (w, g, m, v)))
    return tuple(o.reshape(shape) for o in outs)


SHARDED = ("w_in", "shortconv_w", "w_br_fox", "w_br_conv", "w_br_sgu", "w_br_dil", "w_out", "w_up", "ffn_conv_w",
           "w_down", "w_ple_gate", "w_ple_proj")
REPLICATED = ("norm_mix_g", "fox_forget_b", "sgu_norm_g", "sgu_w", "sgu_b", "norm_ffn_g", "norm_ple_g")
WEIGHTS = ("norm_mix_g", "w_in", "fox_forget_b", "shortconv_w", "sgu_norm_g", "sgu_w", "sgu_b", "w_br_fox",
           "w_br_conv", "w_br_sgu", "w_br_dil", "w_out", "norm_ffn_g", "w_up", "ffn_conv_w", "w_down", "norm_ple_g",
           "w_ple_gate", "w_ple_proj", "final_norm_g")
PACK_COLS = 1024
PACK_ROWS = 256


def _pack(parts, dtype, unit):
    lead = parts[0].shape[:-1] if parts[0].ndim > 1 else ()
    flat = jnp.concatenate([p.astype(dtype) for p in parts], axis=-1)
    n = flat.shape[-1]
    rows = -(-n // (PACK_COLS * unit)) * unit
    flat = jnp.pad(flat, [(0, 0)] * len(lead) + [(0, rows * PACK_COLS - n)])
    return flat.reshape(lead + (rows, PACK_COLS))


def _to_heads(x, heads):
    S = x.shape[0]
    return x.reshape(S, heads, HEAD_DIM).transpose(1, 0, 2)


def _from_heads(x):
    H, S, Dh = x.shape
    return x.transpose(1, 0, 2).reshape(S, H * Dh)


def _to_residue(x, hpg):
    S = x.shape[0]
    gw = hpg * HEAD_DIM
    outs = []
    for g, (_, d) in enumerate(DIL_PATTERNS):
        seg = x[:, g * gw:(g + 1) * gw].reshape(S // d, d, hpg, HEAD_DIM)
        outs.append(seg.transpose(2, 1, 0, 3).reshape(hpg, S, HEAD_DIM))
    return jnp.concatenate(outs, axis=0)


def _from_residue(x, hpg):
    _, S, C = x.shape
    outs = []
    for g, (_, d) in enumerate(DIL_PATTERNS):
        seg = x[g * hpg:(g + 1) * hpg].reshape(hpg, d, S // d, C)
        outs.append(seg.transpose(0, 2, 1, 3).reshape(hpg, S, C))
    return jnp.stack(outs, axis=0)


def _groups_to_residue(x):
    G, hpg, S, C = x.shape
    outs = []
    for g, (_, d) in enumerate(DIL_PATTERNS):
        outs.append(x[g].reshape(hpg, S // d, d, C).transpose(0, 2, 1, 3).reshape(hpg, S, C))
    return jnp.concatenate(outs, axis=0)


def kernel(x, p, positions, norm_mix_g, w_in, fox_forget_b, shortconv_w, sgu_norm_g, sgu_w, sgu_b, w_br_fox, w_br_conv, w_br_sgu, w_br_dil, w_out, norm_ffn_g, w_up, ffn_conv_w, w_down, norm_ple_g, w_ple_gate, w_ple_proj, final_norm_g, loss_target, m_norm_mix_g, m_w_in, m_fox_forget_b, m_shortconv_w, m_sgu_norm_g, m_sgu_w, m_sgu_b, m_w_br_fox, m_w_br_conv, m_w_br_sgu, m_w_br_dil, m_w_out, m_norm_ffn_g, m_w_up, m_ffn_conv_w, m_w_down, m_norm_ple_g, m_w_ple_gate, m_w_ple_proj, m_final_norm_g, v_norm_mix_g, v_w_in, v_fox_forget_b, v_shortconv_w, v_sgu_norm_g, v_sgu_w, v_sgu_b, v_w_br_fox, v_w_br_conv, v_w_br_sgu, v_w_br_dil, v_w_out, v_norm_ffn_g, v_w_up, v_ffn_conv_w, v_w_down, v_norm_ple_g, v_w_ple_gate, v_w_ple_proj, v_final_norm_g):
    given = dict(locals())
    W = {n: given[n] for n in WEIGHTS}
    depth = w_in.shape[0]
    S, D = x.shape[1], x.shape[2]
    x0, target, pos = x[0], loss_target[0], positions[0].reshape(S, 1)
    FH = fox_forget_b.shape[1]
    FW = FH * HEAD_DIM
    CW = shortconv_w.shape[2] * N_DEV
    G = sgu_w.shape[1]
    SW = G * sgu_w.shape[2]
    DOUT = w_br_dil.shape[1]
    hpg = DOUT // HEAD_DIM
    NG = len(DIL_PATTERNS)
    DW = NG * DOUT
    bpcs = tuple(S // d // DIL_SPAN for _, d in DIL_PATTERNS)
    A0, B0 = 0, 3 * FW
    C0 = B0 + 3 * CW
    D0 = C0 + 2 * SW
    G0 = D0 + 3 * DW
    F0 = G0 + 4 * D
    NP = F0 + LANES
    orig = [3 * FW, FH, 3 * CW, 2 * SW, 3 * DW, 4 * D]
    o_off = [sum(orig[:i]) for i in range(len(orig) + 1)]

    me = 4 * lax.axis_index("x") + 2 * lax.axis_index("y") + lax.axis_index("c")
    n_in, o1, o2, d_in = w_in.shape[2], o_off[1], o_off[2], o_off[6]
    NPB = NP // LANES
    WB = -(-(LANES - 1 + n_in) // LANES)
    PW = (WB + 1) * LANES

    def a_of(j, clip=lambda v: min(max(v, 0), FH)):
        return n_in * j - clip(n_in * j - o1)

    def s_of(j):
        return min(a_of(j) // LANES, NPB - WB)

    def s_traced(j):
        return jnp.minimum(a_of(j, lambda v: jnp.clip(v, 0, FH)) // LANES, NPB - WB)

    def runs(j):
        lo, hi, out = n_in * j, n_in * (j + 1), []
        for a, b in ((0, o1), (o1, o2), (o2, d_in)):
            l, h = max(lo, a), min(hi, b)
            if l < h:
                dst = WB * LANES + l - o1 if a == o1 else l - (FH if l >= o2 else 0) - s_of(j) * LANES
                out.append((l - lo, h - l, dst))
        return out

    def to_window(j, shard):
        piece = jnp.zeros((shard.shape[0], PW), shard.dtype)
        for src, ln, dst in runs(j):
            piece = piece.at[:, dst:dst + ln].set(shard[:, src:src + ln])
        return piece

    def from_window(j, win):
        return jnp.concatenate([win[:, dst:dst + ln] for _, ln, dst in runs(j)], axis=1)

    def on_my_index(fn, arg):
        return lax.switch(me, [functools.partial(fn, j) for j in range(N_DEV)], arg)

    def place(axis, length):
        def view(ref, j):
            align = LANES if axis == len(ref.shape) - 1 else 16
            start = pl.multiple_of(j * length, align) if length % align == 0 else j * length
            return ref.at[tuple([slice(None)] * axis + [pl.ds(start, length)])]
        return view

    def slot(ref, j):
        return ref.at[j]

    def whole(ref, j):
        return ref

    n_up, r_down, r_sq = w_up.shape[2], w_down.shape[1], w_out.shape[1]
    n_br, n_sm, n_sc = w_br_fox.shape[2], w_br_dil.shape[2], shortconv_w.shape[2]
    FF2 = n_up * N_DEV
    cw_pad = -(-n_up // LANES) * LANES
    placed = [place(1, n_up), place(0, r_down), place(1, r_sq), place(2, n_br), place(2, n_sm)]

    def gather_plans(i):
        conv = jnp.zeros((8, cw_pad), F32).at[0:3, :n_up].set(ffn_conv_w[i]).at[3:6, :n_sc].set(shortconv_w[i])
        srcs = [on_my_index(to_window, w_in[i].astype(BF16)), w_up[i].astype(BF16), w_down[i].astype(BF16),
                jnp.stack([w_out[i], w_ple_gate[i]]).astype(BF16),
                jnp.stack([w_br_fox[i], w_br_conv[i], w_br_sgu[i]]).astype(BF16),
                jnp.stack([w_br_dil[i], w_ple_proj[i]]).astype(BF16), conv]
        shapes = [SDS((N_DEV, D, PW), BF16), SDS((D, FF2), BF16), SDS((r_down * N_DEV, D), BF16),
                  SDS((2, r_sq * N_DEV, D), BF16), SDS((3, w_br_fox.shape[1], n_br * N_DEV), BF16),
                  SDS((2, w_br_dil.shape[1], n_sm * N_DEV), BF16), SDS((N_DEV, 8, cw_pad), F32)]
        dsts = [slot] + placed + [slot]
        return [(srcs[a:b], shapes[a:b], [whole] * (b - a), dsts[a:b], True)
                for a, b in ((0, 1), (1, 2), (2, 3), (3, 7))]

    def gathered(parts):
        (pieces,), (wup,), (wdown,), (sq, br, sm, cv) = parts
        return {
            "w_in": _assemble(pieces, [s_of(j) for j in range(N_DEV)], WB, NPB, "assemble_w_in"),
            "w_up": wup, "w_down": wdown, "w_out": sq[0], "w_ple_gate": sq[1],
            "w_br_fox": br[0], "w_br_conv": br[1], "w_br_sgu": br[2], "w_br_dil": sm[0], "w_ple_proj": sm[1],
            "ffn_conv_w": jnp.moveaxis(cv[:, 0:3, :n_up], 0, 1).reshape(3, FF2),
            "shortconv_w": jnp.moveaxis(cv[:, 3:6, :n_sc], 0, 1).reshape(3, CW),
        }

    def one_plan(plans):
        return tuple(sum((list(plan[k]) for plan in plans), []) for k in range(4)) + (plans[0][4],)

    first = _comm("comm_gather_weights", *one_plan(gather_plans(0)))
    full = [gathered([first[0:1], first[1:2], first[2:3], first[3:7]])]

    inv = ROPE_THETA ** (-jnp.arange(ROPE_DIM // 2, dtype=F32) * (2.0 / ROPE_DIM))
    inv_row = jnp.tile(jnp.concatenate([inv, inv, jnp.zeros((HEAD_DIM - ROPE_DIM,), F32)]), LANES // HEAD_DIM)
    tables = _rope_tables(pos, inv_row.reshape(1, LANES), "rope_tables")

    def pad_lanes(v):
        return jnp.pad(v, ((0, 0), (0, LANES - v.shape[1])))

    saved = []
    xc = x0
    for i in range(depth):
        fw, sv = full[i], {}
        sv["x"] = xc
        nxt = gather_plans(i + 1) if i + 1 < depth else [None] * 4
        got = [None] * 4

        def host(k, result, nxt=nxt, got=got):
            if nxt[k] is None:
                return result
            got[k] = result[1]
            return result[0]

        h1 = _rms_fwd(xc, norm_mix_g[i], "rms_mix")
        proj = host(0, _mm(h1, fw["w_in"], "nn", F32, "mm_in", comm=nxt[0]))
        sv["h1"], sv["proj"] = h1, proj
        bias = pad_lanes(fox_forget_b[i].reshape(1, FH))
        Fp = _fox_prep(proj, bias, F0 // LANES, "fox_prep")[:, :, :FH].transpose(2, 1, 0)
        t = _tile(S, 256)
        qa, ka, va = (_to_heads(proj[:, A0 + j * FW:A0 + (j + 1) * FW], FH).astype(BF16) for j in range(3))
        ones3, zpad = jnp.ones((FH, S, 3), BF16), jnp.zeros((FH, S, HEAD_DIM - 6), BF16)
        q_aug = jnp.concatenate([qa, Fp, ones3, zpad], axis=-1)
        k_aug = jnp.concatenate([ka, ones3, -Fp, zpad], axis=-1)

        def blocks_t(a):
            return a.reshape(FH, S // t, t, a.shape[-1]).transpose(0, 1, 3, 2)

        kT = blocks_t(k_aug)
        oaT, lse_a = host(1, _foxt_fwd(q_aug, k_aug, blocks_t(va), "fox_fwd", comm=nxt[1]))
        oa_b = oaT.transpose(2, 0, 1).reshape(S, FW).astype(BF16)
        sv.update(bias=bias, q_aug=q_aug, k_aug=k_aug, kT=kT, va=va, lse_a=lse_a, oa_b=oa_b)
        ob = _sconv_fwd(proj, fw["shortconv_w"], B0, CW, "sconv_fwd")
        sgb = sgu_b[i].reshape(G, SGU_CHUNK, 1)
        oc = _sgu_fwd(proj, sgu_norm_g[i].reshape(1, SW), sgu_w[i], sgb, C0, SW, "sgu_fwd")
        qd = _rope_apply(proj, D0, DW, tables, 1.0, BF16, "rope_q")
        kd = _rope_apply(proj, D0 + DW, DW, tables, 1.0, BF16, "rope_k")
        qr, kr = _to_residue(qd, hpg), _to_residue(kd, hpg)
        vr = _to_residue(proj[:, D0 + 2 * DW:D0 + 3 * DW].astype(BF16), hpg)
        od_r, lse_r = host(2, _dil_fwd(qr, kr, vr, bpcs, hpg, "dil_fwd", comm=nxt[2]))
        o3, lse3 = _from_residue(od_r, hpg), _from_residue(lse_r, hpg)
        od = _from_heads(_dil_merge(o3, lse3, "dil_merge")).astype(BF16)
        sv.update(ob=ob, oc=oc, sgb=sgb, qr=qr, kr=kr, vr=vr, lse_r=lse_r, o3=o3, lse3=lse3, od=od)
        brs = [oa_b, ob, oc, od]
        wbr = [fw["w_br_fox"], fw["w_br_conv"], fw["w_br_sgu"], fw["w_br_dil"]]
        merged = _merge_fwd(proj, G0, brs, wbr, "merge_fwd")
        x2 = _mm(merged, fw["w_out"], "nn", F32, "mm_out", res=xc)
        h2 = _rms_fwd(x2, norm_ffn_g[i], "rms_ffn")
        up = host(3, _mm(h2, fw["w_up"], "nn", F32, "mm_up", comm=nxt[3]))
        if i + 1 < depth:
            full.append(gathered(got))
        act = _ffn_act(up, fw["ffn_conv_w"], "ffn_act")
        x3 = _mm(act, fw["w_down"], "nn", F32, "mm_down", res=x2)
        h3 = _rms_fwd(x3, norm_ple_g[i], "rms_ple")
        pgl = _mm(h3, fw["w_ple_gate"], "nn", F32, "mm_ple_gate")
        pb = p[i, 0].astype(BF16)
        pe = _mm(pb, fw["w_ple_proj"], "nn", F32, "mm_ple_proj")
        xc = _ple_fwd(x3, pgl, pe, "ple_fwd")
        sv.update(merged=merged, x2=x2, h2=h2, up=up, act=act, x3=x3, h3=h3, pgl=pgl, pb=pb, pe=pe)
        saved.append(sv)

    loss_row, dx, dxb, dg_final = _final_loss(xc, final_norm_g, target, "final_loss")
    loss = lax.psum(loss_row[0, 0], MESH_AXES)

    def in_window(half):
        def view(ref, j):
            return ref.at[pl.ds(half * (D // 2), D // 2),
                          pl.ds(pl.multiple_of(s_traced(j) * LANES, LANES), WB * LANES)]
        return view

    def in_flogit(ref, j):
        return ref.at[:, pl.ds(F0, LANES)]

    def scatter_plans(gs, parts=(0, 1, 2, 3, 4, 5)):
        stacked = lambda shape, dt=BF16: SDS((N_DEV,) + tuple(shape), dt)

        def build(part):
            if part == 0:
                srcs, views = [gs["w_in"], gs["w_in"]], [in_window(0), in_flogit]
                shapes = [stacked((D // 2, WB * LANES)), stacked((D, LANES))]
            elif part == 1:
                srcs, views, shapes = [gs["w_up"]], [placed[0]], [stacked((D, n_up))]
            elif part == 2:
                srcs, views, shapes = [gs["w_down"]], [placed[1]], [stacked((r_down, D))]
            elif part == 3:
                srcs, views = [jnp.stack([gs["w_out"], gs["w_ple_gate"]])], [placed[2]]
                shapes = [stacked((2, r_sq, D))]
            elif part == 5:
                conv = jnp.zeros((N_DEV, 8, cw_pad), F32)
                conv = conv.at[:, 0:3, :n_up].set(jnp.moveaxis(gs["ffn_conv_w"].reshape(3, N_DEV, n_up), 1, 0))
                conv = conv.at[:, 3:6, :n_sc].set(jnp.moveaxis(gs["shortconv_w"].reshape(3, N_DEV, n_sc), 1, 0))
                srcs = [jnp.stack([gs["w_br_fox"], gs["w_br_conv"], gs["w_br_sgu"]]),
                        jnp.stack([gs["w_br_dil"], gs["w_ple_proj"]]), conv]
                views = placed[3:] + [slot]
                shapes = [stacked((3, w_br_fox.shape[1], n_br)), stacked((2, w_br_dil.shape[1], n_sm)),
                          stacked((8, cw_pad), F32)]
            else:
                srcs, views, shapes = [gs["w_in"]], [in_window(1)], [stacked((D // 2, WB * LANES))]
            return srcs, shapes, views, [slot] * len(srcs), False

        return [build(part) for part in parts]

    grads_sh = [None] * depth
    grads_rep = [None] * depth
    scattered = [None] * depth
    for i in reversed(range(depth)):
        fw, sv = full[i], saved[i]
        gs, gr = {}, {}
        pend = scatter_plans(grads_sh[i + 1]) if i + 1 < depth else [None] * 6
        got = [None] * 6

        def host(k, result, pend=pend, got=got):
            if pend[k] is None:
                return result
            got[k] = result[1]
            return result[0]

        proj = sv["proj"]
        dpgl, dpe = _ple_bwd(dx, sv["pgl"], sv["pe"], "ple_bwd")
        gs["w_ple_proj"] = _mm(sv["pb"], dpe, "tn", BF16, "mm_dw_ple_proj")
        gs["w_ple_gate"] = _mm(sv["h3"], dpgl, "tn", BF16, "mm_dw_ple_gate")
        dh3 = _mm(dpgl, fw["w_ple_gate"], "nt", F32, "mm_dh3")
        dx, dxb, gr["norm_ple_g"] = _rms_bwd(sv["x3"], norm_ple_g[i], dh3, dx, "rms_ple_bwd")
        gs["w_down"] = _mm(sv["act"], dxb, "tn", BF16, "mm_dw_down")
        dact = _mm(dxb, fw["w_down"], "nt", BF16, "mm_dact")
        dug, duv, dwg, dwv = _ffn_act_bwd(sv["up"], fw["ffn_conv_w"], dact, "ffn_act_bwd")
        gs["ffn_conv_w"] = jnp.concatenate([dwg, dwv], axis=1)
        dup = jnp.concatenate([dug, duv], axis=1)
        gs["w_up"] = host(2, _mm(sv["h2"], dup, "tn", BF16, "mm_dw_up", comm=pend[2]))
        own_down, own_up = scatter_plans(gs, (2, 1)) if i == 0 else (None, None)
        dh2 = _mm(dup, fw["w_up"], "nt", F32, "mm_dh2", comm=own_down)
        if i == 0:
            dh2, own_down = dh2
        dx, dxb, gr["norm_ffn_g"] = _rms_bwd(sv["x2"], norm_ffn_g[i], dh2, dx, "rms_ffn_bwd")
        gs["w_out"] = _mm(sv["merged"], dxb, "tn", BF16, "mm_dw_out")
        dmerged = _mm(dxb, fw["w_out"], "nt", F32, "mm_dmerged")
        brs = [sv["oa_b"], sv["ob"], sv["oc"], sv["od"]]
        names = ["w_br_fox", "w_br_conv", "w_br_sgu", "w_br_dil"]
        wbr = [fw[n] for n in names]
        dbo, dgl = _merge_bwd(proj, G0, brs, wbr, dmerged, "merge_bwd")
        dos = []
        for j, n in enumerate(names):
            gs[n] = _mm(brs[j], dbo[j], "tn", BF16, "mm_dw_" + n)
            dos.append(_mm(dbo[j], wbr[j], "nt", F32, "mm_do_" + n))
        doa = _to_heads(dos[0], FH).astype(BF16)
        dqT, delta = host(1, _foxt_bwd_dq(sv["q_aug"], sv["k_aug"], sv["kT"], sv["va"], doa, sv["lse_a"],
                                          "fox_bwd_dq", comm=pend[1]))
        dk_aug, dva, dfk = host(0, _foxt_bwd_dkv(sv["q_aug"], sv["k_aug"], sv["va"], doa, sv["lse_a"], delta,
                                                 "fox_bwd_dkv", comm=pend[0]))
        dF = pad_lanes(dfk[:, :, 0].T)
        daf, dbias = _fox_prep_bwd(proj, sv["bias"], F0 // LANES, dF, "fox_prep_bwd")
        gr["fox_forget_b"] = dbias[0, :FH]
        d_a = [dqT[:, :HEAD_DIM].transpose(2, 0, 1).reshape(S, FW).astype(BF16),
               _from_heads(dk_aug[:, :, :HEAD_DIM]).astype(BF16), _from_heads(dva).astype(BF16)]
        dxb_, dgb, dgc, gs["shortconv_w"] = _sconv_bwd(proj, fw["shortconv_w"], B0, CW, dos[1], "sconv_bwd")
        du, dv, dsw, dsb, dsg = _sgu_bwd(proj, sgu_norm_g[i].reshape(1, SW), sgu_w[i], sv["sgb"], C0, SW, dos[2],
                                         "sgu_bwd")
        gr["sgu_w"], gr["sgu_b"], gr["sgu_norm_g"] = dsw, dsb.reshape(G, SGU_CHUNK), dsg[0]
        do3, cf3 = _dil_merge_bwd(sv["o3"], sv["lse3"], _to_heads(dos[3], hpg), "dil_merge_bwd")
        do_r, cf_r = _groups_to_residue(do3), _groups_to_residue(cf3)
        dq_r = host(5, _dil_bwd_dq(sv["qr"], sv["kr"], sv["vr"], do_r, sv["lse_r"], cf_r, bpcs, hpg, "dil_bwd_dq",
                                   comm=pend[5]))
        dk_r, dv_r = host(3, _dil_bwd_dkv(sv["qr"], sv["kr"], sv["vr"], do_r, sv["lse_r"], cf_r, bpcs, hpg,
                                          "dil_bwd_dkv", comm=pend[3]))

        def natural(t):
            return _from_heads(_from_residue(t, hpg).reshape(NG * hpg, S, HEAD_DIM))

        dqd = _rope_apply(natural(dq_r), 0, DW, tables, -1.0, BF16, "rope_q_bwd")
        dkd = _rope_apply(natural(dk_r), 0, DW, tables, -1.0, BF16, "rope_k_bwd")
        dvd = natural(dv_r).astype(BF16)
        dproj = jnp.concatenate(d_a + [dxb_, dgb, dgc, du, dv, dqd, dkd, dvd] + list(dgl) + [daf], axis=1)
        gs["w_in"] = _mm(sv["h1"], dproj, "tn", BF16, "mm_dw_in", comm=own_up)
        if i == 0:
            gs["w_in"], own_up = gs["w_in"]
        dh1 = host(4, _mm(dproj, fw["w_in"], "nt", F32, "mm_dh1", comm=pend[4]))
        if i + 1 < depth:
            scattered[i + 1] = got
        dx, dxb, gr["norm_mix_g"] = _rms_bwd(sv["x"], norm_mix_g[i], dh1, dx, "rms_mix_bwd")
        gr["norm_mix_g"], gr["norm_ffn_g"], gr["norm_ple_g"] = (gr[n][0] for n in
                                                               ("norm_mix_g", "norm_ffn_g", "norm_ple_g"))
        grads_sh[i], grads_rep[i] = gs, gr
    grad_x = dx.reshape(1, S, D)

    last = _comm("comm_scatter_grads", *one_plan(scatter_plans(grads_sh[0], (0, 3, 4, 5))))
    scattered[0] = [last[0:2], own_up, own_down, last[2:3], last[3:4], last[4:7]]

    def sum_sources(t, name):
        lead = t.shape[1:]
        return _sum8(t.reshape(N_DEV, -1, lead[-1]), name).reshape(lead)

    grads = {n: [None] * depth for n in SHARDED}
    for i in range(depth):
        (gw, gf), (gup,), (gdown,), (gsq,), (gw2,), (gbr, gsm, gcv) = scattered[i]
        rows = jnp.concatenate([sum_sources(gw, "sum_w_in"), sum_sources(gw2, "sum_w_in")], axis=0)
        win = jnp.concatenate([rows, sum_sources(gf, "sum_w_in_flogit")], axis=1)
        grads["w_in"][i] = on_my_index(from_window, win)
        grads["w_up"][i] = sum_sources(gup, "sum_w_up")
        grads["w_down"][i] = sum_sources(gdown, "sum_w_down")
        sq, br, sm = sum_sources(gsq, "sum_sq"), sum_sources(gbr, "sum_br"), sum_sources(gsm, "sum_sm")
        cv = sum_sources(gcv, "sum_conv")
        grads["w_out"][i], grads["w_ple_gate"][i] = sq[0], sq[1]
        grads["w_br_fox"][i], grads["w_br_conv"][i], grads["w_br_sgu"][i] = br[0], br[1], br[2]
        grads["w_br_dil"][i], grads["w_ple_proj"][i] = sm[0], sm[1]
        grads["ffn_conv_w"][i], grads["shortconv_w"][i] = cv[0:3, :n_up], cv[3:6, :n_sc]
    grads = {n: jnp.stack(v, axis=0) for n, v in grads.items()}

    rep_parts = [jnp.stack([grads_rep[i][n] for i in range(depth)], axis=0).reshape(-1) for n in REPLICATED]
    rep_parts.append(dg_final.reshape(-1))
    rep_sizes = [int(r.shape[0]) for r in rep_parts]
    rep_offs = [sum(rep_sizes[:i]) for i in range(len(rep_sizes) + 1)]
    rep = _sum8(_exchange(_pack(rep_parts, F32, 8), True, "comm_gather_small"), "sum_small").reshape(-1)
    for n, a, b in zip(REPLICATED + ("final_norm_g",), rep_offs[:-1], rep_offs[1:]):
        grads[n] = rep[a:b].reshape(W[n].shape)

    deltas, new_m, new_v = {}, {}, {}
    for n in WEIGHTS:
        deltas[n], new_m[n], new_v[n] = _adamw(W[n], grads[n], given["m_" + n], given["v_" + n], "adamw_" + n)
    return (loss, grad_x, *[grads[n] for n in WEIGHTS], *[deltas[n] for n in WEIGHTS],
            *[new_m[n] for n in WEIGHTS], *[new_v[n] for n in WEIGHTS])
```

```python
import functools

import jax
import jax.numpy as jnp
from jax import lax
from jax.experimental import pallas as pl
from jax.experimental.pallas import tpu as pltpu

F32 = jnp.float32
BF16 = jnp.bfloat16
EPS = 1e-6
NEG = -1e30
HEAD_DIM = 64
SGU_CHUNK = 128
DIL_PATTERNS = ((128, 1), (512, 4), (2048, 16))
DIL_SPAN = 128
ROPE_THETA = 500000.0
ROPE_DIM = 16
N_DEV = 8
LANES = 128
VMEM_LIMIT = 56 * 1024 * 1024
ADAM_LR, ADAM_B1, ADAM_B2, ADAM_EPS, ADAM_WD, ADAM_STEP = 0.001, 0.9, 0.999, 1e-08, 0.01, 10
MESH_AXES = ("x", "y", "c")

BS = pl.BlockSpec
SDS = jax.ShapeDtypeStruct


def _cp(*sem):
    return pltpu.CompilerParams(dimension_semantics=sem, vmem_limit_bytes=VMEM_LIMIT)


def _tile(dim, pref, unit=LANES):
    if dim % unit:
        return dim
    best, t = unit, unit
    while t <= min(dim, pref):
        if dim % t == 0:
            best = t
        t += unit
    return best


def _sigmoid(z):
    return 1.0 / (1.0 + jnp.exp(-z))


def _exchange(x, gather, name):
    shape = x.shape if not gather else (N_DEV,) + x.shape

    def body(x_ref, o_ref, send_sems, recv_sems, local_sem):
        ix, iy, ic = lax.axis_index("x"), lax.axis_index("y"), lax.axis_index("c")
        me = 4 * ix + 2 * iy + ic

        def src(j):
            return x_ref if gather else x_ref.at[j]

        local = pltpu.make_async_copy(src(me), o_ref.at[me], local_sem)
        local.start()
        sends, recvs = [], []
        for r in range(1, N_DEV):
            px = 1 - ix if (r >> 2) & 1 else ix
            py = 1 - iy if (r >> 1) & 1 else iy
            pc = 1 - ic if r & 1 else ic
            peer = 4 * px + 2 * py + pc
            sends.append(pltpu.make_async_remote_copy(
                src_ref=src(peer), dst_ref=o_ref.at[me], send_sem=send_sems.at[r - 1], recv_sem=recv_sems.at[r - 1],
                device_id=(px, py, pc), device_id_type=pl.DeviceIdType.MESH))
            recvs.append(pltpu.make_async_remote_copy(
                src_ref=src(peer), dst_ref=o_ref.at[peer], send_sem=send_sems.at[r - 1], recv_sem=recv_sems.at[r - 1],
                device_id=(px, py, pc), device_id_type=pl.DeviceIdType.MESH))
        for cp in sends:
            cp.start()
        for cp in recvs:
            cp.wait_recv()
        for cp in sends:
            cp.wait_send()
        local.wait()

    return pl.pallas_call(
        body, name=name, out_shape=SDS(shape, x.dtype),
        in_specs=[BS(memory_space=pl.ANY)], out_specs=BS(memory_space=pl.ANY),
        scratch_shapes=[pltpu.SemaphoreType.DMA((N_DEV - 1,)), pltpu.SemaphoreType.DMA((N_DEV - 1,)),
                        pltpu.SemaphoreType.DMA],
    )(x)


def _comm(name, srcs, out_shapes, src_views, dst_views, relay=False):
    n = len(srcs)

    def body(*refs):
        copies = _comm_copies(refs[:n], refs[n:2 * n], refs[2 * n:], src_views, dst_views, relay)
        _comm_start(copies)
        _comm_wait(copies)

    anyspec = BS(memory_space=pl.ANY)
    return pl.pallas_call(
        body, name=name, out_shape=tuple(out_shapes), in_specs=[anyspec] * n, out_specs=(anyspec,) * n,
        scratch_shapes=_comm_sems(n))(*srcs)


def _comm_sems(n):
    return [pltpu.SemaphoreType.DMA((N_DEV - 1, n)), pltpu.SemaphoreType.DMA((N_DEV - 1, n)),
            pltpu.SemaphoreType.DMA((n,))]


SIBLING = 1
OTHER_CHIP_SAME_CORE = (2, 4, 6)


def _comm_copies(s_refs, o_refs, sems, src_views, dst_views, relay=False):
    send_sems, recv_sems, local_sems = sems
    n = len(s_refs)
    ix, iy, ic = lax.axis_index("x"), lax.axis_index("y"), lax.axis_index("c")
    me = 4 * ix + 2 * iy + ic

    def at(r):
        px = 1 - ix if (r >> 2) & 1 else ix
        py = 1 - iy if (r >> 1) & 1 else iy
        pc = 1 - ic if r & 1 else ic
        return (px, py, pc), 4 * px + 2 * py + pc

    def copy(r, k, src, dst, to):
        return pltpu.make_async_remote_copy(src_ref=src, dst_ref=dst, send_sem=send_sems.at[r - 1, k],
                                            recv_sem=recv_sems.at[r - 1, k], device_id=to,
                                            device_id_type=pl.DeviceIdType.MESH)

    local = [pltpu.make_async_copy(src_views[k](s_refs[k], me), dst_views[k](o_refs[k], me), local_sems.at[k])
             for k in range(n)]
    sends, passed, arrivals = [], [], []
    for r in range(1, N_DEV):
        to, peer = at(r)
        for k in range(n):
            arrival = copy(r, k, src_views[k](s_refs[k], peer), dst_views[k](o_refs[k], peer), to)
            direct = not relay or r == SIBLING or r in OTHER_CHIP_SAME_CORE
            if direct:
                sends.append(copy(r, k, src_views[k](s_refs[k], peer), dst_views[k](o_refs[k], me), to))
            if relay and r in OTHER_CHIP_SAME_CORE:
                landed = dst_views[k](o_refs[k], peer)
                passed.append((arrival, copy(r ^ SIBLING, k, landed, landed, at(SIBLING)[0])))
            else:
                arrivals.append(arrival)
    return local, sends, passed, arrivals


def _comm_start(copies):
    local, sends, _, _ = copies
    for cp in local + sends:
        cp.start()


def _comm_wait(copies):
    local, sends, passed, arrivals = copies
    for arrival, onward in passed:
        arrival.wait_recv()
        onward.start()
    for cp in arrivals:
        cp.wait_recv()
    for cp in sends + [onward for _, onward in passed]:
        cp.wait_send()
    for cp in local:
        cp.wait()


def _call(body, *, name, grid, in_specs, out_specs, out_shape, args, scratch_shapes=(), sem=None, comm=None):
    single = not isinstance(out_shape, (tuple, list))
    out_shape = (out_shape,) if single else tuple(out_shape)
    out_specs = (out_specs,) if single else tuple(out_specs)
    if comm is None:
        outs = pl.pallas_call(body, name=name, grid=grid, in_specs=list(in_specs), out_specs=out_specs,
                              out_shape=out_shape, scratch_shapes=list(scratch_shapes),
                              compiler_params=_cp(*sem))(*args)
        return outs[0] if single else outs
    srcs, shapes, src_views, dst_views, relay = comm
    n, ni, no, ns = len(srcs), len(in_specs), len(out_shape), len(scratch_shapes)

    def hosted(*refs):
        ins, c_in = refs[:ni], refs[ni:ni + n]
        outs, c_out = refs[ni + n:ni + n + no], refs[ni + n + no:ni + 2 * n + no]
        scr, sems = refs[ni + 2 * n + no:ni + 2 * n + no + ns], refs[ni + 2 * n + no + ns:]
        ids = [pl.program_id(a) for a in range(len(grid))]
        first = functools.reduce(jnp.logical_and, [i == 0 for i in ids])
        last = functools.reduce(jnp.logical_and, [i == g - 1 for i, g in zip(ids, grid)])

        @pl.when(first)
        def _():
            _comm_start(_comm_copies(c_in, c_out, sems, src_views, dst_views, relay))

        body(*ins, *outs, *scr)

        @pl.when(last)
        def _():
            _comm_wait(_comm_copies(c_in, c_out, sems, src_views, dst_views, relay))

    anyspec = BS(memory_space=pl.ANY)
    outs = pl.pallas_call(
        hosted, name=name, grid=grid, in_specs=list(in_specs) + [anyspec] * n,
        out_specs=out_specs + (anyspec,) * n, out_shape=out_shape + tuple(shapes),
        scratch_shapes=list(scratch_shapes) + _comm_sems(n),
        compiler_params=_cp(*(["arbitrary"] * len(grid))))(*args, *srcs)
    return (outs[0] if single else outs[:no]), outs[no:]


def _assemble(pieces, starts, wb, npb, name):
    _, K, pw = pieces.shape
    tr = _tile(K, 256, 16)

    def body(p_ref, o_ref):
        for b in range(npb - 1):
            acc = None
            for j in range(N_DEV):
                if starts[j] <= b < starts[j] + wb:
                    blk = p_ref[j, :, (b - starts[j]) * LANES:(b - starts[j] + 1) * LANES]
                    acc = blk if acc is None else acc + blk
            o_ref[:, b * LANES:(b + 1) * LANES] = acc
        acc = p_ref[0, :, wb * LANES:]
        for j in range(1, N_DEV):
            acc = acc + p_ref[j, :, wb * LANES:]
        o_ref[:, (npb - 1) * LANES:] = acc

    return pl.pallas_call(
        body, name=name, grid=(K // tr,), out_shape=SDS((K, npb * LANES), pieces.dtype),
        in_specs=[BS((N_DEV, tr, pw), lambda i: (0, i, 0))], out_specs=BS((tr, npb * LANES), lambda i: (i, 0)),
        compiler_params=_cp("parallel"))(pieces)


def _sum8(x, name):
    _, R, C = x.shape
    tr = _tile(R, 256, 16)

    def body(x_ref, o_ref):
        acc = x_ref[0].astype(F32)
        for j in range(1, N_DEV):
            acc = acc + x_ref[j].astype(F32)
        o_ref[...] = acc

    return pl.pallas_call(
        body, name=name, grid=(R // tr,), out_shape=SDS((R, C), F32),
        in_specs=[BS((N_DEV, tr, C), lambda i: (0, i, 0))], out_specs=BS((tr, C), lambda i: (i, 0)),
        compiler_params=_cp("parallel"))(x)


_DIMS = {"nn": ((1,), (0,)), "nt": ((1,), (1,)), "tn": ((0,), (0,))}


def _mm(a, b, mode, out_dtype, name, res=None, comm=None):
    if mode == "nn":
        (M, K), N = a.shape, b.shape[1]
    elif mode == "nt":
        (M, K), N = a.shape, b.shape[0]
    else:
        (K, M), N = a.shape, b.shape[1]
    tm, tn, tk = _tile(M, 1024), _tile(N, 1024), _tile(K, 3072)
    nk = K // tk
    a_spec = BS((tk, tm), lambda i, j, k: (k, i)) if mode == "tn" else BS((tm, tk), lambda i, j, k: (i, k))
    b_spec = BS((tn, tk), lambda i, j, k: (j, k)) if mode == "nt" else BS((tk, tn), lambda i, j, k: (k, j))
    o_spec = BS((tm, tn), lambda i, j, k: (i, j))
    dims = (_DIMS[mode], ((), ()))

    def body(a_ref, b_ref, *rest):
        o_ref, acc = rest[-2], rest[-1]
        k = pl.program_id(2)

        @pl.when(k == 0)
        def _():
            acc[...] = jnp.zeros_like(acc)

        acc[...] += lax.dot_general(a_ref[...].astype(BF16), b_ref[...].astype(BF16), dims,
                                    preferred_element_type=F32)

        @pl.when(k == nk - 1)
        def _():
            r = acc[...]
            if res is not None:
                r = r + rest[0][...]
            o_ref[...] = r.astype(o_ref.dtype)

    ins, specs = [a, b], [a_spec, b_spec]
    if res is not None:
        ins.append(res)
        specs.append(o_spec)
    return _call(body, name=name, grid=(M // tm, N // tn, nk), out_shape=SDS((M, N), out_dtype), in_specs=specs,
                 out_specs=o_spec, scratch_shapes=[pltpu.VMEM((tm, tn), F32)],
                 sem=("parallel", "parallel", "arbitrary"), args=ins, comm=comm)


def _rms_fwd(x, g, name):
    S, D = x.shape
    ts = _tile(S, 256, 8)

    def body(x_ref, g_ref, o_ref):
        xv = x_ref[...]
        rstd = lax.rsqrt(jnp.mean(xv * xv, axis=-1, keepdims=True) + EPS)
        o_ref[...] = (xv * rstd * g_ref[...]).astype(o_ref.dtype)

    return pl.pallas_call(
        body, name=name, grid=(S // ts,), out_shape=SDS((S, D), BF16),
        in_specs=[BS((ts, D), lambda i: (i, 0)), BS((1, D), lambda i: (0, 0))],
        out_specs=BS((ts, D), lambda i: (i, 0)), compiler_params=_cp("parallel"))(x, g.reshape(1, D))


def _rms_bwd(x, g, dh, dres, name):
    S, D = x.shape
    ts = _tile(S, 256, 8)

    def body(x_ref, g_ref, dh_ref, dres_ref, dx_ref, dxb_ref, dg_ref):
        xv = x_ref[...]
        rstd = lax.rsqrt(jnp.mean(xv * xv, axis=-1, keepdims=True) + EPS)
        xn = xv * rstd
        dhv = dh_ref[...]

        @pl.when(pl.program_id(0) == 0)
        def _():
            dg_ref[...] = jnp.zeros_like(dg_ref)

        dg_ref[...] += jnp.sum(dhv * xn, axis=0, keepdims=True)
        dhn = dhv * g_ref[...]
        dxv = rstd * (dhn - xn * jnp.mean(dhn * xn, axis=-1, keepdims=True)) + dres_ref[...]
        dx_ref[...] = dxv
        dxb_ref[...] = dxv.astype(BF16)

    row = BS((ts, D), lambda i: (i, 0))
    vec = BS((1, D), lambda i: (0, 0))
    return pl.pallas_call(
        body, name=name, grid=(S // ts,), out_shape=(SDS((S, D), F32), SDS((S, D), BF16), SDS((1, D), F32)),
        in_specs=[row, vec, row, row], out_specs=(row, row, vec), compiler_params=_cp("arbitrary"),
    )(x, g.reshape(1, D), dh, dres)


def _final_loss(x, g, target, name):
    S, D = x.shape
    ts = _tile(S, 256, 8)

    def body(x_ref, g_ref, t_ref, loss_ref, dx_ref, dxb_ref, dg_ref):
        xv = x_ref[...]
        rstd = lax.rsqrt(jnp.mean(xv * xv, axis=-1, keepdims=True) + EPS)
        xn = xv * rstd
        gv = g_ref[...]
        e = xn * gv - t_ref[...]

        @pl.when(pl.program_id(0) == 0)
        def _():
            dg_ref[...] = jnp.zeros_like(dg_ref)
            loss_ref[...] = jnp.zeros_like(loss_ref)

        loss_ref[...] += 0.5 * jnp.sum(jnp.mean(e * e, axis=-1, keepdims=True), axis=0, keepdims=True)
        dy = e * (1.0 / D)
        dg_ref[...] += jnp.sum(dy * xn, axis=0, keepdims=True)
        dhn = dy * gv
        dxv = rstd * (dhn - xn * jnp.mean(dhn * xn, axis=-1, keepdims=True))
        dx_ref[...] = dxv
        dxb_ref[...] = dxv.astype(BF16)

    row = BS((ts, D), lambda i: (i, 0))
    vec = BS((1, D), lambda i: (0, 0))
    return pl.pallas_call(
        body, name=name, grid=(S // ts,),
        out_shape=(SDS((1, LANES), F32), SDS((S, D), F32), SDS((S, D), BF16), SDS((1, D), F32)),
        in_specs=[row, vec, row], out_specs=(BS((1, LANES), lambda i: (0, 0)), row, row, vec),
        compiler_params=_cp("arbitrary"))(x, g.reshape(1, D), target)


def _shift_down(z, k):
    rows = lax.broadcasted_iota(jnp.int32, z.shape, 0)
    return jnp.where(rows >= k, pltpu.roll(z, k, 0), 0.0)


def _shift_up(z, k):
    n = z.shape[0]
    rows = lax.broadcasted_iota(jnp.int32, z.shape, 0)
    return jnp.where(rows < n - k, pltpu.roll(z, n - k, 0), 0.0)


def _cumsum_rows(z, reverse=False):
    n = z.shape[0]
    k = 1
    while k < n:
        z = z + (_shift_up(z, k) if reverse else _shift_down(z, k))
        k *= 2
    return z


def _conv3(z, w_ref):
    return w_ref[0:1, :] * _shift_down(z, 2) + w_ref[1:2, :] * _shift_down(z, 1) + w_ref[2:3, :] * z


def _conv3_bwd(z, dy, w_ref, dw_ref):
    dw_ref[0:1, :] = jnp.sum(dy * _shift_down(z, 2), axis=0, keepdims=True)
    dw_ref[1:2, :] = jnp.sum(dy * _shift_down(z, 1), axis=0, keepdims=True)
    dw_ref[2:3, :] = jnp.sum(dy * z, axis=0, keepdims=True)
    return w_ref[2:3, :] * dy + w_ref[1:2, :] * _shift_up(dy, 1) + w_ref[0:1, :] * _shift_up(dy, 2)


def _fox_prep(proj, bias, col, name):
    S = proj.shape[0]

    def body(a_ref, b_ref, f_ref):
        z = a_ref[...] + b_ref[...]
        logf = jnp.minimum(z, 0.0) - jnp.log(1.0 + jnp.exp(-jnp.abs(z)))
        rest = _cumsum_rows(logf)
        for k in range(3):
            piece = rest.astype(BF16)
            f_ref[k] = piece
            rest = rest - piece.astype(F32)

    return pl.pallas_call(
        body, name=name, grid=(1,), out_shape=SDS((3, S, LANES), BF16),
        in_specs=[BS((S, LANES), lambda i: (0, col)), BS((1, LANES), lambda i: (0, 0))],
        out_specs=BS((3, S, LANES), lambda i: (0, 0, 0)), compiler_params=_cp("arbitrary"))(proj, bias)


def _fox_prep_bwd(proj, bias, col, df, name):
    S = proj.shape[0]

    def body(a_ref, b_ref, df_ref, da_ref, db_ref):
        z = a_ref[...] + b_ref[...]
        da = _cumsum_rows(df_ref[...], reverse=True) * _sigmoid(-z)
        da_ref[...] = da.astype(BF16)
        db_ref[...] = jnp.sum(da, axis=0, keepdims=True)

    full = BS((S, LANES), lambda i: (0, 0))
    vec = BS((1, LANES), lambda i: (0, 0))
    return pl.pallas_call(
        body, name=name, grid=(1,), out_shape=(SDS((S, LANES), BF16), SDS((1, LANES), F32)),
        in_specs=[BS((S, LANES), lambda i: (0, col)), vec, full], out_specs=(full, vec),
        compiler_params=_cp("arbitrary"))(proj, bias, df)


def _fox_scores(qs, kv, fqv, fkv, i, j, t, masked):
    s = lax.dot_general(qs, kv, (_DIMS["nt"], ((), ())), preferred_element_type=F32) + fqv - fkv
    if not masked:
        return s
    row = i * t + lax.broadcasted_iota(jnp.int32, (t, t), 0)
    col = j * t + lax.broadcasted_iota(jnp.int32, (t, t), 1)
    return jnp.where(col <= row, s, NEG)


def _fox_fwd(q, k, v, fq, fk, name, comm=None):
    H, S, Dh = q.shape
    t = _tile(S, 256)
    n = S // t
    scale = Dh ** -0.5

    def body(q_ref, k_ref, v_ref, fq_ref, fk_ref, o_ref, lse_ref):
        i = pl.program_id(1)
        qs, fqv = q_ref[0] * scale, fq_ref[0]

        def step(masked, j, carry):
            m, l, acc = carry
            s = _fox_scores(qs, k_ref[0, j], fqv, fk_ref[0, j], i, j, t, masked)
            mn = jnp.maximum(m, jnp.max(s, axis=-1, keepdims=True))
            al = jnp.exp(m - mn)
            p = jnp.exp(s - mn)
            l = l * al + jnp.sum(p, axis=-1, keepdims=True)
            acc = acc * al + jnp.dot(p.astype(BF16), v_ref[0, j], preferred_element_type=F32)
            return mn, l, acc

        init = (jnp.full((t, 1), NEG, F32), jnp.zeros((t, 1), F32), jnp.zeros((t, Dh), F32))
        m, l, acc = step(True, i, lax.fori_loop(0, i, functools.partial(step, False), init))
        o_ref[0] = acc / l
        lse_ref[0] = m + jnp.log(l)

    blk = BS((1, t, Dh), lambda h, i: (h, i, 0))
    col = BS((1, t, 1), lambda h, i: (h, i, 0))
    res = BS((1, n, t, Dh), lambda h, i: (h, 0, 0, 0))
    return _call(
        body, name=name, grid=(H, n), out_shape=(SDS((H, S, Dh), F32), SDS((H, S, 1), F32)),
        in_specs=[blk, res, res, col, BS((1, n, 1, t), lambda h, i: (h, 0, 0, 0))], out_specs=(blk, col),
        sem=("parallel", "arbitrary"), args=(q, k.reshape(H, n, t, Dh), v.reshape(H, n, t, Dh), fq, fk), comm=comm)


def _fox_bwd_dq(q, k, v, fq, fk, do, lse, name, comm=None):
    H, S, Dh = q.shape
    t = _tile(S, 256)
    n = S // t
    scale = Dh ** -0.5

    def body(q_ref, k_ref, v_ref, fq_ref, fk_ref, do_ref, lse_ref, dq_ref, dl_ref):
        i = pl.program_id(1)
        qs, fqv, dob, lsev = q_ref[0] * scale, fq_ref[0], do_ref[0], lse_ref[0]

        def p_dp(masked, j):
            p = jnp.exp(_fox_scores(qs, k_ref[0, j], fqv, fk_ref[0, j], i, j, t, masked) - lsev)
            return p, lax.dot_general(dob, v_ref[0, j], (_DIMS["nt"], ((), ())), preferred_element_type=F32)

        def sum_step(masked, j, delta):
            p, dp = p_dp(masked, j)
            return delta + jnp.sum(p * dp, axis=-1, keepdims=True)

        delta = sum_step(True, i, lax.fori_loop(0, i, functools.partial(sum_step, False), jnp.zeros((t, 1), F32)))

        def step(masked, j, dq):
            p, dp = p_dp(masked, j)
            return dq + jnp.dot((p * (dp - delta)).astype(BF16), k_ref[0, j], preferred_element_type=F32)

        dq = step(True, i, lax.fori_loop(0, i, functools.partial(step, False), jnp.zeros((t, Dh), F32)))
        dq_ref[0] = dq * scale
        dl_ref[0] = delta

    blk = BS((1, t, Dh), lambda h, i: (h, i, 0))
    col = BS((1, t, 1), lambda h, i: (h, i, 0))
    res = BS((1, n, t, Dh), lambda h, i: (h, 0, 0, 0))
    return _call(
        body, name=name, grid=(H, n), out_shape=(SDS((H, S, Dh), F32), SDS((H, S, 1), F32)),
        in_specs=[blk, res, res, col, BS((1, n, 1, t), lambda h, i: (h, 0, 0, 0)), blk, col],
        out_specs=(blk, col), sem=("parallel", "arbitrary"), comm=comm,
        args=(q, k.reshape(H, n, t, Dh), v.reshape(H, n, t, Dh), fq, fk, do, lse))


def _fox_bwd_dkv(q, k, v, fq, fk, do, lse, delta, name, comm=None):
    H, S, Dh = q.shape
    t = _tile(S, 256)
    n = S // t
    scale = Dh ** -0.5
    tn_dims = (_DIMS["tn"], ((), ()))

    def body(q_ref, k_ref, v_ref, fq_ref, fk_ref, do_ref, lse_ref, dl_ref, dk_ref, dv_ref, dfk_ref):
        j = pl.program_id(1)
        kv, vv, fkv = k_ref[0], v_ref[0], fk_ref[0, 0]

        def step(masked, i, carry):
            dk, dv, dfk = carry
            qs, dob = q_ref[0, i] * scale, do_ref[0, i]
            p = jnp.exp(_fox_scores(qs, kv, fq_ref[0, i], fkv, i, j, t, masked) - lse_ref[0, i])
            dv = dv + lax.dot_general(p.astype(BF16), dob, tn_dims, preferred_element_type=F32)
            dp = lax.dot_general(dob, vv, (_DIMS["nt"], ((), ())), preferred_element_type=F32)
            ds = p * (dp - dl_ref[0, i])
            dk = dk + lax.dot_general(ds.astype(BF16), qs, tn_dims, preferred_element_type=F32)
            return dk, dv, dfk - jnp.sum(ds, axis=0, keepdims=True)

        init = (jnp.zeros((t, Dh), F32), jnp.zeros((t, Dh), F32), jnp.zeros((1, t), F32))
        dk, dv, dfk = lax.fori_loop(j + 1, n, functools.partial(step, False), step(True, j, init))
        dk_ref[0] = dk
        dv_ref[0] = dv
        dfk_ref[0, 0] = dfk

    blk = BS((1, t, Dh), lambda h, j: (h, j, 0))
    res = BS((1, n, t, Dh), lambda h, j: (h, 0, 0, 0))
    rcol = BS((1, n, t, 1), lambda h, j: (h, 0, 0, 0))
    frow = BS((1, 1, 1, t), lambda h, j: (h, j, 0, 0))
    return _call(
        body, name=name, grid=(H, n),
        out_shape=(SDS((H, S, Dh), F32), SDS((H, S, Dh), F32), SDS((H, n, 1, t), F32)),
        in_specs=[res, blk, blk, rcol, frow, res, rcol, rcol], out_specs=(blk, blk, frow),
        sem=("parallel", "arbitrary"), comm=comm,
        args=(q.reshape(H, n, t, Dh), k, v, fq.reshape(H, n, t, 1), fk, do.astype(BF16).reshape(H, n, t, Dh),
              lse.reshape(H, n, t, 1), delta.reshape(H, n, t, 1)))


FOX_HEADS_PER_STEP = 2


def _fox_q_scale(Dh):
    lane = lax.broadcasted_iota(jnp.int32, (1, 2 * Dh), 1)
    return jnp.where(lane < Dh, Dh ** -0.5, 1.0).astype(BF16)


def _foxt_scores(ka, qs, i, j, t, masked):
    s = lax.dot_general(ka, qs, (_DIMS["nt"], ((), ())), preferred_element_type=F32)
    if not masked:
        return s
    key = j * t + lax.broadcasted_iota(jnp.int32, (t, t), 0)
    qry = i * t + lax.broadcasted_iota(jnp.int32, (t, t), 1)
    return jnp.where(key <= qry, s, NEG)


def _foxt_fwd(q_aug, k_aug, vT, name, comm=None):
    H, S, W = q_aug.shape
    Dh = W // 2
    t = _tile(S, 256)
    n = S // t
    hb = FOX_HEADS_PER_STEP

    def body(q_ref, k_ref, v_ref, o_ref, lse_ref):
        i = pl.program_id(1)
        qs = [q_ref[h] * _fox_q_scale(Dh) for h in range(hb)]

        def step(masked, j, carry):
            out = []
            for h in range(hb):
                m, l, acc = carry[h]
                s = _foxt_scores(k_ref[h, j], qs[h], i, j, t, masked)
                mn = jnp.maximum(m, jnp.max(s, axis=0, keepdims=True))
                al = jnp.exp(m - mn)
                p = jnp.exp(s - mn)
                l = l * al + jnp.sum(p, axis=0, keepdims=True)
                acc = acc * al + jnp.dot(v_ref[h, j], p.astype(BF16), preferred_element_type=F32)
                out.append((mn, l, acc))
            return tuple(out)

        init = tuple((jnp.full((1, t), NEG, F32), jnp.zeros((1, t), F32), jnp.zeros((Dh, t), F32))
                     for _ in range(hb))
        fin = step(True, i, lax.fori_loop(0, i, functools.partial(step, False), init))
        for h in range(hb):
            m, l, acc = fin[h]
            o_ref[h] = acc / l
            lse_ref[h] = m + jnp.log(l)

    return _call(
        body, name=name, grid=(H // hb, n), out_shape=(SDS((H, Dh, S), F32), SDS((H, 1, S), F32)),
        in_specs=[BS((hb, t, W), lambda g, i: (g, i, 0)), BS((hb, n, t, W), lambda g, i: (g, 0, 0, 0)),
                  BS((hb, n, Dh, t), lambda g, i: (g, 0, 0, 0))],
        out_specs=(BS((hb, Dh, t), lambda g, i: (g, 0, i)), BS((hb, 1, t), lambda g, i: (g, 0, i))),
        sem=("parallel", "arbitrary"), args=(q_aug, k_aug.reshape(H, n, t, W), vT), comm=comm)


def _foxt_bwd_dq(q_aug, k_aug, kT, v, do, lse, name, comm=None):
    H, S, W = q_aug.shape
    Dh = W // 2
    t = _tile(S, 256)
    n = S // t
    hb = FOX_HEADS_PER_STEP

    def body(q_ref, k_ref, kt_ref, v_ref, do_ref, lse_ref, dq_ref, dl_ref):
        i = pl.program_id(1)
        qs = [q_ref[h] * _fox_q_scale(Dh) for h in range(hb)]

        def p_dp(h, masked, j):
            p = jnp.exp(_foxt_scores(k_ref[h, j], qs[h], i, j, t, masked) - lse_ref[h])
            return p, lax.dot_general(v_ref[h, j], do_ref[h], (_DIMS["nt"], ((), ())), preferred_element_type=F32)

        def sum_step(masked, j, delta):
            out = []
            for h in range(hb):
                p, dp = p_dp(h, masked, j)
                out.append(delta[h] + jnp.sum(p * dp, axis=0, keepdims=True))
            return tuple(out)

        zero = tuple(jnp.zeros((1, t), F32) for _ in range(hb))
        delta = sum_step(True, i, lax.fori_loop(0, i, functools.partial(sum_step, False), zero))

        def step(masked, j, dq):
            out = []
            for h in range(hb):
                p, dp = p_dp(h, masked, j)
                ds = (p * (dp - delta[h])).astype(BF16)
                out.append(dq[h] + jnp.dot(kt_ref[h, j], ds, preferred_element_type=F32))
            return tuple(out)

        zero = tuple(jnp.zeros((W, t), F32) for _ in range(hb))
        dq = step(True, i, lax.fori_loop(0, i, functools.partial(step, False), zero))
        for h in range(hb):
            dq_ref[h] = dq[h] * Dh ** -0.5
            dl_ref[h] = delta[h]

    row = BS((hb, 1, t), lambda g, i: (g, 0, i))
    blk = lambda w: BS((hb, t, w), lambda g, i: (g, i, 0))
    return _call(
        body, name=name, grid=(H // hb, n), out_shape=(SDS((H, W, S), F32), SDS((H, 1, S), F32)),
        in_specs=[blk(W), BS((hb, n, t, W), lambda g, i: (g, 0, 0, 0)), BS((hb, n, W, t), lambda g, i: (g, 0, 0, 0)),
                  BS((hb, n, t, Dh), lambda g, i: (g, 0, 0, 0)), blk(Dh), row],
        out_specs=(BS((hb, W, t), lambda g, i: (g, 0, i)), row), sem=("parallel", "arbitrary"), comm=comm,
        args=(q_aug, k_aug.reshape(H, n, t, W), kT, v.reshape(H, n, t, Dh), do, lse))


def _foxt_bwd_dkv(q_aug, k_aug, v, do, lse, delta, name, comm=None):
    H, S, W = q_aug.shape
    Dh = W // 2
    t = _tile(S, 256)
    n = S // t
    hb = FOX_HEADS_PER_STEP

    def body(q_ref, k_ref, v_ref, do_ref, lse_ref, dl_ref, dk_ref, dv_ref, dfk_ref):
        j = pl.program_id(1)

        def step(masked, i, carry):
            out = []
            for h in range(hb):
                dk, dv, dfk = carry[h]
                qs, dob = q_ref[h, i] * _fox_q_scale(Dh), do_ref[h, i]
                p = jnp.exp(_foxt_scores(k_ref[h], qs, i, j, t, masked) - lse_ref[h, i])
                dv = dv + jnp.dot(p.astype(BF16), dob, preferred_element_type=F32)
                dp = lax.dot_general(v_ref[h], dob, (_DIMS["nt"], ((), ())), preferred_element_type=F32)
                ds = p * (dp - dl_ref[h, i])
                dk = dk + jnp.dot(ds.astype(BF16), qs, preferred_element_type=F32)
                out.append((dk, dv, dfk - jnp.sum(ds, axis=1, keepdims=True)))
            return tuple(out)

        init = tuple((jnp.zeros((t, W), F32), jnp.zeros((t, Dh), F32), jnp.zeros((t, 1), F32)) for _ in range(hb))
        fin = lax.fori_loop(j + 1, n, functools.partial(step, False), step(True, j, init))
        for h in range(hb):
            dk_ref[h], dv_ref[h], dfk_ref[h] = fin[h]

    blk = lambda w: BS((hb, t, w), lambda g, j: (g, j, 0))
    res = lambda w: BS((hb, n, t, w), lambda g, j: (g, 0, 0, 0))
    rows = BS((hb, n, 1, t), lambda g, j: (g, 0, 0, 0))
    return _call(
        body, name=name, grid=(H // hb, n),
        out_shape=(SDS((H, S, W), F32), SDS((H, S, Dh), F32), SDS((H, S, 1), F32)),
        in_specs=[res(W), blk(W), blk(Dh), res(Dh), rows, rows], out_specs=(blk(W), blk(Dh), blk(1)),
        sem=("parallel", "arbitrary"), comm=comm,
        args=(q_aug.reshape(H, n, t, W), k_aug, v, do.reshape(H, n, t, Dh), lse.reshape(H, n, 1, t),
              delta.reshape(H, n, 1, t)))


def _sconv_fwd(proj, w, col0, width, name):
    S = proj.shape[0]
    tc = LANES
    nb, c0 = width // tc, col0 // tc

    def body(x_ref, gb_ref, gc_ref, w_ref, o_ref):
        o_ref[...] = (gb_ref[...] * _conv3(gc_ref[...] * x_ref[...], w_ref)).astype(BF16)

    return pl.pallas_call(
        body, name=name, grid=(nb,), out_shape=SDS((S, width), BF16),
        in_specs=[BS((S, tc), lambda j: (0, c0 + j)), BS((S, tc), lambda j: (0, c0 + nb + j)),
                  BS((S, tc), lambda j: (0, c0 + 2 * nb + j)), BS((3, tc), lambda j: (0, j))],
        out_specs=BS((S, tc), lambda j: (0, j)), compiler_params=_cp("parallel"))(proj, proj, proj, w)


def _sconv_bwd(proj, w, col0, width, do, name):
    S = proj.shape[0]
    tc = LANES
    nb, c0 = width // tc, col0 // tc

    def body(x_ref, gb_ref, gc_ref, w_ref, do_ref, dx_ref, dgb_ref, dgc_ref, dw_ref):
        xb, gb, gc, dov = x_ref[...], gb_ref[...], gc_ref[...], do_ref[...]
        z = gc * xb
        dgb_ref[...] = (dov * _conv3(z, w_ref)).astype(BF16)
        dz = _conv3_bwd(z, dov * gb, w_ref, dw_ref)
        dgc_ref[...] = (dz * xb).astype(BF16)
        dx_ref[...] = (dz * gc).astype(BF16)

    out = BS((S, tc), lambda j: (0, j))
    wspec = BS((3, tc), lambda j: (0, j))
    return pl.pallas_call(
        body, name=name, grid=(nb,),
        out_shape=(SDS((S, width), BF16), SDS((S, width), BF16), SDS((S, width), BF16), SDS((3, width), F32)),
        in_specs=[BS((S, tc), lambda j: (0, c0 + j)), BS((S, tc), lambda j: (0, c0 + nb + j)),
                  BS((S, tc), lambda j: (0, c0 + 2 * nb + j)), wspec, out],
        out_specs=(out, out, out, wspec), compiler_params=_cp("parallel"))(proj, proj, proj, w, do)


def _gelu(x):
    return 0.5 * x * (1.0 + jnp.tanh(0.7978845608028654 * (x + 0.044715 * x * x * x)))


def _gelu_grad(x):
    u = 0.7978845608028654 * (x + 0.044715 * x * x * x)
    th = jnp.tanh(u)
    return 0.5 * (1.0 + th) + 0.5 * x * (1.0 - th * th) * 0.7978845608028654 * (1.0 + 3.0 * 0.044715 * x * x)


def _tril_w(w_ref, g):
    r = lax.broadcasted_iota(jnp.int32, (SGU_CHUNK, SGU_CHUNK), 0)
    c = lax.broadcasted_iota(jnp.int32, (SGU_CHUNK, SGU_CHUNK), 1)
    return jnp.where(c <= r, w_ref[g], 0.0), c <= r


def _sgu_mixed(vn, w_ref, b_ref, nch, G):
    rows = []
    for ch in range(nch):
        cols = []
        for g in range(G):
            wt, _ = _tril_w(w_ref, g)
            blk = vn[ch * SGU_CHUNK:(ch + 1) * SGU_CHUNK, g * LANES:(g + 1) * LANES]
            cols.append(jnp.dot(wt.astype(BF16), blk.astype(BF16), preferred_element_type=F32) + b_ref[g])
        rows.append(jnp.concatenate(cols, axis=1))
    return jnp.concatenate(rows, axis=0)


def _sgu_fwd(proj, gn, w, b, col0, width, name):
    S = proj.shape[0]
    G = w.shape[0]
    ts = _tile(S, 512)
    nch = ts // SGU_CHUNK
    c0 = col0 // width

    def body(u_ref, v_ref, gn_ref, w_ref, b_ref, o_ref):
        cv = _gelu(v_ref[...])
        rstd = lax.rsqrt(jnp.mean(cv * cv, axis=-1, keepdims=True) + EPS)
        mixed = _sgu_mixed(cv * rstd * gn_ref[...], w_ref, b_ref, nch, G)
        o_ref[...] = (_gelu(u_ref[...]) * mixed).astype(BF16)

    return pl.pallas_call(
        body, name=name, grid=(S // ts,), out_shape=SDS((S, width), BF16),
        in_specs=[BS((ts, width), lambda i: (i, c0)), BS((ts, width), lambda i: (i, c0 + 1)),
                  BS((1, width), lambda i: (0, 0)), BS(w.shape, lambda i: (0, 0, 0)), BS(b.shape, lambda i: (0, 0, 0))],
        out_specs=BS((ts, width), lambda i: (i, 0)), compiler_params=_cp("parallel"))(proj, proj, gn, w, b)


def _sgu_bwd(proj, gn, w, b, col0, width, do, name):
    S = proj.shape[0]
    G = w.shape[0]
    ts = _tile(S, 512)
    nch = ts // SGU_CHUNK
    c0 = col0 // width

    def body(u_ref, v_ref, gn_ref, w_ref, b_ref, do_ref, du_ref, dv_ref, dw_ref, db_ref, dgn_ref):
        uin, vin, dov, gnv = u_ref[...], v_ref[...], do_ref[...], gn_ref[...]
        cu, cv = _gelu(uin), _gelu(vin)
        rstd = lax.rsqrt(jnp.mean(cv * cv, axis=-1, keepdims=True) + EPS)
        vhat = cv * rstd
        vn = vhat * gnv
        mixed = _sgu_mixed(vn, w_ref, b_ref, nch, G)
        du_ref[...] = (dov * mixed * _gelu_grad(uin)).astype(BF16)
        dmix = dov * cu

        @pl.when(pl.program_id(0) == 0)
        def _():
            dw_ref[...] = jnp.zeros_like(dw_ref)
            db_ref[...] = jnp.zeros_like(db_ref)
            dgn_ref[...] = jnp.zeros_like(dgn_ref)

        rows = []
        for ch in range(nch):
            cols = []
            for g in range(G):
                wt, mask = _tril_w(w_ref, g)
                sl = (slice(ch * SGU_CHUNK, (ch + 1) * SGU_CHUNK), slice(g * LANES, (g + 1) * LANES))
                dm, vb = dmix[sl], vn[sl].astype(BF16)
                db_ref[g] += jnp.sum(dm, axis=-1, keepdims=True)
                dwg = lax.dot_general(dm.astype(BF16), vb, (_DIMS["nt"], ((), ())), preferred_element_type=F32)
                dw_ref[g] += jnp.where(mask, dwg, 0.0)
                cols.append(lax.dot_general(wt.astype(BF16), dm.astype(BF16), (_DIMS["tn"], ((), ())),
                                            preferred_element_type=F32))
            rows.append(jnp.concatenate(cols, axis=1))
        dvn = jnp.concatenate(rows, axis=0)
        dgn_ref[...] += jnp.sum(dvn * vhat, axis=0, keepdims=True)
        dvh = dvn * gnv
        dcv = rstd * (dvh - vhat * jnp.mean(dvh * vhat, axis=-1, keepdims=True))
        dv_ref[...] = (dcv * _gelu_grad(vin)).astype(BF16)

    row = BS((ts, width), lambda i: (i, 0))
    wsp, bsp, gsp = BS(w.shape, lambda i: (0, 0, 0)), BS(b.shape, lambda i: (0, 0, 0)), BS((1, width), lambda i: (0, 0))
    return pl.pallas_call(
        body, name=name, grid=(S // ts,),
        out_shape=(SDS((S, width), BF16), SDS((S, width), BF16), SDS(w.shape, F32), SDS(b.shape, F32),
                   SDS((1, width), F32)),
        in_specs=[BS((ts, width), lambda i: (i, c0)), BS((ts, width), lambda i: (i, c0 + 1)), gsp, wsp, bsp, row],
        out_specs=(row, row, wsp, bsp, gsp), compiler_params=_cp("arbitrary"))(proj, proj, gn, w, b, do)


def _rope_tables(positions, inv, name):
    S = positions.shape[0]
    ts = _tile(S, 512, 8)
    half = ROPE_DIM // 2

    def body(p_ref, inv_ref, c_ref, sa_ref, sb_ref):
        ang = p_ref[...].astype(F32) * inv_ref[...]
        lane = lax.broadcasted_iota(jnp.int32, (ts, LANES), 1) % HEAD_DIM
        sn = jnp.sin(ang)
        c_ref[...] = jnp.cos(ang)
        sa_ref[...] = jnp.where(lane < half, -sn, 0.0)
        sb_ref[...] = jnp.where((lane >= half) & (lane < ROPE_DIM), sn, 0.0)

    out = BS((ts, LANES), lambda i: (i, 0))
    return pl.pallas_call(
        body, name=name, grid=(S // ts,), out_shape=(SDS((S, LANES), F32),) * 3,
        in_specs=[BS((ts, 1), lambda i: (i, 0)), BS((1, LANES), lambda i: (0, 0))], out_specs=(out, out, out),
        compiler_params=_cp("parallel"))(positions, inv)


def _rope_apply(x, col0, width, tables, sign, out_dtype, name):
    S = x.shape[0]
    ts = _tile(S, 512, 8)
    tc = 2 * LANES
    nb, c0 = width // tc, col0 // tc
    half = ROPE_DIM // 2

    def body(x_ref, c_ref, sa_ref, sb_ref, o_ref):
        xv = x_ref[...].astype(F32)
        wide = lambda r: jnp.concatenate([r[...], r[...]], axis=1)
        y = (xv * wide(c_ref) + pltpu.roll(xv, tc - half, 1) * (sign * wide(sa_ref))
             + pltpu.roll(xv, half, 1) * (sign * wide(sb_ref)))
        o_ref[...] = y.astype(o_ref.dtype)

    tab = BS((ts, LANES), lambda i, j: (i, 0))
    return pl.pallas_call(
        body, name=name, grid=(S // ts, nb), out_shape=SDS((S, width), out_dtype),
        in_specs=[BS((ts, tc), lambda i, j: (i, c0 + j)), tab, tab, tab], out_specs=BS((ts, tc), lambda i, j: (i, j)),
        compiler_params=_cp("parallel", "parallel"))(x, *tables)


def _dil_masks(hh, m, bpcs, hpg):
    g = hh // hpg
    bpc = jnp.where(g == 0, bpcs[0], jnp.where(g == 1, bpcs[1], bpcs[2]))
    r = lax.broadcasted_iota(jnp.int32, (DIL_SPAN, DIL_SPAN), 0)
    c = lax.broadcasted_iota(jnp.int32, (DIL_SPAN, DIL_SPAN), 1)
    return (c >= r) & (lax.rem(m, bpc) != 0), c <= r


def _nt(a, b):
    return lax.dot_general(a, b, (_DIMS["nt"], ((), ())), preferred_element_type=F32)


def _tn(a, b):
    return lax.dot_general(a, b, (_DIMS["tn"], ((), ())), preferred_element_type=F32)


def _dil_fwd(q, k, v, bpcs, hpg, name, comm=None):
    HH, S, Dh = q.shape
    nb = S // DIL_SPAN
    scale = Dh ** -0.5

    def body(q_ref, kp_ref, kc_ref, vp_ref, vc_ref, o_ref, lse_ref):
        mp, mc = _dil_masks(pl.program_id(0) * hpg, pl.program_id(1), bpcs, hpg)
        for h in range(hpg):
            qv = q_ref[h]
            sp = jnp.where(mp, _nt(qv, kp_ref[h]) * scale, NEG)
            sc = jnp.where(mc, _nt(qv, kc_ref[h]) * scale, NEG)
            mx = jnp.maximum(jnp.max(sp, axis=-1, keepdims=True), jnp.max(sc, axis=-1, keepdims=True))
            pp, pc = jnp.exp(sp - mx), jnp.exp(sc - mx)
            l = jnp.sum(pp, axis=-1, keepdims=True) + jnp.sum(pc, axis=-1, keepdims=True)
            acc = (jnp.dot(pp.astype(BF16), vp_ref[h], preferred_element_type=F32)
                   + jnp.dot(pc.astype(BF16), vc_ref[h], preferred_element_type=F32))
            o_ref[h] = acc / l
            lse_ref[h] = mx + jnp.log(l)

    cur = BS((hpg, DIL_SPAN, Dh), lambda g, m: (g, m, 0))
    prev = BS((hpg, DIL_SPAN, Dh), lambda g, m: (g, jnp.maximum(m - 1, 0), 0))
    return _call(
        body, name=name, grid=(HH // hpg, nb), out_shape=(SDS((HH, S, Dh), F32), SDS((HH, S, 1), F32)),
        in_specs=[cur, prev, cur, prev, cur], out_specs=(cur, BS((hpg, DIL_SPAN, 1), lambda g, m: (g, m, 0))),
        sem=("parallel", "parallel"), args=(q, k, k, v, v), comm=comm)


def _dil_bwd_dq(q, k, v, do, lse, coef, bpcs, hpg, name, comm=None):
    HH, S, Dh = q.shape
    nb = S // DIL_SPAN
    scale = Dh ** -0.5

    def body(q_ref, kp_ref, kc_ref, vp_ref, vc_ref, do_ref, lse_ref, cf_ref, dq_ref):
        mp, mc = _dil_masks(pl.program_id(0) * hpg, pl.program_id(1), bpcs, hpg)
        for h in range(hpg):
            qv, dob, lsev, cf = q_ref[h], do_ref[h], lse_ref[h], cf_ref[h]
            pp = jnp.exp(jnp.where(mp, _nt(qv, kp_ref[h]) * scale, NEG) - lsev)
            pc = jnp.exp(jnp.where(mc, _nt(qv, kc_ref[h]) * scale, NEG) - lsev)
            dsp = pp * (_nt(dob, vp_ref[h]) + cf)
            dsc = pc * (_nt(dob, vc_ref[h]) + cf)
            dq_ref[h] = (jnp.dot(dsp.astype(BF16), kp_ref[h], preferred_element_type=F32)
                         + jnp.dot(dsc.astype(BF16), kc_ref[h], preferred_element_type=F32)) * scale

    cur = BS((hpg, DIL_SPAN, Dh), lambda g, m: (g, m, 0))
    prev = BS((hpg, DIL_SPAN, Dh), lambda g, m: (g, jnp.maximum(m - 1, 0), 0))
    col = BS((hpg, DIL_SPAN, 1), lambda g, m: (g, m, 0))
    return _call(
        body, name=name, grid=(HH // hpg, nb), out_shape=SDS((HH, S, Dh), F32),
        in_specs=[cur, prev, cur, prev, cur, cur, col, col], out_specs=cur,
        sem=("parallel", "parallel"), args=(q, k, k, v, v, do, lse, coef), comm=comm)


def _dil_bwd_dkv(q, k, v, do, lse, coef, bpcs, hpg, name, comm=None):
    HH, S, Dh = q.shape
    nb = S // DIL_SPAN
    scale = Dh ** -0.5

    def body(k_ref, v_ref, qc_ref, qn_ref, doc_ref, don_ref, lc_ref, ln_ref, cc_ref, cn_ref, dk_ref, dv_ref):
        hh, m = pl.program_id(0) * hpg, pl.program_id(1)
        _, mc = _dil_masks(hh, m, bpcs, hpg)
        mp, _ = _dil_masks(hh, m + 1, bpcs, hpg)
        mp = mp & (m + 1 < nb)
        for h in range(hpg):
            kv, vv = k_ref[h], v_ref[h]
            pc = jnp.exp(jnp.where(mc, _nt(qc_ref[h], kv) * scale, NEG) - lc_ref[h])
            pn = jnp.exp(jnp.where(mp, _nt(qn_ref[h], kv) * scale, NEG) - ln_ref[h])
            dsc = pc * (_nt(doc_ref[h], vv) + cc_ref[h])
            dsn = pn * (_nt(don_ref[h], vv) + cn_ref[h])
            dv_ref[h] = _tn(pc.astype(BF16), doc_ref[h]) + _tn(pn.astype(BF16), don_ref[h])
            dk_ref[h] = (_tn(dsc.astype(BF16), qc_ref[h]) + _tn(dsn.astype(BF16), qn_ref[h])) * scale

    cur = BS((hpg, DIL_SPAN, Dh), lambda g, m: (g, m, 0))
    nxt = BS((hpg, DIL_SPAN, Dh), lambda g, m: (g, jnp.minimum(m + 1, nb - 1), 0))
    col = BS((hpg, DIL_SPAN, 1), lambda g, m: (g, m, 0))
    ncol = BS((hpg, DIL_SPAN, 1), lambda g, m: (g, jnp.minimum(m + 1, nb - 1), 0))
    return _call(
        body, name=name, grid=(HH // hpg, nb), out_shape=(SDS((HH, S, Dh), F32), SDS((HH, S, Dh), F32)),
        in_specs=[cur, cur, cur, nxt, cur, nxt, col, ncol, col, ncol], out_specs=(cur, cur),
        sem=("parallel", "parallel"), args=(k, v, q, q, do, do, lse, lse, coef, coef), comm=comm)


def _dil_merge(o3, lse3, name):
    G, H, S, Dh = o3.shape
    ts = _tile(S, 512, 8)

    def body(o_ref, l_ref, out_ref):
        lv = [l_ref[g, 0] for g in range(G)]
        mx = functools.reduce(jnp.maximum, lv)
        ex = [jnp.exp(v - mx) for v in lv]
        den = functools.reduce(lambda a, b: a + b, ex)
        out_ref[0] = functools.reduce(lambda a, b: a + b, [(ex[g] / den) * o_ref[g, 0] for g in range(G)])

    return pl.pallas_call(
        body, name=name, grid=(H, S // ts), out_shape=SDS((H, S, Dh), F32),
        in_specs=[BS((G, 1, ts, Dh), lambda h, i: (0, h, i, 0)), BS((G, 1, ts, 1), lambda h, i: (0, h, i, 0))],
        out_specs=BS((1, ts, Dh), lambda h, i: (h, i, 0)), compiler_params=_cp("parallel", "parallel"))(o3, lse3)


def _dil_merge_bwd(o3, lse3, do, name):
    G, H, S, Dh = o3.shape
    ts = _tile(S, 512, 8)

    def body(o_ref, l_ref, do_ref, do3_ref, cf_ref):
        lv = [l_ref[g, 0] for g in range(G)]
        mx = functools.reduce(jnp.maximum, lv)
        ex = [jnp.exp(v - mx) for v in lv]
        den = functools.reduce(lambda a, b: a + b, ex)
        wt = [e / den for e in ex]
        dov = do_ref[0]
        dw = [jnp.sum(dov * o_ref[g, 0], axis=-1, keepdims=True) for g in range(G)]
        mean = functools.reduce(lambda a, b: a + b, [wt[g] * dw[g] for g in range(G)])
        for g in range(G):
            do3_ref[g, 0] = (wt[g] * dov).astype(BF16)
            cf_ref[g, 0] = wt[g] * (dw[g] - mean) - wt[g] * dw[g]

    o_spec = BS((G, 1, ts, Dh), lambda h, i: (0, h, i, 0))
    l_spec = BS((G, 1, ts, 1), lambda h, i: (0, h, i, 0))
    return pl.pallas_call(
        body, name=name, grid=(H, S // ts), out_shape=(SDS((G, H, S, Dh), BF16), SDS((G, H, S, 1), F32)),
        in_specs=[o_spec, l_spec, BS((1, ts, Dh), lambda h, i: (h, i, 0))], out_specs=(o_spec, l_spec),
        compiler_params=_cp("parallel", "parallel"))(o3, lse3, do)


def _merge_fwd(proj, gate_col, os_, ws, name):
    S = proj.shape[0]
    D = ws[0].shape[1]
    nbr = len(os_)
    ts = _tile(S, 512, 8)
    tn = LANES * 2 if gate_col % (2 * LANES) == 0 and D % (2 * LANES) == 0 else LANES
    g0, gstep = gate_col // tn, D // tn

    def body(*refs):
        p_refs, o_refs, w_refs, out_ref = refs[:nbr], refs[nbr:2 * nbr], refs[2 * nbr:3 * nbr], refs[-1]
        acc = None
        for i in range(nbr):
            t = _sigmoid(p_refs[i][...]) * jnp.dot(o_refs[i][...], w_refs[i][...], preferred_element_type=F32)
            acc = t if acc is None else acc + t
        out_ref[...] = acc.astype(BF16)

    specs = ([BS((ts, tn), lambda s, j, i=i: (s, g0 + i * gstep + j)) for i in range(nbr)]
             + [BS((ts, o.shape[1]), lambda s, j: (s, 0)) for o in os_]
             + [BS((w.shape[0], tn), lambda s, j: (0, j)) for w in ws])
    return pl.pallas_call(
        body, name=name, grid=(S // ts, D // tn), out_shape=SDS((S, D), BF16), in_specs=specs,
        out_specs=BS((ts, tn), lambda s, j: (s, j)), compiler_params=_cp("parallel", "parallel"),
    )(*([proj] * nbr), *os_, *ws)


def _merge_bwd(proj, gate_col, os_, ws, dm, name):
    S = proj.shape[0]
    D = ws[0].shape[1]
    nbr = len(os_)
    ts = _tile(S, 512, 8)
    tn = LANES * 2 if gate_col % (2 * LANES) == 0 and D % (2 * LANES) == 0 else LANES
    g0, gstep = gate_col // tn, D // tn

    def body(*refs):
        p_refs, o_refs, w_refs = refs[:nbr], refs[nbr:2 * nbr], refs[2 * nbr:3 * nbr]
        dmv = refs[3 * nbr][...]
        dbo_refs, dgl_refs = refs[3 * nbr + 1:4 * nbr + 1], refs[4 * nbr + 1:]
        for i in range(nbr):
            gt = _sigmoid(p_refs[i][...])
            bo = jnp.dot(o_refs[i][...], w_refs[i][...], preferred_element_type=F32)
            dbo_refs[i][...] = (dmv * gt).astype(BF16)
            dgl_refs[i][...] = (dmv * bo * gt * (1.0 - gt)).astype(BF16)

    tile = BS((ts, tn), lambda s, j: (s, j))
    specs = ([BS((ts, tn), lambda s, j, i=i: (s, g0 + i * gstep + j)) for i in range(nbr)]
             + [BS((ts, o.shape[1]), lambda s, j: (s, 0)) for o in os_]
             + [BS((w.shape[0], tn), lambda s, j: (0, j)) for w in ws] + [tile])
    outs = pl.pallas_call(
        body, name=name, grid=(S // ts, D // tn), out_shape=(SDS((S, D), BF16),) * (2 * nbr), in_specs=specs,
        out_specs=(tile,) * (2 * nbr), compiler_params=_cp("parallel", "parallel"),
    )(*([proj] * nbr), *os_, *ws, dm)
    return outs[:nbr], outs[nbr:]


def _ffn_act(up, w, name):
    S, F2 = up.shape
    F = F2 // 2
    tc = LANES
    nb = F // tc

    def body(g_ref, v_ref, wg_ref, wv_ref, a_ref):
        ug = _conv3(g_ref[...], wg_ref)
        a_ref[...] = (ug * _sigmoid(ug) * _conv3(v_ref[...], wv_ref)).astype(BF16)

    return pl.pallas_call(
        body, name=name, grid=(nb,), out_shape=SDS((S, F), BF16),
        in_specs=[BS((S, tc), lambda j: (0, j)), BS((S, tc), lambda j: (0, nb + j)),
                  BS((3, tc), lambda j: (0, j)), BS((3, tc), lambda j: (0, nb + j))],
        out_specs=BS((S, tc), lambda j: (0, j)), compiler_params=_cp("parallel"))(up, up, w, w)


def _ffn_act_bwd(up, w, da, name):
    S, F2 = up.shape
    F = F2 // 2
    tc = LANES
    nb = F // tc

    def body(g_ref, v_ref, wg_ref, wv_ref, da_ref, dg_ref, dv_ref, dwg_ref, dwv_ref):
        pg, pv, dav = g_ref[...], v_ref[...], da_ref[...].astype(F32)
        ug, uv = _conv3(pg, wg_ref), _conv3(pv, wv_ref)
        sg = _sigmoid(ug)
        dg_ref[...] = _conv3_bwd(pg, dav * uv * (sg * (1.0 + ug * (1.0 - sg))), wg_ref, dwg_ref).astype(BF16)
        dv_ref[...] = _conv3_bwd(pv, dav * ug * sg, wv_ref, dwv_ref).astype(BF16)

    col = BS((S, tc), lambda j: (0, j))
    wsp = BS((3, tc), lambda j: (0, j))
    return pl.pallas_call(
        body, name=name, grid=(nb,),
        out_shape=(SDS((S, F), BF16), SDS((S, F), BF16), SDS((3, F), F32), SDS((3, F), F32)),
        in_specs=[col, BS((S, tc), lambda j: (0, nb + j)), wsp, BS((3, tc), lambda j: (0, nb + j)), col],
        out_specs=(col, col, wsp, wsp), compiler_params=_cp("parallel"))(up, up, w, w, da)


def _ple_fwd(x, pgl, pe, name):
    S, D = x.shape
    ts = _tile(S, 256, 8)

    def body(x_ref, g_ref, e_ref, o_ref):
        o_ref[...] = x_ref[...] + _sigmoid(g_ref[...]) * e_ref[...]

    row = BS((ts, D), lambda i: (i, 0))
    return pl.pallas_call(body, name=name, grid=(S // ts,), out_shape=SDS((S, D), F32), in_specs=[row] * 3,
                          out_specs=row, compiler_params=_cp("parallel"))(x, pgl, pe)


def _ple_bwd(dx, pgl, pe, name):
    S, D = dx.shape
    ts = _tile(S, 256, 8)

    def body(dx_ref, g_ref, e_ref, dg_ref, de_ref):
        dxv, sg = dx_ref[...], _sigmoid(g_ref[...])
        dg_ref[...] = (dxv * e_ref[...] * sg * (1.0 - sg)).astype(BF16)
        de_ref[...] = (dxv * sg).astype(BF16)

    row = BS((ts, D), lambda i: (i, 0))
    return pl.pallas_call(body, name=name, grid=(S // ts,), out_shape=(SDS((S, D), BF16),) * 2, in_specs=[row] * 3,
                          out_specs=(row, row), compiler_params=_cp("parallel"))(dx, pgl, pe)


def _adamw(w, g, m, v, name):
    shape = w.shape
    cols = shape[-1]
    rows = w.size // cols
    tr = _tile(rows, 256, 8)
    c1, c2 = 1.0 / (1.0 - ADAM_B1 ** ADAM_STEP), 1.0 / (1.0 - ADAM_B2 ** ADAM_STEP)

    def body(w_ref, g_ref, m_ref, v_ref, d_ref, nm_ref, nv_ref):
        gv = g_ref[...]
        nm = ADAM_B1 * m_ref[...] + (1.0 - ADAM_B1) * gv
        nv = ADAM_B2 * v_ref[...] + (1.0 - ADAM_B2) * (gv * gv)
        d_ref[...] = -ADAM_LR * ((nm * c1) / (jnp.sqrt(nv * c2) + ADAM_EPS) + ADAM_WD * w_ref[...])
        nm_ref[...] = nm
        nv_ref[...] = nv

    blk = BS((tr, cols), lambda i: (i, 0))
    outs = pl.pallas_call(
        body, name=name, grid=(rows // tr,), out_shape=(SDS((rows, cols), F32),) * 3, in_specs=[blk] * 4,
        out_specs=(blk,) * 3, compiler_params=_cp("parallel"),
    )(*(a.reshape(rows, cols) for a in (w, g, m, v)))
    return tuple(o.reshape(shape) for o in outs)


SHARDED = ("w_in", "shortconv_w", "w_br_fox", "w_br_conv", "w_br_sgu", "w_br_dil", "w_out", "w_up", "ffn_conv_w",
           "w_down", "w_ple_gate", "w_ple_proj")
REPLICATED = ("norm_mix_g", "fox_forget_b", "sgu_norm_g", "sgu_w", "sgu_b", "norm_ffn_g", "norm_ple_g")
WEIGHTS = ("norm_mix_g", "w_in", "fox_forget_b", "shortconv_w", "sgu_norm_g", "sgu_w", "sgu_b", "w_br_fox",
           "w_br_conv", "w_br_sgu", "w_br_dil", "w_out", "norm_ffn_g", "w_up", "ffn_conv_w", "w_down", "norm_ple_g",
           "w_ple_gate", "w_ple_proj", "final_norm_g")
PACK_COLS = 1024
PACK_ROWS = 256


def _pack(parts, dtype, unit):
    lead = parts[0].shape[:-1] if parts[0].ndim > 1 else ()
    flat = jnp.concatenate([p.astype(dtype) for p in parts], axis=-1)
    n = flat.shape[-1]
    rows = -(-n // (PACK_COLS * unit)) * unit
    flat = jnp.pad(flat, [(0, 0)] * len(lead) + [(0, rows * PACK_COLS - n)])
    return flat.reshape(lead + (rows, PACK_COLS))


def _to_heads(x, heads):
    S = x.shape[0]
    return x.reshape(S, heads, HEAD_DIM).transpose(1, 0, 2)


def _from_heads(x):
    H, S, Dh = x.shape
    return x.transpose(1, 0, 2).reshape(S, H * Dh)


def _to_residue(x, hpg):
    S = x.shape[0]
    gw = hpg * HEAD_DIM
    outs = []
    for g, (_, d) in enumerate(DIL_PATTERNS):
        seg = x[:, g * gw:(g + 1) * gw].reshape(S // d, d, hpg, HEAD_DIM)
        outs.append(seg.transpose(2, 1, 0, 3).reshape(hpg, S, HEAD_DIM))
    return jnp.concatenate(outs, axis=0)


def _from_residue(x, hpg):
    _, S, C = x.shape
    outs = []
    for g, (_, d) in enumerate(DIL_PATTERNS):
        seg = x[g * hpg:(g + 1) * hpg].reshape(hpg, d, S // d, C)
        outs.append(seg.transpose(0, 2, 1, 3).reshape(hpg, S, C))
    return jnp.stack(outs, axis=0)


def _groups_to_residue(x):
    G, hpg, S, C = x.shape
    outs = []
    for g, (_, d) in enumerate(DIL_PATTERNS):
        outs.append(x[g].reshape(hpg, S // d, d, C).transpose(0, 2, 1, 3).reshape(hpg, S, C))
    return jnp.concatenate(outs, axis=0)


def kernel(x, p, positions, norm_mix_g, w_in, fox_forget_b, shortconv_w, sgu_norm_g, sgu_w, sgu_b, w_br_fox, w_br_conv, w_br_sgu, w_br_dil, w_out, norm_ffn_g, w_up, ffn_conv_w, w_down, norm_ple_g, w_ple_gate, w_ple_proj, final_norm_g, loss_target, m_norm_mix_g, m_w_in, m_fox_forget_b, m_shortconv_w, m_sgu_norm_g, m_sgu_w, m_sgu_b, m_w_br_fox, m_w_br_conv, m_w_br_sgu, m_w_br_dil, m_w_out, m_norm_ffn_g, m_w_up, m_ffn_conv_w, m_w_down, m_norm_ple_g, m_w_ple_gate, m_w_ple_proj, m_final_norm_g, v_norm_mix_g, v_w_in, v_fox_forget_b, v_shortconv_w, v_sgu_norm_g, v_sgu_w, v_sgu_b, v_w_br_fox, v_w_br_conv, v_w_br_sgu, v_w_br_dil, v_w_out, v_norm_ffn_g, v_w_up, v_ffn_conv_w, v_w_down, v_norm_ple_g, v_w_ple_gate, v_w_ple_proj, v_final_norm_g):
    given = dict(locals())
    W = {n: given[n] for n in WEIGHTS}
    depth = w_in.shape[0]
    S, D = x.shape[1], x.shape[2]
    x0, target, pos = x[0], loss_target[0], positions[0].reshape(S, 1)
    FH = fox_forget_b.shape[1]
    FW = FH * HEAD_DIM
    CW = shortconv_w.shape[2] * N_DEV
    G = sgu_w.shape[1]
    SW = G * sgu_w.shape[2]
    DOUT = w_br_dil.shape[1]
    hpg = DOUT // HEAD_DIM
    NG = len(DIL_PATTERNS)
    DW = NG * DOUT
    bpcs = tuple(S // d // DIL_SPAN for _, d in DIL_PATTERNS)
    A0, B0 = 0, 3 * FW
    C0 = B0 + 3 * CW
    D0 = C0 + 2 * SW
    G0 = D0 + 3 * DW
    F0 = G0 + 4 * D
    NP = F0 + LANES
    orig = [3 * FW, FH, 3 * CW, 2 * SW, 3 * DW, 4 * D]
    o_off = [sum(orig[:i]) for i in range(len(orig) + 1)]

    me = 4 * lax.axis_index("x") + 2 * lax.axis_index("y") + lax.axis_index("c")
    n_in, o1, o2, d_in = w_in.shape[2], o_off[1], o_off[2], o_off[6]
    NPB = NP // LANES
    WB = -(-(LANES - 1 + n_in) // LANES)
    PW = (WB + 1) * LANES

    def a_of(j, clip=lambda v: min(max(v, 0), FH)):
        return n_in * j - clip(n_in * j - o1)

    def s_of(j):
        return min(a_of(j) // LANES, NPB - WB)

    def s_traced(j):
        return jnp.minimum(a_of(j, lambda v: jnp.clip(v, 0, FH)) // LANES, NPB - WB)

    def runs(j):
        lo, hi, out = n_in * j, n_in * (j + 1), []
        for a, b in ((0, o1), (o1, o2), (o2, d_in)):
            l, h = max(lo, a), min(hi, b)
            if l < h:
                dst = WB * LANES + l - o1 if a == o1 else l - (FH if l >= o2 else 0) - s_of(j) * LANES
                out.append((l - lo, h - l, dst))
        return out

    def to_window(j, shard):
        piece = jnp.zeros((shard.shape[0], PW), shard.dtype)
        for src, ln, dst in runs(j):
            piece = piece.at[:, dst:dst + ln].set(shard[:, src:src + ln])
        return piece

    def from_window(j, win):
        return jnp.concatenate([win[:, dst:dst + ln] for _, ln, dst in runs(j)], axis=1)

    def on_my_index(fn, arg):
        return lax.switch(me, [functools.partial(fn, j) for j in range(N_DEV)], arg)

    def place(axis, length):
        def view(ref, j):
            align = LANES if axis == len(ref.shape) - 1 else 16
            start = pl.multiple_of(j * length, align) if length % align == 0 else j * length
            return ref.at[tuple([slice(None)] * axis + [pl.ds(start, length)])]
        return view

    def slot(ref, j):
        return ref.at[j]

    def whole(ref, j):
        return ref

    n_up, r_down, r_sq = w_up.shape[2], w_down.shape[1], w_out.shape[1]
    n_br, n_sm, n_sc = w_br_fox.shape[2], w_br_dil.shape[2], shortconv_w.shape[2]
    FF2 = n_up * N_DEV
    cw_pad = -(-n_up // LANES) * LANES
    placed = [place(1, n_up), place(0, r_down), place(1, r_sq), place(2, n_br), place(2, n_sm)]

    def gather_plans(i):
        conv = jnp.zeros((8, cw_pad), F32).at[0:3, :n_up].set(ffn_conv_w[i]).at[3:6, :n_sc].set(shortconv_w[i])
        srcs = [on_my_index(to_window, w_in[i].astype(BF16)), w_up[i].astype(BF16), w_down[i].astype(BF16),
                jnp.stack([w_out[i], w_ple_gate[i]]).astype(BF16),
                jnp.stack([w_br_fox[i], w_br_conv[i], w_br_sgu[i]]).astype(BF16),
                jnp.stack([w_br_dil[i], w_ple_proj[i]]).astype(BF16), conv]
        shapes = [SDS((N_DEV, D, PW), BF16), SDS((D, FF2), BF16), SDS((r_down * N_DEV, D), BF16),
                  SDS((2, r_sq * N_DEV, D), BF16), SDS((3, w_br_fox.shape[1], n_br * N_DEV), BF16),
                  SDS((2, w_br_dil.shape[1], n_sm * N_DEV), BF16), SDS((N_DEV, 8, cw_pad), F32)]
        dsts = [slot] + placed + [slot]
        return [(srcs[a:b], shapes[a:b], [whole] * (b - a), dsts[a:b], True)
                for a, b in ((0, 1), (1, 2), (2, 3), (3, 7))]

    def gathered(parts):
        (pieces,), (wup,), (wdown,), (sq, br, sm, cv) = parts
        return {
            "w_in": _assemble(pieces, [s_of(j) for j in range(N_DEV)], WB, NPB, "assemble_w_in"),
            "w_up": wup, "w_down": wdown, "w_out": sq[0], "w_ple_gate": sq[1],
            "w_br_fox": br[0], "w_br_conv": br[1], "w_br_sgu": br[2], "w_br_dil": sm[0], "w_ple_proj": sm[1],
            "ffn_conv_w": jnp.moveaxis(cv[:, 0:3, :n_up], 0, 1).reshape(3, FF2),
            "shortconv_w": jnp.moveaxis(cv[:, 3:6, :n_sc], 0, 1).reshape(3, CW),
        }

    def one_plan(plans):
        return tuple(sum((list(plan[k]) for plan in plans), []) for k in range(4)) + (plans[0][4],)

    first = _comm("comm_gather_weights", *one_plan(gather_plans(0)))
    full = [gathered([first[0:1], first[1:2], first[2:3], first[3:7]])]

    inv = ROPE_THETA ** (-jnp.arange(ROPE_DIM // 2, dtype=F32) * (2.0 / ROPE_DIM))
    inv_row = jnp.tile(jnp.concatenate([inv, inv, jnp.zeros((HEAD_DIM - ROPE_DIM,), F32)]), LANES // HEAD_DIM)
    tables = _rope_tables(pos, inv_row.reshape(1, LANES), "rope_tables")

    def pad_lanes(v):
        return jnp.pad(v, ((0, 0), (0, LANES - v.shape[1])))

    saved = []
    xc = x0
    for i in range(depth):
        fw, sv = full[i], {}
        sv["x"] = xc
        nxt = gather_plans(i + 1) if i + 1 < depth else [None] * 4
        got = [None] * 4

        def host(k, result, nxt=nxt, got=got):
            if nxt[k] is None:
                return result
            got[k] = result[1]
            return result[0]

        h1 = _rms_fwd(xc, norm_mix_g[i], "rms_mix")
        proj = host(1, _mm(h1, fw["w_in"], "nn", F32, "mm_in", comm=nxt[1]))
        sv["h1"], sv["proj"] = h1, proj
        bias = pad_lanes(fox_forget_b[i].reshape(1, FH))
        Fp = _fox_prep(proj, bias, F0 // LANES, "fox_prep")[:, :, :FH].transpose(2, 1, 0)
        t = _tile(S, 256)
        qa, ka, va = (_to_heads(proj[:, A0 + j * FW:A0 + (j + 1) * FW], FH).astype(BF16) for j in range(3))
        ones3, zpad = jnp.ones((FH, S, 3), BF16), jnp.zeros((FH, S, HEAD_DIM - 6), BF16)
        q_aug = jnp.concatenate([qa, Fp, ones3, zpad], axis=-1)
        k_aug = jnp.concatenate([ka, ones3, -Fp, zpad], axis=-1)

        def blocks_t(a):
            return a.reshape(FH, S // t, t, a.shape[-1]).transpose(0, 1, 3, 2)

        kT = blocks_t(k_aug)
        oaT, lse_a = host(0, _foxt_fwd(q_aug, k_aug, blocks_t(va), "fox_fwd", comm=nxt[0]))
        oa_b = oaT.transpose(2, 0, 1).reshape(S, FW).astype(BF16)
        sv.update(bias=bias, q_aug=q_aug, k_aug=k_aug, kT=kT, va=va, lse_a=lse_a, oa_b=oa_b)
        ob = _sconv_fwd(proj, fw["shortconv_w"], B0, CW, "sconv_fwd")
        sgb = sgu_b[i].reshape(G, SGU_CHUNK, 1)
        oc = _sgu_fwd(proj, sgu_norm_g[i].reshape(1, SW), sgu_w[i], sgb, C0, SW, "sgu_fwd")
        qd = _rope_apply(proj, D0, DW, tables, 1.0, BF16, "rope_q")
        kd = _rope_apply(proj, D0 + DW, DW, tables, 1.0, BF16, "rope_k")
        qr, kr = _to_residue(qd, hpg), _to_residue(kd, hpg)
        vr = _to_residue(proj[:, D0 + 2 * DW:D0 + 3 * DW].astype(BF16), hpg)
        od_r, lse_r = host(2, _dil_fwd(qr, kr, vr, bpcs, hpg, "dil_fwd", comm=nxt[2]))
        o3, lse3 = _from_residue(od_r, hpg), _from_residue(lse_r, hpg)
        od = _from_heads(_dil_merge(o3, lse3, "dil_merge")).astype(BF16)
        sv.update(ob=ob, oc=oc, sgb=sgb, qr=qr, kr=kr, vr=vr, lse_r=lse_r, o3=o3, lse3=lse3, od=od)
        brs = [oa_b, ob, oc, od]
        wbr = [fw["w_br_fox"], fw["w_br_conv"], fw["w_br_sgu"], fw["w_br_dil"]]
        merged = _merge_fwd(proj, G0, brs, wbr, "merge_fwd")
        x2 = _mm(merged, fw["w_out"], "nn", F32, "mm_out", res=xc)
        h2 = _rms_fwd(x2, norm_ffn_g[i], "rms_ffn")
        up = host(3, _mm(h2, fw["w_up"], "nn", F32, "mm_up", comm=nxt[3]))
        if i + 1 < depth:
            full.append(gathered(got))
        act = _ffn_act(up, fw["ffn_conv_w"], "ffn_act")
        x3 = _mm(act, fw["w_down"], "nn", F32, "mm_down", res=x2)
        h3 = _rms_fwd(x3, norm_ple_g[i], "rms_ple")
        pgl = _mm(h3, fw["w_ple_gate"], "nn", F32, "mm_ple_gate")
        pb = p[i, 0].astype(BF16)
        pe = _mm(pb, fw["w_ple_proj"], "nn", F32, "mm_ple_proj")
        xc = _ple_fwd(x3, pgl, pe, "ple_fwd")
        sv.update(merged=merged, x2=x2, h2=h2, up=up, act=act, x3=x3, h3=h3, pgl=pgl, pb=pb, pe=pe)
        saved.append(sv)

    loss_row, dx, dxb, dg_final = _final_loss(xc, final_norm_g, target, "final_loss")
    loss = lax.psum(loss_row[0, 0], MESH_AXES)

    def in_window(half):
        def view(ref, j):
            return ref.at[pl.ds(half * (D // 2), D // 2),
                          pl.ds(pl.multiple_of(s_traced(j) * LANES, LANES), WB * LANES)]
        return view

    def in_flogit(ref, j):
        return ref.at[:, pl.ds(F0, LANES)]

    def scatter_plans(gs, parts=(0, 1, 2, 3, 4, 5)):
        stacked = lambda shape, dt=BF16: SDS((N_DEV,) + tuple(shape), dt)

        def build(part):
            if part == 0:
                srcs, views = [gs["w_in"], gs["w_in"]], [in_window(0), in_flogit]
                shapes = [stacked((D // 2, WB * LANES)), stacked((D, LANES))]
            elif part == 1:
                srcs, views, shapes = [gs["w_up"]], [placed[0]], [stacked((D, n_up))]
            elif part == 2:
                srcs, views, shapes = [gs["w_down"]], [placed[1]], [stacked((r_down, D))]
            elif part == 3:
                srcs, views = [jnp.stack([gs["w_out"], gs["w_ple_gate"]])], [placed[2]]
                shapes = [stacked((2, r_sq, D))]
            elif part == 5:
                conv = jnp.zeros((N_DEV, 8, cw_pad), F32)
                conv = conv.at[:, 0:3, :n_up].set(jnp.moveaxis(gs["ffn_conv_w"].reshape(3, N_DEV, n_up), 1, 0))
                conv = conv.at[:, 3:6, :n_sc].set(jnp.moveaxis(gs["shortconv_w"].reshape(3, N_DEV, n_sc), 1, 0))
                srcs = [jnp.stack([gs["w_br_fox"], gs["w_br_conv"], gs["w_br_sgu"]]),
                        jnp.stack([gs["w_br_dil"], gs["w_ple_proj"]]), conv]
                views = placed[3:] + [slot]
                shapes = [stacked((3, w_br_fox.shape[1], n_br)), stacked((2, w_br_dil.shape[1], n_sm)),
                          stacked((8, cw_pad), F32)]
            else:
                srcs, views, shapes = [gs["w_in"]], [in_window(1)], [stacked((D // 2, WB * LANES))]
            return srcs, shapes, views, [slot] * len(srcs), False

        return [build(part) for part in parts]

    grads_sh = [None] * depth
    grads_rep = [None] * depth
    scattered = [None] * depth
    for i in reversed(range(depth)):
        fw, sv = full[i], saved[i]
        gs, gr = {}, {}
        pend = scatter_plans(grads_sh[i + 1]) if i + 1 < depth else [None] * 6
        got = [None] * 6

        def host(k, result, pend=pend, got=got):
            if pend[k] is None:
                return result
            got[k] = result[1]
            return result[0]

        proj = sv["proj"]
        dpgl, dpe = _ple_bwd(dx, sv["pgl"], sv["pe"], "ple_bwd")
        gs["w_ple_proj"] = _mm(sv["pb"], dpe, "tn", BF16, "mm_dw_ple_proj")
        gs["w_ple_gate"] = _mm(sv["h3"], dpgl, "tn", BF16, "mm_dw_ple_gate")
        dh3 = _mm(dpgl, fw["w_ple_gate"], "nt", F32, "mm_dh3")
        dx, dxb, gr["norm_ple_g"] = _rms_bwd(sv["x3"], norm_ple_g[i], dh3, dx, "rms_ple_bwd")
        gs["w_down"] = _mm(sv["act"], dxb, "tn", BF16, "mm_dw_down")
        dact = _mm(dxb, fw["w_down"], "nt", BF16, "mm_dact")
        dug, duv, dwg, dwv = _ffn_act_bwd(sv["up"], fw["ffn_conv_w"], dact, "ffn_act_bwd")
        gs["ffn_conv_w"] = jnp.concatenate([dwg, dwv], axis=1)
        dup = jnp.concatenate([dug, duv], axis=1)
        gs["w_up"] = host(2, _mm(sv["h2"], dup, "tn", BF16, "mm_dw_up", comm=pend[2]))
        own_down, own_up = scatter_plans(gs, (2, 1)) if i == 0 else (None, None)
        dh2 = _mm(dup, fw["w_up"], "nt", F32, "mm_dh2", comm=own_down)
        if i == 0:
            dh2, own_down = dh2
        dx, dxb, gr["norm_ffn_g"] = _rms_bwd(sv["x2"], norm_ffn_g[i], dh2, dx, "rms_ffn_bwd")
        gs["w_out"] = _mm(sv["merged"], dxb, "tn", BF16, "mm_dw_out")
        dmerged = _mm(dxb, fw["w_out"], "nt", F32, "mm_dmerged")
        brs = [sv["oa_b"], sv["ob"], sv["oc"], sv["od"]]
        names = ["w_br_fox", "w_br_conv", "w_br_sgu", "w_br_dil"]
        wbr = [fw[n] for n in names]
        dbo, dgl = _merge_bwd(proj, G0, brs, wbr, dmerged, "merge_bwd")
        dos = []
        for j, n in enumerate(names):
            gs[n] = _mm(brs[j], dbo[j], "tn", BF16, "mm_dw_" + n)
            dos.append(_mm(dbo[j], wbr[j], "nt", F32, "mm_do_" + n))
        doa = _to_heads(dos[0], FH).astype(BF16)
        dqT, delta = host(1, _foxt_bwd_dq(sv["q_aug"], sv["k_aug"], sv["kT"], sv["va"], doa, sv["lse_a"],
                                          "fox_bwd_dq", comm=pend[1]))
        dk_aug, dva, dfk = host(0, _foxt_bwd_dkv(sv["q_aug"], sv["k_aug"], sv["va"], doa, sv["lse_a"], delta,
                                                 "fox_bwd_dkv", comm=pend[0]))
        dF = pad_lanes(dfk[:, :, 0].T)
        daf, dbias = _fox_prep_bwd(proj, sv["bias"], F0 // LANES, dF, "fox_prep_bwd")
        gr["fox_forget_b"] = dbias[0, :FH]
        d_a = [dqT[:, :HEAD_DIM].transpose(2, 0, 1).reshape(S, FW).astype(BF16),
               _from_heads(dk_aug[:, :, :HEAD_DIM]).astype(BF16), _from_heads(dva).astype(BF16)]
        dxb_, dgb, dgc, gs["shortconv_w"] = _sconv_bwd(proj, fw["shortconv_w"], B0, CW, dos[1], "sconv_bwd")
        du, dv, dsw, dsb, dsg = _sgu_bwd(proj, sgu_norm_g[i].reshape(1, SW), sgu_w[i], sv["sgb"], C0, SW, dos[2],
                                         "sgu_bwd")
        gr["sgu_w"], gr["sgu_b"], gr["sgu_norm_g"] = dsw, dsb.reshape(G, SGU_CHUNK), dsg[0]
        do3, cf3 = _dil_merge_bwd(sv["o3"], sv["lse3"], _to_heads(dos[3], hpg), "dil_merge_bwd")
        do_r, cf_r = _groups_to_residue(do3), _groups_to_residue(cf3)
        dq_r = host(5, _dil_bwd_dq(sv["qr"], sv["kr"], sv["vr"], do_r, sv["lse_r"], cf_r, bpcs, hpg, "dil_bwd_dq",
                                   comm=pend[5]))
        dk_r, dv_r = host(3, _dil_bwd_dkv(sv["qr"], sv["kr"], sv["vr"], do_r, sv["lse_r"], cf_r, bpcs, hpg,
                                          "dil_bwd_dkv", comm=pend[3]))

        def natural(t):
            return _from_heads(_from_residue(t, hpg).reshape(NG * hpg, S, HEAD_DIM))

        dqd = _rope_apply(natural(dq_r), 0, DW, tables, -1.0, BF16, "rope_q_bwd")
        dkd = _rope_apply(natural(dk_r), 0, DW, tables, -1.0, BF16, "rope_k_bwd")
        dvd = natural(dv_r).astype(BF16)
        dproj = jnp.concatenate(d_a + [dxb_, dgb, dgc, du, dv, dqd, dkd, dvd] + list(dgl) + [daf], axis=1)
        gs["w_in"] = _mm(sv["h1"], dproj, "tn", BF16, "mm_dw_in", comm=own_up)
        if i == 0:
            gs["w_in"], own_up = gs["w_in"]
        dh1 = host(4, _mm(dproj, fw["w_in"], "nt", F32, "mm_dh1", comm=pend[4]))
        if i + 1 < depth:
            scattered[i + 1] = got
        dx, dxb, gr["norm_mix_g"] = _rms_bwd(sv["x"], norm_mix_g[i], dh1, dx, "rms_mix_bwd")
        gr["norm_mix_g"], gr["norm_ffn_g"], gr["norm_ple_g"] = (gr[n][0] for n in
                                                               ("norm_mix_g", "norm_ffn_g", "norm_ple_g"))
        grads_sh[i], grads_rep[i] = gs, gr
    grad_x = dx.reshape(1, S, D)

    last = _comm("comm_scatter_grads", *one_plan(scatter_plans(grads_sh[0], (0, 3, 4, 5))))
    scattered[0] = [last[0:2], own_up, own_down, last[2:3], last[3:4], last[4:7]]

    def sum_sources(t, name):
        lead = t.shape[1:]
        return _sum8(t.reshape(N_DEV, -1, lead[-1]), name).reshape(lead)

    grads = {n: [None] * depth for n in SHARDED}
    for i in range(depth):
        (gw, gf), (gup,), (gdown,), (gsq,), (gw2,), (gbr, gsm, gcv) = scattered[i]
        rows = jnp.concatenate([sum_sources(gw, "sum_w_in"), sum_sources(gw2, "sum_w_in")], axis=0)
        win = jnp.concatenate([rows, sum_sources(gf, "sum_w_in_flogit")], axis=1)
        grads["w_in"][i] = on_my_index(from_window, win)
        grads["w_up"][i] = sum_sources(gup, "sum_w_up")
        grads["w_down"][i] = sum_sources(gdown, "sum_w_down")
        sq, br, sm = sum_sources(gsq, "sum_sq"), sum_sources(gbr, "sum_br"), sum_sources(gsm, "sum_sm")
        cv = sum_sources(gcv, "sum_conv")
        grads["w_out"][i], grads["w_ple_gate"][i] = sq[0], sq[1]
        grads["w_br_fox"][i], grads["w_br_conv"][i], grads["w_br_sgu"][i] = br[0], br[1], br[2]
        grads["w_br_dil"][i], grads["w_ple_proj"][i] = sm[0], sm[1]
        grads["ffn_conv_w"][i], grads["shortconv_w"][i] = cv[0:3, :n_up], cv[3:6, :n_sc]
    grads = {n: jnp.stack(v, axis=0) for n, v in grads.items()}

    rep_parts = [jnp.stack([grads_rep[i][n] for i in range(depth)], axis=0).reshape(-1) for n in REPLICATED]
    rep_parts.append(dg_final.reshape(-1))
    rep_sizes = [int(r.shape[0]) for r in rep_parts]
    rep_offs = [sum(rep_sizes[:i]) for i in range(len(rep_sizes) + 1)]
    rep = _sum8(_exchange(_pack(rep_parts, F32, 8), True, "comm_gather_small"), "sum_small").reshape(-1)
    for n, a, b in zip(REPLICATED + ("final_norm_g",), rep_offs[:-1], rep_offs[1:]):
        grads[n] = rep[a:b].reshape(W[n].shape)

    deltas, new_m, new_v = {}, {}, {}
    for n in WEIGHTS:
        deltas[n], new_m[n], new_v[n] = _adamw(W[n], grads[n], given["m_" + n], given["v_" + n], "adamw_" + n)
    return (loss, grad_x, *[grads[n] for n in WEIGHTS], *[deltas[n] for n in WEIGHTS],
            *[new_m[n] for n in WEIGHTS], *[new_v[n] for n in WEIGHTS])
```

```python
import functools

import jax
import jax.numpy as jnp
from jax import lax
from jax.experimental import pallas as pl
from jax.experimental.pallas import tpu as pltpu

F32 = jnp.float32
BF16 = jnp.bfloat16
EPS = 1e-6
NEG = -1e30
HEAD_DIM = 64
SGU_CHUNK = 128
DIL_PATTERNS = ((128, 1), (512, 4), (2048, 16))
DIL_SPAN = 128
ROPE_THETA = 500000.0
ROPE_DIM = 16
N_DEV = 8
LANES = 128
VMEM_LIMIT = 56 * 1024 * 1024
ADAM_LR, ADAM_B1, ADAM_B2, ADAM_EPS, ADAM_WD, ADAM_STEP = 0.001, 0.9, 0.999, 1e-08, 0.01, 10
MESH_AXES = ("x", "y", "c")

BS = pl.BlockSpec
SDS = jax.ShapeDtypeStruct


def _cp(*sem):
    return pltpu.CompilerParams(dimension_semantics=sem, vmem_limit_bytes=VMEM_LIMIT)


def _tile(dim, pref, unit=LANES):
    if dim % unit:
        return dim
    best, t = unit, unit
    while t <= min(dim, pref):
        if dim % t == 0:
            best = t
        t += unit
    return best


def _sigmoid(z):
    return 1.0 / (1.0 + jnp.exp(-z))


def _exchange(x, gather, name):
    shape = x.shape if not gather else (N_DEV,) + x.shape

    def body(x_ref, o_ref, send_sems, recv_sems, local_sem):
        ix, iy, ic = lax.axis_index("x"), lax.axis_index("y"), lax.axis_index("c")
        me = 4 * ix + 2 * iy + ic

        def src(j):
            return x_ref if gather else x_ref.at[j]

        local = pltpu.make_async_copy(src(me), o_ref.at[me], local_sem)
        local.start()
        sends, recvs = [], []
        for r in range(1, N_DEV):
            px = 1 - ix if (r >> 2) & 1 else ix
            py = 1 - iy if (r >> 1) & 1 else iy
            pc = 1 - ic if r & 1 else ic
            peer = 4 * px + 2 * py + pc
            sends.append(pltpu.make_async_remote_copy(
                src_ref=src(peer), dst_ref=o_ref.at[me], send_sem=send_sems.at[r - 1], recv_sem=recv_sems.at[r - 1],
                device_id=(px, py, pc), device_id_type=pl.DeviceIdType.MESH))
            recvs.append(pltpu.make_async_remote_copy(
                src_ref=src(peer), dst_ref=o_ref.at[peer], send_sem=send_sems.at[r - 1], recv_sem=recv_sems.at[r - 1],
                device_id=(px, py, pc), device_id_type=pl.DeviceIdType.MESH))
        for cp in sends:
            cp.start()
        for cp in recvs:
            cp.wait_recv()
        for cp in sends:
            cp.wait_send()
        local.wait()

    return pl.pallas_call(
        body, name=name, out_shape=SDS(shape, x.dtype),
        in_specs=[BS(memory_space=pl.ANY)], out_specs=BS(memory_space=pl.ANY),
        scratch_shapes=[pltpu.SemaphoreType.DMA((N_DEV - 1,)), pltpu.SemaphoreType.DMA((N_DEV - 1,)),
                        pltpu.SemaphoreType.DMA],
    )(x)


def _comm(name, srcs, out_shapes, src_views, dst_views, relay=False):
    n = len(srcs)

    def body(*refs):
        copies = _comm_copies(refs[:n], refs[n:2 * n], refs[2 * n:], src_views, dst_views, relay)
        _comm_start(copies)
        _comm_wait(copies)

    anyspec = BS(memory_space=pl.ANY)
    return pl.pallas_call(
        body, name=name, out_shape=tuple(out_shapes), in_specs=[anyspec] * n, out_specs=(anyspec,) * n,
        scratch_shapes=_comm_sems(n))(*srcs)


def _comm_sems(n):
    return [pltpu.SemaphoreType.DMA((N_DEV - 1, n)), pltpu.SemaphoreType.DMA((N_DEV - 1, n)),
            pltpu.SemaphoreType.DMA((n,))]


SIBLING = 1
OTHER_CHIP_SAME_CORE = (2, 4, 6)


def _comm_copies(s_refs, o_refs, sems, src_views, dst_views, relay=False):
    send_sems, recv_sems, local_sems = sems
    n = len(s_refs)
    ix, iy, ic = lax.axis_index("x"), lax.axis_index("y"), lax.axis_index("c")
    me = 4 * ix + 2 * iy + ic

    def at(r):
        px = 1 - ix if (r >> 2) & 1 else ix
        py = 1 - iy if (r >> 1) & 1 else iy
        pc = 1 - ic if r & 1 else ic
        return (px, py, pc), 4 * px + 2 * py + pc

    def copy(r, k, src, dst, to):
        return pltpu.make_async_remote_copy(src_ref=src, dst_ref=dst, send_sem=send_sems.at[r - 1, k],
                                            recv_sem=recv_sems.at[r - 1, k], device_id=to,
                                            device_id_type=pl.DeviceIdType.MESH)

    local = [pltpu.make_async_copy(src_views[k](s_refs[k], me), dst_views[k](o_refs[k], me), local_sems.at[k])
             for k in range(n)]
    sends, passed, arrivals = [], [], []
    for r in range(1, N_DEV):
        to, peer = at(r)
        for k in range(n):
            arrival = copy(r, k, src_views[k](s_refs[k], peer), dst_views[k](o_refs[k], peer), to)
            direct = not relay or r == SIBLING or r in OTHER_CHIP_SAME_CORE
            if direct:
                sends.append(copy(r, k, src_views[k](s_refs[k], peer), dst_views[k](o_refs[k], me), to))
            if relay and r in OTHER_CHIP_SAME_CORE:
                landed = dst_views[k](o_refs[k], peer)
                passed.append((arrival, copy(r ^ SIBLING, k, landed, landed, at(SIBLING)[0])))
            else:
                arrivals.append(arrival)
    return local, sends, passed, arrivals


def _comm_start(copies):
    local, sends, _, _ = copies
    for cp in local + sends:
        cp.start()


def _comm_wait(copies):
    local, sends, passed, arrivals = copies
    for arrival, onward in passed:
        arrival.wait_recv()
        onward.start()
    for cp in arrivals:
        cp.wait_recv()
    for cp in sends + [onward for _, onward in passed]:
        cp.wait_send()
    for cp in local:
        cp.wait()


def _call(body, *, name, grid, in_specs, out_specs, out_shape, args, scratch_shapes=(), sem=None, comm=None):
    single = not isinstance(out_shape, (tuple, list))
    out_shape = (out_shape,) if single else tuple(out_shape)
    out_specs = (out_specs,) if single else tuple(out_specs)
    if comm is None:
        outs = pl.pallas_call(body, name=name, grid=grid, in_specs=list(in_specs), out_specs=out_specs,
                              out_shape=out_shape, scratch_shapes=list(scratch_shapes),
                              compiler_params=_cp(*sem))(*args)
        return outs[0] if single else outs
    srcs, shapes, src_views, dst_views, relay = comm
    n, ni, no, ns = len(srcs), len(in_specs), len(out_shape), len(scratch_shapes)

    def hosted(*refs):
        ins, c_in = refs[:ni], refs[ni:ni + n]
        outs, c_out = refs[ni + n:ni + n + no], refs[ni + n + no:ni + 2 * n + no]
        scr, sems = refs[ni + 2 * n + no:ni + 2 * n + no + ns], refs[ni + 2 * n + no + ns:]
        ids = [pl.program_id(a) for a in range(len(grid))]
        first = functools.reduce(jnp.logical_and, [i == 0 for i in ids])
        last = functools.reduce(jnp.logical_and, [i == g - 1 for i, g in zip(ids, grid)])

        @pl.when(first)
        def _():
            _comm_start(_comm_copies(c_in, c_out, sems, src_views, dst_views, relay))

        body(*ins, *outs, *scr)

        @pl.when(last)
        def _():
            _comm_wait(_comm_copies(c_in, c_out, sems, src_views, dst_views, relay))

    anyspec = BS(memory_space=pl.ANY)
    outs = pl.pallas_call(
        hosted, name=name, grid=grid, in_specs=list(in_specs) + [anyspec] * n,
        out_specs=out_specs + (anyspec,) * n, out_shape=out_shape + tuple(shapes),
        scratch_shapes=list(scratch_shapes) + _comm_sems(n),
        compiler_params=_cp(*(["arbitrary"] * len(grid))))(*args, *srcs)
    return (outs[0] if single else outs[:no]), outs[no:]


def _assemble(pieces, starts, wb, npb, name):
    _, K, pw = pieces.shape
    tr = _tile(K, 256, 16)

    def body(p_ref, o_ref):
        for b in range(npb - 1):
            acc = None
            for j in range(N_DEV):
                if starts[j] <= b < starts[j] + wb:
                    blk = p_ref[j, :, (b - starts[j]) * LANES:(b - starts[j] + 1) * LANES]
                    acc = blk if acc is None else acc + blk
            o_ref[:, b * LANES:(b + 1) * LANES] = acc
        acc = p_ref[0, :, wb * LANES:]
        for j in range(1, N_DEV):
            acc = acc + p_ref[j, :, wb * LANES:]
        o_ref[:, (npb - 1) * LANES:] = acc

    return pl.pallas_call(
        body, name=name, grid=(K // tr,), out_shape=SDS((K, npb * LANES), pieces.dtype),
        in_specs=[BS((N_DEV, tr, pw), lambda i: (0, i, 0))], out_specs=BS((tr, npb * LANES), lambda i: (i, 0)),
        compiler_params=_cp("parallel"))(pieces)


def _sum8(x, name):
    _, R, C = x.shape
    tr = _tile(R, 256, 16)

    def body(x_ref, o_ref):
        acc = x_ref[0].astype(F32)
        for j in range(1, N_DEV):
            acc = acc + x_ref[j].astype(F32)
        o_ref[...] = acc

    return pl.pallas_call(
        body, name=name, grid=(R // tr,), out_shape=SDS((R, C), F32),
        in_specs=[BS((N_DEV, tr, C), lambda i: (0, i, 0))], out_specs=BS((tr, C), lambda i: (i, 0)),
        compiler_params=_cp("parallel"))(x)


_DIMS = {"nn": ((1,), (0,)), "nt": ((1,), (1,)), "tn": ((0,), (0,))}


def _mm(a, b, mode, out_dtype, name, res=None, comm=None):
    if mode == "nn":
        (M, K), N = a.shape, b.shape[1]
    elif mode == "nt":
        (M, K), N = a.shape, b.shape[0]
    else:
        (K, M), N = a.shape, b.shape[1]
    tm, tn, tk = _tile(M, 1024), _tile(N, 1024), _tile(K, 3072)
    nk = K // tk
    a_spec = BS((tk, tm), lambda i, j, k: (k, i)) if mode == "tn" else BS((tm, tk), lambda i, j, k: (i, k))
    b_spec = BS((tn, tk), lambda i, j, k: (j, k)) if mode == "nt" else BS((tk, tn), lambda i, j, k: (k, j))
    o_spec = BS((tm, tn), lambda i, j, k: (i, j))
    dims = (_DIMS[mode], ((), ()))

    def body(a_ref, b_ref, *rest):
        o_ref, acc = rest[-2], rest[-1]
        k = pl.program_id(2)

        @pl.when(k == 0)
        def _():
            acc[...] = jnp.zeros_like(acc)

        acc[...] += lax.dot_general(a_ref[...].astype(BF16), b_ref[...].astype(BF16), dims,
                                    preferred_element_type=F32)

        @pl.when(k == nk - 1)
        def _():
            r = acc[...]
            if res is not None:
                r = r + rest[0][...]
            o_ref[...] = r.astype(o_ref.dtype)

    ins, specs = [a, b], [a_spec, b_spec]
    if res is not None:
        ins.append(res)
        specs.append(o_spec)
    return _call(body, name=name, grid=(M // tm, N // tn, nk), out_shape=SDS((M, N), out_dtype), in_specs=specs,
                 out_specs=o_spec, scratch_shapes=[pltpu.VMEM((tm, tn), F32)],
                 sem=("parallel", "parallel", "arbitrary"), args=ins, comm=comm)


def _rms_fwd(x, g, name):
    S, D = x.shape
    ts = _tile(S, 256, 8)

    def body(x_ref, g_ref, o_ref):
        xv = x_ref[...]
        rstd = lax.rsqrt(jnp.mean(xv * xv, axis=-1, keepdims=True) + EPS)
        o_ref[...] = (xv * rstd * g_ref[...]).astype(o_ref.dtype)

    return pl.pallas_call(
        body, name=name, grid=(S // ts,), out_shape=SDS((S, D), BF16),
        in_specs=[BS((ts, D), lambda i: (i, 0)), BS((1, D), lambda i: (0, 0))],
        out_specs=BS((ts, D), lambda i: (i, 0)), compiler_params=_cp("parallel"))(x, g.reshape(1, D))


def _rms_bwd(x, g, dh, dres, name):
    S, D = x.shape
    ts = _tile(S, 256, 8)

    def body(x_ref, g_ref, dh_ref, dres_ref, dx_ref, dxb_ref, dg_ref):
        xv = x_ref[...]
        rstd = lax.rsqrt(jnp.mean(xv * xv, axis=-1, keepdims=True) + EPS)
        xn = xv * rstd
        dhv = dh_ref[...]

        @pl.when(pl.program_id(0) == 0)
        def _():
            dg_ref[...] = jnp.zeros_like(dg_ref)

        dg_ref[...] += jnp.sum(dhv * xn, axis=0, keepdims=True)
        dhn = dhv * g_ref[...]
        dxv = rstd * (dhn - xn * jnp.mean(dhn * xn, axis=-1, keepdims=True)) + dres_ref[...]
        dx_ref[...] = dxv
        dxb_ref[...] = dxv.astype(BF16)

    row = BS((ts, D), lambda i: (i, 0))
    vec = BS((1, D), lambda i: (0, 0))
    return pl.pallas_call(
        body, name=name, grid=(S // ts,), out_shape=(SDS((S, D), F32), SDS((S, D), BF16), SDS((1, D), F32)),
        in_specs=[row, vec, row, row], out_specs=(row, row, vec), compiler_params=_cp("arbitrary"),
    )(x, g.reshape(1, D), dh, dres)


def _final_loss(x, g, target, name):
    S, D = x.shape
    ts = _tile(S, 256, 8)

    def body(x_ref, g_ref, t_ref, loss_ref, dx_ref, dxb_ref, dg_ref):
        xv = x_ref[...]
        rstd = lax.rsqrt(jnp.mean(xv * xv, axis=-1, keepdims=True) + EPS)
        xn = xv * rstd
        gv = g_ref[...]
        e = xn * gv - t_ref[...]

        @pl.when(pl.program_id(0) == 0)
        def _():
            dg_ref[...] = jnp.zeros_like(dg_ref)
            loss_ref[...] = jnp.zeros_like(loss_ref)

        loss_ref[...] += 0.5 * jnp.sum(jnp.mean(e * e, axis=-1, keepdims=True), axis=0, keepdims=True)
        dy = e * (1.0 / D)
        dg_ref[...] += jnp.sum(dy * xn, axis=0, keepdims=True)
        dhn = dy * gv
        dxv = rstd * (dhn - xn * jnp.mean(dhn * xn, axis=-1, keepdims=True))
        dx_ref[...] = dxv
        dxb_ref[...] = dxv.astype(BF16)

    row = BS((ts, D), lambda i: (i, 0))
    vec = BS((1, D), lambda i: (0, 0))
    return pl.pallas_call(
        body, name=name, grid=(S // ts,),
        out_shape=(SDS((1, LANES), F32), SDS((S, D), F32), SDS((S, D), BF16), SDS((1, D), F32)),
        in_specs=[row, vec, row], out_specs=(BS((1, LANES), lambda i: (0, 0)), row, row, vec),
        compiler_params=_cp("arbitrary"))(x, g.reshape(1, D), target)


def _shift_down(z, k):
    rows = lax.broadcasted_iota(jnp.int32, z.shape, 0)
    return jnp.where(rows >= k, pltpu.roll(z, k, 0), 0.0)


def _shift_up(z, k):
    n = z.shape[0]
    rows = lax.broadcasted_iota(jnp.int32, z.shape, 0)
    return jnp.where(rows < n - k, pltpu.roll(z, n - k, 0), 0.0)


def _cumsum_rows(z, reverse=False):
    n = z.shape[0]
    k = 1
    while k < n:
        z = z + (_shift_up(z, k) if reverse else _shift_down(z, k))
        k *= 2
    return z


def _conv3(z, w_ref):
    return w_ref[0:1, :] * _shift_down(z, 2) + w_ref[1:2, :] * _shift_down(z, 1) + w_ref[2:3, :] * z


def _conv3_bwd(z, dy, w_ref, dw_ref):
    dw_ref[0:1, :] = jnp.sum(dy * _shift_down(z, 2), axis=0, keepdims=True)
    dw_ref[1:2, :] = jnp.sum(dy * _shift_down(z, 1), axis=0, keepdims=True)
    dw_ref[2:3, :] = jnp.sum(dy * z, axis=0, keepdims=True)
    return w_ref[2:3, :] * dy + w_ref[1:2, :] * _shift_up(dy, 1) + w_ref[0:1, :] * _shift_up(dy, 2)


def _fox_prep(proj, bias, col, name):
    S = proj.shape[0]

    def body(a_ref, b_ref, f_ref):
        z = a_ref[...] + b_ref[...]
        logf = jnp.minimum(z, 0.0) - jnp.log(1.0 + jnp.exp(-jnp.abs(z)))
        rest = _cumsum_rows(logf)
        for k in range(3):
            piece = rest.astype(BF16)
            f_ref[k] = piece
            rest = rest - piece.astype(F32)

    return pl.pallas_call(
        body, name=name, grid=(1,), out_shape=SDS((3, S, LANES), BF16),
        in_specs=[BS((S, LANES), lambda i: (0, col)), BS((1, LANES), lambda i: (0, 0))],
        out_specs=BS((3, S, LANES), lambda i: (0, 0, 0)), compiler_params=_cp("arbitrary"))(proj, bias)


def _fox_prep_bwd(proj, bias, col, df, name):
    S = proj.shape[0]

    def body(a_ref, b_ref, df_ref, da_ref, db_ref):
        z = a_ref[...] + b_ref[...]
        da = _cumsum_rows(df_ref[...], reverse=True) * _sigmoid(-z)
        da_ref[...] = da.astype(BF16)
        db_ref[...] = jnp.sum(da, axis=0, keepdims=True)

    full = BS((S, LANES), lambda i: (0, 0))
    vec = BS((1, LANES), lambda i: (0, 0))
    return pl.pallas_call(
        body, name=name, grid=(1,), out_shape=(SDS((S, LANES), BF16), SDS((1, LANES), F32)),
        in_specs=[BS((S, LANES), lambda i: (0, col)), vec, full], out_specs=(full, vec),
        compiler_params=_cp("arbitrary"))(proj, bias, df)


def _fox_scores(qs, kv, fqv, fkv, i, j, t, masked):
    s = lax.dot_general(qs, kv, (_DIMS["nt"], ((), ())), preferred_element_type=F32) + fqv - fkv
    if not masked:
        return s
    row = i * t + lax.broadcasted_iota(jnp.int32, (t, t), 0)
    col = j * t + lax.broadcasted_iota(jnp.int32, (t, t), 1)
    return jnp.where(col <= row, s, NEG)


def _fox_fwd(q, k, v, fq, fk, name, comm=None):
    H, S, Dh = q.shape
    t = _tile(S, 256)
    n = S // t
    scale = Dh ** -0.5

    def body(q_ref, k_ref, v_ref, fq_ref, fk_ref, o_ref, lse_ref):
        i = pl.program_id(1)
        qs, fqv = q_ref[0] * scale, fq_ref[0]

        def step(masked, j, carry):
            m, l, acc = carry
            s = _fox_scores(qs, k_ref[0, j], fqv, fk_ref[0, j], i, j, t, masked)
            mn = jnp.maximum(m, jnp.max(s, axis=-1, keepdims=True))
            al = jnp.exp(m - mn)
            p = jnp.exp(s - mn)
            l = l * al + jnp.sum(p, axis=-1, keepdims=True)
            acc = acc * al + jnp.dot(p.astype(BF16), v_ref[0, j], preferred_element_type=F32)
            return mn, l, acc

        init = (jnp.full((t, 1), NEG, F32), jnp.zeros((t, 1), F32), jnp.zeros((t, Dh), F32))
        m, l, acc = step(True, i, lax.fori_loop(0, i, functools.partial(step, False), init))
        o_ref[0] = acc / l
        lse_ref[0] = m + jnp.log(l)

    blk = BS((1, t, Dh), lambda h, i: (h, i, 0))
    col = BS((1, t, 1), lambda h, i: (h, i, 0))
    res = BS((1, n, t, Dh), lambda h, i: (h, 0, 0, 0))
    return _call(
        body, name=name, grid=(H, n), out_shape=(SDS((H, S, Dh), F32), SDS((H, S, 1), F32)),
        in_specs=[blk, res, res, col, BS((1, n, 1, t), lambda h, i: (h, 0, 0, 0))], out_specs=(blk, col),
        sem=("parallel", "arbitrary"), args=(q, k.reshape(H, n, t, Dh), v.reshape(H, n, t, Dh), fq, fk), comm=comm)


def _fox_bwd_dq(q, k, v, fq, fk, do, lse, name, comm=None):
    H, S, Dh = q.shape
    t = _tile(S, 256)
    n = S // t
    scale = Dh ** -0.5

    def body(q_ref, k_ref, v_ref, fq_ref, fk_ref, do_ref, lse_ref, dq_ref, dl_ref):
        i = pl.program_id(1)
        qs, fqv, dob, lsev = q_ref[0] * scale, fq_ref[0], do_ref[0], lse_ref[0]

        def p_dp(masked, j):
            p = jnp.exp(_fox_scores(qs, k_ref[0, j], fqv, fk_ref[0, j], i, j, t, masked) - lsev)
            return p, lax.dot_general(dob, v_ref[0, j], (_DIMS["nt"], ((), ())), preferred_element_type=F32)

        def sum_step(masked, j, delta):
            p, dp = p_dp(masked, j)
            return delta + jnp.sum(p * dp, axis=-1, keepdims=True)

        delta = sum_step(True, i, lax.fori_loop(0, i, functools.partial(sum_step, False), jnp.zeros((t, 1), F32)))

        def step(masked, j, dq):
            p, dp = p_dp(masked, j)
            return dq + jnp.dot((p * (dp - delta)).astype(BF16), k_ref[0, j], preferred_element_type=F32)

        dq = step(True, i, lax.fori_loop(0, i, functools.partial(step, False), jnp.zeros((t, Dh), F32)))
        dq_ref[0] = dq * scale
        dl_ref[0] = delta

    blk = BS((1, t, Dh), lambda h, i: (h, i, 0))
    col = BS((1, t, 1), lambda h, i: (h, i, 0))
    res = BS((1, n, t, Dh), lambda h, i: (h, 0, 0, 0))
    return _call(
        body, name=name, grid=(H, n), out_shape=(SDS((H, S, Dh), F32), SDS((H, S, 1), F32)),
        in_specs=[blk, res, res, col, BS((1, n, 1, t), lambda h, i: (h, 0, 0, 0)), blk, col],
        out_specs=(blk, col), sem=("parallel", "arbitrary"), comm=comm,
        args=(q, k.reshape(H, n, t, Dh), v.reshape(H, n, t, Dh), fq, fk, do, lse))


def _fox_bwd_dkv(q, k, v, fq, fk, do, lse, delta, name, comm=None):
    H, S, Dh = q.shape
    t = _tile(S, 256)
    n = S // t
    scale = Dh ** -0.5
    tn_dims = (_DIMS["tn"], ((), ()))

    def body(q_ref, k_ref, v_ref, fq_ref, fk_ref, do_ref, lse_ref, dl_ref, dk_ref, dv_ref, dfk_ref):
        j = pl.program_id(1)
        kv, vv, fkv = k_ref[0], v_ref[0], fk_ref[0, 0]

        def step(masked, i, carry):
            dk, dv, dfk = carry
            qs, dob = q_ref[0, i] * scale, do_ref[0, i]
            p = jnp.exp(_fox_scores(qs, kv, fq_ref[0, i], fkv, i, j, t, masked) - lse_ref[0, i])
            dv = dv + lax.dot_general(p.astype(BF16), dob, tn_dims, preferred_element_type=F32)
            dp = lax.dot_general(dob, vv, (_DIMS["nt"], ((), ())), preferred_element_type=F32)
            ds = p * (dp - dl_ref[0, i])
            dk = dk + lax.dot_general(ds.astype(BF16), qs, tn_dims, preferred_element_type=F32)
            return dk, dv, dfk - jnp.sum(ds, axis=0, keepdims=True)

        init = (jnp.zeros((t, Dh), F32), jnp.zeros((t, Dh), F32), jnp.zeros((1, t), F32))
        dk, dv, dfk = lax.fori_loop(j + 1, n, functools.partial(step, False), step(True, j, init))
        dk_ref[0] = dk
        dv_ref[0] = dv
        dfk_ref[0, 0] = dfk

    blk = BS((1, t, Dh), lambda h, j: (h, j, 0))
    res = BS((1, n, t, Dh), lambda h, j: (h, 0, 0, 0))
    rcol = BS((1, n, t, 1), lambda h, j: (h, 0, 0, 0))
    frow = BS((1, 1, 1, t), lambda h, j: (h, j, 0, 0))
    return _call(
        body, name=name, grid=(H, n),
        out_shape=(SDS((H, S, Dh), F32), SDS((H, S, Dh), F32), SDS((H, n, 1, t), F32)),
        in_specs=[res, blk, blk, rcol, frow, res, rcol, rcol], out_specs=(blk, blk, frow),
        sem=("parallel", "arbitrary"), comm=comm,
        args=(q.reshape(H, n, t, Dh), k, v, fq.reshape(H, n, t, 1), fk, do.astype(BF16).reshape(H, n, t, Dh),
              lse.reshape(H, n, t, 1), delta.reshape(H, n, t, 1)))


FOX_HEADS_PER_STEP = 4


def _fox_q_scale(Dh):
    lane = lax.broadcasted_iota(jnp.int32, (1, 2 * Dh), 1)
    return jnp.where(lane < Dh, Dh ** -0.5, 1.0).astype(BF16)


def _foxt_scores(ka, qs, i, j, t, masked):
    s = lax.dot_general(ka, qs, (_DIMS["nt"], ((), ())), preferred_element_type=F32)
    if not masked:
        return s
    key = j * t + lax.broadcasted_iota(jnp.int32, (t, t), 0)
    qry = i * t + lax.broadcasted_iota(jnp.int32, (t, t), 1)
    return jnp.where(key <= qry, s, NEG)


def _foxt_fwd(q_aug, k_aug, vT, name, comm=None):
    H, S, W = q_aug.shape
    Dh = W // 2
    t = _tile(S, 256)
    n = S // t
    hb = FOX_HEADS_PER_STEP

    def body(q_ref, k_ref, v_ref, o_ref, lse_ref):
        i = pl.program_id(1)
        qs = [q_ref[h] * _fox_q_scale(Dh) for h in range(hb)]

        def step(masked, j, carry):
            out = []
            for h in range(hb):
                m, l, acc = carry[h]
                s = _foxt_scores(k_ref[h, j], qs[h], i, j, t, masked)
                mn = jnp.maximum(m, jnp.max(s, axis=0, keepdims=True))
                al = jnp.exp(m - mn)
                p = jnp.exp(s - mn)
                l = l * al + jnp.sum(p, axis=0, keepdims=True)
                acc = acc * al + jnp.dot(v_ref[h, j], p.astype(BF16), preferred_element_type=F32)
                out.append((mn, l, acc))
            return tuple(out)

        init = tuple((jnp.full((1, t), NEG, F32), jnp.zeros((1, t), F32), jnp.zeros((Dh, t), F32))
                     for _ in range(hb))
        fin = step(True, i, lax.fori_loop(0, i, functools.partial(step, False), init))
        for h in range(hb):
            m, l, acc = fin[h]
            o_ref[h] = acc / l
            lse_ref[h] = m + jnp.log(l)

    return _call(
        body, name=name, grid=(H // hb, n), out_shape=(SDS((H, Dh, S), F32), SDS((H, 1, S), F32)),
        in_specs=[BS((hb, t, W), lambda g, i: (g, i, 0)), BS((hb, n, t, W), lambda g, i: (g, 0, 0, 0)),
                  BS((hb, n, Dh, t), lambda g, i: (g, 0, 0, 0))],
        out_specs=(BS((hb, Dh, t), lambda g, i: (g, 0, i)), BS((hb, 1, t), lambda g, i: (g, 0, i))),
        sem=("parallel", "arbitrary"), args=(q_aug, k_aug.reshape(H, n, t, W), vT), comm=comm)


def _foxt_bwd_dq(q_aug, k_aug, kT, v, do, lse, name, comm=None):
    H, S, W = q_aug.shape
    Dh = W // 2
    t = _tile(S, 256)
    n = S // t
    hb = FOX_HEADS_PER_STEP

    def body(q_ref, k_ref, kt_ref, v_ref, do_ref, lse_ref, dq_ref, dl_ref):
        i = pl.program_id(1)
        qs = [q_ref[h] * _fox_q_scale(Dh) for h in range(hb)]

        def p_dp(h, masked, j):
            p = jnp.exp(_foxt_scores(k_ref[h, j], qs[h], i, j, t, masked) - lse_ref[h])
            return p, lax.dot_general(v_ref[h, j], do_ref[h], (_DIMS["nt"], ((), ())), preferred_element_type=F32)

        def sum_step(masked, j, delta):
            out = []
            for h in range(hb):
                p, dp = p_dp(h, masked, j)
                out.append(delta[h] + jnp.sum(p * dp, axis=0, keepdims=True))
            return tuple(out)

        zero = tuple(jnp.zeros((1, t), F32) for _ in range(hb))
        delta = sum_step(True, i, lax.fori_loop(0, i, functools.partial(sum_step, False), zero))

        def step(masked, j, dq):
            out = []
            for h in range(hb):
                p, dp = p_dp(h, masked, j)
                ds = (p * (dp - delta[h])).astype(BF16)
                out.append(dq[h] + jnp.dot(kt_ref[h, j], ds, preferred_element_type=F32))
            return tuple(out)

        zero = tuple(jnp.zeros((W, t), F32) for _ in range(hb))
        dq = step(True, i, lax.fori_loop(0, i, functools.partial(step, False), zero))
        for h in range(hb):
            dq_ref[h] = dq[h] * Dh ** -0.5
            dl_ref[h] = delta[h]

    row = BS((hb, 1, t), lambda g, i: (g, 0, i))
    blk = lambda w: BS((hb, t, w), lambda g, i: (g, i, 0))
    return _call(
        body, name=name, grid=(H // hb, n), out_shape=(SDS((H, W, S), F32), SDS((H, 1, S), F32)),
        in_specs=[blk(W), BS((hb, n, t, W), lambda g, i: (g, 0, 0, 0)), BS((hb, n, W, t), lambda g, i: (g, 0, 0, 0)),
                  BS((hb, n, t, Dh), lambda g, i: (g, 0, 0, 0)), blk(Dh), row],
        out_specs=(BS((hb, W, t), lambda g, i: (g, 0, i)), row), sem=("parallel", "arbitrary"), comm=comm,
        args=(q_aug, k_aug.reshape(H, n, t, W), kT, v.reshape(H, n, t, Dh), do, lse))


def _foxt_bwd_dkv(q_aug, k_aug, v, do, lse, delta, name, comm=None):
    H, S, W = q_aug.shape
    Dh = W // 2
    t = _tile(S, 256)
    n = S // t
    hb = FOX_HEADS_PER_STEP

    def body(q_ref, k_ref, v_ref, do_ref, lse_ref, dl_ref, dk_ref, dv_ref, dfk_ref):
        j = pl.program_id(1)

        def step(masked, i, carry):
            out = []
            for h in range(hb):
                dk, dv, dfk = carry[h]
                qs, dob = q_ref[h, i] * _fox_q_scale(Dh), do_ref[h, i]
                p = jnp.exp(_foxt_scores(k_ref[h], qs, i, j, t, masked) - lse_ref[h, i])
                dv = dv + jnp.dot(p.astype(BF16), dob, preferred_element_type=F32)
                dp = lax.dot_general(v_ref[h], dob, (_DIMS["nt"], ((), ())), preferred_element_type=F32)
                ds = p * (dp - dl_ref[h, i])
                dk = dk + jnp.dot(ds.astype(BF16), qs, preferred_element_type=F32)
                out.append((dk, dv, dfk - jnp.sum(ds, axis=1, keepdims=True)))
            return tuple(out)

        init = tuple((jnp.zeros((t, W), F32), jnp.zeros((t, Dh), F32), jnp.zeros((t, 1), F32)) for _ in range(hb))
        fin = lax.fori_loop(j + 1, n, functools.partial(step, False), step(True, j, init))
        for h in range(hb):
            dk_ref[h], dv_ref[h], dfk_ref[h] = fin[h]

    blk = lambda w: BS((hb, t, w), lambda g, j: (g, j, 0))
    res = lambda w: BS((hb, n, t, w), lambda g, j: (g, 0, 0, 0))
    rows = BS((hb, n, 1, t), lambda g, j: (g, 0, 0, 0))
    return _call(
        body, name=name, grid=(H // hb, n),
        out_shape=(SDS((H, S, W), F32), SDS((H, S, Dh), F32), SDS((H, S, 1), F32)),
        in_specs=[res(W), blk(W), blk(Dh), res(Dh), rows, rows], out_specs=(blk(W), blk(Dh), blk(1)),
        sem=("parallel", "arbitrary"), comm=comm,
        args=(q_aug.reshape(H, n, t, W), k_aug, v, do.reshape(H, n, t, Dh), lse.reshape(H, n, 1, t),
              delta.reshape(H, n, 1, t)))


def _sconv_fwd(proj, w, col0, width, name):
    S = proj.shape[0]
    tc = LANES
    nb, c0 = width // tc, col0 // tc

    def body(x_ref, gb_ref, gc_ref, w_ref, o_ref):
        o_ref[...] = (gb_ref[...] * _conv3(gc_ref[...] * x_ref[...], w_ref)).astype(BF16)

    return pl.pallas_call(
        body, name=name, grid=(nb,), out_shape=SDS((S, width), BF16),
        in_specs=[BS((S, tc), lambda j: (0, c0 + j)), BS((S, tc), lambda j: (0, c0 + nb + j)),
                  BS((S, tc), lambda j: (0, c0 + 2 * nb + j)), BS((3, tc), lambda j: (0, j))],
        out_specs=BS((S, tc), lambda j: (0, j)), compiler_params=_cp("parallel"))(proj, proj, proj, w)


def _sconv_bwd(proj, w, col0, width, do, name):
    S = proj.shape[0]
    tc = LANES
    nb, c0 = width // tc, col0 // tc

    def body(x_ref, gb_ref, gc_ref, w_ref, do_ref, dx_ref, dgb_ref, dgc_ref, dw_ref):
        xb, gb, gc, dov = x_ref[...], gb_ref[...], gc_ref[...], do_ref[...]
        z = gc * xb
        dgb_ref[...] = (dov * _conv3(z, w_ref)).astype(BF16)
        dz = _conv3_bwd(z, dov * gb, w_ref, dw_ref)
        dgc_ref[...] = (dz * xb).astype(BF16)
        dx_ref[...] = (dz * gc).astype(BF16)

    out = BS((S, tc), lambda j: (0, j))
    wspec = BS((3, tc), lambda j: (0, j))
    return pl.pallas_call(
        body, name=name, grid=(nb,),
        out_shape=(SDS((S, width), BF16), SDS((S, width), BF16), SDS((S, width), BF16), SDS((3, width), F32)),
        in_specs=[BS((S, tc), lambda j: (0, c0 + j)), BS((S, tc), lambda j: (0, c0 + nb + j)),
                  BS((S, tc), lambda j: (0, c0 + 2 * nb + j)), wspec, out],
        out_specs=(out, out, out, wspec), compiler_params=_cp("parallel"))(proj, proj, proj, w, do)


def _gelu(x):
    return 0.5 * x * (1.0 + jnp.tanh(0.7978845608028654 * (x + 0.044715 * x * x * x)))


def _gelu_grad(x):
    u = 0.7978845608028654 * (x + 0.044715 * x * x * x)
    th = jnp.tanh(u)
    return 0.5 * (1.0 + th) + 0.5 * x * (1.0 - th * th) * 0.7978845608028654 * (1.0 + 3.0 * 0.044715 * x * x)


def _tril_w(w_ref, g):
    r = lax.broadcasted_iota(jnp.int32, (SGU_CHUNK, SGU_CHUNK), 0)
    c = lax.broadcasted_iota(jnp.int32, (SGU_CHUNK, SGU_CHUNK), 1)
    return jnp.where(c <= r, w_ref[g], 0.0), c <= r


def _sgu_mixed(vn, w_ref, b_ref, nch, G):
    rows = []
    for ch in range(nch):
        cols = []
        for g in range(G):
            wt, _ = _tril_w(w_ref, g)
            blk = vn[ch * SGU_CHUNK:(ch + 1) * SGU_CHUNK, g * LANES:(g + 1) * LANES]
            cols.append(jnp.dot(wt.astype(BF16), blk.astype(BF16), preferred_element_type=F32) + b_ref[g])
        rows.append(jnp.concatenate(cols, axis=1))
    return jnp.concatenate(rows, axis=0)


def _sgu_fwd(proj, gn, w, b, col0, width, name):
    S = proj.shape[0]
    G = w.shape[0]
    ts = _tile(S, 512)
    nch = ts // SGU_CHUNK
    c0 = col0 // width

    def body(u_ref, v_ref, gn_ref, w_ref, b_ref, o_ref):
        cv = _gelu(v_ref[...])
        rstd = lax.rsqrt(jnp.mean(cv * cv, axis=-1, keepdims=True) + EPS)
        mixed = _sgu_mixed(cv * rstd * gn_ref[...], w_ref, b_ref, nch, G)
        o_ref[...] = (_gelu(u_ref[...]) * mixed).astype(BF16)

    return pl.pallas_call(
        body, name=name, grid=(S // ts,), out_shape=SDS((S, width), BF16),
        in_specs=[BS((ts, width), lambda i: (i, c0)), BS((ts, width), lambda i: (i, c0 + 1)),
                  BS((1, width), lambda i: (0, 0)), BS(w.shape, lambda i: (0, 0, 0)), BS(b.shape, lambda i: (0, 0, 0))],
        out_specs=BS((ts, width), lambda i: (i, 0)), compiler_params=_cp("parallel"))(proj, proj, gn, w, b)


def _sgu_bwd(proj, gn, w, b, col0, width, do, name):
    S = proj.shape[0]
    G = w.shape[0]
    ts = _tile(S, 512)
    nch = ts // SGU_CHUNK
    c0 = col0 // width

    def body(u_ref, v_ref, gn_ref, w_ref, b_ref, do_ref, du_ref, dv_ref, dw_ref, db_ref, dgn_ref):
        uin, vin, dov, gnv = u_ref[...], v_ref[...], do_ref[...], gn_ref[...]
        cu, cv = _gelu(uin), _gelu(vin)
        rstd = lax.rsqrt(jnp.mean(cv * cv, axis=-1, keepdims=True) + EPS)
        vhat = cv * rstd
        vn = vhat * gnv
        mixed = _sgu_mixed(vn, w_ref, b_ref, nch, G)
        du_ref[...] = (dov * mixed * _gelu_grad(uin)).astype(BF16)
        dmix = dov * cu

        @pl.when(pl.program_id(0) == 0)
        def _():
            dw_ref[...] = jnp.zeros_like(dw_ref)
            db_ref[...] = jnp.zeros_like(db_ref)
            dgn_ref[...] = jnp.zeros_like(dgn_ref)

        rows = []
        for ch in range(nch):
            cols = []
            for g in range(G):
                wt, mask = _tril_w(w_ref, g)
                sl = (slice(ch * SGU_CHUNK, (ch + 1) * SGU_CHUNK), slice(g * LANES, (g + 1) * LANES))
                dm, vb = dmix[sl], vn[sl].astype(BF16)
                db_ref[g] += jnp.sum(dm, axis=-1, keepdims=True)
                dwg = lax.dot_general(dm.astype(BF16), vb, (_DIMS["nt"], ((), ())), preferred_element_type=F32)
                dw_ref[g] += jnp.where(mask, dwg, 0.0)
                cols.append(lax.dot_general(wt.astype(BF16), dm.astype(BF16), (_DIMS["tn"], ((), ())),
                                            preferred_element_type=F32))
            rows.append(jnp.concatenate(cols, axis=1))
        dvn = jnp.concatenate(rows, axis=0)
        dgn_ref[...] += jnp.sum(dvn * vhat, axis=0, keepdims=True)
        dvh = dvn * gnv
        dcv = rstd * (dvh - vhat * jnp.mean(dvh * vhat, axis=-1, keepdims=True))
        dv_ref[...] = (dcv * _gelu_grad(vin)).astype(BF16)

    row = BS((ts, width), lambda i: (i, 0))
    wsp, bsp, gsp = BS(w.shape, lambda i: (0, 0, 0)), BS(b.shape, lambda i: (0, 0, 0)), BS((1, width), lambda i: (0, 0))
    return pl.pallas_call(
        body, name=name, grid=(S // ts,),
        out_shape=(SDS((S, width), BF16), SDS((S, width), BF16), SDS(w.shape, F32), SDS(b.shape, F32),
                   SDS((1, width), F32)),
        in_specs=[BS((ts, width), lambda i: (i, c0)), BS((ts, width), lambda i: (i, c0 + 1)), gsp, wsp, bsp, row],
        out_specs=(row, row, wsp, bsp, gsp), compiler_params=_cp("arbitrary"))(proj, proj, gn, w, b, do)


def _rope_tables(positions, inv, name):
    S = positions.shape[0]
    ts = _tile(S, 512, 8)
    half = ROPE_DIM // 2

    def body(p_ref, inv_ref, c_ref, sa_ref, sb_ref):
        ang = p_ref[...].astype(F32) * inv_ref[...]
        lane = lax.broadcasted_iota(jnp.int32, (ts, LANES), 1) % HEAD_DIM
        sn = jnp.sin(ang)
        c_ref[...] = jnp.cos(ang)
        sa_ref[...] = jnp.where(lane < half, -sn, 0.0)
        sb_ref[...] = jnp.where((lane >= half) & (lane < ROPE_DIM), sn, 0.0)

    out = BS((ts, LANES), lambda i: (i, 0))
    return pl.pallas_call(
        body, name=name, grid=(S // ts,), out_shape=(SDS((S, LANES), F32),) * 3,
        in_specs=[BS((ts, 1), lambda i: (i, 0)), BS((1, LANES), lambda i: (0, 0))], out_specs=(out, out, out),
        compiler_params=_cp("parallel"))(positions, inv)


def _rope_apply(x, col0, width, tables, sign, out_dtype, name):
    S = x.shape[0]
    ts = _tile(S, 512, 8)
    tc = 2 * LANES
    nb, c0 = width // tc, col0 // tc
    half = ROPE_DIM // 2

    def body(x_ref, c_ref, sa_ref, sb_ref, o_ref):
        xv = x_ref[...].astype(F32)
        wide = lambda r: jnp.concatenate([r[...], r[...]], axis=1)
        y = (xv * wide(c_ref) + pltpu.roll(xv, tc - half, 1) * (sign * wide(sa_ref))
             + pltpu.roll(xv, half, 1) * (sign * wide(sb_ref)))
        o_ref[...] = y.astype(o_ref.dtype)

    tab = BS((ts, LANES), lambda i, j: (i, 0))
    return pl.pallas_call(
        body, name=name, grid=(S // ts, nb), out_shape=SDS((S, width), out_dtype),
        in_specs=[BS((ts, tc), lambda i, j: (i, c0 + j)), tab, tab, tab], out_specs=BS((ts, tc), lambda i, j: (i, j)),
        compiler_params=_cp("parallel", "parallel"))(x, *tables)


def _dil_masks(hh, m, bpcs, hpg):
    g = hh // hpg
    bpc = jnp.where(g == 0, bpcs[0], jnp.where(g == 1, bpcs[1], bpcs[2]))
    r = lax.broadcasted_iota(jnp.int32, (DIL_SPAN, DIL_SPAN), 0)
    c = lax.broadcasted_iota(jnp.int32, (DIL_SPAN, DIL_SPAN), 1)
    return (c >= r) & (lax.rem(m, bpc) != 0), c <= r


def _nt(a, b):
    return lax.dot_general(a, b, (_DIMS["nt"], ((), ())), preferred_element_type=F32)


def _tn(a, b):
    return lax.dot_general(a, b, (_DIMS["tn"], ((), ())), preferred_element_type=F32)


def _dil_fwd(q, k, v, bpcs, hpg, name, comm=None):
    HH, S, Dh = q.shape
    nb = S // DIL_SPAN
    scale = Dh ** -0.5

    def body(q_ref, kp_ref, kc_ref, vp_ref, vc_ref, o_ref, lse_ref):
        mp, mc = _dil_masks(pl.program_id(0) * hpg, pl.program_id(1), bpcs, hpg)
        for h in range(hpg):
            qv = q_ref[h]
            sp = jnp.where(mp, _nt(qv, kp_ref[h]) * scale, NEG)
            sc = jnp.where(mc, _nt(qv, kc_ref[h]) * scale, NEG)
            mx = jnp.maximum(jnp.max(sp, axis=-1, keepdims=True), jnp.max(sc, axis=-1, keepdims=True))
            pp, pc = jnp.exp(sp - mx), jnp.exp(sc - mx)
            l = jnp.sum(pp, axis=-1, keepdims=True) + jnp.sum(pc, axis=-1, keepdims=True)
            acc = (jnp.dot(pp.astype(BF16), vp_ref[h], preferred_element_type=F32)
                   + jnp.dot(pc.astype(BF16), vc_ref[h], preferred_element_type=F32))
            o_ref[h] = acc / l
            lse_ref[h] = mx + jnp.log(l)

    cur = BS((hpg, DIL_SPAN, Dh), lambda g, m: (g, m, 0))
    prev = BS((hpg, DIL_SPAN, Dh), lambda g, m: (g, jnp.maximum(m - 1, 0), 0))
    return _call(
        body, name=name, grid=(HH // hpg, nb), out_shape=(SDS((HH, S, Dh), F32), SDS((HH, S, 1), F32)),
        in_specs=[cur, prev, cur, prev, cur], out_specs=(cur, BS((hpg, DIL_SPAN, 1), lambda g, m: (g, m, 0))),
        sem=("parallel", "parallel"), args=(q, k, k, v, v), comm=comm)


def _dil_bwd_dq(q, k, v, do, lse, coef, bpcs, hpg, name, comm=None):
    HH, S, Dh = q.shape
    nb = S // DIL_SPAN
    scale = Dh ** -0.5

    def body(q_ref, kp_ref, kc_ref, vp_ref, vc_ref, do_ref, lse_ref, cf_ref, dq_ref):
        mp, mc = _dil_masks(pl.program_id(0) * hpg, pl.program_id(1), bpcs, hpg)
        for h in range(hpg):
            qv, dob, lsev, cf = q_ref[h], do_ref[h], lse_ref[h], cf_ref[h]
            pp = jnp.exp(jnp.where(mp, _nt(qv, kp_ref[h]) * scale, NEG) - lsev)
            pc = jnp.exp(jnp.where(mc, _nt(qv, kc_ref[h]) * scale, NEG) - lsev)
            dsp = pp * (_nt(dob, vp_ref[h]) + cf)
            dsc = pc * (_nt(dob, vc_ref[h]) + cf)
            dq_ref[h] = (jnp.dot(dsp.astype(BF16), kp_ref[h], preferred_element_type=F32)
                         + jnp.dot(dsc.astype(BF16), kc_ref[h], preferred_element_type=F32)) * scale

    cur = BS((hpg, DIL_SPAN, Dh), lambda g, m: (g, m, 0))
    prev = BS((hpg, DIL_SPAN, Dh), lambda g, m: (g, jnp.maximum(m - 1, 0), 0))
    col = BS((hpg, DIL_SPAN, 1), lambda g, m: (g, m, 0))
    return _call(
        body, name=name, grid=(HH // hpg, nb), out_shape=SDS((HH, S, Dh), F32),
        in_specs=[cur, prev, cur, prev, cur, cur, col, col], out_specs=cur,
        sem=("parallel", "parallel"), args=(q, k, k, v, v, do, lse, coef), comm=comm)


def _dil_bwd_dkv(q, k, v, do, lse, coef, bpcs, hpg, name, comm=None):
    HH, S, Dh = q.shape
    nb = S // DIL_SPAN
    scale = Dh ** -0.5

    def body(k_ref, v_ref, qc_ref, qn_ref, doc_ref, don_ref, lc_ref, ln_ref, cc_ref, cn_ref, dk_ref, dv_ref):
        hh, m = pl.program_id(0) * hpg, pl.program_id(1)
        _, mc = _dil_masks(hh, m, bpcs, hpg)
        mp, _ = _dil_masks(hh, m + 1, bpcs, hpg)
        mp = mp & (m + 1 < nb)
        for h in range(hpg):
            kv, vv = k_ref[h], v_ref[h]
            pc = jnp.exp(jnp.where(mc, _nt(qc_ref[h], kv) * scale, NEG) - lc_ref[h])
            pn = jnp.exp(jnp.where(mp, _nt(qn_ref[h], kv) * scale, NEG) - ln_ref[h])
            dsc = pc * (_nt(doc_ref[h], vv) + cc_ref[h])
            dsn = pn * (_nt(don_ref[h], vv) + cn_ref[h])
            dv_ref[h] = _tn(pc.astype(BF16), doc_ref[h]) + _tn(pn.astype(BF16), don_ref[h])
            dk_ref[h] = (_tn(dsc.astype(BF16), qc_ref[h]) + _tn(dsn.astype(BF16), qn_ref[h])) * scale

    cur = BS((hpg, DIL_SPAN, Dh), lambda g, m: (g, m, 0))
    nxt = BS((hpg, DIL_SPAN, Dh), lambda g, m: (g, jnp.minimum(m + 1, nb - 1), 0))
    col = BS((hpg, DIL_SPAN, 1), lambda g, m: (g, m, 0))
    ncol = BS((hpg, DIL_SPAN, 1), lambda g, m: (g, jnp.minimum(m + 1, nb - 1), 0))
    return _call(
        body, name=name, grid=(HH // hpg, nb), out_shape=(SDS((HH, S, Dh), F32), SDS((HH, S, Dh), F32)),
        in_specs=[cur, cur, cur, nxt, cur, nxt, col, ncol, col, ncol], out_specs=(cur, cur),
        sem=("parallel", "parallel"), args=(k, v, q, q, do, do, lse, lse, coef, coef), comm=comm)


def _dil_merge(o3, lse3, name):
    G, H, S, Dh = o3.shape
    ts = _tile(S, 512, 8)

    def body(o_ref, l_ref, out_ref):
        lv = [l_ref[g, 0] for g in range(G)]
        mx = functools.reduce(jnp.maximum, lv)
        ex = [jnp.exp(v - mx) for v in lv]
        den = functools.reduce(lambda a, b: a + b, ex)
        out_ref[0] = functools.reduce(lambda a, b: a + b, [(ex[g] / den) * o_ref[g, 0] for g in range(G)])

    return pl.pallas_call(
        body, name=name, grid=(H, S // ts), out_shape=SDS((H, S, Dh), F32),
        in_specs=[BS((G, 1, ts, Dh), lambda h, i: (0, h, i, 0)), BS((G, 1, ts, 1), lambda h, i: (0, h, i, 0))],
        out_specs=BS((1, ts, Dh), lambda h, i: (h, i, 0)), compiler_params=_cp("parallel", "parallel"))(o3, lse3)


def _dil_merge_bwd(o3, lse3, do, name):
    G, H, S, Dh = o3.shape
    ts = _tile(S, 512, 8)

    def body(o_ref, l_ref, do_ref, do3_ref, cf_ref):
        lv = [l_ref[g, 0] for g in range(G)]
        mx = functools.reduce(jnp.maximum, lv)
        ex = [jnp.exp(v - mx) for v in lv]
        den = functools.reduce(lambda a, b: a + b, ex)
        wt = [e / den for e in ex]
        dov = do_ref[0]
        dw = [jnp.sum(dov * o_ref[g, 0], axis=-1, keepdims=True) for g in range(G)]
        mean = functools.reduce(lambda a, b: a + b, [wt[g] * dw[g] for g in range(G)])
        for g in range(G):
            do3_ref[g, 0] = (wt[g] * dov).astype(BF16)
            cf_ref[g, 0] = wt[g] * (dw[g] - mean) - wt[g] * dw[g]

    o_spec = BS((G, 1, ts, Dh), lambda h, i: (0, h, i, 0))
    l_spec = BS((G, 1, ts, 1), lambda h, i: (0, h, i, 0))
    return pl.pallas_call(
        body, name=name, grid=(H, S // ts), out_shape=(SDS((G, H, S, Dh), BF16), SDS((G, H, S, 1), F32)),
        in_specs=[o_spec, l_spec, BS((1, ts, Dh), lambda h, i: (h, i, 0))], out_specs=(o_spec, l_spec),
        compiler_params=_cp("parallel", "parallel"))(o3, lse3, do)


def _merge_fwd(proj, gate_col, os_, ws, name):
    S = proj.shape[0]
    D = ws[0].shape[1]
    nbr = len(os_)
    ts = _tile(S, 512, 8)
    tn = LANES * 2 if gate_col % (2 * LANES) == 0 and D % (2 * LANES) == 0 else LANES
    g0, gstep = gate_col // tn, D // tn

    def body(*refs):
        p_refs, o_refs, w_refs, out_ref = refs[:nbr], refs[nbr:2 * nbr], refs[2 * nbr:3 * nbr], refs[-1]
        acc = None
        for i in range(nbr):
            t = _sigmoid(p_refs[i][...]) * jnp.dot(o_refs[i][...], w_refs[i][...], preferred_element_type=F32)
            acc = t if acc is None else acc + t
        out_ref[...] = acc.astype(BF16)

    specs = ([BS((ts, tn), lambda s, j, i=i: (s, g0 + i * gstep + j)) for i in range(nbr)]
             + [BS((ts, o.shape[1]), lambda s, j: (s, 0)) for o in os_]
             + [BS((w.shape[0], tn), lambda s, j: (0, j)) for w in ws])
    return pl.pallas_call(
        body, name=name, grid=(S // ts, D // tn), out_shape=SDS((S, D), BF16), in_specs=specs,
        out_specs=BS((ts, tn), lambda s, j: (s, j)), compiler_params=_cp("parallel", "parallel"),
    )(*([proj] * nbr), *os_, *ws)


def _merge_bwd(proj, gate_col, os_, ws, dm, name):
    S = proj.shape[0]
    D = ws[0].shape[1]
    nbr = len(os_)
    ts = _tile(S, 512, 8)
    tn = LANES * 2 if gate_col % (2 * LANES) == 0 and D % (2 * LANES) == 0 else LANES
    g0, gstep = gate_col // tn, D // tn

    def body(*refs):
        p_refs, o_refs, w_refs = refs[:nbr], refs[nbr:2 * nbr], refs[2 * nbr:3 * nbr]
        dmv = refs[3 * nbr][...]
        dbo_refs, dgl_refs = refs[3 * nbr + 1:4 * nbr + 1], refs[4 * nbr + 1:]
        for i in range(nbr):
            gt = _sigmoid(p_refs[i][...])
            bo = jnp.dot(o_refs[i][...], w_refs[i][...], preferred_element_type=F32)
            dbo_refs[i][...] = (dmv * gt).astype(BF16)
            dgl_refs[i][...] = (dmv * bo * gt * (1.0 - gt)).astype(BF16)

    tile = BS((ts, tn), lambda s, j: (s, j))
    specs = ([BS((ts, tn), lambda s, j, i=i: (s, g0 + i * gstep + j)) for i in range(nbr)]
             + [BS((ts, o.shape[1]), lambda s, j: (s, 0)) for o in os_]
             + [BS((w.shape[0], tn), lambda s, j: (0, j)) for w in ws] + [tile])
    outs = pl.pallas_call(
        body, name=name, grid=(S // ts, D // tn), out_shape=(SDS((S, D), BF16),) * (2 * nbr), in_specs=specs,
        out_specs=(tile,) * (2 * nbr), compiler_params=_cp("parallel", "parallel"),
    )(*([proj] * nbr), *os_, *ws, dm)
    return outs[:nbr], outs[nbr:]


def _ffn_act(up, w, name):
    S, F2 = up.shape
    F = F2 // 2
    tc = LANES
    nb = F // tc

    def body(g_ref, v_ref, wg_ref, wv_ref, a_ref):
        ug = _conv3(g_ref[...], wg_ref)
        a_ref[...] = (ug * _sigmoid(ug) * _conv3(v_ref[...], wv_ref)).astype(BF16)

    return pl.pallas_call(
        body, name=name, grid=(nb,), out_shape=SDS((S, F), BF16),
        in_specs=[BS((S, tc), lambda j: (0, j)), BS((S, tc), lambda j: (0, nb + j)),
                  BS((3, tc), lambda j: (0, j)), BS((3, tc), lambda j: (0, nb + j))],
        out_specs=BS((S, tc), lambda j: (0, j)), compiler_params=_cp("parallel"))(up, up, w, w)


def _ffn_act_bwd(up, w, da, name):
    S, F2 = up.shape
    F = F2 // 2
    tc = LANES
    nb = F // tc

    def body(g_ref, v_ref, wg_ref, wv_ref, da_ref, dg_ref, dv_ref, dwg_ref, dwv_ref):
        pg, pv, dav = g_ref[...], v_ref[...], da_ref[...].astype(F32)
        ug, uv = _conv3(pg, wg_ref), _conv3(pv, wv_ref)
        sg = _sigmoid(ug)
        dg_ref[...] = _conv3_bwd(pg, dav * uv * (sg * (1.0 + ug * (1.0 - sg))), wg_ref, dwg_ref).astype(BF16)
        dv_ref[...] = _conv3_bwd(pv, dav * ug * sg, wv_ref, dwv_ref).astype(BF16)

    col = BS((S, tc), lambda j: (0, j))
    wsp = BS((3, tc), lambda j: (0, j))
    return pl.pallas_call(
        body, name=name, grid=(nb,),
        out_shape=(SDS((S, F), BF16), SDS((S, F), BF16), SDS((3, F), F32), SDS((3, F), F32)),
        in_specs=[col, BS((S, tc), lambda j: (0, nb + j)), wsp, BS((3, tc), lambda j: (0, nb + j)), col],
        out_specs=(col, col, wsp, wsp), compiler_params=_cp("parallel"))(up, up, w, w, da)


def _ple_fwd(x, pgl, pe, name):
    S, D = x.shape
    ts = _tile(S, 256, 8)

    def body(x_ref, g_ref, e_ref, o_ref):
        o_ref[...] = x_ref[...] + _sigmoid(g_ref[...]) * e_ref[...]

    row = BS((ts, D), lambda i: (i, 0))
    return pl.pallas_call(body, name=name, grid=(S // ts,), out_shape=SDS((S, D), F32), in_specs=[row] * 3,
                          out_specs=row, compiler_params=_cp("parallel"))(x, pgl, pe)


def _ple_bwd(dx, pgl, pe, name):
    S, D = dx.shape
    ts = _tile(S, 256, 8)

    def body(dx_ref, g_ref, e_ref, dg_ref, de_ref):
        dxv, sg = dx_ref[...], _sigmoid(g_ref[...])
        dg_ref[...] = (dxv * e_ref[...] * sg * (1.0 - sg)).astype(BF16)
        de_ref[...] = (dxv * sg).astype(BF16)

    row = BS((ts, D), lambda i: (i, 0))
    return pl.pallas_call(body, name=name, grid=(S // ts,), out_shape=(SDS((S, D), BF16),) * 2, in_specs=[row] * 3,
                          out_specs=(row, row), compiler_params=_cp("parallel"))(dx, pgl, pe)


def _adamw(w, g, m, v, name):
    shape = w.shape
    cols = shape[-1]
    rows = w.size // cols
    tr = _tile(rows, 256, 8)
    c1, c2 = 1.0 / (1.0 - ADAM_B1 ** ADAM_STEP), 1.0 / (1.0 - ADAM_B2 ** ADAM_STEP)

    def body(w_ref, g_ref, m_ref, v_ref, d_ref, nm_ref, nv_ref):
        gv = g_ref[...]
        nm = ADAM_B1 * m_ref[...] + (1.0 - ADAM_B1) * gv
        nv = ADAM_B2 * v_ref[...] + (1.0 - ADAM_B2) * (gv * gv)
        d_ref[...] = -ADAM_LR * ((nm * c1) / (jnp.sqrt(nv * c2) + ADAM_EPS) + ADAM_WD * w_ref[...])
        nm_ref[...] = nm
        nv_ref[...] = nv

    blk = BS((tr, cols), lambda i: (i, 0))
    outs = pl.pallas_call(
        body, name=name, grid=(rows // tr,), out_shape=(SDS((rows, cols), F32),) * 3, in_specs=[blk] * 4,
        out_specs=(blk,) * 3, compiler_params=_cp("parallel"),
    )(*(a.reshape(rows, cols) for a in (w, g, m, v)))
    return tuple(o.reshape(shape) for o in outs)


SHARDED = ("w_in", "shortconv_w", "w_br_fox", "w_br_conv", "w_br_sgu", "w_br_dil", "w_out", "w_up", "ffn_conv_w",
           "w_down", "w_ple_gate", "w_ple_proj")
REPLICATED = ("norm_mix_g", "fox_forget_b", "sgu_norm_g", "sgu_w", "sgu_b", "norm_ffn_g", "norm_ple_g")
WEIGHTS = ("norm_mix_g", "w_in", "fox_forget_b", "shortconv_w", "sgu_norm_g", "sgu_w", "sgu_b", "w_br_fox",
           "w_br_conv", "w_br_sgu", "w_br_dil", "w_out", "norm_ffn_g", "w_up", "ffn_conv_w", "w_down", "norm_ple_g",
           "w_ple_gate", "w_ple_proj", "final_norm_g")
PACK_COLS = 1024
PACK_ROWS = 256


def _pack(parts, dtype, unit):
    lead = parts[0].shape[:-1] if parts[0].ndim > 1 else ()
    flat = jnp.concatenate([p.astype(dtype) for p in parts], axis=-1)
    n = flat.shape[-1]
    rows = -(-n // (PACK_COLS * unit)) * unit
    flat = jnp.pad(flat, [(0, 0)] * len(lead) + [(0, rows * PACK_COLS - n)])
    return flat.reshape(lead + (rows, PACK_COLS))


def _to_heads(x, heads):
    S = x.shape[0]
    return x.reshape(S, heads, HEAD_DIM).transpose(1, 0, 2)


def _from_heads(x):
    H, S, Dh = x.shape
    return x.transpose(1, 0, 2).reshape(S, H * Dh)


def _to_residue(x, hpg):
    S = x.shape[0]
    gw = hpg * HEAD_DIM
    outs = []
    for g, (_, d) in enumerate(DIL_PATTERNS):
        seg = x[:, g * gw:(g + 1) * gw].reshape(S // d, d, hpg, HEAD_DIM)
        outs.append(seg.transpose(2, 1, 0, 3).reshape(hpg, S, HEAD_DIM))
    return jnp.concatenate(outs, axis=0)


def _from_residue(x, hpg):
    _, S, C = x.shape
    outs = []
    for g, (_, d) in enumerate(DIL_PATTERNS):
        seg = x[g * hpg:(g + 1) * hpg].reshape(hpg, d, S // d, C)
        outs.append(seg.transpose(0, 2, 1, 3).reshape(hpg, S, C))
    return jnp.stack(outs, axis=0)


def _groups_to_residue(x):
    G, hpg, S, C = x.shape
    outs = []
    for g, (_, d) in enumerate(DIL_PATTERNS):
        outs.append(x[g].reshape(hpg, S // d, d, C).transpose(0, 2, 1, 3).reshape(hpg, S, C))
    return jnp.concatenate(outs, axis=0)


def kernel(x, p, positions, norm_mix_g, w_in, fox_forget_b, shortconv_w, sgu_norm_g, sgu_w, sgu_b, w_br_fox, w_br_conv, w_br_sgu, w_br_dil, w_out, norm_ffn_g, w_up, ffn_conv_w, w_down, norm_ple_g, w_ple_gate, w_ple_proj, final_norm_g, loss_target, m_norm_mix_g, m_w_in, m_fox_forget_b, m_shortconv_w, m_sgu_norm_g, m_sgu_w, m_sgu_b, m_w_br_fox, m_w_br_conv, m_w_br_sgu, m_w_br_dil, m_w_out, m_norm_ffn_g, m_w_up, m_ffn_conv_w, m_w_down, m_norm_ple_g, m_w_ple_gate, m_w_ple_proj, m_final_norm_g, v_norm_mix_g, v_w_in, v_fox_forget_b, v_shortconv_w, v_sgu_norm_g, v_sgu_w, v_sgu_b, v_w_br_fox, v_w_br_conv, v_w_br_sgu, v_w_br_dil, v_w_out, v_norm_ffn_g, v_w_up, v_ffn_conv_w, v_w_down, v_norm_ple_g, v_w_ple_gate, v_w_ple_proj, v_final_norm_g):
    given = dict(locals())
    W = {n: given[n] for n in WEIGHTS}
    depth = w_in.shape[0]
    S, D = x.shape[1], x.shape[2]
    x0, target, pos = x[0], loss_target[0], positions[0].reshape(S, 1)
    FH = fox_forget_b.shape[1]
    FW = FH * HEAD_DIM
    CW = shortconv_w.shape[2] * N_DEV
    G = sgu_w.shape[1]
    SW = G * sgu_w.shape[2]
    DOUT = w_br_dil.shape[1]
    hpg = DOUT // HEAD_DIM
    NG = len(DIL_PATTERNS)
    DW = NG * DOUT
    bpcs = tuple(S // d // DIL_SPAN for _, d in DIL_PATTERNS)
    A0, B0 = 0, 3 * FW
    C0 = B0 + 3 * CW
    D0 = C0 + 2 * SW
    G0 = D0 + 3 * DW
    F0 = G0 + 4 * D
    NP = F0 + LANES
    orig = [3 * FW, FH, 3 * CW, 2 * SW, 3 * DW, 4 * D]
    o_off = [sum(orig[:i]) for i in range(len(orig) + 1)]

    me = 4 * lax.axis_index("x") + 2 * lax.axis_index("y") + lax.axis_index("c")
    n_in, o1, o2, d_in = w_in.shape[2], o_off[1], o_off[2], o_off[6]
    NPB = NP // LANES
    WB = -(-(LANES - 1 + n_in) // LANES)
    PW = (WB + 1) * LANES

    def a_of(j, clip=lambda v: min(max(v, 0), FH)):
        return n_in * j - clip(n_in * j - o1)

    def s_of(j):
        return min(a_of(j) // LANES, NPB - WB)

    def s_traced(j):
        return jnp.minimum(a_of(j, lambda v: jnp.clip(v, 0, FH)) // LANES, NPB - WB)

    def runs(j):
        lo, hi, out = n_in * j, n_in * (j + 1), []
        for a, b in ((0, o1), (o1, o2), (o2, d_in)):
            l, h = max(lo, a), min(hi, b)
            if l < h:
                dst = WB * LANES + l - o1 if a == o1 else l - (FH if l >= o2 else 0) - s_of(j) * LANES
                out.append((l - lo, h - l, dst))
        return out

    def to_window(j, shard):
        piece = jnp.zeros((shard.shape[0], PW), shard.dtype)
        for src, ln, dst in runs(j):
            piece = piece.at[:, dst:dst + ln].set(shard[:, src:src + ln])
        return piece

    def from_window(j, win):
        return jnp.concatenate([win[:, dst:dst + ln] for _, ln, dst in runs(j)], axis=1)

    def on_my_index(fn, arg):
        return lax.switch(me, [functools.partial(fn, j) for j in range(N_DEV)], arg)

    def place(axis, length):
        def view(ref, j):
            align = LANES if axis == len(ref.shape) - 1 else 16
            start = pl.multiple_of(j * length, align) if length % align == 0 else j * length
            return ref.at[tuple([slice(None)] * axis + [pl.ds(start, length)])]
        return view

    def slot(ref, j):
        return ref.at[j]

    def whole(ref, j):
        return ref

    n_up, r_down, r_sq = w_up.shape[2], w_down.shape[1], w_out.shape[1]
    n_br, n_sm, n_sc = w_br_fox.shape[2], w_br_dil.shape[2], shortconv_w.shape[2]
    FF2 = n_up * N_DEV
    cw_pad = -(-n_up // LANES) * LANES
    placed = [place(1, n_up), place(0, r_down), place(1, r_sq), place(2, n_br), place(2, n_sm)]

    def gather_plans(i):
        conv = jnp.zeros((8, cw_pad), F32).at[0:3, :n_up].set(ffn_conv_w[i]).at[3:6, :n_sc].set(shortconv_w[i])
        srcs = [on_my_index(to_window, w_in[i].astype(BF16)), w_up[i].astype(BF16), w_down[i].astype(BF16),
                jnp.stack([w_out[i], w_ple_gate[i]]).astype(BF16),
                jnp.stack([w_br_fox[i], w_br_conv[i], w_br_sgu[i]]).astype(BF16),
                jnp.stack([w_br_dil[i], w_ple_proj[i]]).astype(BF16), conv]
        shapes = [SDS((N_DEV, D, PW), BF16), SDS((D, FF2), BF16), SDS((r_down * N_DEV, D), BF16),
                  SDS((2, r_sq * N_DEV, D), BF16), SDS((3, w_br_fox.shape[1], n_br * N_DEV), BF16),
                  SDS((2, w_br_dil.shape[1], n_sm * N_DEV), BF16), SDS((N_DEV, 8, cw_pad), F32)]
        dsts = [slot] + placed + [slot]
        return [(srcs[a:b], shapes[a:b], [whole] * (b - a), dsts[a:b], True)
                for a, b in ((0, 1), (1, 2), (2, 3), (3, 7))]

    def gathered(parts):
        (pieces,), (wup,), (wdown,), (sq, br, sm, cv) = parts
        return {
            "w_in": _assemble(pieces, [s_of(j) for j in range(N_DEV)], WB, NPB, "assemble_w_in"),
            "w_up": wup, "w_down": wdown, "w_out": sq[0], "w_ple_gate": sq[1],
            "w_br_fox": br[0], "w_br_conv": br[1], "w_br_sgu": br[2], "w_br_dil": sm[0], "w_ple_proj": sm[1],
            "ffn_conv_w": jnp.moveaxis(cv[:, 0:3, :n_up], 0, 1).reshape(3, FF2),
            "shortconv_w": jnp.moveaxis(cv[:, 3:6, :n_sc], 0, 1).reshape(3, CW),
        }

    def one_plan(plans):
        return tuple(sum((list(plan[k]) for plan in plans), []) for k in range(4)) + (plans[0][4],)

    first = _comm("comm_gather_weights", *one_plan(gather_plans(0)))
    full = [gathered([first[0:1], first[1:2], first[2:3], first[3:7]])]

    inv = ROPE_THETA ** (-jnp.arange(ROPE_DIM // 2, dtype=F32) * (2.0 / ROPE_DIM))
    inv_row = jnp.tile(jnp.concatenate([inv, inv, jnp.zeros((HEAD_DIM - ROPE_DIM,), F32)]), LANES // HEAD_DIM)
    tables = _rope_tables(pos, inv_row.reshape(1, LANES), "rope_tables")

    def pad_lanes(v):
        return jnp.pad(v, ((0, 0), (0, LANES - v.shape[1])))

    saved = []
    xc = x0
    for i in range(depth):
        fw, sv = full[i], {}
        sv["x"] = xc
        nxt = gather_plans(i + 1) if i + 1 < depth else [None] * 4
        got = [None] * 4

        def host(k, result, nxt=nxt, got=got):
            if nxt[k] is None:
                return result
            got[k] = result[1]
            return result[0]

        h1 = _rms_fwd(xc, norm_mix_g[i], "rms_mix")
        proj = host(1, _mm(h1, fw["w_in"], "nn", F32, "mm_in", comm=nxt[1]))
        sv["h1"], sv["proj"] = h1, proj
        bias = pad_lanes(fox_forget_b[i].reshape(1, FH))
        Fp = _fox_prep(proj, bias, F0 // LANES, "fox_prep")[:, :, :FH].transpose(2, 1, 0)
        t = _tile(S, 256)
        qa, ka, va = (_to_heads(proj[:, A0 + j * FW:A0 + (j + 1) * FW], FH).astype(BF16) for j in range(3))
        ones3, zpad = jnp.ones((FH, S, 3), BF16), jnp.zeros((FH, S, HEAD_DIM - 6), BF16)
        q_aug = jnp.concatenate([qa, Fp, ones3, zpad], axis=-1)
        k_aug = jnp.concatenate([ka, ones3, -Fp, zpad], axis=-1)

        def blocks_t(a):
            return a.reshape(FH, S // t, t, a.shape[-1]).transpose(0, 1, 3, 2)

        kT = blocks_t(k_aug)
        oaT, lse_a = host(0, _foxt_fwd(q_aug, k_aug, blocks_t(va), "fox_fwd", comm=nxt[0]))
        oa_b = oaT.transpose(2, 0, 1).reshape(S, FW).astype(BF16)
        sv.update(bias=bias, q_aug=q_aug, k_aug=k_aug, kT=kT, va=va, lse_a=lse_a, oa_b=oa_b)
        ob = _sconv_fwd(proj, fw["shortconv_w"], B0, CW, "sconv_fwd")
        sgb = sgu_b[i].reshape(G, SGU_CHUNK, 1)
        oc = _sgu_fwd(proj, sgu_norm_g[i].reshape(1, SW), sgu_w[i], sgb, C0, SW, "sgu_fwd")
        qd = _rope_apply(proj, D0, DW, tables, 1.0, BF16, "rope_q")
        kd = _rope_apply(proj, D0 + DW, DW, tables, 1.0, BF16, "rope_k")
        qr, kr = _to_residue(qd, hpg), _to_residue(kd, hpg)
        vr = _to_residue(proj[:, D0 + 2 * DW:D0 + 3 * DW].astype(BF16), hpg)
        od_r, lse_r = host(2, _dil_fwd(qr, kr, vr, bpcs, hpg, "dil_fwd", comm=nxt[2]))
        o3, lse3 = _from_residue(od_r, hpg), _from_residue(lse_r, hpg)
        od = _from_heads(_dil_merge(o3, lse3, "dil_merge")).astype(BF16)
        sv.update(ob=ob, oc=oc, sgb=sgb, qr=qr, kr=kr, vr=vr, lse_r=lse_r, o3=o3, lse3=lse3, od=od)
        brs = [oa_b, ob, oc, od]
        wbr = [fw["w_br_fox"], fw["w_br_conv"], fw["w_br_sgu"], fw["w_br_dil"]]
        merged = _merge_fwd(proj, G0, brs, wbr, "merge_fwd")
        x2 = _mm(merged, fw["w_out"], "nn", F32, "mm_out", res=xc)
        h2 = _rms_fwd(x2, norm_ffn_g[i], "rms_ffn")
        up = host(3, _mm(h2, fw["w_up"], "nn", F32, "mm_up", comm=nxt[3]))
        if i + 1 < depth:
            full.append(gathered(got))
        act = _ffn_act(up, fw["ffn_conv_w"], "ffn_act")
        x3 = _mm(act, fw["w_down"], "nn", F32, "mm_down", res=x2)
        h3 = _rms_fwd(x3, norm_ple_g[i], "rms_ple")
        pgl = _mm(h3, fw["w_ple_gate"], "nn", F32, "mm_ple_gate")
        pb = p[i, 0].astype(BF16)
        pe = _mm(pb, fw["w_ple_proj"], "nn", F32, "mm_ple_proj")
        xc = _ple_fwd(x3, pgl, pe, "ple_fwd")
        sv.update(merged=merged, x2=x2, h2=h2, up=up, act=act, x3=x3, h3=h3, pgl=pgl, pb=pb, pe=pe)
        saved.append(sv)

    loss_row, dx, dxb, dg_final = _final_loss(xc, final_norm_g, target, "final_loss")
    loss = lax.psum(loss_row[0, 0], MESH_AXES)

    def in_window(half):
        def view(ref, j):
            return ref.at[pl.ds(half * (D // 2), D // 2),
                          pl.ds(pl.multiple_of(s_traced(j) * LANES, LANES), WB * LANES)]
        return view

    def in_flogit(ref, j):
        return ref.at[:, pl.ds(F0, LANES)]

    def scatter_plans(gs, parts=(0, 1, 2, 3, 4, 5)):
        stacked = lambda shape, dt=BF16: SDS((N_DEV,) + tuple(shape), dt)

        def build(part):
            if part == 0:
                srcs, views = [gs["w_in"], gs["w_in"]], [in_window(0), in_flogit]
                shapes = [stacked((D // 2, WB * LANES)), stacked((D, LANES))]
            elif part == 1:
                srcs, views, shapes = [gs["w_up"]], [placed[0]], [stacked((D, n_up))]
            elif part == 2:
                srcs, views, shapes = [gs["w_down"]], [placed[1]], [stacked((r_down, D))]
            elif part == 3:
                srcs, views = [jnp.stack([gs["w_out"], gs["w_ple_gate"]])], [placed[2]]
                shapes = [stacked((2, r_sq, D))]
            elif part == 5:
                conv = jnp.zeros((N_DEV, 8, cw_pad), F32)
                conv = conv.at[:, 0:3, :n_up].set(jnp.moveaxis(gs["ffn_conv_w"].reshape(3, N_DEV, n_up), 1, 0))
                conv = conv.at[:, 3:6, :n_sc].set(jnp.moveaxis(gs["shortconv_w"].reshape(3, N_DEV, n_sc), 1, 0))
                srcs = [jnp.stack([gs["w_br_fox"], gs["w_br_conv"], gs["w_br_sgu"]]),
                        jnp.stack([gs["w_br_dil"], gs["w_ple_proj"]]), conv]
                views = placed[3:] + [slot]
                shapes = [stacked((3, w_br_fox.shape[1], n_br)), stacked((2, w_br_dil.shape[1], n_sm)),
                          stacked((8, cw_pad), F32)]
            else:
                srcs, views, shapes = [gs["w_in"]], [in_window(1)], [stacked((D // 2, WB * LANES))]
            return srcs, shapes, views, [slot] * len(srcs), False

        return [build(part) for part in parts]

    grads_sh = [None] * depth
    grads_rep = [None] * depth
    scattered = [None] * depth
    for i in reversed(range(depth)):
        fw, sv = full[i], saved[i]
        gs, gr = {}, {}
        pend = scatter_plans(grads_sh[i + 1]) if i + 1 < depth else [None] * 6
        got = [None] * 6

        def host(k, result, pend=pend, got=got):
            if pend[k] is None:
                return result
            got[k] = result[1]
            return result[0]

        proj = sv["proj"]
        dpgl, dpe = _ple_bwd(dx, sv["pgl"], sv["pe"], "ple_bwd")
        gs["w_ple_proj"] = _mm(sv["pb"], dpe, "tn", BF16, "mm_dw_ple_proj")
        gs["w_ple_gate"] = _mm(sv["h3"], dpgl, "tn", BF16, "mm_dw_ple_gate")
        dh3 = _mm(dpgl, fw["w_ple_gate"], "nt", F32, "mm_dh3")
        dx, dxb, gr["norm_ple_g"] = _rms_bwd(sv["x3"], norm_ple_g[i], dh3, dx, "rms_ple_bwd")
        gs["w_down"] = _mm(sv["act"], dxb, "tn", BF16, "mm_dw_down")
        dact = _mm(dxb, fw["w_down"], "nt", BF16, "mm_dact")
        dug, duv, dwg, dwv = _ffn_act_bwd(sv["up"], fw["ffn_conv_w"], dact, "ffn_act_bwd")
        gs["ffn_conv_w"] = jnp.concatenate([dwg, dwv], axis=1)
        dup = jnp.concatenate([dug, duv], axis=1)
        gs["w_up"] = host(2, _mm(sv["h2"], dup, "tn", BF16, "mm_dw_up", comm=pend[2]))
        own_down, own_up = scatter_plans(gs, (2, 1)) if i == 0 else (None, None)
        dh2 = _mm(dup, fw["w_up"], "nt", F32, "mm_dh2", comm=own_down)
        if i == 0:
            dh2, own_down = dh2
        dx, dxb, gr["norm_ffn_g"] = _rms_bwd(sv["x2"], norm_ffn_g[i], dh2, dx, "rms_ffn_bwd")
        gs["w_out"] = _mm(sv["merged"], dxb, "tn", BF16, "mm_dw_out")
        dmerged = _mm(dxb, fw["w_out"], "nt", F32, "mm_dmerged")
        brs = [sv["oa_b"], sv["ob"], sv["oc"], sv["od"]]
        names = ["w_br_fox", "w_br_conv", "w_br_sgu", "w_br_dil"]
        wbr = [fw[n] for n in names]
        dbo, dgl = _merge_bwd(proj, G0, brs, wbr, dmerged, "merge_bwd")
        dos = []
        for j, n in enumerate(names):
            gs[n] = _mm(brs[j], dbo[j], "tn", BF16, "mm_dw_" + n)
            dos.append(_mm(dbo[j], wbr[j], "nt", F32, "mm_do_" + n))
        doa = _to_heads(dos[0], FH).astype(BF16)
        dqT, delta = host(1, _foxt_bwd_dq(sv["q_aug"], sv["k_aug"], sv["kT"], sv["va"], doa, sv["lse_a"],
                                          "fox_bwd_dq", comm=pend[1]))
        dk_aug, dva, dfk = host(0, _foxt_bwd_dkv(sv["q_aug"], sv["k_aug"], sv["va"], doa, sv["lse_a"], delta,
                                                 "fox_bwd_dkv", comm=pend[0]))
        dF = pad_lanes(dfk[:, :, 0].T)
        daf, dbias = _fox_prep_bwd(proj, sv["bias"], F0 // LANES, dF, "fox_prep_bwd")
        gr["fox_forget_b"] = dbias[0, :FH]
        d_a = [dqT[:, :HEAD_DIM].transpose(2, 0, 1).reshape(S, FW).astype(BF16),
               _from_heads(dk_aug[:, :, :HEAD_DIM]).astype(BF16), _from_heads(dva).astype(BF16)]
        dxb_, dgb, dgc, gs["shortconv_w"] = _sconv_bwd(proj, fw["shortconv_w"], B0, CW, dos[1], "sconv_bwd")
        du, dv, dsw, dsb, dsg = _sgu_bwd(proj, sgu_norm_g[i].reshape(1, SW), sgu_w[i], sv["sgb"], C0, SW, dos[2],
                                         "sgu_bwd")
        gr["sgu_w"], gr["sgu_b"], gr["sgu_norm_g"] = dsw, dsb.reshape(G, SGU_CHUNK), dsg[0]
        do3, cf3 = _dil_merge_bwd(sv["o3"], sv["lse3"], _to_heads(dos[3], hpg), "dil_merge_bwd")
        do_r, cf_r = _groups_to_residue(do3), _groups_to_residue(cf3)
        dq_r = host(5, _dil_bwd_dq(sv["qr"], sv["kr"], sv["vr"], do_r, sv["lse_r"], cf_r, bpcs, hpg, "dil_bwd_dq",
                                   comm=pend[5]))
        dk_r, dv_r = host(3, _dil_bwd_dkv(sv["qr"], sv["kr"], sv["vr"], do_r, sv["lse_r"], cf_r, bpcs, hpg,
                                          "dil_bwd_dkv", comm=pend[3]))

        def natural(t):
            return _from_heads(_from_residue(t, hpg).reshape(NG * hpg, S, HEAD_DIM))

        dqd = _rope_apply(natural(dq_r), 0, DW, tables, -1.0, BF16, "rope_q_bwd")
        dkd = _rope_apply(natural(dk_r), 0, DW, tables, -1.0, BF16, "rope_k_bwd")
        dvd = natural(dv_r).astype(BF16)
        dproj = jnp.concatenate(d_a + [dxb_, dgb, dgc, du, dv, dqd, dkd, dvd] + list(dgl) + [daf], axis=1)
        gs["w_in"] = _mm(sv["h1"], dproj, "tn", BF16, "mm_dw_in", comm=own_up)
        if i == 0:
            gs["w_in"], own_up = gs["w_in"]
        dh1 = host(4, _mm(dproj, fw["w_in"], "nt", F32, "mm_dh1", comm=pend[4]))
        if i + 1 < depth:
            scattered[i + 1] = got
        dx, dxb, gr["norm_mix_g"] = _rms_bwd(sv["x"], norm_mix_g[i], dh1, dx, "rms_mix_bwd")
        gr["norm_mix_g"], gr["norm_ffn_g"], gr["norm_ple_g"] = (gr[n][0] for n in
                                                               ("norm_mix_g", "norm_ffn_g", "norm_ple_g"))
        grads_sh[i], grads_rep[i] = gs, gr
    grad_x = dx.reshape(1, S, D)

    last = _comm("comm_scatter_grads", *one_plan(scatter_plans(grads_sh[0], (0, 3, 4, 5))))
    scattered[0] = [last[0:2], own_up, own_down, last[2:3], last[3:4], last[4:7]]

    def sum_sources(t, name):
        lead = t.shape[1:]
        return _sum8(t.reshape(N_DEV, -1, lead[-1]), name).reshape(lead)

    grads = {n: [None] * depth for n in SHARDED}
    for i in range(depth):
        (gw, gf), (gup,), (gdown,), (gsq,), (gw2,), (gbr, gsm, gcv) = scattered[i]
        rows = jnp.concatenate([sum_sources(gw, "sum_w_in"), sum_sources(gw2, "sum_w_in")], axis=0)
        win = jnp.concatenate([rows, sum_sources(gf, "sum_w_in_flogit")], axis=1)
        grads["w_in"][i] = on_my_index(from_window, win)
        grads["w_up"][i] = sum_sources(gup, "sum_w_up")
        grads["w_down"][i] = sum_sources(gdown, "sum_w_down")
        sq, br, sm = sum_sources(gsq, "sum_sq"), sum_sources(gbr, "sum_br"), sum_sources(gsm, "sum_sm")
        cv = sum_sources(gcv, "sum_conv")
        grads["w_out"][i], grads["w_ple_gate"][i] = sq[0], sq[1]
        grads["w_br_fox"][i], grads["w_br_conv"][i], grads["w_br_sgu"][i] = br[0], br[1], br[2]
        grads["w_br_dil"][i], grads["w_ple_proj"][i] = sm[0], sm[1]
        grads["ffn_conv_w"][i], grads["shortconv_w"][i] = cv[0:3, :n_up], cv[3:6, :n_sc]
    grads = {n: jnp.stack(v, axis=0) for n, v in grads.items()}

    rep_parts = [jnp.stack([grads_rep[i][n] for i in range(depth)], axis=0).reshape(-1) for n in REPLICATED]
    rep_parts.append(dg_final.reshape(-1))
    rep_sizes = [int(r.shape[0]) for r in rep_parts]
    rep_offs = [sum(rep_sizes[:i]) for i in range(len(rep_sizes) + 1)]
    rep = _sum8(_exchange(_pack(rep_parts, F32, 8), True, "comm_gather_small"), "sum_small").reshape(-1)
    for n, a, b in zip(REPLICATED + ("final_norm_g",), rep_offs[:-1], rep_offs[1:]):
        grads[n] = rep[a:b].reshape(W[n].shape)

    deltas, new_m, new_v = {}, {}, {}
    for n in WEIGHTS:
        deltas[n], new_m[n], new_v[n] = _adamw(W[n], grads[n], given["m_" + n], given["v_" + n], "adamw_" + n)
    return (loss, grad_x, *[grads[n] for n in WEIGHTS], *[deltas[n] for n in WEIGHTS],
            *[new_m[n] for n in WEIGHTS], *[new_v[n] for n in WEIGHTS])
```

```python
import functools

import jax
import jax.numpy as jnp
from jax import lax
from jax.experimental import pallas as pl
from jax.experimental.pallas import tpu as pltpu

F32 = jnp.float32
BF16 = jnp.bfloat16
EPS = 1e-6
NEG = -1e30
HEAD_DIM = 64
SGU_CHUNK = 128
DIL_PATTERNS = ((128, 1), (512, 4), (2048, 16))
DIL_SPAN = 128
ROPE_THETA = 500000.0
ROPE_DIM = 16
N_DEV = 8
LANES = 128
VMEM_LIMIT = 56 * 1024 * 1024
ADAM_LR, ADAM_B1, ADAM_B2, ADAM_EPS, ADAM_WD, ADAM_STEP = 0.001, 0.9, 0.999, 1e-08, 0.01, 10
MESH_AXES = ("x", "y", "c")

BS = pl.BlockSpec
SDS = jax.ShapeDtypeStruct


def _cp(*sem):
    return pltpu.CompilerParams(dimension_semantics=sem, vmem_limit_bytes=VMEM_LIMIT)


def _tile(dim, pref, unit=LANES):
    if dim % unit:
        return dim
    best, t = unit, unit
    while t <= min(dim, pref):
        if dim % t == 0:
            best = t
        t += unit
    return best


def _sigmoid(z):
    return 1.0 / (1.0 + jnp.exp(-z))


def _exchange(x, gather, name):
    shape = x.shape if not gather else (N_DEV,) + x.shape

    def body(x_ref, o_ref, send_sems, recv_sems, local_sem):
        ix, iy, ic = lax.axis_index("x"), lax.axis_index("y"), lax.axis_index("c")
        me = 4 * ix + 2 * iy + ic

        def src(j):
            return x_ref if gather else x_ref.at[j]

        local = pltpu.make_async_copy(src(me), o_ref.at[me], local_sem)
        local.start()
        sends, recvs = [], []
        for r in range(1, N_DEV):
            px = 1 - ix if (r >> 2) & 1 else ix
            py = 1 - iy if (r >> 1) & 1 else iy
            pc = 1 - ic if r & 1 else ic
            peer = 4 * px + 2 * py + pc
            sends.append(pltpu.make_async_remote_copy(
                src_ref=src(peer), dst_ref=o_ref.at[me], send_sem=send_sems.at[r - 1], recv_sem=recv_sems.at[r - 1],
                device_id=(px, py, pc), device_id_type=pl.DeviceIdType.MESH))
            recvs.append(pltpu.make_async_remote_copy(
                src_ref=src(peer), dst_ref=o_ref.at[peer], send_sem=send_sems.at[r - 1], recv_sem=recv_sems.at[r - 1],
                device_id=(px, py, pc), device_id_type=pl.DeviceIdType.MESH))
        for cp in sends:
            cp.start()
        for cp in recvs:
            cp.wait_recv()
        for cp in sends:
            cp.wait_send()
        local.wait()

    return pl.pallas_call(
        body, name=name, out_shape=SDS(shape, x.dtype),
        in_specs=[BS(memory_space=pl.ANY)], out_specs=BS(memory_space=pl.ANY),
        scratch_shapes=[pltpu.SemaphoreType.DMA((N_DEV - 1,)), pltpu.SemaphoreType.DMA((N_DEV - 1,)),
                        pltpu.SemaphoreType.DMA],
    )(x)


def _comm(name, srcs, out_shapes, src_views, dst_views, relay=False):
    n = len(srcs)

    def body(*refs):
        copies = _comm_copies(refs[:n], refs[n:2 * n], refs[2 * n:], src_views, dst_views, relay)
        _comm_start(copies)
        _comm_wait(copies)

    anyspec = BS(memory_space=pl.ANY)
    return pl.pallas_call(
        body, name=name, out_shape=tuple(out_shapes), in_specs=[anyspec] * n, out_specs=(anyspec,) * n,
        scratch_shapes=_comm_sems(n))(*srcs)


def _comm_sems(n):
    return [pltpu.SemaphoreType.DMA((N_DEV - 1, n)), pltpu.SemaphoreType.DMA((N_DEV - 1, n)),
            pltpu.SemaphoreType.DMA((n,))]


SIBLING = 1
OTHER_CHIP_SAME_CORE = (2, 4, 6)


def _comm_copies(s_refs, o_refs, sems, src_views, dst_views, relay=False):
    send_sems, recv_sems, local_sems = sems
    n = len(s_refs)
    ix, iy, ic = lax.axis_index("x"), lax.axis_index("y"), lax.axis_index("c")
    me = 4 * ix + 2 * iy + ic

    def at(r):
        px = 1 - ix if (r >> 2) & 1 else ix
        py = 1 - iy if (r >> 1) & 1 else iy
        pc = 1 - ic if r & 1 else ic
        return (px, py, pc), 4 * px + 2 * py + pc

    def copy(r, k, src, dst, to):
        return pltpu.make_async_remote_copy(src_ref=src, dst_ref=dst, send_sem=send_sems.at[r - 1, k],
                                            recv_sem=recv_sems.at[r - 1, k], device_id=to,
                                            device_id_type=pl.DeviceIdType.MESH)

    local = [pltpu.make_async_copy(src_views[k](s_refs[k], me), dst_views[k](o_refs[k], me), local_sems.at[k])
             for k in range(n)]
    sends, passed, arrivals = [], [], []
    for r in range(1, N_DEV):
        to, peer = at(r)
        for k in range(n):
            arrival = copy(r, k, src_views[k](s_refs[k], peer), dst_views[k](o_refs[k], peer), to)
            direct = not relay or r == SIBLING or r in OTHER_CHIP_SAME_CORE
            if direct:
                sends.append(copy(r, k, src_views[k](s_refs[k], peer), dst_views[k](o_refs[k], me), to))
            if relay and r in OTHER_CHIP_SAME_CORE:
                landed = dst_views[k](o_refs[k], peer)
                passed.append((arrival, copy(r ^ SIBLING, k, landed, landed, at(SIBLING)[0])))
            else:
                arrivals.append(arrival)
    return local, sends, passed, arrivals


def _comm_start(copies):
    local, sends, _, _ = copies
    for cp in local + sends:
        cp.start()


def _comm_wait(copies):
    local, sends, passed, arrivals = copies
    for arrival, onward in passed:
        arrival.wait_recv()
        onward.start()
    for cp in arrivals:
        cp.wait_recv()
    for cp in sends + [onward for _, onward in passed]:
        cp.wait_send()
    for cp in local:
        cp.wait()


def _call(body, *, name, grid, in_specs, out_specs, out_shape, args, scratch_shapes=(), sem=None, comm=None):
    single = not isinstance(out_shape, (tuple, list))
    out_shape = (out_shape,) if single else tuple(out_shape)
    out_specs = (out_specs,) if single else tuple(out_specs)
    if comm is None:
        outs = pl.pallas_call(body, name=name, grid=grid, in_specs=list(in_specs), out_specs=out_specs,
                              out_shape=out_shape, scratch_shapes=list(scratch_shapes),
                              compiler_params=_cp(*sem))(*args)
        return outs[0] if single else outs
    srcs, shapes, src_views, dst_views, relay = comm
    n, ni, no, ns = len(srcs), len(in_specs), len(out_shape), len(scratch_shapes)

    def hosted(*refs):
        ins, c_in = refs[:ni], refs[ni:ni + n]
        outs, c_out = refs[ni + n:ni + n + no], refs[ni + n + no:ni + 2 * n + no]
        scr, sems = refs[ni + 2 * n + no:ni + 2 * n + no + ns], refs[ni + 2 * n + no + ns:]
        ids = [pl.program_id(a) for a in range(len(grid))]
        first = functools.reduce(jnp.logical_and, [i == 0 for i in ids])
        last = functools.reduce(jnp.logical_and, [i == g - 1 for i, g in zip(ids, grid)])

        @pl.when(first)
        def _():
            _comm_start(_comm_copies(c_in, c_out, sems, src_views, dst_views, relay))

        body(*ins, *outs, *scr)

        @pl.when(last)
        def _():
            _comm_wait(_comm_copies(c_in, c_out, sems, src_views, dst_views, relay))

    anyspec = BS(memory_space=pl.ANY)
    outs = pl.pallas_call(
        hosted, name=name, grid=grid, in_specs=list(in_specs) + [anyspec] * n,
        out_specs=out_specs + (anyspec,) * n, out_shape=out_shape + tuple(shapes),
        scratch_shapes=list(scratch_shapes) + _comm_sems(n),
        compiler_params=_cp(*(["arbitrary"] * len(grid))))(*args, *srcs)
    return (outs[0] if single else outs[:no]), outs[no:]


def _assemble(pieces, starts, wb, npb, name):
    _, K, pw = pieces.shape
    tr = _tile(K, 256, 16)

    def body(p_ref, o_ref):
        for b in range(npb - 1):
            acc = None
            for j in range(N_DEV):
                if starts[j] <= b < starts[j] + wb:
                    blk = p_ref[j, :, (b - starts[j]) * LANES:(b - starts[j] + 1) * LANES]
                    acc = blk if acc is None else acc + blk
            o_ref[:, b * LANES:(b + 1) * LANES] = acc
        acc = p_ref[0, :, wb * LANES:]
        for j in range(1, N_DEV):
            acc = acc + p_ref[j, :, wb * LANES:]
        o_ref[:, (npb - 1) * LANES:] = acc

    return pl.pallas_call(
        body, name=name, grid=(K // tr,), out_shape=SDS((K, npb * LANES), pieces.dtype),
        in_specs=[BS((N_DEV, tr, pw), lambda i: (0, i, 0))], out_specs=BS((tr, npb * LANES), lambda i: (i, 0)),
        compiler_params=_cp("parallel"))(pieces)


def _sum8(x, name):
    _, R, C = x.shape
    tr = _tile(R, 256, 16)

    def body(x_ref, o_ref):
        acc = x_ref[0].astype(F32)
        for j in range(1, N_DEV):
            acc = acc + x_ref[j].astype(F32)
        o_ref[...] = acc

    return pl.pallas_call(
        body, name=name, grid=(R // tr,), out_shape=SDS((R, C), F32),
        in_specs=[BS((N_DEV, tr, C), lambda i: (0, i, 0))], out_specs=BS((tr, C), lambda i: (i, 0)),
        compiler_params=_cp("parallel"))(x)


_DIMS = {"nn": ((1,), (0,)), "nt": ((1,), (1,)), "tn": ((0,), (0,))}


def _mm(a, b, mode, out_dtype, name, res=None, comm=None):
    if mode == "nn":
        (M, K), N = a.shape, b.shape[1]
    elif mode == "nt":
        (M, K), N = a.shape, b.shape[0]
    else:
        (K, M), N = a.shape, b.shape[1]
    tm, tn, tk = _tile(M, 1024), _tile(N, 1024), _tile(K, 3072)
    nk = K // tk
    a_spec = BS((tk, tm), lambda i, j, k: (k, i)) if mode == "tn" else BS((tm, tk), lambda i, j, k: (i, k))
    b_spec = BS((tn, tk), lambda i, j, k: (j, k)) if mode == "nt" else BS((tk, tn), lambda i, j, k: (k, j))
    o_spec = BS((tm, tn), lambda i, j, k: (i, j))
    dims = (_DIMS[mode], ((), ()))

    def body(a_ref, b_ref, *rest):
        if nk == 1:
            r = lax.dot_general(a_ref[...].astype(BF16), b_ref[...].astype(BF16), dims, preferred_element_type=F32)
            if res is not None:
                r = r + rest[0][...]
            rest[-1][...] = r.astype(rest[-1].dtype)
            return
        o_ref, acc = rest[-2], rest[-1]
        k = pl.program_id(2)

        @pl.when(k == 0)
        def _():
            acc[...] = jnp.zeros_like(acc)

        acc[...] += lax.dot_general(a_ref[...].astype(BF16), b_ref[...].astype(BF16), dims,
                                    preferred_element_type=F32)

        @pl.when(k == nk - 1)
        def _():
            r = acc[...]
            if res is not None:
                r = r + rest[0][...]
            o_ref[...] = r.astype(o_ref.dtype)

    ins, specs = [a, b], [a_spec, b_spec]
    if res is not None:
        ins.append(res)
        specs.append(o_spec)
    return _call(body, name=name, grid=(M // tm, N // tn, nk), out_shape=SDS((M, N), out_dtype), in_specs=specs,
                 out_specs=o_spec, scratch_shapes=[] if nk == 1 else [pltpu.VMEM((tm, tn), F32)],
                 sem=("parallel", "parallel", "arbitrary"), args=ins, comm=comm)


def _rms_fwd(x, g, name):
    S, D = x.shape
    ts = _tile(S, 256, 8)

    def body(x_ref, g_ref, o_ref):
        xv = x_ref[...]
        rstd = lax.rsqrt(jnp.mean(xv * xv, axis=-1, keepdims=True) + EPS)
        o_ref[...] = (xv * rstd * g_ref[...]).astype(o_ref.dtype)

    return pl.pallas_call(
        body, name=name, grid=(S // ts,), out_shape=SDS((S, D), BF16),
        in_specs=[BS((ts, D), lambda i: (i, 0)), BS((1, D), lambda i: (0, 0))],
        out_specs=BS((ts, D), lambda i: (i, 0)), compiler_params=_cp("parallel"))(x, g.reshape(1, D))


def _rms_bwd(x, g, dh, dres, name):
    S, D = x.shape
    ts = _tile(S, 256, 8)

    def body(x_ref, g_ref, dh_ref, dres_ref, dx_ref, dxb_ref, dg_ref):
        xv = x_ref[...]
        rstd = lax.rsqrt(jnp.mean(xv * xv, axis=-1, keepdims=True) + EPS)
        xn = xv * rstd
        dhv = dh_ref[...]

        @pl.when(pl.program_id(0) == 0)
        def _():
            dg_ref[...] = jnp.zeros_like(dg_ref)

        dg_ref[...] += jnp.sum(dhv * xn, axis=0, keepdims=True)
        dhn = dhv * g_ref[...]
        dxv = rstd * (dhn - xn * jnp.mean(dhn * xn, axis=-1, keepdims=True)) + dres_ref[...]
        dx_ref[...] = dxv
        dxb_ref[...] = dxv.astype(BF16)

    row = BS((ts, D), lambda i: (i, 0))
    vec = BS((1, D), lambda i: (0, 0))
    return pl.pallas_call(
        body, name=name, grid=(S // ts,), out_shape=(SDS((S, D), F32), SDS((S, D), BF16), SDS((1, D), F32)),
        in_specs=[row, vec, row, row], out_specs=(row, row, vec), compiler_params=_cp("arbitrary"),
    )(x, g.reshape(1, D), dh, dres)


def _final_loss(x, g, target, name):
    S, D = x.shape
    ts = _tile(S, 256, 8)

    def body(x_ref, g_ref, t_ref, loss_ref, dx_ref, dxb_ref, dg_ref):
        xv = x_ref[...]
        rstd = lax.rsqrt(jnp.mean(xv * xv, axis=-1, keepdims=True) + EPS)
        xn = xv * rstd
        gv = g_ref[...]
        e = xn * gv - t_ref[...]

        @pl.when(pl.program_id(0) == 0)
        def _():
            dg_ref[...] = jnp.zeros_like(dg_ref)
            loss_ref[...] = jnp.zeros_like(loss_ref)

        loss_ref[...] += 0.5 * jnp.sum(jnp.mean(e * e, axis=-1, keepdims=True), axis=0, keepdims=True)
        dy = e * (1.0 / D)
        dg_ref[...] += jnp.sum(dy * xn, axis=0, keepdims=True)
        dhn = dy * gv
        dxv = rstd * (dhn - xn * jnp.mean(dhn * xn, axis=-1, keepdims=True))
        dx_ref[...] = dxv
        dxb_ref[...] = dxv.astype(BF16)

    row = BS((ts, D), lambda i: (i, 0))
    vec = BS((1, D), lambda i: (0, 0))
    return pl.pallas_call(
        body, name=name, grid=(S // ts,),
        out_shape=(SDS((1, LANES), F32), SDS((S, D), F32), SDS((S, D), BF16), SDS((1, D), F32)),
        in_specs=[row, vec, row], out_specs=(BS((1, LANES), lambda i: (0, 0)), row, row, vec),
        compiler_params=_cp("arbitrary"))(x, g.reshape(1, D), target)


def _shift_down(z, k):
    rows = lax.broadcasted_iota(jnp.int32, z.shape, 0)
    return jnp.where(rows >= k, pltpu.roll(z, k, 0), 0.0)


def _shift_up(z, k):
    n = z.shape[0]
    rows = lax.broadcasted_iota(jnp.int32, z.shape, 0)
    return jnp.where(rows < n - k, pltpu.roll(z, n - k, 0), 0.0)


def _cumsum_rows(z, reverse=False):
    n = z.shape[0]
    k = 1
    while k < n:
        z = z + (_shift_up(z, k) if reverse else _shift_down(z, k))
        k *= 2
    return z


def _conv3(z, w_ref):
    return w_ref[0:1, :] * _shift_down(z, 2) + w_ref[1:2, :] * _shift_down(z, 1) + w_ref[2:3, :] * z


def _conv3_bwd(z, dy, w_ref, dw_ref):
    dw_ref[0:1, :] = jnp.sum(dy * _shift_down(z, 2), axis=0, keepdims=True)
    dw_ref[1:2, :] = jnp.sum(dy * _shift_down(z, 1), axis=0, keepdims=True)
    dw_ref[2:3, :] = jnp.sum(dy * z, axis=0, keepdims=True)
    return w_ref[2:3, :] * dy + w_ref[1:2, :] * _shift_up(dy, 1) + w_ref[0:1, :] * _shift_up(dy, 2)


def _fox_prep(proj, bias, col, name):
    S = proj.shape[0]

    def body(a_ref, b_ref, f_ref):
        z = a_ref[...] + b_ref[...]
        logf = jnp.minimum(z, 0.0) - jnp.log(1.0 + jnp.exp(-jnp.abs(z)))
        rest = _cumsum_rows(logf)
        for k in range(3):
            piece = rest.astype(BF16)
            f_ref[k] = piece
            rest = rest - piece.astype(F32)

    return pl.pallas_call(
        body, name=name, grid=(1,), out_shape=SDS((3, S, LANES), BF16),
        in_specs=[BS((S, LANES), lambda i: (0, col)), BS((1, LANES), lambda i: (0, 0))],
        out_specs=BS((3, S, LANES), lambda i: (0, 0, 0)), compiler_params=_cp("arbitrary"))(proj, bias)


def _fox_prep_bwd(proj, bias, col, df, name):
    S = proj.shape[0]

    def body(a_ref, b_ref, df_ref, da_ref, db_ref):
        z = a_ref[...] + b_ref[...]
        da = _cumsum_rows(df_ref[...], reverse=True) * _sigmoid(-z)
        da_ref[...] = da.astype(BF16)
        db_ref[...] = jnp.sum(da, axis=0, keepdims=True)

    full = BS((S, LANES), lambda i: (0, 0))
    vec = BS((1, LANES), lambda i: (0, 0))
    return pl.pallas_call(
        body, name=name, grid=(1,), out_shape=(SDS((S, LANES), BF16), SDS((1, LANES), F32)),
        in_specs=[BS((S, LANES), lambda i: (0, col)), vec, full], out_specs=(full, vec),
        compiler_params=_cp("arbitrary"))(proj, bias, df)


def _fox_scores(qs, kv, fqv, fkv, i, j, t, masked):
    s = lax.dot_general(qs, kv, (_DIMS["nt"], ((), ())), preferred_element_type=F32) + fqv - fkv
    if not masked:
        return s
    row = i * t + lax.broadcasted_iota(jnp.int32, (t, t), 0)
    col = j * t + lax.broadcasted_iota(jnp.int32, (t, t), 1)
    return jnp.where(col <= row, s, NEG)


def _fox_fwd(q, k, v, fq, fk, name, comm=None):
    H, S, Dh = q.shape
    t = _tile(S, 256)
    n = S // t
    scale = Dh ** -0.5

    def body(q_ref, k_ref, v_ref, fq_ref, fk_ref, o_ref, lse_ref):
        i = pl.program_id(1)
        qs, fqv = q_ref[0] * scale, fq_ref[0]

        def step(masked, j, carry):
            m, l, acc = carry
            s = _fox_scores(qs, k_ref[0, j], fqv, fk_ref[0, j], i, j, t, masked)
            mn = jnp.maximum(m, jnp.max(s, axis=-1, keepdims=True))
            al = jnp.exp(m - mn)
            p = jnp.exp(s - mn)
            l = l * al + jnp.sum(p, axis=-1, keepdims=True)
            acc = acc * al + jnp.dot(p.astype(BF16), v_ref[0, j], preferred_element_type=F32)
            return mn, l, acc

        init = (jnp.full((t, 1), NEG, F32), jnp.zeros((t, 1), F32), jnp.zeros((t, Dh), F32))
        m, l, acc = step(True, i, lax.fori_loop(0, i, functools.partial(step, False), init))
        o_ref[0] = acc / l
        lse_ref[0] = m + jnp.log(l)

    blk = BS((1, t, Dh), lambda h, i: (h, i, 0))
    col = BS((1, t, 1), lambda h, i: (h, i, 0))
    res = BS((1, n, t, Dh), lambda h, i: (h, 0, 0, 0))
    return _call(
        body, name=name, grid=(H, n), out_shape=(SDS((H, S, Dh), F32), SDS((H, S, 1), F32)),
        in_specs=[blk, res, res, col, BS((1, n, 1, t), lambda h, i: (h, 0, 0, 0))], out_specs=(blk, col),
        sem=("parallel", "arbitrary"), args=(q, k.reshape(H, n, t, Dh), v.reshape(H, n, t, Dh), fq, fk), comm=comm)


def _fox_bwd_dq(q, k, v, fq, fk, do, lse, name, comm=None):
    H, S, Dh = q.shape
    t = _tile(S, 256)
    n = S // t
    scale = Dh ** -0.5

    def body(q_ref, k_ref, v_ref, fq_ref, fk_ref, do_ref, lse_ref, dq_ref, dl_ref):
        i = pl.program_id(1)
        qs, fqv, dob, lsev = q_ref[0] * scale, fq_ref[0], do_ref[0], lse_ref[0]

        def p_dp(masked, j):
            p = jnp.exp(_fox_scores(qs, k_ref[0, j], fqv, fk_ref[0, j], i, j, t, masked) - lsev)
            return p, lax.dot_general(dob, v_ref[0, j], (_DIMS["nt"], ((), ())), preferred_element_type=F32)

        def sum_step(masked, j, delta):
            p, dp = p_dp(masked, j)
            return delta + jnp.sum(p * dp, axis=-1, keepdims=True)

        delta = sum_step(True, i, lax.fori_loop(0, i, functools.partial(sum_step, False), jnp.zeros((t, 1), F32)))

        def step(masked, j, dq):
            p, dp = p_dp(masked, j)
            return dq + jnp.dot((p * (dp - delta)).astype(BF16), k_ref[0, j], preferred_element_type=F32)

        dq = step(True, i, lax.fori_loop(0, i, functools.partial(step, False), jnp.zeros((t, Dh), F32)))
        dq_ref[0] = dq * scale
        dl_ref[0] = delta

    blk = BS((1, t, Dh), lambda h, i: (h, i, 0))
    col = BS((1, t, 1), lambda h, i: (h, i, 0))
    res = BS((1, n, t, Dh), lambda h, i: (h, 0, 0, 0))
    return _call(
        body, name=name, grid=(H, n), out_shape=(SDS((H, S, Dh), F32), SDS((H, S, 1), F32)),
        in_specs=[blk, res, res, col, BS((1, n, 1, t), lambda h, i: (h, 0, 0, 0)), blk, col],
        out_specs=(blk, col), sem=("parallel", "arbitrary"), comm=comm,
        args=(q, k.reshape(H, n, t, Dh), v.reshape(H, n, t, Dh), fq, fk, do, lse))


def _fox_bwd_dkv(q, k, v, fq, fk, do, lse, delta, name, comm=None):
    H, S, Dh = q.shape
    t = _tile(S, 256)
    n = S // t
    scale = Dh ** -0.5
    tn_dims = (_DIMS["tn"], ((), ()))

    def body(q_ref, k_ref, v_ref, fq_ref, fk_ref, do_ref, lse_ref, dl_ref, dk_ref, dv_ref, dfk_ref):
        j = pl.program_id(1)
        kv, vv, fkv = k_ref[0], v_ref[0], fk_ref[0, 0]

        def step(masked, i, carry):
            dk, dv, dfk = carry
            qs, dob = q_ref[0, i] * scale, do_ref[0, i]
            p = jnp.exp(_fox_scores(qs, kv, fq_ref[0, i], fkv, i, j, t, masked) - lse_ref[0, i])
            dv = dv + lax.dot_general(p.astype(BF16), dob, tn_dims, preferred_element_type=F32)
            dp = lax.dot_general(dob, vv, (_DIMS["nt"], ((), ())), preferred_element_type=F32)
            ds = p * (dp - dl_ref[0, i])
            dk = dk + lax.dot_general(ds.astype(BF16), qs, tn_dims, preferred_element_type=F32)
            return dk, dv, dfk - jnp.sum(ds, axis=0, keepdims=True)

        init = (jnp.zeros((t, Dh), F32), jnp.zeros((t, Dh), F32), jnp.zeros((1, t), F32))
        dk, dv, dfk = lax.fori_loop(j + 1, n, functools.partial(step, False), step(True, j, init))
        dk_ref[0] = dk
        dv_ref[0] = dv
        dfk_ref[0, 0] = dfk

    blk = BS((1, t, Dh), lambda h, j: (h, j, 0))
    res = BS((1, n, t, Dh), lambda h, j: (h, 0, 0, 0))
    rcol = BS((1, n, t, 1), lambda h, j: (h, 0, 0, 0))
    frow = BS((1, 1, 1, t), lambda h, j: (h, j, 0, 0))
    return _call(
        body, name=name, grid=(H, n),
        out_shape=(SDS((H, S, Dh), F32), SDS((H, S, Dh), F32), SDS((H, n, 1, t), F32)),
        in_specs=[res, blk, blk, rcol, frow, res, rcol, rcol], out_specs=(blk, blk, frow),
        sem=("parallel", "arbitrary"), comm=comm,
        args=(q.reshape(H, n, t, Dh), k, v, fq.reshape(H, n, t, 1), fk, do.astype(BF16).reshape(H, n, t, Dh),
              lse.reshape(H, n, t, 1), delta.reshape(H, n, t, 1)))


FOX_HEADS_PER_STEP = 4


def _fox_q_scale(Dh):
    lane = lax.broadcasted_iota(jnp.int32, (1, 2 * Dh), 1)
    return jnp.where(lane < Dh, Dh ** -0.5, 1.0).astype(BF16)


def _foxt_scores(ka, qs, i, j, t, masked):
    s = lax.dot_general(ka, qs, (_DIMS["nt"], ((), ())), preferred_element_type=F32)
    if not masked:
        return s
    key = j * t + lax.broadcasted_iota(jnp.int32, (t, t), 0)
    qry = i * t + lax.broadcasted_iota(jnp.int32, (t, t), 1)
    return jnp.where(key <= qry, s, NEG)


def _foxt_fwd(q_aug, k_aug, vT, name, comm=None):
    H, S, W = q_aug.shape
    Dh = W // 2
    t = _tile(S, 256)
    n = S // t
    hb = FOX_HEADS_PER_STEP

    def body(q_ref, k_ref, v_ref, o_ref, lse_ref):
        i = pl.program_id(1)
        qs = [q_ref[h] * _fox_q_scale(Dh) for h in range(hb)]

        def step(masked, j, carry):
            out = []
            for h in range(hb):
                m, l, acc = carry[h]
                s = _foxt_scores(k_ref[h, j], qs[h], i, j, t, masked)
                mn = jnp.maximum(m, jnp.max(s, axis=0, keepdims=True))
                al = jnp.exp(m - mn)
                p = jnp.exp(s - mn)
                l = l * al + jnp.sum(p, axis=0, keepdims=True)
                acc = acc * al + jnp.dot(v_ref[h, j], p.astype(BF16), preferred_element_type=F32)
                out.append((mn, l, acc))
            return tuple(out)

        init = tuple((jnp.full((1, t), NEG, F32), jnp.zeros((1, t), F32), jnp.zeros((Dh, t), F32))
                     for _ in range(hb))
        fin = step(True, i, lax.fori_loop(0, i, functools.partial(step, False), init))
        for h in range(hb):
            m, l, acc = fin[h]
            o_ref[h] = acc / l
            lse_ref[h] = m + jnp.log(l)

    return _call(
        body, name=name, grid=(H // hb, n), out_shape=(SDS((H, Dh, S), F32), SDS((H, 1, S), F32)),
        in_specs=[BS((hb, t, W), lambda g, i: (g, i, 0)), BS((hb, n, t, W), lambda g, i: (g, 0, 0, 0)),
                  BS((hb, n, Dh, t), lambda g, i: (g, 0, 0, 0))],
        out_specs=(BS((hb, Dh, t), lambda g, i: (g, 0, i)), BS((hb, 1, t), lambda g, i: (g, 0, i))),
        sem=("parallel", "arbitrary"), args=(q_aug, k_aug.reshape(H, n, t, W), vT), comm=comm)


def _foxt_bwd_dq(q_aug, k_aug, kT, v, do, lse, name, comm=None):
    H, S, W = q_aug.shape
    Dh = W // 2
    t = _tile(S, 256)
    n = S // t
    hb = FOX_HEADS_PER_STEP

    def body(q_ref, k_ref, kt_ref, v_ref, do_ref, lse_ref, dq_ref, dl_ref):
        i = pl.program_id(1)
        qs = [q_ref[h] * _fox_q_scale(Dh) for h in range(hb)]

        def p_dp(h, masked, j):
            p = jnp.exp(_foxt_scores(k_ref[h, j], qs[h], i, j, t, masked) - lse_ref[h])
            return p, lax.dot_general(v_ref[h, j], do_ref[h], (_DIMS["nt"], ((), ())), preferred_element_type=F32)

        def sum_step(masked, j, delta):
            out = []
            for h in range(hb):
                p, dp = p_dp(h, masked, j)
                out.append(delta[h] + jnp.sum(p * dp, axis=0, keepdims=True))
            return tuple(out)

        zero = tuple(jnp.zeros((1, t), F32) for _ in range(hb))
        delta = sum_step(True, i, lax.fori_loop(0, i, functools.partial(sum_step, False), zero))

        def step(masked, j, dq):
            out = []
            for h in range(hb):
                p, dp = p_dp(h, masked, j)
                ds = (p * (dp - delta[h])).astype(BF16)
                out.append(dq[h] + jnp.dot(kt_ref[h, j], ds, preferred_element_type=F32))
            return tuple(out)

        zero = tuple(jnp.zeros((W, t), F32) for _ in range(hb))
        dq = step(True, i, lax.fori_loop(0, i, functools.partial(step, False), zero))
        for h in range(hb):
            dq_ref[h] = dq[h] * Dh ** -0.5
            dl_ref[h] = delta[h]

    row = BS((hb, 1, t), lambda g, i: (g, 0, i))
    blk = lambda w: BS((hb, t, w), lambda g, i: (g, i, 0))
    return _call(
        body, name=name, grid=(H // hb, n), out_shape=(SDS((H, W, S), F32), SDS((H, 1, S), F32)),
        in_specs=[blk(W), BS((hb, n, t, W), lambda g, i: (g, 0, 0, 0)), BS((hb, n, W, t), lambda g, i: (g, 0, 0, 0)),
                  BS((hb, n, t, Dh), lambda g, i: (g, 0, 0, 0)), blk(Dh), row],
        out_specs=(BS((hb, W, t), lambda g, i: (g, 0, i)), row), sem=("parallel", "arbitrary"), comm=comm,
        args=(q_aug, k_aug.reshape(H, n, t, W), kT, v.reshape(H, n, t, Dh), do, lse))


def _foxt_bwd_dkv(q_aug, k_aug, v, do, lse, delta, name, comm=None):
    H, S, W = q_aug.shape
    Dh = W // 2
    t = _tile(S, 256)
    n = S // t
    hb = FOX_HEADS_PER_STEP

    def body(q_ref, k_ref, v_ref, do_ref, lse_ref, dl_ref, dk_ref, dv_ref, dfk_ref):
        j = pl.program_id(1)

        def step(masked, i, carry):
            out = []
            for h in range(hb):
                dk, dv, dfk = carry[h]
                qs, dob = q_ref[h, i] * _fox_q_scale(Dh), do_ref[h, i]
                p = jnp.exp(_foxt_scores(k_ref[h], qs, i, j, t, masked) - lse_ref[h, i])
                dv = dv + jnp.dot(p.astype(BF16), dob, preferred_element_type=F32)
                dp = lax.dot_general(v_ref[h], dob, (_DIMS["nt"], ((), ())), preferred_element_type=F32)
                ds = p * (dp - dl_ref[h, i])
                dk = dk + jnp.dot(ds.astype(BF16), qs, preferred_element_type=F32)
                out.append((dk, dv, dfk - jnp.sum(ds, axis=1, keepdims=True)))
            return tuple(out)

        init = tuple((jnp.zeros((t, W), F32), jnp.zeros((t, Dh), F32), jnp.zeros((t, 1), F32)) for _ in range(hb))
        fin = lax.fori_loop(j + 1, n, functools.partial(step, False), step(True, j, init))
        for h in range(hb):
            dk_ref[h], dv_ref[h], dfk_ref[h] = fin[h]

    blk = lambda w: BS((hb, t, w), lambda g, j: (g, j, 0))
    res = lambda w: BS((hb, n, t, w), lambda g, j: (g, 0, 0, 0))
    rows = BS((hb, n, 1, t), lambda g, j: (g, 0, 0, 0))
    return _call(
        body, name=name, grid=(H // hb, n),
        out_shape=(SDS((H, S, W), F32), SDS((H, S, Dh), F32), SDS((H, S, 1), F32)),
        in_specs=[res(W), blk(W), blk(Dh), res(Dh), rows, rows], out_specs=(blk(W), blk(Dh), blk(1)),
        sem=("parallel", "arbitrary"), comm=comm,
        args=(q_aug.reshape(H, n, t, W), k_aug, v, do.reshape(H, n, t, Dh), lse.reshape(H, n, 1, t),
              delta.reshape(H, n, 1, t)))


def _sconv_fwd(proj, w, col0, width, name):
    S = proj.shape[0]
    tc = LANES
    nb, c0 = width // tc, col0 // tc

    def body(x_ref, gb_ref, gc_ref, w_ref, o_ref):
        o_ref[...] = (gb_ref[...] * _conv3(gc_ref[...] * x_ref[...], w_ref)).astype(BF16)

    return pl.pallas_call(
        body, name=name, grid=(nb,), out_shape=SDS((S, width), BF16),
        in_specs=[BS((S, tc), lambda j: (0, c0 + j)), BS((S, tc), lambda j: (0, c0 + nb + j)),
                  BS((S, tc), lambda j: (0, c0 + 2 * nb + j)), BS((3, tc), lambda j: (0, j))],
        out_specs=BS((S, tc), lambda j: (0, j)), compiler_params=_cp("parallel"))(proj, proj, proj, w)


def _sconv_bwd(proj, w, col0, width, do, name):
    S = proj.shape[0]
    tc = LANES
    nb, c0 = width // tc, col0 // tc

    def body(x_ref, gb_ref, gc_ref, w_ref, do_ref, dx_ref, dgb_ref, dgc_ref, dw_ref):
        xb, gb, gc, dov = x_ref[...], gb_ref[...], gc_ref[...], do_ref[...]
        z = gc * xb
        dgb_ref[...] = (dov * _conv3(z, w_ref)).astype(BF16)
        dz = _conv3_bwd(z, dov * gb, w_ref, dw_ref)
        dgc_ref[...] = (dz * xb).astype(BF16)
        dx_ref[...] = (dz * gc).astype(BF16)

    out = BS((S, tc), lambda j: (0, j))
    wspec = BS((3, tc), lambda j: (0, j))
    return pl.pallas_call(
        body, name=name, grid=(nb,),
        out_shape=(SDS((S, width), BF16), SDS((S, width), BF16), SDS((S, width), BF16), SDS((3, width), F32)),
        in_specs=[BS((S, tc), lambda j: (0, c0 + j)), BS((S, tc), lambda j: (0, c0 + nb + j)),
                  BS((S, tc), lambda j: (0, c0 + 2 * nb + j)), wspec, out],
        out_specs=(out, out, out, wspec), compiler_params=_cp("parallel"))(proj, proj, proj, w, do)


def _gelu(x):
    return 0.5 * x * (1.0 + jnp.tanh(0.7978845608028654 * (x + 0.044715 * x * x * x)))


def _gelu_grad(x):
    u = 0.7978845608028654 * (x + 0.044715 * x * x * x)
    th = jnp.tanh(u)
    return 0.5 * (1.0 + th) + 0.5 * x * (1.0 - th * th) * 0.7978845608028654 * (1.0 + 3.0 * 0.044715 * x * x)


def _tril_w(w_ref, g):
    r = lax.broadcasted_iota(jnp.int32, (SGU_CHUNK, SGU_CHUNK), 0)
    c = lax.broadcasted_iota(jnp.int32, (SGU_CHUNK, SGU_CHUNK), 1)
    return jnp.where(c <= r, w_ref[g], 0.0), c <= r


def _sgu_mixed(vn, w_ref, b_ref, nch, G):
    rows = []
    for ch in range(nch):
        cols = []
        for g in range(G):
            wt, _ = _tril_w(w_ref, g)
            blk = vn[ch * SGU_CHUNK:(ch + 1) * SGU_CHUNK, g * LANES:(g + 1) * LANES]
            cols.append(jnp.dot(wt.astype(BF16), blk.astype(BF16), preferred_element_type=F32) + b_ref[g])
        rows.append(jnp.concatenate(cols, axis=1))
    return jnp.concatenate(rows, axis=0)


def _sgu_fwd(proj, gn, w, b, col0, width, name):
    S = proj.shape[0]
    G = w.shape[0]
    ts = _tile(S, 512)
    nch = ts // SGU_CHUNK
    c0 = col0 // width

    def body(u_ref, v_ref, gn_ref, w_ref, b_ref, o_ref):
        cv = _gelu(v_ref[...])
        rstd = lax.rsqrt(jnp.mean(cv * cv, axis=-1, keepdims=True) + EPS)
        mixed = _sgu_mixed(cv * rstd * gn_ref[...], w_ref, b_ref, nch, G)
        o_ref[...] = (_gelu(u_ref[...]) * mixed).astype(BF16)

    return pl.pallas_call(
        body, name=name, grid=(S // ts,), out_shape=SDS((S, width), BF16),
        in_specs=[BS((ts, width), lambda i: (i, c0)), BS((ts, width), lambda i: (i, c0 + 1)),
                  BS((1, width), lambda i: (0, 0)), BS(w.shape, lambda i: (0, 0, 0)), BS(b.shape, lambda i: (0, 0, 0))],
        out_specs=BS((ts, width), lambda i: (i, 0)), compiler_params=_cp("parallel"))(proj, proj, gn, w, b)


def _sgu_bwd(proj, gn, w, b, col0, width, do, name):
    S = proj.shape[0]
    G = w.shape[0]
    ts = _tile(S, 512)
    nch = ts // SGU_CHUNK
    c0 = col0 // width

    def body(u_ref, v_ref, gn_ref, w_ref, b_ref, do_ref, du_ref, dv_ref, dw_ref, db_ref, dgn_ref):
        uin, vin, dov, gnv = u_ref[...], v_ref[...], do_ref[...], gn_ref[...]
        cu, cv = _gelu(uin), _gelu(vin)
        rstd = lax.rsqrt(jnp.mean(cv * cv, axis=-1, keepdims=True) + EPS)
        vhat = cv * rstd
        vn = vhat * gnv
        mixed = _sgu_mixed(vn, w_ref, b_ref, nch, G)
        du_ref[...] = (dov * mixed * _gelu_grad(uin)).astype(BF16)
        dmix = dov * cu

        @pl.when(pl.program_id(0) == 0)
        def _():
            dw_ref[...] = jnp.zeros_like(dw_ref)
            db_ref[...] = jnp.zeros_like(db_ref)
            dgn_ref[...] = jnp.zeros_like(dgn_ref)

        rows = []
        for ch in range(nch):
            cols = []
            for g in range(G):
                wt, mask = _tril_w(w_ref, g)
                sl = (slice(ch * SGU_CHUNK, (ch + 1) * SGU_CHUNK), slice(g * LANES, (g + 1) * LANES))
                dm, vb = dmix[sl], vn[sl].astype(BF16)
                db_ref[g] += jnp.sum(dm, axis=-1, keepdims=True)
                dwg = lax.dot_general(dm.astype(BF16), vb, (_DIMS["nt"], ((), ())), preferred_element_type=F32)
                dw_ref[g] += jnp.where(mask, dwg, 0.0)
                cols.append(lax.dot_general(wt.astype(BF16), dm.astype(BF16), (_DIMS["tn"], ((), ())),
                                            preferred_element_type=F32))
            rows.append(jnp.concatenate(cols, axis=1))
        dvn = jnp.concatenate(rows, axis=0)
        dgn_ref[...] += jnp.sum(dvn * vhat, axis=0, keepdims=True)
        dvh = dvn * gnv
        dcv = rstd * (dvh - vhat * jnp.mean(dvh * vhat, axis=-1, keepdims=True))
        dv_ref[...] = (dcv * _gelu_grad(vin)).astype(BF16)

    row = BS((ts, width), lambda i: (i, 0))
    wsp, bsp, gsp = BS(w.shape, lambda i: (0, 0, 0)), BS(b.shape, lambda i: (0, 0, 0)), BS((1, width), lambda i: (0, 0))
    return pl.pallas_call(
        body, name=name, grid=(S // ts,),
        out_shape=(SDS((S, width), BF16), SDS((S, width), BF16), SDS(w.shape, F32), SDS(b.shape, F32),
                   SDS((1, width), F32)),
        in_specs=[BS((ts, width), lambda i: (i, c0)), BS((ts, width), lambda i: (i, c0 + 1)), gsp, wsp, bsp, row],
        out_specs=(row, row, wsp, bsp, gsp), compiler_params=_cp("arbitrary"))(proj, proj, gn, w, b, do)


def _rope_tables(positions, inv, name):
    S = positions.shape[0]
    ts = _tile(S, 512, 8)
    half = ROPE_DIM // 2

    def body(p_ref, inv_ref, c_ref, sa_ref, sb_ref):
        ang = p_ref[...].astype(F32) * inv_ref[...]
        lane = lax.broadcasted_iota(jnp.int32, (ts, LANES), 1) % HEAD_DIM
        sn = jnp.sin(ang)
        c_ref[...] = jnp.cos(ang)
        sa_ref[...] = jnp.where(lane < half, -sn, 0.0)
        sb_ref[...] = jnp.where((lane >= half) & (lane < ROPE_DIM), sn, 0.0)

    out = BS((ts, LANES), lambda i: (i, 0))
    return pl.pallas_call(
        body, name=name, grid=(S // ts,), out_shape=(SDS((S, LANES), F32),) * 3,
        in_specs=[BS((ts, 1), lambda i: (i, 0)), BS((1, LANES), lambda i: (0, 0))], out_specs=(out, out, out),
        compiler_params=_cp("parallel"))(positions, inv)


def _rope_apply(x, col0, width, tables, sign, out_dtype, name):
    S = x.shape[0]
    ts = _tile(S, 512, 8)
    tc = 2 * LANES
    nb, c0 = width // tc, col0 // tc
    half = ROPE_DIM // 2

    def body(x_ref, c_ref, sa_ref, sb_ref, o_ref):
        xv = x_ref[...].astype(F32)
        wide = lambda r: jnp.concatenate([r[...], r[...]], axis=1)
        y = (xv * wide(c_ref) + pltpu.roll(xv, tc - half, 1) * (sign * wide(sa_ref))
             + pltpu.roll(xv, half, 1) * (sign * wide(sb_ref)))
        o_ref[...] = y.astype(o_ref.dtype)

    tab = BS((ts, LANES), lambda i, j: (i, 0))
    return pl.pallas_call(
        body, name=name, grid=(S // ts, nb), out_shape=SDS((S, width), out_dtype),
        in_specs=[BS((ts, tc), lambda i, j: (i, c0 + j)), tab, tab, tab], out_specs=BS((ts, tc), lambda i, j: (i, j)),
        compiler_params=_cp("parallel", "parallel"))(x, *tables)


def _dil_masks(hh, m, bpcs, hpg):
    g = hh // hpg
    bpc = jnp.where(g == 0, bpcs[0], jnp.where(g == 1, bpcs[1], bpcs[2]))
    r = lax.broadcasted_iota(jnp.int32, (DIL_SPAN, DIL_SPAN), 0)
    c = lax.broadcasted_iota(jnp.int32, (DIL_SPAN, DIL_SPAN), 1)
    return (c >= r) & (lax.rem(m, bpc) != 0), c <= r


def _nt(a, b):
    return lax.dot_general(a, b, (_DIMS["nt"], ((), ())), preferred_element_type=F32)


def _tn(a, b):
    return lax.dot_general(a, b, (_DIMS["tn"], ((), ())), preferred_element_type=F32)


def _dil_fwd(q, k, v, bpcs, hpg, name, comm=None):
    HH, S, Dh = q.shape
    nb = S // DIL_SPAN
    scale = Dh ** -0.5

    def body(q_ref, kp_ref, kc_ref, vp_ref, vc_ref, o_ref, lse_ref):
        mp, mc = _dil_masks(pl.program_id(0) * hpg, pl.program_id(1), bpcs, hpg)
        for h in range(hpg):
            qv = q_ref[h]
            sp = jnp.where(mp, _nt(qv, kp_ref[h]) * scale, NEG)
            sc = jnp.where(mc, _nt(qv, kc_ref[h]) * scale, NEG)
            mx = jnp.maximum(jnp.max(sp, axis=-1, keepdims=True), jnp.max(sc, axis=-1, keepdims=True))
            pp, pc = jnp.exp(sp - mx), jnp.exp(sc - mx)
            l = jnp.sum(pp, axis=-1, keepdims=True) + jnp.sum(pc, axis=-1, keepdims=True)
            acc = (jnp.dot(pp.astype(BF16), vp_ref[h], preferred_element_type=F32)
                   + jnp.dot(pc.astype(BF16), vc_ref[h], preferred_element_type=F32))
            o_ref[h] = acc / l
            lse_ref[h] = mx + jnp.log(l)

    cur = BS((hpg, DIL_SPAN, Dh), lambda g, m: (g, m, 0))
    prev = BS((hpg, DIL_SPAN, Dh), lambda g, m: (g, jnp.maximum(m - 1, 0), 0))
    return _call(
        body, name=name, grid=(HH // hpg, nb), out_shape=(SDS((HH, S, Dh), F32), SDS((HH, S, 1), F32)),
        in_specs=[cur, prev, cur, prev, cur], out_specs=(cur, BS((hpg, DIL_SPAN, 1), lambda g, m: (g, m, 0))),
        sem=("parallel", "parallel"), args=(q, k, k, v, v), comm=comm)


def _dil_bwd_dq(q, k, v, do, lse, coef, bpcs, hpg, name, comm=None):
    HH, S, Dh = q.shape
    nb = S // DIL_SPAN
    scale = Dh ** -0.5

    def body(q_ref, kp_ref, kc_ref, vp_ref, vc_ref, do_ref, lse_ref, cf_ref, dq_ref):
        mp, mc = _dil_masks(pl.program_id(0) * hpg, pl.program_id(1), bpcs, hpg)
        for h in range(hpg):
            qv, dob, lsev, cf = q_ref[h], do_ref[h], lse_ref[h], cf_ref[h]
            pp = jnp.exp(jnp.where(mp, _nt(qv, kp_ref[h]) * scale, NEG) - lsev)
            pc = jnp.exp(jnp.where(mc, _nt(qv, kc_ref[h]) * scale, NEG) - lsev)
            dsp = pp * (_nt(dob, vp_ref[h]) + cf)
            dsc = pc * (_nt(dob, vc_ref[h]) + cf)
            dq_ref[h] = (jnp.dot(dsp.astype(BF16), kp_ref[h], preferred_element_type=F32)
                         + jnp.dot(dsc.astype(BF16), kc_ref[h], preferred_element_type=F32)) * scale

    cur = BS((hpg, DIL_SPAN, Dh), lambda g, m: (g, m, 0))
    prev = BS((hpg, DIL_SPAN, Dh), lambda g, m: (g, jnp.maximum(m - 1, 0), 0))
    col = BS((hpg, DIL_SPAN, 1), lambda g, m: (g, m, 0))
    return _call(
        body, name=name, grid=(HH // hpg, nb), out_shape=SDS((HH, S, Dh), F32),
        in_specs=[cur, prev, cur, prev, cur, cur, col, col], out_specs=cur,
        sem=("parallel", "parallel"), args=(q, k, k, v, v, do, lse, coef), comm=comm)


def _dil_bwd_dkv(q, k, v, do, lse, coef, bpcs, hpg, name, comm=None):
    HH, S, Dh = q.shape
    nb = S // DIL_SPAN
    scale = Dh ** -0.5

    def body(k_ref, v_ref, qc_ref, qn_ref, doc_ref, don_ref, lc_ref, ln_ref, cc_ref, cn_ref, dk_ref, dv_ref):
        hh, m = pl.program_id(0) * hpg, pl.program_id(1)
        _, mc = _dil_masks(hh, m, bpcs, hpg)
        mp, _ = _dil_masks(hh, m + 1, bpcs, hpg)
        mp = mp & (m + 1 < nb)
        for h in range(hpg):
            kv, vv = k_ref[h], v_ref[h]
            pc = jnp.exp(jnp.where(mc, _nt(qc_ref[h], kv) * scale, NEG) - lc_ref[h])
            pn = jnp.exp(jnp.where(mp, _nt(qn_ref[h], kv) * scale, NEG) - ln_ref[h])
            dsc = pc * (_nt(doc_ref[h], vv) + cc_ref[h])
            dsn = pn * (_nt(don_ref[h], vv) + cn_ref[h])
            dv_ref[h] = _tn(pc.astype(BF16), doc_ref[h]) + _tn(pn.astype(BF16), don_ref[h])
            dk_ref[h] = (_tn(dsc.astype(BF16), qc_ref[h]) + _tn(dsn.astype(BF16), qn_ref[h])) * scale

    cur = BS((hpg, DIL_SPAN, Dh), lambda g, m: (g, m, 0))
    nxt = BS((hpg, DIL_SPAN, Dh), lambda g, m: (g, jnp.minimum(m + 1, nb - 1), 0))
    col = BS((hpg, DIL_SPAN, 1), lambda g, m: (g, m, 0))
    ncol = BS((hpg, DIL_SPAN, 1), lambda g, m: (g, jnp.minimum(m + 1, nb - 1), 0))
    return _call(
        body, name=name, grid=(HH // hpg, nb), out_shape=(SDS((HH, S, Dh), F32), SDS((HH, S, Dh), F32)),
        in_specs=[cur, cur, cur, nxt, cur, nxt, col, ncol, col, ncol], out_specs=(cur, cur),
        sem=("parallel", "parallel"), args=(k, v, q, q, do, do, lse, lse, coef, coef), comm=comm)


def _dil_merge(o3, lse3, name):
    G, H, S, Dh = o3.shape
    ts = _tile(S, 512, 8)

    def body(o_ref, l_ref, out_ref):
        lv = [l_ref[g, 0] for g in range(G)]
        mx = functools.reduce(jnp.maximum, lv)
        ex = [jnp.exp(v - mx) for v in lv]
        den = functools.reduce(lambda a, b: a + b, ex)
        out_ref[0] = functools.reduce(lambda a, b: a + b, [(ex[g] / den) * o_ref[g, 0] for g in range(G)])

    return pl.pallas_call(
        body, name=name, grid=(H, S // ts), out_shape=SDS((H, S, Dh), F32),
        in_specs=[BS((G, 1, ts, Dh), lambda h, i: (0, h, i, 0)), BS((G, 1, ts, 1), lambda h, i: (0, h, i, 0))],
        out_specs=BS((1, ts, Dh), lambda h, i: (h, i, 0)), compiler_params=_cp("parallel", "parallel"))(o3, lse3)


def _dil_merge_bwd(o3, lse3, do, name):
    G, H, S, Dh = o3.shape
    ts = _tile(S, 512, 8)

    def body(o_ref, l_ref, do_ref, do3_ref, cf_ref):
        lv = [l_ref[g, 0] for g in range(G)]
        mx = functools.reduce(jnp.maximum, lv)
        ex = [jnp.exp(v - mx) for v in lv]
        den = functools.reduce(lambda a, b: a + b, ex)
        wt = [e / den for e in ex]
        dov = do_ref[0]
        dw = [jnp.sum(dov * o_ref[g, 0], axis=-1, keepdims=True) for g in range(G)]
        mean = functools.reduce(lambda a, b: a + b, [wt[g] * dw[g] for g in range(G)])
        for g in range(G):
            do3_ref[g, 0] = (wt[g] * dov).astype(BF16)
            cf_ref[g, 0] = wt[g] * (dw[g] - mean) - wt[g] * dw[g]

    o_spec = BS((G, 1, ts, Dh), lambda h, i: (0, h, i, 0))
    l_spec = BS((G, 1, ts, 1), lambda h, i: (0, h, i, 0))
    return pl.pallas_call(
        body, name=name, grid=(H, S // ts), out_shape=(SDS((G, H, S, Dh), BF16), SDS((G, H, S, 1), F32)),
        in_specs=[o_spec, l_spec, BS((1, ts, Dh), lambda h, i: (h, i, 0))], out_specs=(o_spec, l_spec),
        compiler_params=_cp("parallel", "parallel"))(o3, lse3, do)


def _merge_fwd(proj, gate_col, os_, ws, name):
    S = proj.shape[0]
    D = ws[0].shape[1]
    nbr = len(os_)
    ts = _tile(S, 512, 8)
    tn = LANES * 2 if gate_col % (2 * LANES) == 0 and D % (2 * LANES) == 0 else LANES
    g0, gstep = gate_col // tn, D // tn

    def body(*refs):
        p_refs, o_refs, w_refs, out_ref = refs[:nbr], refs[nbr:2 * nbr], refs[2 * nbr:3 * nbr], refs[-1]
        acc = None
        for i in range(nbr):
            t = _sigmoid(p_refs[i][...]) * jnp.dot(o_refs[i][...], w_refs[i][...], preferred_element_type=F32)
            acc = t if acc is None else acc + t
        out_ref[...] = acc.astype(BF16)

    specs = ([BS((ts, tn), lambda s, j, i=i: (s, g0 + i * gstep + j)) for i in range(nbr)]
             + [BS((ts, o.shape[1]), lambda s, j: (s, 0)) for o in os_]
             + [BS((w.shape[0], tn), lambda s, j: (0, j)) for w in ws])
    return pl.pallas_call(
        body, name=name, grid=(S // ts, D // tn), out_shape=SDS((S, D), BF16), in_specs=specs,
        out_specs=BS((ts, tn), lambda s, j: (s, j)), compiler_params=_cp("parallel", "parallel"),
    )(*([proj] * nbr), *os_, *ws)


def _merge_bwd(proj, gate_col, os_, ws, dm, name):
    S = proj.shape[0]
    D = ws[0].shape[1]
    nbr = len(os_)
    ts = _tile(S, 512, 8)
    tn = LANES * 2 if gate_col % (2 * LANES) == 0 and D % (2 * LANES) == 0 else LANES
    g0, gstep = gate_col // tn, D // tn

    def body(*refs):
        p_refs, o_refs, w_refs = refs[:nbr], refs[nbr:2 * nbr], refs[2 * nbr:3 * nbr]
        dmv = refs[3 * nbr][...]
        dbo_refs, dgl_refs = refs[3 * nbr + 1:4 * nbr + 1], refs[4 * nbr + 1:]
        for i in range(nbr):
            gt = _sigmoid(p_refs[i][...])
            bo = jnp.dot(o_refs[i][...], w_refs[i][...], preferred_element_type=F32)
            dbo_refs[i][...] = (dmv * gt).astype(BF16)
            dgl_refs[i][...] = (dmv * bo * gt * (1.0 - gt)).astype(BF16)

    tile = BS((ts, tn), lambda s, j: (s, j))
    specs = ([BS((ts, tn), lambda s, j, i=i: (s, g0 + i * gstep + j)) for i in range(nbr)]
             + [BS((ts, o.shape[1]), lambda s, j: (s, 0)) for o in os_]
             + [BS((w.shape[0], tn), lambda s, j: (0, j)) for w in ws] + [tile])
    outs = pl.pallas_call(
        body, name=name, grid=(S // ts, D // tn), out_shape=(SDS((S, D), BF16),) * (2 * nbr), in_specs=specs,
        out_specs=(tile,) * (2 * nbr), compiler_params=_cp("parallel", "parallel"),
    )(*([proj] * nbr), *os_, *ws, dm)
    return outs[:nbr], outs[nbr:]


def _ffn_act(up, w, name):
    S, F2 = up.shape
    F = F2 // 2
    tc = LANES
    nb = F // tc

    def body(g_ref, v_ref, wg_ref, wv_ref, a_ref):
        ug = _conv3(g_ref[...], wg_ref)
        a_ref[...] = (ug * _sigmoid(ug) * _conv3(v_ref[...], wv_ref)).astype(BF16)

    return pl.pallas_call(
        body, name=name, grid=(nb,), out_shape=SDS((S, F), BF16),
        in_specs=[BS((S, tc), lambda j: (0, j)), BS((S, tc), lambda j: (0, nb + j)),
                  BS((3, tc), lambda j: (0, j)), BS((3, tc), lambda j: (0, nb + j))],
        out_specs=BS((S, tc), lambda j: (0, j)), compiler_params=_cp("parallel"))(up, up, w, w)


def _ffn_act_bwd(up, w, da, name):
    S, F2 = up.shape
    F = F2 // 2
    tc = LANES
    nb = F // tc

    def body(g_ref, v_ref, wg_ref, wv_ref, da_ref, dg_ref, dv_ref, dwg_ref, dwv_ref):
        pg, pv, dav = g_ref[...], v_ref[...], da_ref[...].astype(F32)
        ug, uv = _conv3(pg, wg_ref), _conv3(pv, wv_ref)
        sg = _sigmoid(ug)
        dg_ref[...] = _conv3_bwd(pg, dav * uv * (sg * (1.0 + ug * (1.0 - sg))), wg_ref, dwg_ref).astype(BF16)
        dv_ref[...] = _conv3_bwd(pv, dav * ug * sg, wv_ref, dwv_ref).astype(BF16)

    col = BS((S, tc), lambda j: (0, j))
    wsp = BS((3, tc), lambda j: (0, j))
    return pl.pallas_call(
        body, name=name, grid=(nb,),
        out_shape=(SDS((S, F), BF16), SDS((S, F), BF16), SDS((3, F), F32), SDS((3, F), F32)),
        in_specs=[col, BS((S, tc), lambda j: (0, nb + j)), wsp, BS((3, tc), lambda j: (0, nb + j)), col],
        out_specs=(col, col, wsp, wsp), compiler_params=_cp("parallel"))(up, up, w, w, da)


def _ple_fwd(x, pgl, pe, name):
    S, D = x.shape
    ts = _tile(S, 256, 8)

    def body(x_ref, g_ref, e_ref, o_ref):
        o_ref[...] = x_ref[...] + _sigmoid(g_ref[...]) * e_ref[...]

    row = BS((ts, D), lambda i: (i, 0))
    return pl.pallas_call(body, name=name, grid=(S // ts,), out_shape=SDS((S, D), F32), in_specs=[row] * 3,
                          out_specs=row, compiler_params=_cp("parallel"))(x, pgl, pe)


def _ple_bwd(dx, pgl, pe, name):
    S, D = dx.shape
    ts = _tile(S, 256, 8)

    def body(dx_ref, g_ref, e_ref, dg_ref, de_ref):
        dxv, sg = dx_ref[...], _sigmoid(g_ref[...])
        dg_ref[...] = (dxv * e_ref[...] * sg * (1.0 - sg)).astype(BF16)
        de_ref[...] = (dxv * sg).astype(BF16)

    row = BS((ts, D), lambda i: (i, 0))
    return pl.pallas_call(body, name=name, grid=(S // ts,), out_shape=(SDS((S, D), BF16),) * 2, in_specs=[row] * 3,
                          out_specs=(row, row), compiler_params=_cp("parallel"))(dx, pgl, pe)


def _adamw(w, g, m, v, name):
    shape = w.shape
    cols = shape[-1]
    rows = w.size // cols
    tr = _tile(rows, 256, 8)
    c1, c2 = 1.0 / (1.0 - ADAM_B1 ** ADAM_STEP), 1.0 / (1.0 - ADAM_B2 ** ADAM_STEP)

    def body(w_ref, g_ref, m_ref, v_ref, d_ref, nm_ref, nv_ref):
        gv = g_ref[...]
        nm = ADAM_B1 * m_ref[...] + (1.0 - ADAM_B1) * gv
        nv = ADAM_B2 * v_ref[...] + (1.0 - ADAM_B2) * (gv * gv)
        d_ref[...] = -ADAM_LR * ((nm * c1) / (jnp.sqrt(nv * c2) + ADAM_EPS) + ADAM_WD * w_ref[...])
        nm_ref[...] = nm
        nv_ref[...] = nv

    blk = BS((tr, cols), lambda i: (i, 0))
    outs = pl.pallas_call(
        body, name=name, grid=(rows // tr,), out_shape=(SDS((rows, cols), F32),) * 3, in_specs=[blk] * 4,
        out_specs=(blk,) * 3, compiler_params=_cp("parallel"),
    )(*(a.reshape(rows, cols) for a in (w, g, m, v)))
    return tuple(o.reshape(shape) for o in outs)


SHARDED = ("w_in", "shortconv_w", "w_br_fox", "w_br_conv", "w_br_sgu", "w_br_dil", "w_out", "w_up", "ffn_conv_w",
           "w_down", "w_ple_gate", "w_ple_proj")
REPLICATED = ("norm_mix_g", "fox_forget_b", "sgu_norm_g", "sgu_w", "sgu_b", "norm_ffn_g", "norm_ple_g")
WEIGHTS = ("norm_mix_g", "w_in", "fox_forget_b", "shortconv_w", "sgu_norm_g", "sgu_w", "sgu_b", "w_br_fox",
           "w_br_conv", "w_br_sgu", "w_br_dil", "w_out", "norm_ffn_g", "w_up", "ffn_conv_w", "w_down", "norm_ple_g",
           "w_ple_gate", "w_ple_proj", "final_norm_g")
PACK_COLS = 1024
PACK_ROWS = 256


def _pack(parts, dtype, unit):
    lead = parts[0].shape[:-1] if parts[0].ndim > 1 else ()
    flat = jnp.concatenate([p.astype(dtype) for p in parts], axis=-1)
    n = flat.shape[-1]
    rows = -(-n // (PACK_COLS * unit)) * unit
    flat = jnp.pad(flat, [(0, 0)] * len(lead) + [(0, rows * PACK_COLS - n)])
    return flat.reshape(lead + (rows, PACK_COLS))


def _to_heads(x, heads):
    S = x.shape[0]
    return x.reshape(S, heads, HEAD_DIM).transpose(1, 0, 2)


def _from_heads(x):
    H, S, Dh = x.shape
    return x.transpose(1, 0, 2).reshape(S, H * Dh)


def _to_residue(x, hpg):
    S = x.shape[0]
    gw = hpg * HEAD_DIM
    outs = []
    for g, (_, d) in enumerate(DIL_PATTERNS):
        seg = x[:, g * gw:(g + 1) * gw].reshape(S // d, d, hpg, HEAD_DIM)
        outs.append(seg.transpose(2, 1, 0, 3).reshape(hpg, S, HEAD_DIM))
    return jnp.concatenate(outs, axis=0)


def _from_residue(x, hpg):
    _, S, C = x.shape
    outs = []
    for g, (_, d) in enumerate(DIL_PATTERNS):
        seg = x[g * hpg:(g + 1) * hpg].reshape(hpg, d, S // d, C)
        outs.append(seg.transpose(0, 2, 1, 3).reshape(hpg, S, C))
    return jnp.stack(outs, axis=0)


def _groups_to_residue(x):
    G, hpg, S, C = x.shape
    outs = []
    for g, (_, d) in enumerate(DIL_PATTERNS):
        outs.append(x[g].reshape(hpg, S // d, d, C).transpose(0, 2, 1, 3).reshape(hpg, S, C))
    return jnp.concatenate(outs, axis=0)


def kernel(x, p, positions, norm_mix_g, w_in, fox_forget_b, shortconv_w, sgu_norm_g, sgu_w, sgu_b, w_br_fox, w_br_conv, w_br_sgu, w_br_dil, w_out, norm_ffn_g, w_up, ffn_conv_w, w_down, norm_ple_g, w_ple_gate, w_ple_proj, final_norm_g, loss_target, m_norm_mix_g, m_w_in, m_fox_forget_b, m_shortconv_w, m_sgu_norm_g, m_sgu_w, m_sgu_b, m_w_br_fox, m_w_br_conv, m_w_br_sgu, m_w_br_dil, m_w_out, m_norm_ffn_g, m_w_up, m_ffn_conv_w, m_w_down, m_norm_ple_g, m_w_ple_gate, m_w_ple_proj, m_final_norm_g, v_norm_mix_g, v_w_in, v_fox_forget_b, v_shortconv_w, v_sgu_norm_g, v_sgu_w, v_sgu_b, v_w_br_fox, v_w_br_conv, v_w_br_sgu, v_w_br_dil, v_w_out, v_norm_ffn_g, v_w_up, v_ffn_conv_w, v_w_down, v_norm_ple_g, v_w_ple_gate, v_w_ple_proj, v_final_norm_g):
    given = dict(locals())
    W = {n: given[n] for n in WEIGHTS}
    depth = w_in.shape[0]
    S, D = x.shape[1], x.shape[2]
    x0, target, pos = x[0], loss_target[0], positions[0].reshape(S, 1)
    FH = fox_forget_b.shape[1]
    FW = FH * HEAD_DIM
    CW = shortconv_w.shape[2] * N_DEV
    G = sgu_w.shape[1]
    SW = G * sgu_w.shape[2]
    DOUT = w_br_dil.shape[1]
    hpg = DOUT // HEAD_DIM
    NG = len(DIL_PATTERNS)
    DW = NG * DOUT
    bpcs = tuple(S // d // DIL_SPAN for _, d in DIL_PATTERNS)
    A0, B0 = 0, 3 * FW
    C0 = B0 + 3 * CW
    D0 = C0 + 2 * SW
    G0 = D0 + 3 * DW
    F0 = G0 + 4 * D
    NP = F0 + LANES
    orig = [3 * FW, FH, 3 * CW, 2 * SW, 3 * DW, 4 * D]
    o_off = [sum(orig[:i]) for i in range(len(orig) + 1)]

    me = 4 * lax.axis_index("x") + 2 * lax.axis_index("y") + lax.axis_index("c")
    n_in, o1, o2, d_in = w_in.shape[2], o_off[1], o_off[2], o_off[6]
    NPB = NP // LANES
    WB = -(-(LANES - 1 + n_in) // LANES)
    PW = (WB + 1) * LANES

    def a_of(j, clip=lambda v: min(max(v, 0), FH)):
        return n_in * j - clip(n_in * j - o1)

    def s_of(j):
        return min(a_of(j) // LANES, NPB - WB)

    def s_traced(j):
        return jnp.minimum(a_of(j, lambda v: jnp.clip(v, 0, FH)) // LANES, NPB - WB)

    def runs(j):
        lo, hi, out = n_in * j, n_in * (j + 1), []
        for a, b in ((0, o1), (o1, o2), (o2, d_in)):
            l, h = max(lo, a), min(hi, b)
            if l < h:
                dst = WB * LANES + l - o1 if a == o1 else l - (FH if l >= o2 else 0) - s_of(j) * LANES
                out.append((l - lo, h - l, dst))
        return out

    def to_window(j, shard):
        piece = jnp.zeros((shard.shape[0], PW), shard.dtype)
        for src, ln, dst in runs(j):
            piece = piece.at[:, dst:dst + ln].set(shard[:, src:src + ln])
        return piece

    def from_window(j, win):
        return jnp.concatenate([win[:, dst:dst + ln] for _, ln, dst in runs(j)], axis=1)

    def on_my_index(fn, arg):
        return lax.switch(me, [functools.partial(fn, j) for j in range(N_DEV)], arg)

    def place(axis, length):
        def view(ref, j):
            align = LANES if axis == len(ref.shape) - 1 else 16
            start = pl.multiple_of(j * length, align) if length % align == 0 else j * length
            return ref.at[tuple([slice(None)] * axis + [pl.ds(start, length)])]
        return view

    def slot(ref, j):
        return ref.at[j]

    def whole(ref, j):
        return ref

    n_up, r_down, r_sq = w_up.shape[2], w_down.shape[1], w_out.shape[1]
    n_br, n_sm, n_sc = w_br_fox.shape[2], w_br_dil.shape[2], shortconv_w.shape[2]
    FF2 = n_up * N_DEV
    cw_pad = -(-n_up // LANES) * LANES
    placed = [place(1, n_up), place(0, r_down), place(1, r_sq), place(2, n_br), place(2, n_sm)]

    def gather_plans(i):
        conv = jnp.zeros((8, cw_pad), F32).at[0:3, :n_up].set(ffn_conv_w[i]).at[3:6, :n_sc].set(shortconv_w[i])
        srcs = [on_my_index(to_window, w_in[i].astype(BF16)), w_up[i].astype(BF16), w_down[i].astype(BF16),
                jnp.stack([w_out[i], w_ple_gate[i]]).astype(BF16),
                jnp.stack([w_br_fox[i], w_br_conv[i], w_br_sgu[i]]).astype(BF16),
                jnp.stack([w_br_dil[i], w_ple_proj[i]]).astype(BF16), conv]
        shapes = [SDS((N_DEV, D, PW), BF16), SDS((D, FF2), BF16), SDS((r_down * N_DEV, D), BF16),
                  SDS((2, r_sq * N_DEV, D), BF16), SDS((3, w_br_fox.shape[1], n_br * N_DEV), BF16),
                  SDS((2, w_br_dil.shape[1], n_sm * N_DEV), BF16), SDS((N_DEV, 8, cw_pad), F32)]
        dsts = [slot] + placed + [slot]
        return [(srcs[a:b], shapes[a:b], [whole] * (b - a), dsts[a:b], True)
                for a, b in ((0, 1), (1, 2), (2, 3), (3, 7))]

    def gathered(parts):
        (pieces,), (wup,), (wdown,), (sq, br, sm, cv) = parts
        return {
            "w_in": _assemble(pieces, [s_of(j) for j in range(N_DEV)], WB, NPB, "assemble_w_in"),
            "w_up": wup, "w_down": wdown, "w_out": sq[0], "w_ple_gate": sq[1],
            "w_br_fox": br[0], "w_br_conv": br[1], "w_br_sgu": br[2], "w_br_dil": sm[0], "w_ple_proj": sm[1],
            "ffn_conv_w": jnp.moveaxis(cv[:, 0:3, :n_up], 0, 1).reshape(3, FF2),
            "shortconv_w": jnp.moveaxis(cv[:, 3:6, :n_sc], 0, 1).reshape(3, CW),
        }

    def one_plan(plans):
        return tuple(sum((list(plan[k]) for plan in plans), []) for k in range(4)) + (plans[0][4],)

    first = _comm("comm_gather_weights", *one_plan(gather_plans(0)))
    full = [gathered([first[0:1], first[1:2], first[2:3], first[3:7]])]

    inv = ROPE_THETA ** (-jnp.arange(ROPE_DIM // 2, dtype=F32) * (2.0 / ROPE_DIM))
    inv_row = jnp.tile(jnp.concatenate([inv, inv, jnp.zeros((HEAD_DIM - ROPE_DIM,), F32)]), LANES // HEAD_DIM)
    tables = _rope_tables(pos, inv_row.reshape(1, LANES), "rope_tables")

    def pad_lanes(v):
        return jnp.pad(v, ((0, 0), (0, LANES - v.shape[1])))

    saved = []
    xc = x0
    for i in range(depth):
        fw, sv = full[i], {}
        sv["x"] = xc
        nxt = gather_plans(i + 1) if i + 1 < depth else [None] * 4
        got = [None] * 4

        def host(k, result, nxt=nxt, got=got):
            if nxt[k] is None:
                return result
            got[k] = result[1]
            return result[0]

        h1 = _rms_fwd(xc, norm_mix_g[i], "rms_mix")
        proj = host(1, _mm(h1, fw["w_in"], "nn", F32, "mm_in", comm=nxt[1]))
        sv["h1"], sv["proj"] = h1, proj
        bias = pad_lanes(fox_forget_b[i].reshape(1, FH))
        Fp = _fox_prep(proj, bias, F0 // LANES, "fox_prep")[:, :, :FH].transpose(2, 1, 0)
        t = _tile(S, 256)
        qa, ka, va = (_to_heads(proj[:, A0 + j * FW:A0 + (j + 1) * FW], FH).astype(BF16) for j in range(3))
        ones3, zpad = jnp.ones((FH, S, 3), BF16), jnp.zeros((FH, S, HEAD_DIM - 6), BF16)
        q_aug = jnp.concatenate([qa, Fp, ones3, zpad], axis=-1)
        k_aug = jnp.concatenate([ka, ones3, -Fp, zpad], axis=-1)

        def blocks_t(a):
            return a.reshape(FH, S // t, t, a.shape[-1]).transpose(0, 1, 3, 2)

        kT = blocks_t(k_aug)
        oaT, lse_a = host(0, _foxt_fwd(q_aug, k_aug, blocks_t(va), "fox_fwd", comm=nxt[0]))
        oa_b = oaT.transpose(2, 0, 1).reshape(S, FW).astype(BF16)
        sv.update(bias=bias, q_aug=q_aug, k_aug=k_aug, kT=kT, va=va, lse_a=lse_a, oa_b=oa_b)
        ob = _sconv_fwd(proj, fw["shortconv_w"], B0, CW, "sconv_fwd")
        sgb = sgu_b[i].reshape(G, SGU_CHUNK, 1)
        oc = _sgu_fwd(proj, sgu_norm_g[i].reshape(1, SW), sgu_w[i], sgb, C0, SW, "sgu_fwd")
        qd = _rope_apply(proj, D0, DW, tables, 1.0, BF16, "rope_q")
        kd = _rope_apply(proj, D0 + DW, DW, tables, 1.0, BF16, "rope_k")
        qr, kr = _to_residue(qd, hpg), _to_residue(kd, hpg)
        vr = _to_residue(proj[:, D0 + 2 * DW:D0 + 3 * DW].astype(BF16), hpg)
        od_r, lse_r = host(2, _dil_fwd(qr, kr, vr, bpcs, hpg, "dil_fwd", comm=nxt[2]))
        o3, lse3 = _from_residue(od_r, hpg), _from_residue(lse_r, hpg)
        od = _from_heads(_dil_merge(o3, lse3, "dil_merge")).astype(BF16)
        sv.update(ob=ob, oc=oc, sgb=sgb, qr=qr, kr=kr, vr=vr, lse_r=lse_r, o3=o3, lse3=lse3, od=od)
        brs = [oa_b, ob, oc, od]
        wbr = [fw["w_br_fox"], fw["w_br_conv"], fw["w_br_sgu"], fw["w_br_dil"]]
        merged = _merge_fwd(proj, G0, brs, wbr, "merge_fwd")
        x2 = _mm(merged, fw["w_out"], "nn", F32, "mm_out", res=xc)
        h2 = _rms_fwd(x2, norm_ffn_g[i], "rms_ffn")
        up = host(3, _mm(h2, fw["w_up"], "nn", F32, "mm_up", comm=nxt[3]))
        if i + 1 < depth:
            full.append(gathered(got))
        act = _ffn_act(up, fw["ffn_conv_w"], "ffn_act")
        x3 = _mm(act, fw["w_down"], "nn", F32, "mm_down", res=x2)
        h3 = _rms_fwd(x3, norm_ple_g[i], "rms_ple")
        pgl = _mm(h3, fw["w_ple_gate"], "nn", F32, "mm_ple_gate")
        pb = p[i, 0].astype(BF16)
        pe = _mm(pb, fw["w_ple_proj"], "nn", F32, "mm_ple_proj")
        xc = _ple_fwd(x3, pgl, pe, "ple_fwd")
        sv.update(merged=merged, x2=x2, h2=h2, up=up, act=act, x3=x3, h3=h3, pgl=pgl, pb=pb, pe=pe)
        saved.append(sv)

    loss_row, dx, dxb, dg_final = _final_loss(xc, final_norm_g, target, "final_loss")
    loss = lax.psum(loss_row[0, 0], MESH_AXES)

    def in_window(half):
        def view(ref, j):
            return ref.at[pl.ds(half * (D // 2), D // 2),
                          pl.ds(pl.multiple_of(s_traced(j) * LANES, LANES), WB * LANES)]
        return view

    def in_flogit(ref, j):
        return ref.at[:, pl.ds(F0, LANES)]

    def scatter_plans(gs, parts=(0, 1, 2, 3, 4, 5)):
        stacked = lambda shape, dt=BF16: SDS((N_DEV,) + tuple(shape), dt)

        def build(part):
            if part == 0:
                srcs, views = [gs["w_in"], gs["w_in"]], [in_window(0), in_flogit]
                shapes = [stacked((D // 2, WB * LANES)), stacked((D, LANES))]
            elif part == 1:
                srcs, views, shapes = [gs["w_up"]], [placed[0]], [stacked((D, n_up))]
            elif part == 2:
                srcs, views, shapes = [gs["w_down"]], [placed[1]], [stacked((r_down, D))]
            elif part == 3:
                srcs, views = [jnp.stack([gs["w_out"], gs["w_ple_gate"]])], [placed[2]]
                shapes = [stacked((2, r_sq, D))]
            elif part == 5:
                conv = jnp.zeros((N_DEV, 8, cw_pad), F32)
                conv = conv.at[:, 0:3, :n_up].set(jnp.moveaxis(gs["ffn_conv_w"].reshape(3, N_DEV, n_up), 1, 0))
                conv = conv.at[:, 3:6, :n_sc].set(jnp.moveaxis(gs["shortconv_w"].reshape(3, N_DEV, n_sc), 1, 0))
                srcs = [jnp.stack([gs["w_br_fox"], gs["w_br_conv"], gs["w_br_sgu"]]),
                        jnp.stack([gs["w_br_dil"], gs["w_ple_proj"]]), conv]
                views = placed[3:] + [slot]
                shapes = [stacked((3, w_br_fox.shape[1], n_br)), stacked((2, w_br_dil.shape[1], n_sm)),
                          stacked((8, cw_pad), F32)]
            else:
                srcs, views, shapes = [gs["w_in"]], [in_window(1)], [stacked((D // 2, WB * LANES))]
            return srcs, shapes, views, [slot] * len(srcs), False

        return [build(part) for part in parts]

    grads_sh = [None] * depth
    grads_rep = [None] * depth
    scattered = [None] * depth
    for i in reversed(range(depth)):
        fw, sv = full[i], saved[i]
        gs, gr = {}, {}
        pend = scatter_plans(grads_sh[i + 1]) if i + 1 < depth else [None] * 6
        got = [None] * 6

        def host(k, result, pend=pend, got=got):
            if pend[k] is None:
                return result
            got[k] = result[1]
            return result[0]

        proj = sv["proj"]
        dpgl, dpe = _ple_bwd(dx, sv["pgl"], sv["pe"], "ple_bwd")
        gs["w_ple_proj"] = _mm(sv["pb"], dpe, "tn", BF16, "mm_dw_ple_proj")
        gs["w_ple_gate"] = _mm(sv["h3"], dpgl, "tn", BF16, "mm_dw_ple_gate")
        dh3 = _mm(dpgl, fw["w_ple_gate"], "nt", F32, "mm_dh3")
        dx, dxb, gr["norm_ple_g"] = _rms_bwd(sv["x3"], norm_ple_g[i], dh3, dx, "rms_ple_bwd")
        gs["w_down"] = _mm(sv["act"], dxb, "tn", BF16, "mm_dw_down")
        dact = _mm(dxb, fw["w_down"], "nt", BF16, "mm_dact")
        dug, duv, dwg, dwv = _ffn_act_bwd(sv["up"], fw["ffn_conv_w"], dact, "ffn_act_bwd")
        gs["ffn_conv_w"] = jnp.concatenate([dwg, dwv], axis=1)
        dup = jnp.concatenate([dug, duv], axis=1)
        gs["w_up"] = host(2, _mm(sv["h2"], dup, "tn", BF16, "mm_dw_up", comm=pend[2]))
        own_down, own_up = scatter_plans(gs, (2, 1)) if i == 0 else (None, None)
        dh2 = _mm(dup, fw["w_up"], "nt", F32, "mm_dh2", comm=own_down)
        if i == 0:
            dh2, own_down = dh2
        dx, dxb, gr["norm_ffn_g"] = _rms_bwd(sv["x2"], norm_ffn_g[i], dh2, dx, "rms_ffn_bwd")
        gs["w_out"] = _mm(sv["merged"], dxb, "tn", BF16, "mm_dw_out")
        dmerged = _mm(dxb, fw["w_out"], "nt", F32, "mm_dmerged")
        brs = [sv["oa_b"], sv["ob"], sv["oc"], sv["od"]]
        names = ["w_br_fox", "w_br_conv", "w_br_sgu", "w_br_dil"]
        wbr = [fw[n] for n in names]
        dbo, dgl = _merge_bwd(proj, G0, brs, wbr, dmerged, "merge_bwd")
        dos = []
        for j, n in enumerate(names):
            gs[n] = _mm(brs[j], dbo[j], "tn", BF16, "mm_dw_" + n)
            dos.append(_mm(dbo[j], wbr[j], "nt", F32, "mm_do_" + n))
        doa = _to_heads(dos[0], FH).astype(BF16)
        dqT, delta = host(1, _foxt_bwd_dq(sv["q_aug"], sv["k_aug"], sv["kT"], sv["va"], doa, sv["lse_a"],
                                          "fox_bwd_dq", comm=pend[1]))
        dk_aug, dva, dfk = host(0, _foxt_bwd_dkv(sv["q_aug"], sv["k_aug"], sv["va"], doa, sv["lse_a"], delta,
                                                 "fox_bwd_dkv", comm=pend[0]))
        dF = pad_lanes(dfk[:, :, 0].T)
        daf, dbias = _fox_prep_bwd(proj, sv["bias"], F0 // LANES, dF, "fox_prep_bwd")
        gr["fox_forget_b"] = dbias[0, :FH]
        d_a = [dqT[:, :HEAD_DIM].transpose(2, 0, 1).reshape(S, FW).astype(BF16),
               _from_heads(dk_aug[:, :, :HEAD_DIM]).astype(BF16), _from_heads(dva).astype(BF16)]
        dxb_, dgb, dgc, gs["shortconv_w"] = _sconv_bwd(proj, fw["shortconv_w"], B0, CW, dos[1], "sconv_bwd")
        du, dv, dsw, dsb, dsg = _sgu_bwd(proj, sgu_norm_g[i].reshape(1, SW), sgu_w[i], sv["sgb"], C0, SW, dos[2],
                                         "sgu_bwd")
        gr["sgu_w"], gr["sgu_b"], gr["sgu_norm_g"] = dsw, dsb.reshape(G, SGU_CHUNK), dsg[0]
        do3, cf3 = _dil_merge_bwd(sv["o3"], sv["lse3"], _to_heads(dos[3], hpg), "dil_merge_bwd")
        do_r, cf_r = _groups_to_residue(do3), _groups_to_residue(cf3)
        dq_r = host(5, _dil_bwd_dq(sv["qr"], sv["kr"], sv["vr"], do_r, sv["lse_r"], cf_r, bpcs, hpg, "dil_bwd_dq",
                                   comm=pend[5]))
        dk_r, dv_r = host(3, _dil_bwd_dkv(sv["qr"], sv["kr"], sv["vr"], do_r, sv["lse_r"], cf_r, bpcs, hpg,
                                          "dil_bwd_dkv", comm=pend[3]))

        def natural(t):
            return _from_heads(_from_residue(t, hpg).reshape(NG * hpg, S, HEAD_DIM))

        dqd = _rope_apply(natural(dq_r), 0, DW, tables, -1.0, BF16, "rope_q_bwd")
        dkd = _rope_apply(natural(dk_r), 0, DW, tables, -1.0, BF16, "rope_k_bwd")
        dvd = natural(dv_r).astype(BF16)
        dproj = jnp.concatenate(d_a + [dxb_, dgb, dgc, du, dv, dqd, dkd, dvd] + list(dgl) + [daf], axis=1)
        gs["w_in"] = _mm(sv["h1"], dproj, "tn", BF16, "mm_dw_in", comm=own_up)
        if i == 0:
            gs["w_in"], own_up = gs["w_in"]
        dh1 = host(4, _mm(dproj, fw["w_in"], "nt", F32, "mm_dh1", comm=pend[4]))
        if i + 1 < depth:
            scattered[i + 1] = got
        dx, dxb, gr["norm_mix_g"] = _rms_bwd(sv["x"], norm_mix_g[i], dh1, dx, "rms_mix_bwd")
        gr["norm_mix_g"], gr["norm_ffn_g"], gr["norm_ple_g"] = (gr[n][0] for n in
                                                               ("norm_mix_g", "norm_ffn_g", "norm_ple_g"))
        grads_sh[i], grads_rep[i] = gs, gr
    grad_x = dx.reshape(1, S, D)

    last = _comm("comm_scatter_grads", *one_plan(scatter_plans(grads_sh[0], (0, 3, 4, 5))))
    scattered[0] = [last[0:2], own_up, own_down, last[2:3], last[3:4], last[4:7]]

    def sum_sources(t, name):
        lead = t.shape[1:]
        return _sum8(t.reshape(N_DEV, -1, lead[-1]), name).reshape(lead)

    grads = {n: [None] * depth for n in SHARDED}
    for i in range(depth):
        (gw, gf), (gup,), (gdown,), (gsq,), (gw2,), (gbr, gsm, gcv) = scattered[i]
        rows = jnp.concatenate([sum_sources(gw, "sum_w_in"), sum_sources(gw2, "sum_w_in")], axis=0)
        win = jnp.concatenate([rows, sum_sources(gf, "sum_w_in_flogit")], axis=1)
        grads["w_in"][i] = on_my_index(from_window, win)
        grads["w_up"][i] = sum_sources(gup, "sum_w_up")
        grads["w_down"][i] = sum_sources(gdown, "sum_w_down")
        sq, br, sm = sum_sources(gsq, "sum_sq"), sum_sources(gbr, "sum_br"), sum_sources(gsm, "sum_sm")
        cv = sum_sources(gcv, "sum_conv")
        grads["w_out"][i], grads["w_ple_gate"][i] = sq[0], sq[1]
        grads["w_br_fox"][i], grads["w_br_conv"][i], grads["w_br_sgu"][i] = br[0], br[1], br[2]
        grads["w_br_dil"][i], grads["w_ple_proj"][i] = sm[0], sm[1]
        grads["ffn_conv_w"][i], grads["shortconv_w"][i] = cv[0:3, :n_up], cv[3:6, :n_sc]
    grads = {n: jnp.stack(v, axis=0) for n, v in grads.items()}

    rep_parts = [jnp.stack([grads_rep[i][n] for i in range(depth)], axis=0).reshape(-1) for n in REPLICATED]
    rep_parts.append(dg_final.reshape(-1))
    rep_sizes = [int(r.shape[0]) for r in rep_parts]
    rep_offs = [sum(rep_sizes[:i]) for i in range(len(rep_sizes) + 1)]
    rep = _sum8(_exchange(_pack(rep_parts, F32, 8), True, "comm_gather_small"), "sum_small").reshape(-1)
    for n, a, b in zip(REPLICATED + ("final_norm_g",), rep_offs[:-1], rep_offs[1:]):
        grads[n] = rep[a:b].reshape(W[n].shape)

    deltas, new_m, new_v = {}, {}, {}
    for n in WEIGHTS:
        deltas[n], new_m[n], new_v[n] = _adamw(W[n], grads[n], given["m_" + n], given["v_" + n], "adamw_" + n)
    return (loss, grad_x, *[grads[n] for n in WEIGHTS], *[deltas[n] for n in WEIGHTS],
            *[new_m[n] for n in WEIGHTS], *[new_v[n] for n in WEIGHTS])
```
